```python
import math
import jax, jax.numpy as jnp
from jax import lax
import numpy as np

D_MODEL = 1024
BATCH = 32
SEQ = 2048
DEPTH = 4

N_A_LAYERS = max(1, DEPTH // 2)
N_B_LAYERS = DEPTH - N_A_LAYERS
POOL_WINDOWS = (2, 4, 8, 16)
N_POOL_GROUPS = len(POOL_WINDOWS)
POOL_GROUP_DIM = D_MODEL // N_POOL_GROUPS
HEAD_DIM = 64
N_HEADS = D_MODEL // HEAD_DIM
DILATED_GROUPS = ((128, 1), (512, 4), (2048, 16))
N_GROUPS = len(DILATED_GROUPS)
ATTN_DIM = N_HEADS * HEAD_DIM
Q_DIM = N_GROUPS * ATTN_DIM
ROPE_THETA = 10000.0
D_FF = 2816
CONV_WIDTH = 3
DEEPNORM_ALPHA = (2.0 * DEPTH) ** 0.25
DEEPNORM_BETA = (8.0 * DEPTH) ** -0.25
LN_EPS = 1e-5

kernel_name = "yoco_pool_dilated_attn_convffn_deepnorm"


def layer_norm(x, g, b):
    xf = x.astype(jnp.float32)
    mu = xf.mean(-1, keepdims=True)
    var = jnp.square(xf - mu).mean(-1, keepdims=True)
    y = (xf - mu) * lax.rsqrt(var + LN_EPS) * g.astype(jnp.float32) + b.astype(jnp.float32)
    return y.astype(x.dtype)


def rope_tables(seq):
    inv_freq = ROPE_THETA ** (-jnp.arange(0, HEAD_DIM, 2, dtype=jnp.float32) / HEAD_DIM)
    ang = jnp.arange(seq, dtype=jnp.float32)[:, None] * inv_freq[None, :]
    return jnp.cos(ang), jnp.sin(ang)


def apply_rope(t, cos, sin):
    tf = t.astype(jnp.float32)
    x1, x2 = tf[..., : HEAD_DIM // 2], tf[..., HEAD_DIM // 2:]
    c, s = cos[None, :, None, :], sin[None, :, None, :]
    return jnp.concatenate([x1 * c - x2 * s, x2 * c + x1 * s], axis=-1).astype(t.dtype)


def pool_mixer(x, pool_w, pool_scale):
    B, S, D = x.shape
    xg = x.reshape(B, S, N_POOL_GROUPS, POOL_GROUP_DIM)
    csum = jnp.cumsum(xg.astype(jnp.float32), axis=1)
    c0 = jnp.concatenate([jnp.zeros_like(csum[:, :1]), csum], axis=1)
    pos = jnp.arange(S, dtype=jnp.float32)
    pooled = []
    for g, w in enumerate(POOL_WINDOWS):
        w_eff = min(w, S)
        lagged = jnp.concatenate([jnp.zeros_like(c0[:, : w_eff - 1, g]), c0[:, : S - w_eff + 1, g]], axis=1)
        count = jnp.minimum(pos + 1.0, float(w))[None, :, None]
        pooled.append((c0[:, 1:, g] - lagged) / count)
    pooled = jnp.stack(pooled, axis=2).astype(x.dtype) - xg
    y = jnp.einsum('bsgc,gce->bsge', pooled, pool_w).reshape(B, S, D)
    return y * pool_scale


def dilated_branch(q, k, v, dilation, span):
    B, S, H, Dh = q.shape
    L = S // dilation
    nb = -(-L // span)
    pad = nb * span - L

    def strided_blocks(t):
        t = t.reshape(B, L, dilation, H, Dh).transpose(0, 2, 3, 1, 4)
        t = jnp.pad(t, ((0, 0), (0, 0), (0, 0), (0, pad), (0, 0)))
        return t.reshape(B, dilation, H, nb, span, Dh)

    def with_prev(t):
        prev = jnp.concatenate([jnp.zeros_like(t[:, :, :, :1]), t[:, :, :, :-1]], axis=3)
        return jnp.concatenate([prev, t], axis=4)

    qb = strided_blocks(q)
    kk = with_prev(strided_blocks(k))
    vv = with_prev(strided_blocks(v))
    s = jnp.einsum('brhnqc,brhnkc->brhnqk', qb, kk, preferred_element_type=jnp.float32)
    qi = jnp.arange(span)[:, None]
    kj = jnp.arange(2 * span)[None, :]
    rel = span + qi - kj
    band = (rel >= 0) & (rel <= span)
    has_prev = (jnp.arange(nb) > 0)[:, None, None] | (kj >= span)[None]
    valid = band[None] & has_prev
    s = jnp.where(valid, s, -jnp.inf)
    m = s.max(-1, keepdims=True)
    p = jnp.exp(s - m)
    l = p.sum(-1, keepdims=True)
    o = jnp.einsum('brhnqk,brhnkc->brhnqc', (p / l).astype(v.dtype), vv)
    lse = (m + jnp.log(l))[..., 0]
    o = o.reshape(B, dilation, H, nb * span, Dh)[:, :, :, :L].transpose(0, 3, 1, 2, 4).reshape(B, S, H, Dh)
    lse = lse.reshape(B, dilation, H, nb * span)[..., :L].transpose(0, 3, 1, 2).reshape(B, S, H)
    return o, lse


def dilated_attention(x, k_shared, v_shared, w_q, w_o, cos, sin):
    B, S, _ = x.shape
    q = (x @ w_q).reshape(B, S, N_GROUPS * N_HEADS, HEAD_DIM)
    q = (apply_rope(q, cos, sin) * (HEAD_DIM ** -0.5)).reshape(B, S, N_GROUPS, N_HEADS, HEAD_DIM)
    outs, lses = [], []
    for g, (window, dilation) in enumerate(DILATED_GROUPS):
        o, lse = dilated_branch(q[:, :, g], k_shared[:, :, g], v_shared[:, :, g], dilation, window // dilation)
        outs.append(o)
        lses.append(lse)
    weights = jax.nn.softmax(jnp.stack(lses, axis=0), axis=0)
    o = jnp.sum(weights[..., None].astype(x.dtype) * jnp.stack(outs, axis=0), axis=0)
    return o.reshape(B, S, ATTN_DIM) @ w_o


def conv_ffn(x, w_gate, w_up, conv_w, conv_b, w_down):
    S = x.shape[1]
    g = x @ w_gate
    u = x @ w_up
    gp = jnp.pad(g, ((0, 0), (CONV_WIDTH - 1, 0), (0, 0)))
    conv = conv_b
    for j in range(CONV_WIDTH):
        conv = conv + conv_w[j] * gp[:, j: j + S]
    h = jax.nn.gelu(conv) * u
    return h @ w_down


def _fwd_setup_inputs(seed: int = 0) -> dict:
    key = jax.random.key(seed)
    ks = jax.random.split(key, 16)
    f32 = jnp.float32
    beta = DEEPNORM_BETA
    nrm = lambda k, shape, scale: jax.random.normal(k, shape, f32) * scale
    x = jax.random.normal(ks[0], (BATCH, SEQ, D_MODEL), f32)
    pool_w = nrm(ks[1], (N_A_LAYERS, N_POOL_GROUPS, POOL_GROUP_DIM, POOL_GROUP_DIM), beta * POOL_GROUP_DIM ** -0.5)
    pool_scale = 1.0 + nrm(ks[2], (N_A_LAYERS, D_MODEL), 0.1)
    w_q = nrm(ks[3], (N_B_LAYERS, D_MODEL, Q_DIM), D_MODEL ** -0.5)
    w_k = nrm(ks[4], (D_MODEL, Q_DIM), D_MODEL ** -0.5)
    w_v = nrm(ks[5], (D_MODEL, Q_DIM), beta * D_MODEL ** -0.5)
    w_kv = jnp.concatenate([w_k, w_v], axis=1)
    w_o = nrm(ks[6], (N_B_LAYERS, ATTN_DIM, D_MODEL), beta * ATTN_DIM ** -0.5)
    ffn_w_gate = nrm(ks[7], (DEPTH, D_MODEL, D_FF), D_MODEL ** -0.5)
    ffn_w_up = nrm(ks[8], (DEPTH, D_MODEL, D_FF), beta * D_MODEL ** -0.5)
    ffn_conv_w = nrm(ks[9], (DEPTH, CONV_WIDTH, D_FF), CONV_WIDTH ** -0.5)
    ffn_conv_b = nrm(ks[10], (DEPTH, D_FF), 0.02)
    ffn_w_down = nrm(ks[11], (DEPTH, D_FF, D_MODEL), beta * D_FF ** -0.5)
    ln1_g = 1.0 + nrm(ks[12], (DEPTH, D_MODEL), 0.05)
    ln1_b = nrm(ks[13], (DEPTH, D_MODEL), 0.02)
    ln2_g = 1.0 + nrm(ks[14], (DEPTH, D_MODEL), 0.05)
    ln2_b = nrm(ks[15], (DEPTH, D_MODEL), 0.02)
    return {"x": x, "pool_w": pool_w, "pool_scale": pool_scale, "w_q": w_q, "w_kv": w_kv,
            "w_o": w_o, "ffn_w_gate": ffn_w_gate, "ffn_w_up": ffn_w_up, "ffn_conv_w": ffn_conv_w,
            "ffn_conv_b": ffn_conv_b, "ffn_w_down": ffn_w_down, "ln1_g": ln1_g, "ln1_b": ln1_b,
            "ln2_g": ln2_g, "ln2_b": ln2_b}


def _fwd_reference(x, pool_w, pool_scale, w_q, w_kv, w_o, ffn_w_gate, ffn_w_up, ffn_conv_w,
              ffn_conv_b, ffn_w_down, ln1_g, ln1_b, ln2_g, ln2_b):
    B, S, _ = x.shape
    cos, sin = rope_tables(S)
    k_shared = None
    v_shared = None
    for i in range(DEPTH):
        if i < N_A_LAYERS:
            mix = pool_mixer(x, pool_w[i], pool_scale[i])
        else:
            j = i - N_A_LAYERS
            mix = dilated_attention(x, k_shared, v_shared, w_q[j], w_o[j], cos, sin)
        x = layer_norm(DEEPNORM_ALPHA * x + mix, ln1_g[i], ln1_b[i])
        ffn = conv_ffn(x, ffn_w_gate[i], ffn_w_up[i], ffn_conv_w[i], ffn_conv_b[i], ffn_w_down[i])
        x = layer_norm(DEEPNORM_ALPHA * x + ffn, ln2_g[i], ln2_b[i])
        if i == N_A_LAYERS - 1:
            kv = (x @ w_kv).reshape(B, S, 2, N_GROUPS * N_HEADS, HEAD_DIM)
            k_shared = apply_rope(kv[:, :, 0], cos, sin).reshape(B, S, N_GROUPS, N_HEADS, HEAD_DIM)
            v_shared = kv[:, :, 1].reshape(B, S, N_GROUPS, N_HEADS, HEAD_DIM)
    return x


import jax as _jax
import jax.numpy as _jnp

TWIN_FORMAT = 'train_step'
FWD_PARAMS = ['x', 'pool_w', 'pool_scale', 'w_q', 'w_kv', 'w_o', 'ffn_w_gate', 'ffn_w_up', 'ffn_conv_w', 'ffn_conv_b', 'ffn_w_down', 'ln1_g', 'ln1_b', 'ln2_g', 'ln2_b']
TWIN_WEIGHTS = ['pool_w', 'pool_scale', 'w_q', 'w_kv', 'w_o', 'ffn_w_gate', 'ffn_w_up', 'ffn_conv_w', 'ffn_conv_b', 'ffn_w_down', 'ln1_g', 'ln1_b', 'ln2_g', 'ln2_b']
TWIN_DIFF_INPUT = 'x'
TWIN_INPUTS = ['x', 'pool_w', 'pool_scale', 'w_q', 'w_kv', 'w_o', 'ffn_w_gate', 'ffn_w_up', 'ffn_conv_w', 'ffn_conv_b', 'ffn_w_down', 'ln1_g', 'ln1_b', 'ln2_g', 'ln2_b', 'loss_target', 'm_pool_w', 'm_pool_scale', 'm_w_q', 'm_w_kv', 'm_w_o', 'm_ffn_w_gate', 'm_ffn_w_up', 'm_ffn_conv_w', 'm_ffn_conv_b', 'm_ffn_w_down', 'm_ln1_g', 'm_ln1_b', 'm_ln2_g', 'm_ln2_b', 'v_pool_w', 'v_pool_scale', 'v_w_q', 'v_w_kv', 'v_w_o', 'v_ffn_w_gate', 'v_ffn_w_up', 'v_ffn_conv_w', 'v_ffn_conv_b', 'v_ffn_w_down', 'v_ln1_g', 'v_ln1_b', 'v_ln2_g', 'v_ln2_b']
TWIN_OUTPUTS = ['loss', 'grad_x', 'grad_pool_w', 'grad_pool_scale', 'grad_w_q', 'grad_w_kv', 'grad_w_o', 'grad_ffn_w_gate', 'grad_ffn_w_up', 'grad_ffn_conv_w', 'grad_ffn_conv_b', 'grad_ffn_w_down', 'grad_ln1_g', 'grad_ln1_b', 'grad_ln2_g', 'grad_ln2_b', 'delta_pool_w', 'delta_pool_scale', 'delta_w_q', 'delta_w_kv', 'delta_w_o', 'delta_ffn_w_gate', 'delta_ffn_w_up', 'delta_ffn_conv_w', 'delta_ffn_conv_b', 'delta_ffn_w_down', 'delta_ln1_g', 'delta_ln1_b', 'delta_ln2_g', 'delta_ln2_b', 'new_m_pool_w', 'new_m_pool_scale', 'new_m_w_q', 'new_m_w_kv', 'new_m_w_o', 'new_m_ffn_w_gate', 'new_m_ffn_w_up', 'new_m_ffn_conv_w', 'new_m_ffn_conv_b', 'new_m_ffn_w_down', 'new_m_ln1_g', 'new_m_ln1_b', 'new_m_ln2_g', 'new_m_ln2_b', 'new_v_pool_w', 'new_v_pool_scale', 'new_v_w_q', 'new_v_w_kv', 'new_v_w_o', 'new_v_ffn_w_gate', 'new_v_ffn_w_up', 'new_v_ffn_conv_w', 'new_v_ffn_conv_b', 'new_v_ffn_w_down', 'new_v_ln1_g', 'new_v_ln1_b', 'new_v_ln2_g', 'new_v_ln2_b']
TWIN_LEAF_KINDS = {'loss': 'loss', 'grad_x': 'grad_x', 'grad_pool_w': 'grad_w', 'grad_pool_scale': 'grad_w', 'grad_w_q': 'grad_w', 'grad_w_kv': 'grad_w', 'grad_w_o': 'grad_w', 'grad_ffn_w_gate': 'grad_w', 'grad_ffn_w_up': 'grad_w', 'grad_ffn_conv_w': 'grad_w', 'grad_ffn_conv_b': 'grad_w', 'grad_ffn_w_down': 'grad_w', 'grad_ln1_g': 'grad_w', 'grad_ln1_b': 'grad_w', 'grad_ln2_g': 'grad_w', 'grad_ln2_b': 'grad_w', 'delta_pool_w': 'delta_w', 'delta_pool_scale': 'delta_w', 'delta_w_q': 'delta_w', 'delta_w_kv': 'delta_w', 'delta_w_o': 'delta_w', 'delta_ffn_w_gate': 'delta_w', 'delta_ffn_w_up': 'delta_w', 'delta_ffn_conv_w': 'delta_w', 'delta_ffn_conv_b': 'delta_w', 'delta_ffn_w_down': 'delta_w', 'delta_ln1_g': 'delta_w', 'delta_ln1_b': 'delta_w', 'delta_ln2_g': 'delta_w', 'delta_ln2_b': 'delta_w', 'new_m_pool_w': 'new_m', 'new_m_pool_scale': 'new_m', 'new_m_w_q': 'new_m', 'new_m_w_kv': 'new_m', 'new_m_w_o': 'new_m', 'new_m_ffn_w_gate': 'new_m', 'new_m_ffn_w_up': 'new_m', 'new_m_ffn_conv_w': 'new_m', 'new_m_ffn_conv_b': 'new_m', 'new_m_ffn_w_down': 'new_m', 'new_m_ln1_g': 'new_m', 'new_m_ln1_b': 'new_m', 'new_m_ln2_g': 'new_m', 'new_m_ln2_b': 'new_m', 'new_v_pool_w': 'new_v', 'new_v_pool_scale': 'new_v', 'new_v_w_q': 'new_v', 'new_v_w_kv': 'new_v', 'new_v_w_o': 'new_v', 'new_v_ffn_w_gate': 'new_v', 'new_v_ffn_w_up': 'new_v', 'new_v_ffn_conv_w': 'new_v', 'new_v_ffn_conv_b': 'new_v', 'new_v_ffn_w_down': 'new_v', 'new_v_ln1_g': 'new_v', 'new_v_ln1_b': 'new_v', 'new_v_ln2_g': 'new_v', 'new_v_ln2_b': 'new_v'}


def _forward(args):
    return _fwd_reference(*[args[k] for k in FWD_PARAMS])


def _output_shape():
    out = _jax.eval_shape(lambda: _forward(_fwd_setup_inputs(0)))
    return out.shape, out.dtype

N_MICROBATCH = 1
ADAM_LR = 0.001
ADAM_B1 = 0.9
ADAM_B2 = 0.999
ADAM_EPS = 1e-08
ADAM_WD = 0.01
ADAM_STEP = 10
PER_EXAMPLE_BATCH_AXIS = {'x': 0, 'loss_target': 0}
SHARED_INPUTS = []
_WEIGHT_DTYPES = {'pool_w': _jnp.float32, 'pool_scale': _jnp.float32, 'w_q': _jnp.float32, 'w_kv': _jnp.float32, 'w_o': _jnp.float32, 'ffn_w_gate': _jnp.float32, 'ffn_w_up': _jnp.float32, 'ffn_conv_w': _jnp.float32, 'ffn_conv_b': _jnp.float32, 'ffn_w_down': _jnp.float32, 'ln1_g': _jnp.float32, 'ln1_b': _jnp.float32, 'ln2_g': _jnp.float32, 'ln2_b': _jnp.float32}
MOMENT_SCALE = {'pool_w': 1.543860e-01, 'pool_scale': 1.991745e-01, 'w_q': 2.550412e-03, 'w_kv': 7.450448e-03, 'w_o': 1.207241e-02, 'ffn_w_gate': 1.092586e-02, 'ffn_w_up': 2.545025e-02, 'ffn_conv_w': 1.111347e-02, 'ffn_conv_b': 1.105169e-02, 'ffn_w_down': 4.257117e-02, 'ln1_g': 6.020041e+00, 'ln1_b': 8.069773e-01, 'ln2_g': 3.364374e+01, 'ln2_b': 1.517190e+00}


def _to_microbatches(a, axis):
    t = _jnp.moveaxis(a, axis, 0)
    t = t.reshape((N_MICROBATCH, t.shape[0] // N_MICROBATCH) + t.shape[1:])
    return _jnp.moveaxis(t, 1, axis + 1)


def setup_inputs(seed: int = 0) -> dict:
    inp = _fwd_setup_inputs(seed)
    key = _jax.random.fold_in(_jax.random.key(seed), 7919)
    shape, _ = _output_shape()
    out = dict(inp)
    out["loss_target"] = _jax.random.normal(_jax.random.fold_in(key, 0), shape, _jnp.float32)
    for i, name in enumerate(TWIN_WEIGHTS):
        w = inp[name].astype(_jnp.float32)
        if MOMENT_SCALE is None:
            s = _jnp.sqrt(_jnp.mean(_jnp.square(w)) + 1e-30)
        else:
            s = MOMENT_SCALE[name]
        km, kv = _jax.random.split(_jax.random.fold_in(key, i + 1))
        out[name] = w
        out["m_" + name] = s * _jax.random.normal(km, w.shape, _jnp.float32)
        out["v_" + name] = (s * s) * _jax.random.uniform(kv, w.shape, _jnp.float32, 0.5, 1.5)
    if N_MICROBATCH > 1:
        for name, axis in PER_EXAMPLE_BATCH_AXIS.items():
            out[name] = _to_microbatches(out[name], axis)
    return {'x': out['x'], 'pool_w': out['pool_w'], 'pool_scale': out['pool_scale'], 'w_q': out['w_q'], 'w_kv': out['w_kv'], 'w_o': out['w_o'], 'ffn_w_gate': out['ffn_w_gate'], 'ffn_w_up': out['ffn_w_up'], 'ffn_conv_w': out['ffn_conv_w'], 'ffn_conv_b': out['ffn_conv_b'], 'ffn_w_down': out['ffn_w_down'], 'ln1_g': out['ln1_g'], 'ln1_b': out['ln1_b'], 'ln2_g': out['ln2_g'], 'ln2_b': out['ln2_b'], 'loss_target': out['loss_target'], 'm_pool_w': out['m_pool_w'], 'm_pool_scale': out['m_pool_scale'], 'm_w_q': out['m_w_q'], 'm_w_kv': out['m_w_kv'], 'm_w_o': out['m_w_o'], 'm_ffn_w_gate': out['m_ffn_w_gate'], 'm_ffn_w_up': out['m_ffn_w_up'], 'm_ffn_conv_w': out['m_ffn_conv_w'], 'm_ffn_conv_b': out['m_ffn_conv_b'], 'm_ffn_w_down': out['m_ffn_w_down'], 'm_ln1_g': out['m_ln1_g'], 'm_ln1_b': out['m_ln1_b'], 'm_ln2_g': out['m_ln2_g'], 'm_ln2_b': out['m_ln2_b'], 'v_pool_w': out['v_pool_w'], 'v_pool_scale': out['v_pool_scale'], 'v_w_q': out['v_w_q'], 'v_w_kv': out['v_w_kv'], 'v_w_o': out['v_w_o'], 'v_ffn_w_gate': out['v_ffn_w_gate'], 'v_ffn_w_up': out['v_ffn_w_up'], 'v_ffn_conv_w': out['v_ffn_conv_w'], 'v_ffn_conv_b': out['v_ffn_conv_b'], 'v_ffn_w_down': out['v_ffn_w_down'], 'v_ln1_g': out['v_ln1_g'], 'v_ln1_b': out['v_ln1_b'], 'v_ln2_g': out['v_ln2_g'], 'v_ln2_b': out['v_ln2_b']}


def _loss(weights, diff, rest, loss_target):
    with _jax.named_scope("forward"):
        args = {**rest, TWIN_DIFF_INPUT: diff, **{k: w.astype(_WEIGHT_DTYPES[k]) for k, w in weights.items()}}
        y = _forward(args)
    with _jax.named_scope("loss_head"):
        err = _jnp.square(y.astype(_jnp.float32) - loss_target)
        return 0.5 * _jnp.sum(_jnp.mean(err, axis=-1)) if err.ndim else 0.5 * err


def _adamw(w, g, m, v):
    m = ADAM_B1 * m + (1.0 - ADAM_B1) * g
    v = ADAM_B2 * v + (1.0 - ADAM_B2) * _jnp.square(g)
    m_hat = m / (1.0 - ADAM_B1 ** ADAM_STEP)
    v_hat = v / (1.0 - ADAM_B2 ** ADAM_STEP)
    delta = -ADAM_LR * (m_hat / (_jnp.sqrt(v_hat) + ADAM_EPS) + ADAM_WD * w)
    return delta, m, v


def reference(x, pool_w, pool_scale, w_q, w_kv, w_o, ffn_w_gate, ffn_w_up, ffn_conv_w, ffn_conv_b, ffn_w_down, ln1_g, ln1_b, ln2_g, ln2_b, loss_target, m_pool_w, m_pool_scale, m_w_q, m_w_kv, m_w_o, m_ffn_w_gate, m_ffn_w_up, m_ffn_conv_w, m_ffn_conv_b, m_ffn_w_down, m_ln1_g, m_ln1_b, m_ln2_g, m_ln2_b, v_pool_w, v_pool_scale, v_w_q, v_w_kv, v_w_o, v_ffn_w_gate, v_ffn_w_up, v_ffn_conv_w, v_ffn_conv_b, v_ffn_w_down, v_ln1_g, v_ln1_b, v_ln2_g, v_ln2_b):
    given = dict(x=x, pool_w=pool_w, pool_scale=pool_scale, w_q=w_q, w_kv=w_kv, w_o=w_o, ffn_w_gate=ffn_w_gate, ffn_w_up=ffn_w_up, ffn_conv_w=ffn_conv_w, ffn_conv_b=ffn_conv_b, ffn_w_down=ffn_w_down, ln1_g=ln1_g, ln1_b=ln1_b, ln2_g=ln2_g, ln2_b=ln2_b, loss_target=loss_target, m_pool_w=m_pool_w, m_pool_scale=m_pool_scale, m_w_q=m_w_q, m_w_kv=m_w_kv, m_w_o=m_w_o, m_ffn_w_gate=m_ffn_w_gate, m_ffn_w_up=m_ffn_w_up, m_ffn_conv_w=m_ffn_conv_w, m_ffn_conv_b=m_ffn_conv_b, m_ffn_w_down=m_ffn_w_down, m_ln1_g=m_ln1_g, m_ln1_b=m_ln1_b, m_ln2_g=m_ln2_g, m_ln2_b=m_ln2_b, v_pool_w=v_pool_w, v_pool_scale=v_pool_scale, v_w_q=v_w_q, v_w_kv=v_w_kv, v_w_o=v_w_o, v_ffn_w_gate=v_ffn_w_gate, v_ffn_w_up=v_ffn_w_up, v_ffn_conv_w=v_ffn_conv_w, v_ffn_conv_b=v_ffn_conv_b, v_ffn_w_down=v_ffn_w_down, v_ln1_g=v_ln1_g, v_ln1_b=v_ln1_b, v_ln2_g=v_ln2_g, v_ln2_b=v_ln2_b)
    weights = {n: given[n] for n in TWIN_WEIGHTS}
    shared = {n: given[n] for n in SHARED_INPUTS}
    per_example = {n: given[n] for n in ['x']}
    grad_fn = _jax.value_and_grad(_loss, argnums=(0, 1))

    def one_microbatch(ex, loss_target):
        ex = dict(ex)
        diff = ex.pop(TWIN_DIFF_INPUT)
        return grad_fn(weights, diff, {**shared, **ex}, loss_target)

    if N_MICROBATCH == 1:
        loss, (grad_w, grad_x) = one_microbatch(per_example, given["loss_target"])
    else:
        def body(carry, xs):
            loss_sum, grad_sum = carry
            l_k, (gw_k, gx_k) = one_microbatch(xs[0], xs[1])
            with _jax.named_scope("update"):
                return (loss_sum + l_k, _jax.tree.map(_jnp.add, grad_sum, gw_k)), gx_k

        init = (_jnp.zeros((), _jnp.float32), _jax.tree.map(_jnp.zeros_like, weights))
        (loss, grad_w), grad_x = _jax.lax.scan(body, init, (per_example, given["loss_target"]))
    with _jax.named_scope("update"):
        delta_w, new_m, new_v = {}, {}, {}
        for n in TWIN_WEIGHTS:
            delta_w[n], new_m[n], new_v[n] = _adamw(weights[n], grad_w[n], given["m_" + n], given["v_" + n])
    return (loss, grad_x, *[grad_w[n] for n in TWIN_WEIGHTS], *[delta_w[n] for n in TWIN_WEIGHTS],
            *[new_m[n] for n in TWIN_WEIGHTS], *[new_v[n] for n in TWIN_WEIGHTS])
```

```python
import functools
import math

import jax
import jax.numpy as jnp
from jax import lax
from jax.experimental import pallas as pl
from jax.experimental.pallas import tpu as pltpu

F32 = jnp.float32
BF16 = jnp.bfloat16
MXU = jnp.bfloat16

DEPTH = 4
N_POOL_LAYERS = 2
POOL_WINDOWS = (2, 4, 8, 16)
POOL_GROUP_DIM = 256
HEAD_DIM = 64
N_HEADS = 16
DILATIONS = (1, 4, 16)
SPAN = 128
ROPE_THETA = 10000.0
ALPHA = (2.0 * DEPTH) ** 0.25
LN_EPS = 1e-5
ADAM_LR, ADAM_B1, ADAM_B2, ADAM_EPS, ADAM_WD, ADAM_STEP = 0.001, 0.9, 0.999, 1e-08, 0.01, 10

N_CHIPS = 4
VMEM_LIMIT = 56 * 1024 * 1024
MESH = pl.DeviceIdType.MESH

NN = (((1,), (0,)), ((), ()))
NT = (((1,), (1,)), ((), ()))
TN = (((0,), (0,)), ((), ()))


def _params(sem=None):
    return pltpu.CompilerParams(dimension_semantics=sem, vmem_limit_bytes=VMEM_LIMIT)


def _chunks(n_rows, ch, fn):
    def step(i, carry):
        fn(pl.multiple_of(i * ch, ch))
        return carry

    lax.fori_loop(0, n_rows // ch, step, 0)


def _fold8(v):
    return jnp.sum(v.reshape(v.shape[0] // 8, 8, v.shape[1]), axis=0)


def _down(v, k):
    return pltpu.roll(v, k, 0)


def _up(v, k):
    return pltpu.roll(v, v.shape[0] - k, 0)


def _mm(name, pairs, dims, grid, out_shape, out_dtype, out_block, out_map, nk, into=None):
    n_pairs = len(pairs)
    kax = len(grid) - 1

    def body(*refs):
        o_ref = refs[2 * n_pairs + (1 if into is not None else 0)]
        part = None
        for p in range(n_pairs):
            t = lax.dot_general(refs[2 * p][...], refs[2 * p + 1][...], dims, preferred_element_type=F32)
            part = t if part is None else part + t
        if nk == 1:
            o_ref[...] = part.astype(o_ref.dtype)
        else:
            k = pl.program_id(kax)

            @pl.when(k == 0)
            def _():
                o_ref[...] = part

            @pl.when(k > 0)
            def _():
                o_ref[...] += part

    operands, in_specs = [], []
    for a, a_block, a_map, b, b_block, b_map in pairs:
        operands += [a, b]
        in_specs += [pl.BlockSpec(a_block, a_map), pl.BlockSpec(b_block, b_map)]
    aliases = {}
    if into is not None:
        operands.append(into)
        in_specs.append(pl.BlockSpec(memory_space=pl.ANY))
        aliases = {2 * n_pairs: 0}
    assert nk == 1 or out_dtype == F32
    sem = ("parallel",) * kax + ("arbitrary",)
    return pl.pallas_call(
        body, name=name, grid=grid, in_specs=in_specs, out_specs=pl.BlockSpec(out_block, out_map),
        out_shape=jax.ShapeDtypeStruct(out_shape, out_dtype), input_output_aliases=aliases,
        compiler_params=_params(sem),
    )(*operands)


TM = 1024


def _mm_cols(name, a, w, w_idx, out_dtype, cb=None):
    n, k = a.shape
    j_n, c = w.shape[0], w.shape[-1]
    cb = c if cb is None else cb
    s = c // cb
    wb = (None,) * (w.ndim - 2) + (k, cb)
    return _mm(name, [(a, (TM, k), lambda q, m, _: (m, 0), w, wb, lambda q, m, _: (q // s, *w_idx, 0, q % s))], NN,
               (j_n * s, n // TM, 1), (n, j_n * c), out_dtype, (TM, cb), lambda q, m, _: (m, q), 1)


def _mm_shard_out(name, a, w, w_idx, out_dtype):
    n, k = a.shape
    j_n, c = w.shape[0], w.shape[-1]
    wb = (None,) * (w.ndim - 2) + (k, c)
    return _mm(name, [(a, (TM, k), lambda j, m, _: (m, 0), w, wb, lambda j, m, _: (j, *w_idx, 0, 0))], NN,
               (j_n, n // TM, 1), (j_n, n, c), out_dtype, (None, TM, c), lambda j, m, _: (j, m, 0), 1)


def _mm_shard_in(name, a4, w, w_idx):
    j_n, n, c = a4.shape
    k = w.shape[-1]
    wb = (None,) * (w.ndim - 2) + (c, k)
    return _mm(name, [(a4, (None, TM, c), lambda m, j: (j, m, 0), w, wb, lambda m, j: (j, *w_idx, 0, 0))], NN,
               (n // TM, j_n), (n, k), F32, (TM, k), lambda m, j: (m, 0), j_n)


def _mm_nt_shard_out(name, a, w, w_idx):
    n, k = a.shape
    j_n, c = w.shape[0], w.shape[-2]
    wb = (None,) * (w.ndim - 2) + (c, k)
    return _mm(name, [(a, (TM, k), lambda j, m, _: (m, 0), w, wb, lambda j, m, _: (j, *w_idx, 0, 0))], NT,
               (j_n, n // TM, 1), (j_n, n, c), F32, (None, TM, c), lambda j, m, _: (j, m, 0), 1)


def _mm_nt_shard_in(name, terms):
    pairs = []
    for a4, w, w_idx in terms:
        j_n, n, c = a4.shape
        k = w.shape[-2]
        wb = (None,) * (w.ndim - 2) + (k, c)
        pairs.append((a4, (None, TM, c), lambda m, j: (j, m, 0), w, wb,
                      functools.partial(lambda m, j, w_idx: (j, *w_idx, 0, 0), w_idx=w_idx)))
    return _mm(name, pairs, NT, (n // TM, j_n), (n, k), F32, (TM, k), lambda m, j: (m, 0), j_n)


def _mm_nt_cols_in(name, a, w, w_idx, cb):
    n, ct = a.shape
    j_n, k, c = w.shape[0], w.shape[-2], w.shape[-1]
    s = c // cb
    wb = (None,) * (w.ndim - 2) + (k, cb)
    return _mm(name, [(a, (TM, cb), lambda m, q: (m, q), w, wb, lambda m, q: (q // s, *w_idx, 0, q % s))], NT,
               (n // TM, ct // cb), (n, k), F32, (TM, k), lambda m, q: (m, 0), ct // cb)


def _mm_tn(name, a, a_cols, b, b_cols, n_blocks, out_shape, out_block, out_map, into=None, a_lead=None, b_lead=None):
    n = a.shape[-2]
    a_nb = a.shape[-1] // a_cols
    b_nb = b.shape[-1] // b_cols
    if a_lead is None:
        a_block, a_map = (TM, a_cols), lambda q, t: (t, q if a_nb > 1 else 0)
    else:
        a_block, a_map = (None, TM, a_cols), lambda q, t: (*a_lead(q), t, 0)
    if b_lead is None:
        b_block, b_map = (TM, b_cols), lambda q, t: (t, q if b_nb > 1 else 0)
    else:
        b_block, b_map = (None, TM, b_cols), lambda q, t: (*b_lead(q), t, 0)
    return _mm(name, [(a, a_block, a_map, b, b_block, b_map)], TN, (n_blocks, n // TM), out_shape, F32,
               out_block, lambda q, t: out_map(q), n // TM, into=into)


LN_ROWS = 512
LN_CH = 32


def _ln_stats(r):
    mu = jnp.mean(r, -1, keepdims=True)
    xc = r - mu
    var = jnp.mean(xc * xc, -1, keepdims=True)
    return xc, lax.rsqrt(var + LN_EPS)


def _add_ln(name, a, mix, g, b):
    n, d = a.shape

    def body(a_ref, m_ref, g_ref, b_ref, r_ref, h_ref, hb_ref):
        gg, bb = g_ref[...], b_ref[...]

        def chunk(s):
            rows = pl.ds(s, LN_CH)
            r = ALPHA * a_ref[rows, :] + m_ref[rows, :]
            xc, rstd = _ln_stats(r)
            y = xc * rstd * gg + bb
            r_ref[rows, :] = r
            h_ref[rows, :] = y
            hb_ref[rows, :] = y.astype(MXU)

        _chunks(LN_ROWS, LN_CH, chunk)

    tile = pl.BlockSpec((LN_ROWS, d), lambda i: (i, 0))
    vec = pl.BlockSpec((1, d), lambda i: (0, 0))
    return pl.pallas_call(
        body, name=name, grid=(n // LN_ROWS,), in_specs=[tile, tile, vec, vec], out_specs=[tile, tile, tile],
        out_shape=[jax.ShapeDtypeStruct((n, d), F32), jax.ShapeDtypeStruct((n, d), F32), jax.ShapeDtypeStruct((n, d), MXU)],
        compiler_params=_params(("parallel",)),
    )(a, mix, g, b)


def _ln_bwd(name, r, g, d_a, scale_a, d_rest):
    n, d = r.shape
    n_rest = len(d_rest)
    steps = n // LN_ROWS

    def body(*refs):
        r_ref, g_ref, da_ref = refs[:3]
        rest = refs[3:3 + n_rest]
        dr_ref, drb_ref, gb_ref, acc = refs[3 + n_rest:]
        i = pl.program_id(0)

        @pl.when(i == 0)
        def _():
            acc[...] = jnp.zeros_like(acc)

        gg = g_ref[...]

        def chunk(s):
            rows = pl.ds(s, LN_CH)
            xc, rstd = _ln_stats(r_ref[rows, :])
            xhat = xc * rstd
            dy = da_ref[rows, :] if scale_a == 1.0 else scale_a * da_ref[rows, :]
            for t in rest:
                dy = dy + t[rows, :]
            dyg = dy * gg
            m1 = jnp.mean(dyg, -1, keepdims=True)
            m2 = jnp.mean(dyg * xhat, -1, keepdims=True)
            dr = rstd * (dyg - m1 - xhat * m2)
            dr_ref[rows, :] = dr
            drb_ref[rows, :] = dr.astype(MXU)
            acc[0] += _fold8(dy * xhat)
            acc[1] += _fold8(dy)

        _chunks(LN_ROWS, LN_CH, chunk)

        @pl.when(i == steps - 1)
        def _():
            gb_ref[0:1, :] = jnp.sum(acc[0], axis=0, keepdims=True)
            gb_ref[1:2, :] = jnp.sum(acc[1], axis=0, keepdims=True)

    tile = pl.BlockSpec((LN_ROWS, d), lambda i: (i, 0))
    vec = pl.BlockSpec((1, d), lambda i: (0, 0))
    return pl.pallas_call(
        body, name=name, grid=(steps,), in_specs=[tile, vec, tile] + [tile] * n_rest,
        out_specs=[tile, tile, pl.BlockSpec((2, d), lambda i: (0, 0))],
        out_shape=[jax.ShapeDtypeStruct((n, d), F32), jax.ShapeDtypeStruct((n, d), MXU), jax.ShapeDtypeStruct((2, d), F32)],
        scratch_shapes=[pltpu.VMEM((2, 8, d), F32)],
        compiler_params=_params(("arbitrary",)),
    )(r, g, d_a, *d_rest)


FFN_ROWS = 512
FFN_CH = 32
GELU_C1 = math.sqrt(2.0 / math.pi)
GELU_C2 = 0.044715


def _conv3(v, prev8, w0, w1, w2, bias):
    n = v.shape[0]
    ext = jnp.concatenate([prev8, v], axis=0)
    g1 = _down(ext, 1)[8:8 + n]
    g2 = _down(ext, 2)[8:8 + n]
    return bias + w0 * g2 + w1 * g1 + w2 * v, g1, g2


def _glu_fwd(name, g4, u4, conv_w, conv_b):
    j_n, b_n, s_n, f = g4.shape
    tiles = s_n // FFN_ROWS

    def body(g_ref, halo_ref, u_ref, w_ref, b_ref, o_ref, gs):
        s = pl.program_id(2)
        gs[0:8, :] = jnp.where(s > 0, halo_ref[...], 0.0)
        gs[8:, :] = g_ref[...]
        w0, w1, w2, bias = w_ref[0:1, :], w_ref[1:2, :], w_ref[2:3, :], b_ref[...]

        def chunk(st):
            v = gs[pl.ds(pl.multiple_of(st + 8, 8), FFN_CH), :]
            conv, _, _ = _conv3(v, gs[pl.ds(st, 8), :], w0, w1, w2, bias)
            cdf = 0.5 * (1.0 + jnp.tanh(GELU_C1 * (conv + GELU_C2 * (conv * conv * conv))))
            o_ref[pl.ds(st, FFN_CH), :] = (conv * cdf * u_ref[pl.ds(st, FFN_CH), :]).astype(o_ref.dtype)

        _chunks(FFN_ROWS, FFN_CH, chunk)

    tile = pl.BlockSpec((None, None, FFN_ROWS, f), lambda j, b, s: (j, b, s, 0))
    halo = pl.BlockSpec((None, None, 8, f), lambda j, b, s: (j, b, jnp.maximum(s * (FFN_ROWS // 8) - 1, 0), 0))
    return pl.pallas_call(
        body, name=name, grid=(j_n, b_n, tiles),
        in_specs=[tile, halo, tile,
                  pl.BlockSpec((None, 3, f), lambda j, b, s: (j, 0, 0)),
                  pl.BlockSpec((None, 1, f), lambda j, b, s: (j, 0, 0))],
        out_specs=tile, out_shape=jax.ShapeDtypeStruct(g4.shape, MXU),
        scratch_shapes=[pltpu.VMEM((8 + FFN_ROWS, f), F32)],
        compiler_params=_params(("parallel", "parallel", "parallel")),
    )(g4, g4, u4, conv_w, conv_b)


def _glu_bwd(name, dh4, g4, u4, conv_w, conv_b):
    j_n, b_n, s_n, f = g4.shape
    tiles = s_n // FFN_ROWS
    ext = FFN_CH + 8

    def body(d_ref, dnext_ref, g_ref, gprev_ref, gnext_ref, u_ref, unext_ref, w_ref, b_ref,
             dg_ref, du_ref, wb_ref, gs, us, ds, acc):
        b, s = pl.program_id(1), pl.program_id(2)
        last = s == tiles - 1

        @pl.when((b == 0) & (s == 0))
        def _():
            acc[...] = jnp.zeros_like(acc)

        gs[0:8, :] = jnp.where(s > 0, gprev_ref[...], 0.0)
        gs[8:8 + FFN_ROWS, :] = g_ref[...]
        gs[8 + FFN_ROWS:, :] = gnext_ref[...]
        us[0:FFN_ROWS, :] = u_ref[...]
        us[FFN_ROWS:, :] = unext_ref[...]
        ds[0:FFN_ROWS, :] = d_ref[...]
        ds[FFN_ROWS:, :] = jnp.where(last, 0.0, dnext_ref[...])
        w0, w1, w2, bias = w_ref[0:1, :], w_ref[1:2, :], w_ref[2:3, :], b_ref[...]

        def chunk(st):
            v = gs[pl.ds(pl.multiple_of(st + 8, 8), ext), :]
            conv, g1, g2 = _conv3(v, gs[pl.ds(st, 8), :], w0, w1, w2, bias)
            th = jnp.tanh(GELU_C1 * (conv + GELU_C2 * (conv * conv * conv)))
            cdf = 0.5 * (1.0 + th)
            dact = cdf + conv * (0.5 * GELU_C1) * (1.0 - th * th) * (1.0 + (3.0 * GELU_C2) * (conv * conv))
            de = ds[pl.ds(st, ext), :]
            dconv = de * us[pl.ds(st, ext), :] * dact
            du_ref[pl.ds(st, FFN_CH), :] = (de * (conv * cdf))[:FFN_CH].astype(du_ref.dtype)
            dg = w2 * dconv + w1 * _up(dconv, 1) + w0 * _up(dconv, 2)
            dg_ref[pl.ds(st, FFN_CH), :] = dg[:FFN_CH].astype(dg_ref.dtype)
            dc = dconv[:FFN_CH]
            acc[0] += _fold8(dc * g2[:FFN_CH])
            acc[1] += _fold8(dc * g1[:FFN_CH])
            acc[2] += _fold8(dc * v[:FFN_CH])
            acc[3] += _fold8(dc)

        _chunks(FFN_ROWS, FFN_CH, chunk)

        @pl.when((b == b_n - 1) & last)
        def _():
            for k in range(4):
                wb_ref[k:k + 1, :] = jnp.sum(acc[k], axis=0, keepdims=True)

    blocks8 = FFN_ROWS // 8
    tile = pl.BlockSpec((None, None, FFN_ROWS, f), lambda j, b, s: (j, b, s, 0))
    prev = pl.BlockSpec((None, None, 8, f), lambda j, b, s: (j, b, jnp.maximum(s * blocks8 - 1, 0), 0))
    nxt = pl.BlockSpec((None, None, 8, f), lambda j, b, s: (j, b, jnp.minimum((s + 1) * blocks8, s_n // 8 - 1), 0))
    return pl.pallas_call(
        body, name=name, grid=(j_n, b_n, tiles),
        in_specs=[tile, nxt, tile, prev, nxt, tile, nxt,
                  pl.BlockSpec((None, 3, f), lambda j, b, s: (j, 0, 0)),
                  pl.BlockSpec((None, 1, f), lambda j, b, s: (j, 0, 0))],
        out_specs=[tile, tile, pl.BlockSpec((None, 4, f), lambda j, b, s: (j, 0, 0))],
        out_shape=[jax.ShapeDtypeStruct(g4.shape, MXU), jax.ShapeDtypeStruct(g4.shape, MXU),
                   jax.ShapeDtypeStruct((j_n, 4, f), F32)],
        scratch_shapes=[pltpu.VMEM((16 + FFN_ROWS, f), F32), pltpu.VMEM((8 + FFN_ROWS, f), F32),
                        pltpu.VMEM((8 + FFN_ROWS, f), F32), pltpu.VMEM((4, 8, f), F32)],
        compiler_params=_params(("parallel", "arbitrary", "arbitrary")),
    )(dh4, dh4, g4, g4, g4, u4, u4, conv_w, conv_b)


POOL_ROWS = 512
POOL_CH = 32
POOL_HALO = 16


def _pool_windows(v, t0, gi, causal):
    shift = _down if causal else _up
    acc, k = v, 1
    while k < POOL_WINDOWS[gi]:
        acc = acc + shift(acc, k)
        k *= 2
    return acc


def _count(t0, n, w):
    t = t0 + lax.broadcasted_iota(jnp.int32, (n, 1), 0)
    return jnp.minimum(t + 1, w).astype(F32)


def _pooled_into(xs, pooled, t_tile):
    def chunk(st):
        for gi, w in enumerate(POOL_WINDOWS):
            cols = slice(gi * POOL_GROUP_DIM, (gi + 1) * POOL_GROUP_DIM)
            v = xs[pl.ds(st, POOL_CH + POOL_HALO), cols]
            sums = _pool_windows(v, None, gi, True)[POOL_HALO:]
            val = sums / _count(t_tile + st, POOL_CH, w) - v[POOL_HALO:]
            pooled[pl.ds(st, POOL_CH), cols] = val.astype(pooled.dtype)

    _chunks(POOL_ROWS, POOL_CH, chunk)


def _pool_specs(b_n, s_n, d):
    per = POOL_ROWS // POOL_HALO
    tile = pl.BlockSpec((None, POOL_ROWS, d), lambda b, s: (b, s, 0))
    prev = pl.BlockSpec((None, POOL_HALO, d), lambda b, s: (b, jnp.maximum(s * per - 1, 0), 0))
    nxt = pl.BlockSpec((None, POOL_HALO, d), lambda b, s: (b, jnp.minimum((s + 1) * per, s_n // POOL_HALO - 1), 0))
    return tile, prev, nxt


def _pool_fwd(name, h3, w, scale):
    b_n, s_n, d = h3.shape
    tile, prev, _ = _pool_specs(b_n, s_n, d)

    def body(h_ref, halo_ref, w_ref, sc_ref, o_ref, xs, pooled):
        s = pl.program_id(1)
        xs[0:POOL_HALO, :] = jnp.where(s > 0, halo_ref[...], 0.0)
        xs[POOL_HALO:, :] = h_ref[...]
        _pooled_into(xs, pooled, s * POOL_ROWS)
        for gi in range(len(POOL_WINDOWS)):
            cols = slice(gi * POOL_GROUP_DIM, (gi + 1) * POOL_GROUP_DIM)
            y = jnp.dot(pooled[:, cols], w_ref[gi], preferred_element_type=F32)
            o_ref[:, cols] = y * sc_ref[:, cols]

    return pl.pallas_call(
        body, name=name, grid=(b_n, s_n // POOL_ROWS),
        in_specs=[tile, prev, pl.BlockSpec(w.shape, lambda b, s: (0, 0, 0)), pl.BlockSpec((1, d), lambda b, s: (0, 0))],
        out_specs=tile, out_shape=jax.ShapeDtypeStruct(h3.shape, F32),
        scratch_shapes=[pltpu.VMEM((POOL_HALO + POOL_ROWS, d), F32), pltpu.VMEM((POOL_ROWS, d), MXU)],
        compiler_params=_params(("parallel", "parallel")),
    )(h3, h3, w, scale)


def _pool_bwd(name, h3, dm3, w, scale):
    b_n, s_n, d = h3.shape
    tile, prev, nxt = _pool_specs(b_n, s_n, d)
    tiles = s_n // POOL_ROWS
    ext = POOL_ROWS + POOL_HALO

    def body(h_ref, halo_ref, dm_ref, dnext_ref, w_ref, sc_ref, dh_ref, dw_ref, dsc_ref, xs, pooled, ds, dp):
        b, s = pl.program_id(0), pl.program_id(1)

        @pl.when((b == 0) & (s == 0))
        def _():
            dw_ref[...] = jnp.zeros_like(dw_ref)
            dsc_ref[...] = jnp.zeros_like(dsc_ref)

        xs[0:POOL_HALO, :] = jnp.where(s > 0, halo_ref[...], 0.0)
        xs[POOL_HALO:, :] = h_ref[...]
        ds[0:POOL_ROWS, :] = dm_ref[...]
        ds[POOL_ROWS:, :] = jnp.where(s == tiles - 1, 0.0, dnext_ref[...])
        _pooled_into(xs, pooled, s * POOL_ROWS)
        for gi in range(len(POOL_WINDOWS)):
            cols = slice(gi * POOL_GROUP_DIM, (gi + 1) * POOL_GROUP_DIM)
            dyb = (ds[:, cols] * sc_ref[:, cols]).astype(MXU)
            dp[:, cols] = lax.dot_general(dyb, w_ref[gi], NT, preferred_element_type=F32)
            pg = pooled[:, cols]
            dw_ref[gi] += lax.dot_general(pg, dyb[:POOL_ROWS], TN, preferred_element_type=F32)
            ypre = jnp.dot(pg, w_ref[gi], preferred_element_type=F32)
            dsc_ref[:, cols] += jnp.sum(ds[0:POOL_ROWS, cols] * ypre, axis=0, keepdims=True)

        def chunk(st):
            for gi, w_len in enumerate(POOL_WINDOWS):
                cols = slice(gi * POOL_GROUP_DIM, (gi + 1) * POOL_GROUP_DIM)
                v = dp[pl.ds(st, POOL_CH + POOL_HALO), cols]
                q = v / _count(s * POOL_ROWS + st, POOL_CH + POOL_HALO, w_len)
                back = _pool_windows(q, None, gi, False)[:POOL_CH] - v[:POOL_CH]
                dh_ref[pl.ds(st, POOL_CH), cols] = ALPHA * ds[pl.ds(st, POOL_CH), cols] + back

        _chunks(POOL_ROWS, POOL_CH, chunk)

    return pl.pallas_call(
        body, name=name, grid=(b_n, tiles),
        in_specs=[tile, prev, tile, nxt, pl.BlockSpec(w.shape, lambda b, s: (0, 0, 0)), pl.BlockSpec((1, d), lambda b, s: (0, 0))],
        out_specs=[tile, pl.BlockSpec(w.shape, lambda b, s: (0, 0, 0)), pl.BlockSpec((1, d), lambda b, s: (0, 0))],
        out_shape=[jax.ShapeDtypeStruct(h3.shape, F32), jax.ShapeDtypeStruct(w.shape, F32), jax.ShapeDtypeStruct((1, d), F32)],
        scratch_shapes=[pltpu.VMEM((POOL_HALO + POOL_ROWS, d), F32), pltpu.VMEM((POOL_ROWS, d), MXU),
                        pltpu.VMEM((ext, d), F32), pltpu.VMEM((ext, d), F32)],
        compiler_params=_params(("arbitrary", "arbitrary")),
    )(h3, h3, dm3, dm3, w, scale)


ROPE_ROWS = 256


def _rope_tables(s_n):
    inv_freq = ROPE_THETA ** (-jnp.arange(0, HEAD_DIM, 2, dtype=F32) / HEAD_DIM)
    ang = jnp.arange(s_n, dtype=F32)[:, None] * inv_freq[None, :]
    cos, sin = jnp.cos(ang), jnp.sin(ang)
    cos_l = jnp.tile(cos, (1, 4))
    sin_l = jnp.tile(jnp.concatenate([-sin, sin], axis=1), (1, 2))
    return cos_l, sin_l


def _rope(name, x3, col0, n_col, cos_l, sin_l, scale, out_dtype, add3=None):
    b_n, s_n, _ = x3.shape
    cw = N_HEADS * HEAD_DIM
    n_in = 1 if add3 is None else 2

    def body(*refs):
        x_refs, (c_ref, s_ref, o_ref) = refs[:n_in], refs[n_in:]
        lane = lax.broadcasted_iota(jnp.int32, (ROPE_ROWS, 128), 1)
        first_half = (lane % HEAD_DIM) < (HEAD_DIM // 2)
        cos, sin = c_ref[...], s_ref[...]
        for cb in range(cw // 128):
            cols = slice(cb * 128, (cb + 1) * 128)
            x = x_refs[0][:, cols]
            if add3 is not None:
                x = x + x_refs[1][:, cols]
            other = jnp.where(first_half, pltpu.roll(x, 128 - HEAD_DIM // 2, 1), pltpu.roll(x, HEAD_DIM // 2, 1))
            y = x * cos + other * sin
            o_ref[:, cols] = (y if scale == 1.0 else y * scale).astype(o_ref.dtype)

    tile = pl.BlockSpec((None, ROPE_ROWS, cw), lambda b, s, c: (b, s, col0 + c))
    tab = pl.BlockSpec((ROPE_ROWS, 128), lambda b, s, c: (s, 0))
    return pl.pallas_call(
        body, name=name, grid=(b_n, s_n // ROPE_ROWS, n_col), in_specs=[tile] * n_in + [tab, tab],
        out_specs=pl.BlockSpec((None, ROPE_ROWS, cw), lambda b, s, c: (b, s, c)),
        out_shape=jax.ShapeDtypeStruct((b_n, s_n, n_col * cw), out_dtype),
        compiler_params=_params(("parallel", "parallel", "parallel")),
    )(*([x3] if add3 is None else [x3, add3]), cos_l, sin_l)


def _to_strided(a, d):
    if d == 1:
        return a
    b_n, s_n, c = a.shape
    return a.reshape(b_n, s_n // d, d, c).transpose(0, 2, 1, 3).reshape(b_n, s_n, c)


def _from_strided(a, d):
    if d == 1:
        return a
    b_n, s_n, c = a.shape
    return a.reshape(b_n, d, s_n // d, c).transpose(0, 2, 1, 3).reshape(b_n, s_n, c)


def _attn_fwd(name, q, q_col, k, v, blocks_per_seq):
    b_n, s_n, cw = k.shape
    nb = s_n // SPAN
    with_prev = blocks_per_seq > 1

    def body(*refs):
        if with_prev:
            q_ref, kc_ref, kp_ref, vc_ref, vp_ref, o_ref, lse_ref = refs
        else:
            q_ref, kc_ref, vc_ref, o_ref, lse_ref = refs
        n = pl.program_id(1)
        qi = lax.broadcasted_iota(jnp.int32, (SPAN, SPAN), 0)
        kj = lax.broadcasted_iota(jnp.int32, (SPAN, SPAN), 1)
        mask_c = kj <= qi
        if with_prev:
            mask_p = kj >= qi + jnp.where((n % blocks_per_seq) != 0, 0, 2 * SPAN)
        for h in range(N_HEADS):
            cols = slice(h * HEAD_DIM, (h + 1) * HEAD_DIM)
            qh = q_ref[:, cols]
            sc = jnp.where(mask_c, lax.dot_general(qh, kc_ref[:, cols], NT, preferred_element_type=F32), -jnp.inf)
            m = jnp.max(sc, -1, keepdims=True)
            if with_prev:
                sp = jnp.where(mask_p, lax.dot_general(qh, kp_ref[:, cols], NT, preferred_element_type=F32), -jnp.inf)
                m = jnp.maximum(m, jnp.max(sp, -1, keepdims=True))
            pc = jnp.exp(sc - m)
            l = jnp.sum(pc, -1, keepdims=True)
            if with_prev:
                pp = jnp.exp(sp - m)
                l = l + jnp.sum(pp, -1, keepdims=True)
            inv = 1.0 / l
            o = jnp.dot((pc * inv).astype(MXU), vc_ref[:, cols], preferred_element_type=F32)
            if with_prev:
                o = o + jnp.dot((pp * inv).astype(MXU), vp_ref[:, cols], preferred_element_type=F32)
            o_ref[:, cols] = o
            lse_ref[:, h:h + 1] = m + jnp.log(l)

    cur = lambda b, n: (b, n, 0)
    prv = lambda b, n: (b, jnp.maximum(n - 1, 0), 0)
    blk = (None, SPAN, cw)
    kv_specs = [pl.BlockSpec(blk, cur), pl.BlockSpec(blk, prv)] if with_prev else [pl.BlockSpec(blk, cur)]
    operands = [q, k, k, v, v] if with_prev else [q, k, v]
    return pl.pallas_call(
        body, name=name, grid=(b_n, nb),
        in_specs=[pl.BlockSpec(blk, lambda b, n: (b, n, q_col))] + kv_specs + kv_specs,
        out_specs=[pl.BlockSpec(blk, cur), pl.BlockSpec((None, SPAN, N_HEADS), cur)],
        out_shape=[jax.ShapeDtypeStruct((b_n, s_n, cw), F32), jax.ShapeDtypeStruct((b_n, s_n, N_HEADS), F32)],
        compiler_params=_params(("parallel", "parallel")),
    )(*operands)


def _attn_bwd(name, q, q_col, k, v, do, lse, dd, blocks_per_seq, dk_prev=None, dv_prev=None):
    b_n, s_n, cw = k.shape
    nb = s_n // SPAN
    with_next = blocks_per_seq > 1
    accumulate = dk_prev is not None
    rows = 2 * SPAN if with_next else SPAN

    def body(*refs):
        refs = list(refs)
        qc_ref, doc_ref, lsec_ref, ddc_ref = refs[:4]
        del refs[:4]
        if with_next:
            qn_ref, don_ref, lsen_ref, ddn_ref = refs[:4]
            del refs[:4]
        k_ref, v_ref = refs[:2]
        del refs[:2]
        if accumulate:
            dkp_ref, dvp_ref = refs[:2]
            del refs[:2]
        dq_ref, dk_ref, dv_ref = refs[:3]
        carry = refs[3] if with_next else None
        n = pl.program_id(1)
        qi = lax.broadcasted_iota(jnp.int32, (rows, SPAN), 0)
        kj = lax.broadcasted_iota(jnp.int32, (rows, SPAN), 1)
        if with_next:
            first = (n % blocks_per_seq) == 0
            reach = jnp.where(((n + 1) % blocks_per_seq) != 0, SPAN, -2 * SPAN)
            mask = ((qi < SPAN) & (kj <= qi)) | ((qi >= SPAN) & (kj >= qi - reach))
        else:
            mask = kj <= qi
        for h in range(N_HEADS):
            cols = slice(h * HEAD_DIM, (h + 1) * HEAD_DIM)
            hc = slice(h, h + 1)
            if with_next:
                qq = jnp.concatenate([qc_ref[:, cols], qn_ref[:, cols]], axis=0)
                dd_o = jnp.concatenate([doc_ref[:, cols], don_ref[:, cols]], axis=0)
                lse2 = jnp.concatenate([lsec_ref[:, hc], lsen_ref[:, hc]], axis=0)
                dd2 = jnp.concatenate([ddc_ref[:, hc], ddn_ref[:, hc]], axis=0)
            else:
                qq, dd_o, lse2, dd2 = qc_ref[:, cols], doc_ref[:, cols], lsec_ref[:, hc], ddc_ref[:, hc]
            kh, vh = k_ref[:, cols], v_ref[:, cols]
            s = lax.dot_general(qq, kh, NT, preferred_element_type=F32)
            p = jnp.where(mask, jnp.exp(s - lse2), 0.0)
            dv = lax.dot_general(p.astype(MXU), dd_o, TN, preferred_element_type=F32)
            dp = lax.dot_general(dd_o, vh, NT, preferred_element_type=F32)
            ds = (p * (dp - dd2)).astype(MXU)
            dk = lax.dot_general(ds, qq, TN, preferred_element_type=F32)
            dq2 = jnp.dot(ds, kh, preferred_element_type=F32)
            if accumulate:
                dk = dk + dkp_ref[:, cols]
                dv = dv + dvp_ref[:, cols]
            dk_ref[:, cols] = dk
            dv_ref[:, cols] = dv
            if with_next:
                dq_ref[:, cols] = dq2[:SPAN] + jnp.where(first, 0.0, carry[:, cols])
                carry[:, cols] = dq2[SPAN:]
            else:
                dq_ref[:, cols] = dq2

    cur = lambda b, n: (b, n, 0)
    nxt = lambda b, n: (b, jnp.minimum(n + 1, nb - 1), 0)
    blk, hblk = (None, SPAN, cw), (None, SPAN, N_HEADS)
    q_specs = lambda m: [pl.BlockSpec(blk, lambda b, n: (*m(b, n)[:2], q_col)), pl.BlockSpec(blk, m),
                         pl.BlockSpec(hblk, m), pl.BlockSpec(hblk, m)]
    in_specs = q_specs(cur) + (q_specs(nxt) if with_next else []) + [pl.BlockSpec(blk, cur)] * (4 if accumulate else 2)
    operands = [q, do, lse, dd] * (2 if with_next else 1) + [k, v] + ([dk_prev, dv_prev] if accumulate else [])
    out = jax.ShapeDtypeStruct((b_n, s_n, cw), F32)
    return pl.pallas_call(
        body, name=name, grid=(b_n, nb), in_specs=in_specs, out_specs=[pl.BlockSpec(blk, cur)] * 3,
        out_shape=[out, out, out],
        scratch_shapes=[pltpu.VMEM((SPAN, cw), F32)] if with_next else [],
        compiler_params=_params(("parallel", "arbitrary")),
    )(*operands)


MIX_ROWS = 256


def _group_weights(lse_refs):
    ls = [r[...] for r in lse_refs]
    m = functools.reduce(jnp.maximum, ls)
    es = [jnp.exp(l - m) for l in ls]
    tot = functools.reduce(lambda a, b: a + b, es)
    return [e / tot for e in es]


def _attn_mix(name, outs, lses):
    n, cw = outs[0].shape
    g_n = len(outs)

    def body(*refs):
        o_refs, l_refs, out_ref = refs[:g_n], refs[g_n:2 * g_n], refs[2 * g_n]
        ws = _group_weights(l_refs)
        for h in range(N_HEADS):
            cols = slice(h * HEAD_DIM, (h + 1) * HEAD_DIM)
            acc = None
            for g in range(g_n):
                t = ws[g][:, h:h + 1] * o_refs[g][:, cols]
                acc = t if acc is None else acc + t
            out_ref[:, cols] = acc.astype(out_ref.dtype)

    tile = pl.BlockSpec((MIX_ROWS, cw), lambda i: (i, 0))
    htile = pl.BlockSpec((MIX_ROWS, N_HEADS), lambda i: (i, 0))
    return pl.pallas_call(
        body, name=name, grid=(n // MIX_ROWS,), in_specs=[tile] * g_n + [htile] * g_n, out_specs=tile,
        out_shape=jax.ShapeDtypeStruct((n, cw), MXU), compiler_params=_params(("parallel",)),
    )(*outs, *lses)


def _attn_mix_bwd(name, do, outs, lses):
    n, cw = do.shape
    g_n = len(outs)

    def body(*refs):
        do_ref, o_refs, l_refs = refs[0], refs[1:1 + g_n], refs[1 + g_n:1 + 2 * g_n]
        dog_refs, dd_refs = refs[1 + 2 * g_n:1 + 3 * g_n], refs[1 + 3 * g_n:]
        ws = _group_weights(l_refs)
        for h in range(N_HEADS):
            cols = slice(h * HEAD_DIM, (h + 1) * HEAD_DIM)
            dh = do_ref[:, cols]
            o = None
            for g in range(g_n):
                t = ws[g][:, h:h + 1] * o_refs[g][:, cols]
                o = t if o is None else o + t
            dot = jnp.sum(dh * o, -1, keepdims=True)
            for g in range(g_n):
                wg = ws[g][:, h:h + 1]
                dog_refs[g][:, cols] = (wg * dh).astype(dog_refs[g].dtype)
                dd_refs[g][:, h:h + 1] = wg * dot

    tile = pl.BlockSpec((MIX_ROWS, cw), lambda i: (i, 0))
    htile = pl.BlockSpec((MIX_ROWS, N_HEADS), lambda i: (i, 0))
    return pl.pallas_call(
        body, name=name, grid=(n // MIX_ROWS,), in_specs=[tile] * (1 + g_n) + [htile] * g_n,
        out_specs=[tile] * g_n + [htile] * g_n,
        out_shape=[jax.ShapeDtypeStruct((n, cw), MXU)] * g_n + [jax.ShapeDtypeStruct((n, N_HEADS), F32)] * g_n,
        compiler_params=_params(("parallel",)),
    )(do, *outs, *lses)


def _loss_head(name, y, target):
    n, d = y.shape
    steps = n // LN_ROWS

    def body(y_ref, t_ref, dy_ref, l_ref, acc):
        i = pl.program_id(0)

        @pl.when(i == 0)
        def _():
            acc[...] = jnp.zeros_like(acc)

        def chunk(s):
            rows = pl.ds(s, LN_CH)
            err = y_ref[rows, :] - t_ref[rows, :]
            dy_ref[rows, :] = err / d
            acc[...] += _fold8(err * err)

        _chunks(LN_ROWS, LN_CH, chunk)

        @pl.when(i == steps - 1)
        def _():
            l_ref[...] = jnp.full((8, 128), 0.5 / d, F32) * jnp.sum(acc[...])

    tile = pl.BlockSpec((LN_ROWS, d), lambda i: (i, 0))
    return pl.pallas_call(
        body, name=name, grid=(steps,), in_specs=[tile, tile],
        out_specs=[tile, pl.BlockSpec((8, 128), lambda i: (0, 0))],
        out_shape=[jax.ShapeDtypeStruct((n, d), F32), jax.ShapeDtypeStruct((8, 128), F32)],
        scratch_shapes=[pltpu.VMEM((8, d), F32)], compiler_params=_params(("arbitrary",)),
    )(y, target)


EW_TILE_BYTES = 1 << 20


def _row_tile(rows, cols):
    tr = 8
    while rows % (2 * tr) == 0 and 2 * tr * cols * 4 <= EW_TILE_BYTES:
        tr *= 2
    return tr if rows % tr == 0 else rows


def _add_halves(name, grad, recv, half):
    j_n, _, r, c = grad.shape
    tr = _row_tile(r, c)

    def body(half_ref, g_ref, r_ref, o_ref):
        o_ref[...] = g_ref[...] + r_ref[...]

    return pl.pallas_call(
        body, name=name, out_shape=jax.ShapeDtypeStruct((j_n, r, c), F32),
        grid_spec=pltpu.PrefetchScalarGridSpec(
            num_scalar_prefetch=1, grid=(j_n, r // tr),
            in_specs=[pl.BlockSpec((None, None, tr, c), lambda j, i, hf: (j, hf[0], i, 0)),
                      pl.BlockSpec((None, tr, c), lambda j, i, hf: (j, i, 0))],
            out_specs=pl.BlockSpec((None, tr, c), lambda j, i, hf: (j, i, 0))),
        compiler_params=_params(("parallel", "parallel")),
    )(half, grad, recv)


def _add_chips(name, mine, recv, chip):
    j_n, r, c = mine.shape
    tr = _row_tile(r, c)

    def body(chip_ref, m_ref, r_ref, o_ref):
        total = m_ref[...]
        for k in range(j_n - 1):
            total = total + r_ref[k]
        o_ref[...] = total

    return pl.pallas_call(
        body, name=name, out_shape=jax.ShapeDtypeStruct((r, c), F32),
        grid_spec=pltpu.PrefetchScalarGridSpec(
            num_scalar_prefetch=1, grid=(r // tr,),
            in_specs=[pl.BlockSpec((None, tr, c), lambda i, ch: (ch[0], i, 0)),
                      pl.BlockSpec((j_n - 1, tr, c), lambda i, ch: (0, i, 0))],
            out_specs=pl.BlockSpec((tr, c), lambda i, ch: (i, 0))),
        compiler_params=_params(("parallel",)),
    )(chip, mine, recv)


def _adam_math(w, g, m, v):
    m = ADAM_B1 * m + (1.0 - ADAM_B1) * g
    v = ADAM_B2 * v + (1.0 - ADAM_B2) * (g * g)
    m_hat = m / (1.0 - ADAM_B1 ** ADAM_STEP)
    v_hat = v / (1.0 - ADAM_B2 ** ADAM_STEP)
    delta = -ADAM_LR * (m_hat / (jnp.sqrt(v_hat) + ADAM_EPS) + ADAM_WD * w)
    return delta, m, v


def _adam_halves(name, own, other, half, w, m, v):
    _, r, c = w.shape
    tr = _row_tile(r, c)

    def body(half_ref, own_ref, oth_ref, w_ref, m_ref, v_ref, g_out, d_out, m_out, v_out):
        g = jnp.where(pl.program_id(0) == half_ref[0], own_ref[...], oth_ref[...])
        delta, m_new, v_new = _adam_math(w_ref[...], g, m_ref[...], v_ref[...])
        g_out[...] = g
        d_out[...] = delta
        m_out[...] = m_new
        v_out[...] = v_new

    flat = pl.BlockSpec((tr, c), lambda h, i, hf: (i, 0))
    full = pl.BlockSpec((None, tr, c), lambda h, i, hf: (h, i, 0))
    out = jax.ShapeDtypeStruct(w.shape, F32)
    return pl.pallas_call(
        body, name=name, out_shape=[out] * 4,
        grid_spec=pltpu.PrefetchScalarGridSpec(num_scalar_prefetch=1, grid=(2, r // tr), in_specs=[flat, flat, full, full, full],
                                               out_specs=[full] * 4),
        compiler_params=_params(("parallel", "parallel")),
    )(half, own, other, w, m, v)


def _adam_small(name, g, w, m, v):
    def body(g_ref, w_ref, m_ref, v_ref, d_out, m_out, v_out):
        delta, m_new, v_new = _adam_math(w_ref[...], g_ref[...], m_ref[...], v_ref[...])
        d_out[...] = delta
        m_out[...] = m_new
        v_out[...] = v_new

    out = jax.ShapeDtypeStruct(w.shape, F32)
    return pl.pallas_call(body, name=name, out_shape=[out] * 3)(g, w, m, v)


def _place():
    x, y, c = lax.axis_index("x"), lax.axis_index("y"), lax.axis_index("c")
    chips = [(1 - x, y), (x, 1 - y), (1 - x, 1 - y)]
    return x, y, c, chips


ANY = pl.BlockSpec(memory_space=pl.ANY)


def _gather_shards(name, shards):
    n = len(shards)

    def body(*refs):
        ins, outs = refs[:n], refs[n:2 * n]
        send, recv, local = refs[2 * n:]
        x, y, c, chips = _place()
        me = 2 * x + y
        sibling = (x, y, 1 - c)

        def copy(t, k, src, dst, to):
            return pltpu.make_async_remote_copy(src_ref=src, dst_ref=dst, send_sem=send.at[t, k], recv_sem=recv.at[t, k],
                                                device_id=to, device_id_type=MESH)

        own = [pltpu.make_async_copy(ins[t], outs[t].at[me], local.at[t]) for t in range(n)]
        for cp in own:
            cp.start()
        sent = []
        for t in range(n):
            for k, (px, py) in enumerate(chips):
                sent.append(copy(t, k, ins[t].at[c], outs[t].at[me, c], (px, py, c)))
                sent[-1].start()
        for t in range(n):
            for k, (px, py) in enumerate(chips):
                slab = outs[t].at[2 * px + py, c]
                copy(t, k, ins[t].at[c], slab, (px, py, c)).wait_recv()
                sent.append(copy(t, 3 + k, slab, slab, sibling))
                sent[-1].start()
        for t in range(n):
            for k, (px, py) in enumerate(chips):
                slab = outs[t].at[2 * px + py, 1 - c]
                copy(t, 3 + k, slab, slab, sibling).wait_recv()
        for cp in sent:
            cp.wait_send()
        for cp in own:
            cp.wait()

    return pl.pallas_call(
        body, name=name, in_specs=[ANY] * n, out_specs=[ANY] * n,
        out_shape=[jax.ShapeDtypeStruct((N_CHIPS, *s.shape), s.dtype) for s in shards],
        scratch_shapes=[pltpu.SemaphoreType.DMA((n, 6)), pltpu.SemaphoreType.DMA((n, 6)), pltpu.SemaphoreType.DMA((n,))],
    )(*shards)


def _to_sibling(name, arrays, pick_other_half):
    n = len(arrays)

    def body(*refs):
        ins, outs = refs[:n], refs[n:2 * n]
        send, recv = refs[2 * n:]
        x, y, c, _ = _place()
        copies = []
        for t in range(n):
            src = ins[t].at[:, 1 - c] if pick_other_half else ins[t]
            copies.append(pltpu.make_async_remote_copy(src_ref=src, dst_ref=outs[t], send_sem=send.at[t], recv_sem=recv.at[t],
                                                       device_id=(x, y, 1 - c), device_id_type=MESH))
            copies[-1].start()
        for cp in copies:
            cp.wait()

    shapes = [(a.shape[0], *a.shape[2:]) if pick_other_half else a.shape for a in arrays]
    return pl.pallas_call(
        body, name=name, in_specs=[ANY] * n, out_specs=[ANY] * n,
        out_shape=[jax.ShapeDtypeStruct(s, a.dtype) for s, a in zip(shapes, arrays)],
        scratch_shapes=[pltpu.SemaphoreType.DMA((n,)), pltpu.SemaphoreType.DMA((n,))],
    )(*arrays)


def _to_owner_chips(name, arrays):
    n = len(arrays)

    def body(*refs):
        ins, outs = refs[:n], refs[n:2 * n]
        send, recv = refs[2 * n:]
        x, y, c, chips = _place()
        copies = []
        for t in range(n):
            for k, (px, py) in enumerate(chips):
                copies.append(pltpu.make_async_remote_copy(
                    src_ref=ins[t].at[2 * px + py], dst_ref=outs[t].at[k], send_sem=send.at[t, k], recv_sem=recv.at[t, k],
                    device_id=(px, py, c), device_id_type=MESH))
                copies[-1].start()
        for cp in copies:
            cp.wait()

    return pl.pallas_call(
        body, name=name, in_specs=[ANY] * n, out_specs=[ANY] * n,
        out_shape=[jax.ShapeDtypeStruct((N_CHIPS - 1, *a.shape[1:]), a.dtype) for a in arrays],
        scratch_shapes=[pltpu.SemaphoreType.DMA((n, 3)), pltpu.SemaphoreType.DMA((n, 3))],
    )(*arrays)


def _sum_all_devices(name, part):
    r, c = part.shape

    def body(p_ref, o_ref, buf, send, recv):
        x, y, cc, _ = _place()
        me = 4 * x + 2 * y + cc
        copies = []
        for mask in range(1, 8):
            fx, fy, fc = (mask >> 2) & 1, (mask >> 1) & 1, mask & 1
            to = (x ^ fx, y ^ fy, cc ^ fc)
            copies.append((mask, pltpu.make_async_remote_copy(
                src_ref=p_ref, dst_ref=buf.at[me], send_sem=send.at[mask - 1], recv_sem=recv.at[mask - 1],
                device_id=to, device_id_type=MESH)))
            copies[-1][1].start()
        buf[me] = p_ref[...]
        for mask, cp in copies:
            pltpu.make_async_remote_copy(src_ref=p_ref, dst_ref=buf.at[me ^ mask], send_sem=send.at[mask - 1],
                                         recv_sem=recv.at[mask - 1], device_id=(x, y, cc), device_id_type=MESH).wait_recv()
        for _, cp in copies:
            cp.wait_send()
        total = buf[0]
        for d in range(1, 8):
            total = total + buf[d]
        o_ref[...] = total

    vm = pl.BlockSpec(memory_space=pltpu.VMEM)
    return pl.pallas_call(
        body, name=name, in_specs=[vm], out_specs=vm, out_shape=jax.ShapeDtypeStruct((r, c), F32),
        scratch_shapes=[pltpu.VMEM((8, r, c), F32), pltpu.SemaphoreType.DMA((7,)), pltpu.SemaphoreType.DMA((7,))],
    )(part)


def kernel(x, pool_w, pool_scale, w_q, w_kv, w_o, ffn_w_gate, ffn_w_up, ffn_conv_w, ffn_conv_b, ffn_w_down, ln1_g, ln1_b, ln2_g, ln2_b, loss_target, m_pool_w, m_pool_scale, m_w_q, m_w_kv, m_w_o, m_ffn_w_gate, m_ffn_w_up, m_ffn_conv_w, m_ffn_conv_b, m_ffn_w_down, m_ln1_g, m_ln1_b, m_ln2_g, m_ln2_b, v_pool_w, v_pool_scale, v_w_q, v_w_kv, v_w_o, v_ffn_w_gate, v_ffn_w_up, v_ffn_conv_w, v_ffn_conv_b, v_ffn_w_down, v_ln1_g, v_ln1_b, v_ln2_g, v_ln2_b):
    b_n, s_n, d = x.shape
    n = b_n * s_n
    f = ffn_w_gate.shape[-1]
    qc = w_q.shape[-1]
    kvb = w_kv.shape[-1] // 2
    n_attn = w_q.shape[0]
    g_n = len(DILATIONS)
    cw = N_HEADS * HEAD_DIM
    xi, yi, ci = lax.axis_index("x"), lax.axis_index("y"), lax.axis_index("c")
    half = jnp.reshape(ci, (1,)).astype(jnp.int32)
    chip = jnp.reshape(2 * xi + yi, (1,)).astype(jnp.int32)

    sharded = {
        "pool_w": (pool_w, m_pool_w, v_pool_w, (2, 4 * 64, POOL_GROUP_DIM)),
        "pool_scale": (pool_scale, m_pool_scale, v_pool_scale, (2, 1, pool_scale.shape[-1])),
        "w_q": (w_q, m_w_q, v_w_q, (2, d, qc)),
        "w_kv": (w_kv, m_w_kv, v_w_kv, (2, d // 2, w_kv.shape[-1])),
        "w_o": (w_o, m_w_o, v_w_o, (2, w_o.shape[1], d)),
        "ffn_w_gate": (ffn_w_gate, m_ffn_w_gate, v_ffn_w_gate, (2, 2 * d, f)),
        "ffn_w_up": (ffn_w_up, m_ffn_w_up, v_ffn_w_up, (2, 2 * d, f)),
        "ffn_conv_w": (ffn_conv_w, m_ffn_conv_w, v_ffn_conv_w, (2, 6, f)),
        "ffn_w_down": (ffn_w_down, m_ffn_w_down, v_ffn_w_down, (2, 2 * f, d)),
    }
    mxu_weights = ("pool_w", "w_q", "w_kv", "w_o", "ffn_w_gate", "ffn_w_up", "ffn_w_down")
    names = list(sharded)
    gathered = _gather_shards("gather_weights", [
        sharded[k][0].reshape(sharded[k][3]).astype(MXU if k in mxu_weights else F32) for k in names])
    full = dict(zip(names, gathered))
    per_layer = lambda t, shape: [t.reshape(N_CHIPS, shape[0], *shape[1:])[:, i] for i in range(shape[0])]
    wq_l = per_layer(full["w_q"], (n_attn, d, qc))
    wkv_all = full["w_kv"].reshape(N_CHIPS, d, w_kv.shape[-1])
    wg_l = per_layer(full["ffn_w_gate"], (DEPTH, d, f))
    wu_l = per_layer(full["ffn_w_up"], (DEPTH, d, f))
    wd_l = per_layer(full["ffn_w_down"], (DEPTH, f, d))
    cw_l = per_layer(full["ffn_conv_w"], (DEPTH, 3, f))
    cb_l = [ffn_conv_b[i].reshape(N_CHIPS, 1, f) for i in range(DEPTH)]
    pw_nat = full["pool_w"].reshape(N_CHIPS, N_POOL_LAYERS, 4, 64, POOL_GROUP_DIM).transpose(1, 2, 0, 3, 4).reshape(
        N_POOL_LAYERS, 4, POOL_GROUP_DIM, POOL_GROUP_DIM)
    ps_nat = full["pool_scale"].reshape(N_CHIPS, N_POOL_LAYERS, -1).transpose(1, 0, 2).reshape(N_POOL_LAYERS, 1, d)
    wo_nat = full["w_o"].reshape(N_CHIPS, n_attn, -1, d).transpose(1, 0, 2, 3).reshape(n_attn, cw, d)
    cos_l, sin_l = _rope_tables(s_n)

    def vec(a, layer):
        return a[layer].reshape(1, d)

    h = x.reshape(n, d)
    hb = None
    saved = []
    k_str = v_str = None
    for layer in range(DEPTH):
        keep = {"h": h, "hb": hb}
        if layer < N_POOL_LAYERS:
            mix = _pool_fwd("pool_fwd", h.reshape(b_n, s_n, d), pw_nat[layer], ps_nat[layer]).reshape(n, d)
        else:
            a = layer - N_POOL_LAYERS
            q = _mm_cols("q_proj", hb, wq_l[a], (), F32)
            qr = _rope("q_rope", q.reshape(b_n, s_n, -1), 0, g_n, cos_l, sin_l, HEAD_DIM ** -0.5, MXU)
            q_str, o_str, lse_str, outs, lses = [], [], [], [], []
            for g, dil in enumerate(DILATIONS):
                if dil == 1:
                    qg, col = qr, g
                else:
                    qg, col = _to_strided(qr[:, :, g * cw:(g + 1) * cw], dil), 0
                o_g, lse_g = _attn_fwd(f"attn_fwd_d{dil}", qg, col, k_str[g], v_str[g], s_n // dil // SPAN)
                q_str.append((qg, col))
                lse_str.append(lse_g)
                outs.append(_from_strided(o_g, dil).reshape(n, cw))
                lses.append(_from_strided(lse_g, dil).reshape(n, N_HEADS))
            ob = _attn_mix("attn_mix", outs, lses)
            mix = _mm("o_proj", [(ob, (TM, cw), lambda m, _: (m, 0), wo_nat[a], (cw, d), lambda m, _: (0, 0))],
                      NN, (n // TM, 1), (n, d), F32, (TM, d), lambda m, _: (m, 0), 1)
            keep.update(q_str=q_str, lse_str=lse_str, outs=outs, lses=lses, ob=ob)
        r1, h1, h1b = _add_ln("ln_fwd", h, mix, vec(ln1_g, layer), vec(ln1_b, layer))
        gate = _mm_shard_out("gate_up_proj", h1b, wg_l[layer], (), F32)
        up = _mm_shard_out("gate_up_proj", h1b, wu_l[layer], (), F32)
        g4, u4 = gate.reshape(N_CHIPS, b_n, s_n, f), up.reshape(N_CHIPS, b_n, s_n, f)
        hmid = _glu_fwd("glu_fwd", g4, u4, cw_l[layer], cb_l[layer]).reshape(N_CHIPS, n, f)
        ffn = _mm_shard_in("down_proj", hmid, wd_l[layer], ())
        r2, h2, h2b = _add_ln("ln_fwd", h1, ffn, vec(ln2_g, layer), vec(ln2_b, layer))
        keep.update(r1=r1, h1b=h1b, g4=g4, u4=u4, hmid=hmid, r2=r2, h2b=h2b)
        saved.append(keep)
        if layer == N_POOL_LAYERS - 1:
            kv = _mm_cols("kv_proj", h2b, wkv_all, (), F32, cb=kvb).reshape(b_n, s_n, -1)
            k_nat = _rope("k_rope", kv, 0, g_n, cos_l, sin_l, 1.0, MXU)
            v_nat = kv[:, :, g_n * cw:].astype(MXU)
            k_str = [_to_strided(k_nat[:, :, g * cw:(g + 1) * cw], dil) for g, dil in enumerate(DILATIONS)]
            v_str = [_to_strided(v_nat[:, :, g * cw:(g + 1) * cw], dil) for g, dil in enumerate(DILATIONS)]
        h, hb = h2, h2b

    dy, loss_tile = _loss_head("loss_head", h, loss_target.reshape(n, d))

    d_wq, d_wo = [None] * n_attn, [None] * n_attn
    d_wg, d_wu, d_wd, d_cw = [None] * DEPTH, [None] * DEPTH, [None] * DEPTH, [None] * DEPTH
    d_pw, d_ps = [None] * N_POOL_LAYERS, [None] * N_POOL_LAYERS
    d_ln = {}
    dk_str = dv_str = None
    d_top, top_scale, top_rest = dy, 1.0, []
    for layer in reversed(range(DEPTH)):
        sv = saved[layer]
        dr2, dr2b, d_ln["ln2", layer] = _ln_bwd("ln_bwd", sv["r2"], vec(ln2_g, layer), d_top, top_scale, top_rest)
        dhmid = _mm_nt_shard_out("down_bwd", dr2b, wd_l[layer], ())
        d_wd[layer] = _mm_tn("down_dw", sv["hmid"], f, dr2b, d, N_CHIPS, (N_CHIPS, f, d), (None, f, d),
                             lambda j: (j, 0, 0), a_lead=lambda j: (j,))
        dg4, du4, d_cw[layer] = _glu_bwd("glu_bwd", dhmid.reshape(N_CHIPS, b_n, s_n, f), sv["g4"], sv["u4"],
                                         cw_l[layer], cb_l[layer])
        dg, du = dg4.reshape(N_CHIPS, n, f), du4.reshape(N_CHIPS, n, f)
        dh1 = _mm_nt_shard_in("gate_up_bwd", [(dg, wg_l[layer], ()), (du, wu_l[layer], ())])
        d_wg[layer] = _mm_tn("gate_up_dw", sv["h1b"], d, dg, f, N_CHIPS, (N_CHIPS, d, f), (None, d, f),
                             lambda j: (j, 0, 0), b_lead=lambda j: (j,))
        d_wu[layer] = _mm_tn("gate_up_dw", sv["h1b"], d, du, f, N_CHIPS, (N_CHIPS, d, f), (None, d, f),
                             lambda j: (j, 0, 0), b_lead=lambda j: (j,))
        dr1, dr1b, d_ln["ln1", layer] = _ln_bwd("ln_bwd", sv["r1"], vec(ln1_g, layer), dr2, ALPHA, [dh1])
        if layer < N_POOL_LAYERS:
            d_in, d_pw[layer], d_ps[layer] = _pool_bwd("pool_bwd", sv["h"].reshape(b_n, s_n, d),
                                                       dr1.reshape(b_n, s_n, d), pw_nat[layer], ps_nat[layer])
            d_top, top_scale, top_rest = d_in.reshape(n, d), 1.0, []
        else:
            a = layer - N_POOL_LAYERS
            do = _mm("o_bwd", [(dr1b, (TM, d), lambda m, _: (m, 0), wo_nat[a], (cw, d), lambda m, _: (0, 0))],
                     NT, (n // TM, 1), (n, cw), F32, (TM, cw), lambda m, _: (m, 0), 1)
            d_wo[a] = _mm_tn("o_dw", sv["ob"], cw // N_CHIPS, dr1b, d, N_CHIPS, (N_CHIPS, cw // N_CHIPS, d),
                             (None, cw // N_CHIPS, d), lambda j: (j, 0, 0))
            mixed = _attn_mix_bwd("attn_mix_bwd", do, sv["outs"], sv["lses"])
            dq_nat, dk_new, dv_new = [], [], []
            for g, dil in enumerate(DILATIONS):
                do_g = _to_strided(mixed[g].reshape(b_n, s_n, cw), dil)
                dd_g = _to_strided(mixed[g_n + g].reshape(b_n, s_n, N_HEADS), dil)
                qg, col = sv["q_str"][g]
                dq_g, dk_g, dv_g = _attn_bwd(f"attn_bwd_d{dil}", qg, col, k_str[g], v_str[g], do_g, sv["lse_str"][g], dd_g,
                                             s_n // dil // SPAN, *((dk_str[g], dv_str[g]) if dk_str else ()))
                dq_nat.append(_from_strided(dq_g, dil))
                dk_new.append(dk_g)
                dv_new.append(dv_g)
            dk_str, dv_str = dk_new, dv_new
            dq = _rope("q_rope_bwd", jnp.concatenate(dq_nat, axis=-1), 0, g_n, cos_l, -sin_l, HEAD_DIM ** -0.5, MXU)
            dq = dq.reshape(n, g_n * cw)
            d_attn = _mm_nt_cols_in("q_bwd", dq, wq_l[a], (), qc)
            d_wq[a] = _mm_tn("q_dw", sv["hb"], d, dq, qc, N_CHIPS, (N_CHIPS, d, qc), (None, d, qc), lambda j: (j, 0, 0))
            d_top, top_scale, top_rest = dr1, ALPHA, [d_attn]
            if a == 0:
                dk_nat = jnp.concatenate([_from_strided(t, dil) for t, dil in zip(dk_str, DILATIONS)], axis=-1)
                dv_nat = jnp.concatenate([_from_strided(t, dil) for t, dil in zip(dv_str, DILATIONS)], axis=-1)
                dk_pre = _rope("k_rope_bwd", dk_nat, 0, g_n, cos_l, -sin_l, 1.0, MXU).reshape(n, g_n * cw)
                dkv = jnp.concatenate([dk_pre, dv_nat.reshape(n, g_n * cw).astype(MXU)], axis=-1)
                h_kv = saved[N_POOL_LAYERS - 1]["h2b"]
                top_rest = top_rest + [_mm_nt_cols_in("kv_bwd", dkv, wkv_all, (), kvb)]
                d_wkv = _mm_tn("kv_dw", h_kv, d, dkv, kvb, 2 * N_CHIPS, (2 * N_CHIPS, d, kvb), (None, d, kvb), lambda q: (q, 0, 0))
    grad_x = d_top.reshape(b_n, s_n, d)

    d_wkv = d_wkv.reshape(N_CHIPS, 2, d, kvb).transpose(0, 2, 1, 3)
    d_pw_all = jnp.stack(d_pw).reshape(N_POOL_LAYERS, 4, N_CHIPS, 64, POOL_GROUP_DIM).transpose(2, 0, 1, 3, 4)
    d_ps_all = jnp.stack(d_ps).reshape(N_POOL_LAYERS, N_CHIPS, -1).transpose(1, 0, 2)
    d_cw_all = jnp.stack([t[:, :3] for t in d_cw], axis=1)
    by_layer = lambda parts: jnp.stack(parts, axis=1)
    local = {
        "pool_w": d_pw_all, "pool_scale": d_ps_all, "w_q": by_layer(d_wq), "w_kv": d_wkv, "w_o": by_layer(d_wo),
        "ffn_w_gate": by_layer(d_wg), "ffn_w_up": by_layer(d_wu), "ffn_conv_w": d_cw_all, "ffn_w_down": by_layer(d_wd),
    }
    grads4 = [local[k].reshape(N_CHIPS, *sharded[k][3]) for k in names]
    from_sibling = _to_sibling("grads_to_sibling", grads4, True)
    core_sums = [_add_halves(f"add_halves_{k}", g4_, r_, half) for k, g4_, r_ in zip(names, grads4, from_sibling)]
    from_chips = _to_owner_chips("grads_to_owner", core_sums)
    owned = [_add_chips(f"add_chips_{k}", s_, r_, chip) for k, s_, r_ in zip(names, core_sums, from_chips)]
    others = _to_sibling("halves_to_sibling", owned, False)

    out_grad, out_delta, out_m, out_v = {}, {}, {}, {}
    for k, own, oth in zip(names, owned, others):
        w_, m_, v_, shape = sharded[k]
        res = _adam_halves(f"adam_{k}", own, oth, half, w_.reshape(shape), m_.reshape(shape), v_.reshape(shape))
        out_grad[k], out_delta[k], out_m[k], out_v[k] = (t.reshape(w_.shape) for t in res)

    d_cb = jnp.stack([t[:, 3] for t in d_cw], axis=0).reshape(DEPTH * N_CHIPS * f // d, d)
    ln_rows = jnp.concatenate([jnp.stack([d_ln[which, layer][row] for layer in range(DEPTH)])
                               for which, row in (("ln1", 0), ("ln1", 1), ("ln2", 0), ("ln2", 1))])
    rows = jnp.concatenate([ln_rows, d_cb, jnp.broadcast_to(loss_tile[0:1, 0:1], (1, d))])
    pad = (-rows.shape[0]) % 8
    total = _sum_all_devices("sum_small", jnp.pad(rows, ((0, pad), (0, 0))))
    small = {"ln1_g": total[0:4], "ln1_b": total[4:8], "ln2_g": total[8:12], "ln2_b": total[12:16],
             "ffn_conv_b": total[16:16 + d_cb.shape[0]].reshape(ffn_conv_b.shape)}
    loss = total[16 + d_cb.shape[0], 0]
    small_in = {"ln1_g": (ln1_g, m_ln1_g, v_ln1_g), "ln1_b": (ln1_b, m_ln1_b, v_ln1_b), "ln2_g": (ln2_g, m_ln2_g, v_ln2_g),
                "ln2_b": (ln2_b, m_ln2_b, v_ln2_b), "ffn_conv_b": (ffn_conv_b, m_ffn_conv_b, v_ffn_conv_b)}
    for k, (w_, m_, v_) in small_in.items():
        out_grad[k] = small[k]
        out_delta[k], out_m[k], out_v[k] = _adam_small(f"adam_{k}", small[k], w_, m_, v_)

    order = ["pool_w", "pool_scale", "w_q", "w_kv", "w_o", "ffn_w_gate", "ffn_w_up", "ffn_conv_w", "ffn_conv_b",
             "ffn_w_down", "ln1_g", "ln1_b", "ln2_g", "ln2_b"]
    return (loss, grad_x, *[out_grad[k] for k in order], *[out_delta[k] for k in order],
            *[out_m[k] for k in order], *[out_v[k] for k in order])
```

```python
import functools
import math

import jax
import jax.numpy as jnp
from jax import lax
from jax.experimental import pallas as pl
from jax.experimental.pallas import tpu as pltpu

F32 = jnp.float32
BF16 = jnp.bfloat16
MXU = jnp.bfloat16

DEPTH = 4
N_POOL_LAYERS = 2
POOL_WINDOWS = (2, 4, 8, 16)
POOL_GROUP_DIM = 256
HEAD_DIM = 64
N_HEADS = 16
DILATIONS = (1, 4, 16)
SPAN = 128
ROPE_THETA = 10000.0
ALPHA = (2.0 * DEPTH) ** 0.25
LN_EPS = 1e-5
ADAM_LR, ADAM_B1, ADAM_B2, ADAM_EPS, ADAM_WD, ADAM_STEP = 0.001, 0.9, 0.999, 1e-08, 0.01, 10

N_CHIPS = 4
VMEM_LIMIT = 56 * 1024 * 1024
MESH = pl.DeviceIdType.MESH

NN = (((1,), (0,)), ((), ()))
NT = (((1,), (1,)), ((), ()))
TN = (((0,), (0,)), ((), ()))


def _params(sem=None):
    return pltpu.CompilerParams(dimension_semantics=sem, vmem_limit_bytes=VMEM_LIMIT)


def _chunks(n_rows, ch, fn):
    def step(i, carry):
        fn(pl.multiple_of(i * ch, ch))
        return carry

    lax.fori_loop(0, n_rows // ch, step, 0)


def _fold8(v):
    return jnp.sum(v.reshape(v.shape[0] // 8, 8, v.shape[1]), axis=0)


def _down(v, k):
    return pltpu.roll(v, k, 0)


def _up(v, k):
    return pltpu.roll(v, v.shape[0] - k, 0)


def _mm(name, pairs, dims, grid, out_shape, out_dtype, out_block, out_map, nk, into=None):
    n_pairs = len(pairs)
    kax = len(grid) - 1

    def body(*refs):
        o_ref = refs[2 * n_pairs + (1 if into is not None else 0)]
        part = None
        for p in range(n_pairs):
            t = lax.dot_general(refs[2 * p][...], refs[2 * p + 1][...], dims, preferred_element_type=F32)
            part = t if part is None else part + t
        if nk == 1:
            o_ref[...] = part.astype(o_ref.dtype)
        else:
            k = pl.program_id(kax)

            @pl.when(k == 0)
            def _():
                o_ref[...] = part

            @pl.when(k > 0)
            def _():
                o_ref[...] += part

    operands, in_specs = [], []
    for a, a_block, a_map, b, b_block, b_map in pairs:
        operands += [a, b]
        in_specs += [pl.BlockSpec(a_block, a_map), pl.BlockSpec(b_block, b_map)]
    aliases = {}
    if into is not None:
        operands.append(into)
        in_specs.append(pl.BlockSpec(memory_space=pl.ANY))
        aliases = {2 * n_pairs: 0}
    assert nk == 1 or out_dtype == F32
    sem = ("parallel",) * kax + ("arbitrary",)
    return pl.pallas_call(
        body, name=name, grid=grid, in_specs=in_specs, out_specs=pl.BlockSpec(out_block, out_map),
        out_shape=jax.ShapeDtypeStruct(out_shape, out_dtype), input_output_aliases=aliases,
        compiler_params=_params(sem),
    )(*operands)


TM = 1024


def _mm_cols(name, a, w, w_idx, out_dtype, cb=None):
    n, k = a.shape
    j_n, c = w.shape[0], w.shape[-1]
    cb = c if cb is None else cb
    s = c // cb
    wb = (None,) * (w.ndim - 2) + (k, cb)
    return _mm(name, [(a, (TM, k), lambda q, m, _: (m, 0), w, wb, lambda q, m, _: (q // s, *w_idx, 0, q % s))], NN,
               (j_n * s, n // TM, 1), (n, j_n * c), out_dtype, (TM, cb), lambda q, m, _: (m, q), 1)


def _mm_shard_out(name, a, w, w_idx, out_dtype):
    n, k = a.shape
    j_n, c = w.shape[0], w.shape[-1]
    wb = (None,) * (w.ndim - 2) + (k, c)
    return _mm(name, [(a, (TM, k), lambda j, m, _: (m, 0), w, wb, lambda j, m, _: (j, *w_idx, 0, 0))], NN,
               (j_n, n // TM, 1), (j_n, n, c), out_dtype, (None, TM, c), lambda j, m, _: (j, m, 0), 1)


def _mm_shard_in(name, a4, w, w_idx):
    j_n, n, c = a4.shape
    k = w.shape[-1]
    wb = (None,) * (w.ndim - 2) + (c, k)
    return _mm(name, [(a4, (None, TM, c), lambda m, j: (j, m, 0), w, wb, lambda m, j: (j, *w_idx, 0, 0))], NN,
               (n // TM, j_n), (n, k), F32, (TM, k), lambda m, j: (m, 0), j_n)


def _mm_nt_shard_out(name, a, w, w_idx):
    n, k = a.shape
    j_n, c = w.shape[0], w.shape[-2]
    wb = (None,) * (w.ndim - 2) + (c, k)
    return _mm(name, [(a, (TM, k), lambda j, m, _: (m, 0), w, wb, lambda j, m, _: (j, *w_idx, 0, 0))], NT,
               (j_n, n // TM, 1), (j_n, n, c), F32, (None, TM, c), lambda j, m, _: (j, m, 0), 1)


def _mm_nt_shard_in(name, terms):
    pairs = []
    for a4, w, w_idx in terms:
        j_n, n, c = a4.shape
        k = w.shape[-2]
        wb = (None,) * (w.ndim - 2) + (k, c)
        pairs.append((a4, (None, TM, c), lambda m, j: (j, m, 0), w, wb,
                      functools.partial(lambda m, j, w_idx: (j, *w_idx, 0, 0), w_idx=w_idx)))
    return _mm(name, pairs, NT, (n // TM, j_n), (n, k), F32, (TM, k), lambda m, j: (m, 0), j_n)


def _mm_nt_cols_in(name, a, w, w_idx, cb):
    n, ct = a.shape
    j_n, k, c = w.shape[0], w.shape[-2], w.shape[-1]
    s = c // cb
    wb = (None,) * (w.ndim - 2) + (k, cb)
    return _mm(name, [(a, (TM, cb), lambda m, q: (m, q), w, wb, lambda m, q: (q // s, *w_idx, 0, q % s))], NT,
               (n // TM, ct // cb), (n, k), F32, (TM, k), lambda m, q: (m, 0), ct // cb)


def _mm_tn(name, a, a_cols, b, b_cols, n_blocks, out_shape, out_block, out_map, into=None, a_lead=None, b_lead=None):
    n = a.shape[-2]
    a_nb = a.shape[-1] // a_cols
    b_nb = b.shape[-1] // b_cols
    if a_lead is None:
        a_block, a_map = (TM, a_cols), lambda q, t: (t, q if a_nb > 1 else 0)
    else:
        a_block, a_map = (None, TM, a_cols), lambda q, t: (*a_lead(q), t, 0)
    if b_lead is None:
        b_block, b_map = (TM, b_cols), lambda q, t: (t, q if b_nb > 1 else 0)
    else:
        b_block, b_map = (None, TM, b_cols), lambda q, t: (*b_lead(q), t, 0)
    return _mm(name, [(a, a_block, a_map, b, b_block, b_map)], TN, (n_blocks, n // TM), out_shape, F32,
               out_block, lambda q, t: out_map(q), n // TM, into=into)


LN_ROWS = 512
LN_CH = 32


def _ln_stats(r):
    mu = jnp.mean(r, -1, keepdims=True)
    xc = r - mu
    var = jnp.mean(xc * xc, -1, keepdims=True)
    return xc, lax.rsqrt(var + LN_EPS)


def _add_ln(name, a, mix, g, b):
    n, d = a.shape

    def body(a_ref, m_ref, g_ref, b_ref, r_ref, h_ref, hb_ref):
        gg, bb = g_ref[...], b_ref[...]

        def chunk(s):
            rows = pl.ds(s, LN_CH)
            r = ALPHA * a_ref[rows, :] + m_ref[rows, :]
            xc, rstd = _ln_stats(r)
            y = xc * rstd * gg + bb
            r_ref[rows, :] = r
            h_ref[rows, :] = y
            hb_ref[rows, :] = y.astype(MXU)

        _chunks(LN_ROWS, LN_CH, chunk)

    tile = pl.BlockSpec((LN_ROWS, d), lambda i: (i, 0))
    vec = pl.BlockSpec((1, d), lambda i: (0, 0))
    return pl.pallas_call(
        body, name=name, grid=(n // LN_ROWS,), in_specs=[tile, tile, vec, vec], out_specs=[tile, tile, tile],
        out_shape=[jax.ShapeDtypeStruct((n, d), F32), jax.ShapeDtypeStruct((n, d), F32), jax.ShapeDtypeStruct((n, d), MXU)],
        compiler_params=_params(("parallel",)),
    )(a, mix, g, b)


def _ln_bwd(name, r, g, d_a, scale_a, d_rest):
    n, d = r.shape
    n_rest = len(d_rest)
    steps = n // LN_ROWS

    def body(*refs):
        r_ref, g_ref, da_ref = refs[:3]
        rest = refs[3:3 + n_rest]
        dr_ref, drb_ref, gb_ref, acc = refs[3 + n_rest:]
        i = pl.program_id(0)

        @pl.when(i == 0)
        def _():
            acc[...] = jnp.zeros_like(acc)

        gg = g_ref[...]

        def chunk(s):
            rows = pl.ds(s, LN_CH)
            xc, rstd = _ln_stats(r_ref[rows, :])
            xhat = xc * rstd
            dy = da_ref[rows, :] if scale_a == 1.0 else scale_a * da_ref[rows, :]
            for t in rest:
                dy = dy + t[rows, :]
            dyg = dy * gg
            m1 = jnp.mean(dyg, -1, keepdims=True)
            m2 = jnp.mean(dyg * xhat, -1, keepdims=True)
            dr = rstd * (dyg - m1 - xhat * m2)
            dr_ref[rows, :] = dr
            drb_ref[rows, :] = dr.astype(MXU)
            acc[0] += _fold8(dy * xhat)
            acc[1] += _fold8(dy)

        _chunks(LN_ROWS, LN_CH, chunk)

        @pl.when(i == steps - 1)
        def _():
            gb_ref[0:1, :] = jnp.sum(acc[0], axis=0, keepdims=True)
            gb_ref[1:2, :] = jnp.sum(acc[1], axis=0, keepdims=True)

    tile = pl.BlockSpec((LN_ROWS, d), lambda i: (i, 0))
    vec = pl.BlockSpec((1, d), lambda i: (0, 0))
    return pl.pallas_call(
        body, name=name, grid=(steps,), in_specs=[tile, vec, tile] + [tile] * n_rest,
        out_specs=[tile, tile, pl.BlockSpec((2, d), lambda i: (0, 0))],
        out_shape=[jax.ShapeDtypeStruct((n, d), F32), jax.ShapeDtypeStruct((n, d), MXU), jax.ShapeDtypeStruct((2, d), F32)],
        scratch_shapes=[pltpu.VMEM((2, 8, d), F32)],
        compiler_params=_params(("arbitrary",)),
    )(r, g, d_a, *d_rest)


FFN_ROWS = 512
FFN_CH = 32
GELU_C1 = math.sqrt(2.0 / math.pi)
GELU_C2 = 0.044715


def _conv3(v, prev8, w0, w1, w2, bias):
    n = v.shape[0]
    ext = jnp.concatenate([prev8, v], axis=0)
    g1 = _down(ext, 1)[8:8 + n]
    g2 = _down(ext, 2)[8:8 + n]
    return bias + w0 * g2 + w1 * g1 + w2 * v, g1, g2


def _glu_fwd(name, g4, u4, conv_w, conv_b):
    j_n, b_n, s_n, f = g4.shape
    tiles = s_n // FFN_ROWS

    def body(g_ref, halo_ref, u_ref, w_ref, b_ref, o_ref, gs):
        s = pl.program_id(2)
        gs[0:8, :] = jnp.where(s > 0, halo_ref[...], 0.0)
        gs[8:, :] = g_ref[...]
        w0, w1, w2, bias = w_ref[0:1, :], w_ref[1:2, :], w_ref[2:3, :], b_ref[...]

        def chunk(st):
            v = gs[pl.ds(pl.multiple_of(st + 8, 8), FFN_CH), :]
            conv, _, _ = _conv3(v, gs[pl.ds(st, 8), :], w0, w1, w2, bias)
            cdf = 0.5 * (1.0 + jnp.tanh(GELU_C1 * (conv + GELU_C2 * (conv * conv * conv))))
            o_ref[pl.ds(st, FFN_CH), :] = (conv * cdf * u_ref[pl.ds(st, FFN_CH), :]).astype(o_ref.dtype)

        _chunks(FFN_ROWS, FFN_CH, chunk)

    tile = pl.BlockSpec((None, None, FFN_ROWS, f), lambda j, b, s: (j, b, s, 0))
    halo = pl.BlockSpec((None, None, 8, f), lambda j, b, s: (j, b, jnp.maximum(s * (FFN_ROWS // 8) - 1, 0), 0))
    return pl.pallas_call(
        body, name=name, grid=(j_n, b_n, tiles),
        in_specs=[tile, halo, tile,
                  pl.BlockSpec((None, 3, f), lambda j, b, s: (j, 0, 0)),
                  pl.BlockSpec((None, 1, f), lambda j, b, s: (j, 0, 0))],
        out_specs=tile, out_shape=jax.ShapeDtypeStruct(g4.shape, MXU),
        scratch_shapes=[pltpu.VMEM((8 + FFN_ROWS, f), F32)],
        compiler_params=_params(("parallel", "parallel", "parallel")),
    )(g4, g4, u4, conv_w, conv_b)


def _glu_bwd(name, dh4, g4, u4, conv_w, conv_b):
    j_n, b_n, s_n, f = g4.shape
    tiles = s_n // FFN_ROWS
    ext = FFN_CH + 8

    def body(d_ref, dnext_ref, g_ref, gprev_ref, gnext_ref, u_ref, unext_ref, w_ref, b_ref,
             dg_ref, du_ref, wb_ref, gs, us, ds, acc):
        b, s = pl.program_id(1), pl.program_id(2)
        last = s == tiles - 1

        @pl.when((b == 0) & (s == 0))
        def _():
            acc[...] = jnp.zeros_like(acc)

        gs[0:8, :] = jnp.where(s > 0, gprev_ref[...], 0.0)
        gs[8:8 + FFN_ROWS, :] = g_ref[...]
        gs[8 + FFN_ROWS:, :] = gnext_ref[...]
        us[0:FFN_ROWS, :] = u_ref[...]
        us[FFN_ROWS:, :] = unext_ref[...]
        ds[0:FFN_ROWS, :] = d_ref[...]
        ds[FFN_ROWS:, :] = jnp.where(last, 0.0, dnext_ref[...])
        w0, w1, w2, bias = w_ref[0:1, :], w_ref[1:2, :], w_ref[2:3, :], b_ref[...]

        def chunk(st):
            v = gs[pl.ds(pl.multiple_of(st + 8, 8), ext), :]
            conv, g1, g2 = _conv3(v, gs[pl.ds(st, 8), :], w0, w1, w2, bias)
            th = jnp.tanh(GELU_C1 * (conv + GELU_C2 * (conv * conv * conv)))
            cdf = 0.5 * (1.0 + th)
            dact = cdf + conv * (0.5 * GELU_C1) * (1.0 - th * th) * (1.0 + (3.0 * GELU_C2) * (conv * conv))
            de = ds[pl.ds(st, ext), :]
            dconv = de * us[pl.ds(st, ext), :] * dact
            du_ref[pl.ds(st, FFN_CH), :] = (de * (conv * cdf))[:FFN_CH].astype(du_ref.dtype)
            dg = w2 * dconv + w1 * _up(dconv, 1) + w0 * _up(dconv, 2)
            dg_ref[pl.ds(st, FFN_CH), :] = dg[:FFN_CH].astype(dg_ref.dtype)
            dc = dconv[:FFN_CH]
            acc[0] += _fold8(dc * g2[:FFN_CH])
            acc[1] += _fold8(dc * g1[:FFN_CH])
            acc[2] += _fold8(dc * v[:FFN_CH])
            acc[3] += _fold8(dc)

        _chunks(FFN_ROWS, FFN_CH, chunk)

        @pl.when((b == b_n - 1) & last)
        def _():
            for k in range(4):
                wb_ref[k:k + 1, :] = jnp.sum(acc[k], axis=0, keepdims=True)

    blocks8 = FFN_ROWS // 8
    tile = pl.BlockSpec((None, None, FFN_ROWS, f), lambda j, b, s: (j, b, s, 0))
    prev = pl.BlockSpec((None, None, 8, f), lambda j, b, s: (j, b, jnp.maximum(s * blocks8 - 1, 0), 0))
    nxt = pl.BlockSpec((None, None, 8, f), lambda j, b, s: (j, b, jnp.minimum((s + 1) * blocks8, s_n // 8 - 1), 0))
    return pl.pallas_call(
        body, name=name, grid=(j_n, b_n, tiles),
        in_specs=[tile, nxt, tile, prev, nxt, tile, nxt,
                  pl.BlockSpec((None, 3, f), lambda j, b, s: (j, 0, 0)),
                  pl.BlockSpec((None, 1, f), lambda j, b, s: (j, 0, 0))],
        out_specs=[tile, tile, pl.BlockSpec((None, 4, f), lambda j, b, s: (j, 0, 0))],
        out_shape=[jax.ShapeDtypeStruct(g4.shape, MXU), jax.ShapeDtypeStruct(g4.shape, MXU),
                   jax.ShapeDtypeStruct((j_n, 4, f), F32)],
        scratch_shapes=[pltpu.VMEM((16 + FFN_ROWS, f), F32), pltpu.VMEM((8 + FFN_ROWS, f), F32),
                        pltpu.VMEM((8 + FFN_ROWS, f), F32), pltpu.VMEM((4, 8, f), F32)],
        compiler_params=_params(("parallel", "arbitrary", "arbitrary")),
    )(dh4, dh4, g4, g4, g4, u4, u4, conv_w, conv_b)


POOL_ROWS = 512
POOL_CH = 32
POOL_HALO = 16


def _pool_windows(v, t0, gi, causal):
    shift = _down if causal else _up
    acc, k = v, 1
    while k < POOL_WINDOWS[gi]:
        acc = acc + shift(acc, k)
        k *= 2
    return acc


def _count(t0, n, w):
    t = t0 + lax.broadcasted_iota(jnp.int32, (n, 1), 0)
    return jnp.minimum(t + 1, w).astype(F32)


def _pooled_into(xs, pooled, t_tile):
    def chunk(st):
        for gi, w in enumerate(POOL_WINDOWS):
            cols = slice(gi * POOL_GROUP_DIM, (gi + 1) * POOL_GROUP_DIM)
            v = xs[pl.ds(st, POOL_CH + POOL_HALO), cols]
            sums = _pool_windows(v, None, gi, True)[POOL_HALO:]
            val = sums / _count(t_tile + st, POOL_CH, w) - v[POOL_HALO:]
            pooled[pl.ds(st, POOL_CH), cols] = val.astype(pooled.dtype)

    _chunks(POOL_ROWS, POOL_CH, chunk)


def _pool_specs(b_n, s_n, d):
    per = POOL_ROWS // POOL_HALO
    tile = pl.BlockSpec((None, POOL_ROWS, d), lambda b, s: (b, s, 0))
    prev = pl.BlockSpec((None, POOL_HALO, d), lambda b, s: (b, jnp.maximum(s * per - 1, 0), 0))
    nxt = pl.BlockSpec((None, POOL_HALO, d), lambda b, s: (b, jnp.minimum((s + 1) * per, s_n // POOL_HALO - 1), 0))
    return tile, prev, nxt


def _pool_fwd(name, h3, w, scale):
    b_n, s_n, d = h3.shape
    tile, prev, _ = _pool_specs(b_n, s_n, d)

    def body(h_ref, halo_ref, w_ref, sc_ref, o_ref, xs, pooled):
        s = pl.program_id(1)
        xs[0:POOL_HALO, :] = jnp.where(s > 0, halo_ref[...], 0.0)
        xs[POOL_HALO:, :] = h_ref[...]
        _pooled_into(xs, pooled, s * POOL_ROWS)
        for gi in range(len(POOL_WINDOWS)):
            cols = slice(gi * POOL_GROUP_DIM, (gi + 1) * POOL_GROUP_DIM)
            y = jnp.dot(pooled[:, cols], w_ref[gi], preferred_element_type=F32)
            o_ref[:, cols] = y * sc_ref[:, cols]

    return pl.pallas_call(
        body, name=name, grid=(b_n, s_n // POOL_ROWS),
        in_specs=[tile, prev, pl.BlockSpec(w.shape, lambda b, s: (0, 0, 0)), pl.BlockSpec((1, d), lambda b, s: (0, 0))],
        out_specs=tile, out_shape=jax.ShapeDtypeStruct(h3.shape, F32),
        scratch_shapes=[pltpu.VMEM((POOL_HALO + POOL_ROWS, d), F32), pltpu.VMEM((POOL_ROWS, d), MXU)],
        compiler_params=_params(("parallel", "parallel")),
    )(h3, h3, w, scale)


def _pool_bwd(name, h3, dm3, w, scale):
    b_n, s_n, d = h3.shape
    tile, prev, nxt = _pool_specs(b_n, s_n, d)
    tiles = s_n // POOL_ROWS
    ext = POOL_ROWS + POOL_HALO

    def body(h_ref, halo_ref, dm_ref, dnext_ref, w_ref, sc_ref, dh_ref, dw_ref, dsc_ref, xs, pooled, ds, dp):
        b, s = pl.program_id(0), pl.program_id(1)

        @pl.when((b == 0) & (s == 0))
        def _():
            dw_ref[...] = jnp.zeros_like(dw_ref)
            dsc_ref[...] = jnp.zeros_like(dsc_ref)

        xs[0:POOL_HALO, :] = jnp.where(s > 0, halo_ref[...], 0.0)
        xs[POOL_HALO:, :] = h_ref[...]
        ds[0:POOL_ROWS, :] = dm_ref[...]
        ds[POOL_ROWS:, :] = jnp.where(s == tiles - 1, 0.0, dnext_ref[...])
        _pooled_into(xs, pooled, s * POOL_ROWS)
        for gi in range(len(POOL_WINDOWS)):
            cols = slice(gi * POOL_GROUP_DIM, (gi + 1) * POOL_GROUP_DIM)
            dyb = (ds[:, cols] * sc_ref[:, cols]).astype(MXU)
            dp[:, cols] = lax.dot_general(dyb, w_ref[gi], NT, preferred_element_type=F32)
            pg = pooled[:, cols]
            dw_ref[gi] += lax.dot_general(pg, dyb[:POOL_ROWS], TN, preferred_element_type=F32)
            ypre = jnp.dot(pg, w_ref[gi], preferred_element_type=F32)
            dsc_ref[:, cols] += jnp.sum(ds[0:POOL_ROWS, cols] * ypre, axis=0, keepdims=True)

        def chunk(st):
            for gi, w_len in enumerate(POOL_WINDOWS):
                cols = slice(gi * POOL_GROUP_DIM, (gi + 1) * POOL_GROUP_DIM)
                v = dp[pl.ds(st, POOL_CH + POOL_HALO), cols]
                q = v / _count(s * POOL_ROWS + st, POOL_CH + POOL_HALO, w_len)
                back = _pool_windows(q, None, gi, False)[:POOL_CH] - v[:POOL_CH]
                dh_ref[pl.ds(st, POOL_CH), cols] = ALPHA * ds[pl.ds(st, POOL_CH), cols] + back

        _chunks(POOL_ROWS, POOL_CH, chunk)

    return pl.pallas_call(
        body, name=name, grid=(b_n, tiles),
        in_specs=[tile, prev, tile, nxt, pl.BlockSpec(w.shape, lambda b, s: (0, 0, 0)), pl.BlockSpec((1, d), lambda b, s: (0, 0))],
        out_specs=[tile, pl.BlockSpec(w.shape, lambda b, s: (0, 0, 0)), pl.BlockSpec((1, d), lambda b, s: (0, 0))],
        out_shape=[jax.ShapeDtypeStruct(h3.shape, F32), jax.ShapeDtypeStruct(w.shape, F32), jax.ShapeDtypeStruct((1, d), F32)],
        scratch_shapes=[pltpu.VMEM((POOL_HALO + POOL_ROWS, d), F32), pltpu.VMEM((POOL_ROWS, d), MXU),
                        pltpu.VMEM((ext, d), F32), pltpu.VMEM((ext, d), F32)],
        compiler_params=_params(("arbitrary", "arbitrary")),
    )(h3, h3, dm3, dm3, w, scale)


ROPE_ROWS = 256


def _rope_tables(s_n):
    inv_freq = ROPE_THETA ** (-jnp.arange(0, HEAD_DIM, 2, dtype=F32) / HEAD_DIM)
    ang = jnp.arange(s_n, dtype=F32)[:, None] * inv_freq[None, :]
    cos, sin = jnp.cos(ang), jnp.sin(ang)
    cos_l = jnp.tile(cos, (1, 4))
    sin_l = jnp.tile(jnp.concatenate([-sin, sin], axis=1), (1, 2))
    return cos_l, sin_l


def _rope(name, x3, col0, n_col, cos_l, sin_l, scale, out_dtype, rotate=True, out_cols=None, out_col0=0, into=None):
    b_n, s_n, _ = x3.shape
    cw = N_HEADS * HEAD_DIM
    out_cols = n_col if out_cols is None else out_cols

    def body(x_ref, c_ref, s_ref, *rest):
        o_ref = rest[-1]
        lane = lax.broadcasted_iota(jnp.int32, (ROPE_ROWS, 128), 1)
        first_half = (lane % HEAD_DIM) < (HEAD_DIM // 2)
        cos, sin = c_ref[...], s_ref[...]
        for cb in range(cw // 128):
            cols = slice(cb * 128, (cb + 1) * 128)
            y = x_ref[:, cols]
            if rotate:
                other = jnp.where(first_half, pltpu.roll(y, 128 - HEAD_DIM // 2, 1), pltpu.roll(y, HEAD_DIM // 2, 1))
                y = y * cos + other * sin
            o_ref[:, cols] = (y if scale == 1.0 else y * scale).astype(o_ref.dtype)

    tile = pl.BlockSpec((None, ROPE_ROWS, cw), lambda b, s, c: (b, s, col0 + c))
    tab = pl.BlockSpec((ROPE_ROWS, 128), lambda b, s, c: (s, 0))
    extra, extra_specs, aliases = [], [], {}
    if into is not None:
        extra, extra_specs, aliases = [into], [pl.BlockSpec(memory_space=pl.ANY)], {3: 0}
    return pl.pallas_call(
        body, name=name, grid=(b_n, s_n // ROPE_ROWS, n_col), in_specs=[tile, tab, tab] + extra_specs,
        out_specs=pl.BlockSpec((None, ROPE_ROWS, cw), lambda b, s, c: (b, s, out_col0 + c)),
        out_shape=jax.ShapeDtypeStruct((b_n, s_n, out_cols * cw), out_dtype), input_output_aliases=aliases,
        compiler_params=_params(("parallel", "parallel", "parallel")),
    )(x3, cos_l, sin_l, *extra)


def _to_strided(a, d):
    if d == 1:
        return a
    b_n, s_n, c = a.shape
    return a.reshape(b_n, s_n // d, d, c).transpose(0, 2, 1, 3).reshape(b_n, s_n, c)


def _from_strided(a, d):
    if d == 1:
        return a
    b_n, s_n, c = a.shape
    return a.reshape(b_n, d, s_n // d, c).transpose(0, 2, 1, 3).reshape(b_n, s_n, c)


def _attn_fwd(name, q, q_col, k, v, blocks_per_seq):
    b_n, s_n, cw = k.shape
    nb = s_n // SPAN
    with_prev = blocks_per_seq > 1

    keys = 2 * SPAN if with_prev else SPAN
    pair = 2 * HEAD_DIM

    def body(*refs):
        if with_prev:
            q_ref, kc_ref, kp_ref, vc_ref, vp_ref, o_ref, lse_ref, k_all, v_all, s_buf, m_buf = refs
            k_all[0:SPAN, :] = kp_ref[...]
            k_all[SPAN:, :] = kc_ref[...]
            v_all[0:SPAN, :] = vp_ref[...]
            v_all[SPAN:, :] = vc_ref[...]
        else:
            q_ref, k_all, v_all, o_ref, lse_ref, s_buf, m_buf = refs
        n = pl.program_id(1)
        qi = lax.broadcasted_iota(jnp.int32, (SPAN, keys), 0)
        kj = lax.broadcasted_iota(jnp.int32, (SPAN, keys), 1)
        if with_prev:
            back = jnp.where((n % blocks_per_seq) != 0, 0, 2 * SPAN)
            mask = ((kj < SPAN) & (kj >= qi + back)) | ((kj >= SPAN) & (kj - SPAN <= qi))
        else:
            mask = kj <= qi
        lane = lax.broadcasted_iota(jnp.int32, (SPAN, pair), 1)
        low = lane < HEAD_DIM
        ones = jnp.ones((keys, pair), MXU)
        for h in range(N_HEADS):
            grp = slice((h // 2) * pair, (h // 2 + 1) * pair)
            q2 = q_ref[:, grp]
            qh = jnp.where(low if h % 2 == 0 else ~low, q2, jnp.zeros_like(q2))
            s = jnp.where(mask, lax.dot_general(qh, k_all[:, grp], NT, preferred_element_type=F32), -jnp.inf)
            s_buf[h] = s
            m_buf[h] = jnp.broadcast_to(jnp.max(s, -1, keepdims=True), (SPAN, pair))
        lse_all = jnp.zeros((SPAN, pair), F32)
        for hp in range(N_HEADS // 2):
            grp = slice(hp * pair, (hp + 1) * pair)
            v2 = v_all[:, grp]
            halves = []
            for h in (2 * hp, 2 * hp + 1):
                m = m_buf[h]
                p = jnp.exp(s_buf[h] - jnp.tile(m, (1, keys // pair))).astype(MXU)
                tot = jnp.dot(p, ones, preferred_element_type=F32)
                halves.append(jnp.dot(p, v2, preferred_element_type=F32) / tot)
                lse_all = jnp.where(lane == h, m + jnp.log(tot), lse_all)
            o_ref[:, grp] = jnp.where(low, halves[0], halves[1])
        lse_ref[...] = lse_all[:, 0:N_HEADS]

    cur = lambda b, n: (b, n, 0)
    prv = lambda b, n: (b, jnp.maximum(n - 1, 0), 0)
    blk = (None, SPAN, cw)
    kv_specs = [pl.BlockSpec(blk, cur), pl.BlockSpec(blk, prv)] if with_prev else [pl.BlockSpec(blk, cur)]
    operands = [q, k, k, v, v] if with_prev else [q, k, v]
    stage = [pltpu.VMEM((keys, cw), MXU)] * 2 if with_prev else []
    return pl.pallas_call(
        body, name=name, grid=(b_n, nb),
        in_specs=[pl.BlockSpec(blk, lambda b, n: (b, n, q_col))] + kv_specs + kv_specs,
        out_specs=[pl.BlockSpec(blk, cur), pl.BlockSpec((None, SPAN, N_HEADS), cur)],
        out_shape=[jax.ShapeDtypeStruct((b_n, s_n, cw), F32), jax.ShapeDtypeStruct((b_n, s_n, N_HEADS), F32)],
        scratch_shapes=stage + [pltpu.VMEM((N_HEADS, SPAN, keys), F32), pltpu.VMEM((N_HEADS, SPAN, pair), F32)],
        compiler_params=_params(("parallel", "parallel")),
    )(*operands)


def _attn_bwd(name, q, q_col, k, v, do, lse, dd, blocks_per_seq, dk_prev=None, dv_prev=None):
    b_n, s_n, cw = k.shape
    nb = s_n // SPAN
    with_next = blocks_per_seq > 1
    accumulate = dk_prev is not None
    rows = 2 * SPAN if with_next else SPAN

    def body(*refs):
        refs = list(refs)
        qc_ref, doc_ref, lsec_ref, ddc_ref = refs[:4]
        del refs[:4]
        if with_next:
            qn_ref, don_ref, lsen_ref, ddn_ref = refs[:4]
            del refs[:4]
        k_ref, v_ref = refs[:2]
        del refs[:2]
        if accumulate:
            dkp_ref, dvp_ref = refs[:2]
            del refs[:2]
        dq_ref, dk_ref, dv_ref = refs[:3]
        del refs[:3]
        if with_next:
            carry, q_all, do_all, side = refs[:4]
            del refs[:4]
            q_all[0:SPAN, :] = qc_ref[...]
            q_all[SPAN:, :] = qn_ref[...]
            do_all[0:SPAN, :] = doc_ref[...]
            do_all[SPAN:, :] = don_ref[...]
            side[0, 0:SPAN, :] = lsec_ref[...]
            side[0, SPAN:, :] = lsen_ref[...]
            side[1, 0:SPAN, :] = ddc_ref[...]
            side[1, SPAN:, :] = ddn_ref[...]
            lse_at = lambda h: side[0, :, h:h + 1]
            dd_at = lambda h: side[1, :, h:h + 1]
        else:
            q_all, do_all = qc_ref, doc_ref
            lse_at = lambda h: lsec_ref[:, h:h + 1]
            dd_at = lambda h: ddc_ref[:, h:h + 1]
        p_buf, ds_buf = refs
        n = pl.program_id(1)
        qi = lax.broadcasted_iota(jnp.int32, (rows, SPAN), 0)
        kj = lax.broadcasted_iota(jnp.int32, (rows, SPAN), 1)
        if with_next:
            first = (n % blocks_per_seq) == 0
            reach = jnp.where(((n + 1) % blocks_per_seq) != 0, SPAN, -2 * SPAN)
            mask = ((qi < SPAN) & (kj <= qi)) | ((qi >= SPAN) & (kj >= qi - reach))
        else:
            mask = kj <= qi
        pair = 2 * HEAD_DIM
        low = lax.broadcasted_iota(jnp.int32, (rows, pair), 1) < HEAD_DIM
        low_k = lax.broadcasted_iota(jnp.int32, (SPAN, pair), 1) < HEAD_DIM

        def pick(v, h, low_mask):
            return jnp.where(low_mask if h % 2 == 0 else ~low_mask, v, jnp.zeros_like(v))

        for h in range(N_HEADS):
            grp = slice((h // 2) * pair, (h // 2 + 1) * pair)
            s = lax.dot_general(pick(q_all[:, grp], h, low), k_ref[:, grp], NT, preferred_element_type=F32)
            p = jnp.where(mask, jnp.exp(s - lse_at(h)), 0.0)
            dp = lax.dot_general(pick(do_all[:, grp], h, low), v_ref[:, grp], NT, preferred_element_type=F32)
            p_buf[h] = p.astype(MXU)
            ds_buf[h] = (p * (dp - dd_at(h))).astype(MXU)
        for hp in range(N_HEADS // 2):
            grp = slice(hp * pair, (hp + 1) * pair)
            q2, do2, k2 = q_all[:, grp], do_all[:, grp], k_ref[:, grp]
            dv = dk = dq2 = None
            for h in (2 * hp, 2 * hp + 1):
                t_dv = lax.dot_general(p_buf[h], pick(do2, h, low), TN, preferred_element_type=F32)
                t_dk = lax.dot_general(ds_buf[h], pick(q2, h, low), TN, preferred_element_type=F32)
                t_dq = jnp.dot(ds_buf[h], pick(k2, h, low_k), preferred_element_type=F32)
                dv = t_dv if dv is None else dv + t_dv
                dk = t_dk if dk is None else dk + t_dk
                dq2 = t_dq if dq2 is None else dq2 + t_dq
            if accumulate:
                dk = dk + dkp_ref[:, grp]
                dv = dv + dvp_ref[:, grp]
            dk_ref[:, grp] = dk
            dv_ref[:, grp] = dv
            if with_next:
                dq_ref[:, grp] = dq2[:SPAN] + jnp.where(first, 0.0, carry[:, grp])
                carry[:, grp] = dq2[SPAN:]
            else:
                dq_ref[:, grp] = dq2

    cur = lambda b, n: (b, n, 0)
    nxt = lambda b, n: (b, jnp.minimum(n + 1, nb - 1), 0)
    blk, hblk = (None, SPAN, cw), (None, SPAN, N_HEADS)
    q_specs = lambda m: [pl.BlockSpec(blk, lambda b, n: (*m(b, n)[:2], q_col)), pl.BlockSpec(blk, m),
                         pl.BlockSpec(hblk, m), pl.BlockSpec(hblk, m)]
    in_specs = q_specs(cur) + (q_specs(nxt) if with_next else []) + [pl.BlockSpec(blk, cur)] * (4 if accumulate else 2)
    operands = [q, do, lse, dd] * (2 if with_next else 1) + [k, v] + ([dk_prev, dv_prev] if accumulate else [])
    out = jax.ShapeDtypeStruct((b_n, s_n, cw), F32)
    return pl.pallas_call(
        body, name=name, grid=(b_n, nb), in_specs=in_specs, out_specs=[pl.BlockSpec(blk, cur)] * 3,
        out_shape=[out, out, out],
        scratch_shapes=([pltpu.VMEM((SPAN, cw), F32), pltpu.VMEM((rows, cw), MXU), pltpu.VMEM((rows, cw), MXU),
                         pltpu.VMEM((2, rows, N_HEADS), F32)] if with_next else [])
        + [pltpu.VMEM((N_HEADS, rows, SPAN), MXU)] * 2,
        compiler_params=_params(("parallel", "arbitrary")),
    )(*operands)


MIX_ROWS = 256


def _group_weights(lse_refs):
    ls = [r[...] for r in lse_refs]
    m = functools.reduce(jnp.maximum, ls)
    es = [jnp.exp(l - m) for l in ls]
    tot = functools.reduce(lambda a, b: a + b, es)
    return [e / tot for e in es]


def _attn_mix(name, outs, lses):
    n, cw = outs[0].shape
    g_n = len(outs)

    def body(*refs):
        o_refs, l_refs, out_ref = refs[:g_n], refs[g_n:2 * g_n], refs[2 * g_n]
        ws = _group_weights(l_refs)
        for h in range(N_HEADS):
            cols = slice(h * HEAD_DIM, (h + 1) * HEAD_DIM)
            acc = None
            for g in range(g_n):
                t = ws[g][:, h:h + 1] * o_refs[g][:, cols]
                acc = t if acc is None else acc + t
            out_ref[:, cols] = acc.astype(out_ref.dtype)

    tile = pl.BlockSpec((MIX_ROWS, cw), lambda i: (i, 0))
    htile = pl.BlockSpec((MIX_ROWS, N_HEADS), lambda i: (i, 0))
    return pl.pallas_call(
        body, name=name, grid=(n // MIX_ROWS,), in_specs=[tile] * g_n + [htile] * g_n, out_specs=tile,
        out_shape=jax.ShapeDtypeStruct((n, cw), MXU), compiler_params=_params(("parallel",)),
    )(*outs, *lses)


def _attn_mix_bwd(name, do, outs, lses):
    n, cw = do.shape
    g_n = len(outs)

    def body(*refs):
        do_ref, o_refs, l_refs = refs[0], refs[1:1 + g_n], refs[1 + g_n:1 + 2 * g_n]
        dog_refs, dd_refs = refs[1 + 2 * g_n:1 + 3 * g_n], refs[1 + 3 * g_n:]
        ws = _group_weights(l_refs)
        for h in range(N_HEADS):
            cols = slice(h * HEAD_DIM, (h + 1) * HEAD_DIM)
            dh = do_ref[:, cols]
            o = None
            for g in range(g_n):
                t = ws[g][:, h:h + 1] * o_refs[g][:, cols]
                o = t if o is None else o + t
            dot = jnp.sum(dh * o, -1, keepdims=True)
            for g in range(g_n):
                wg = ws[g][:, h:h + 1]
                dog_refs[g][:, cols] = (wg * dh).astype(dog_refs[g].dtype)
                dd_refs[g][:, h:h + 1] = wg * dot

    tile = pl.BlockSpec((MIX_ROWS, cw), lambda i: (i, 0))
    htile = pl.BlockSpec((MIX_ROWS, N_HEADS), lambda i: (i, 0))
    return pl.pallas_call(
        body, name=name, grid=(n // MIX_ROWS,), in_specs=[tile] * (1 + g_n) + [htile] * g_n,
        out_specs=[tile] * g_n + [htile] * g_n,
        out_shape=[jax.ShapeDtypeStruct((n, cw), MXU)] * g_n + [jax.ShapeDtypeStruct((n, N_HEADS), F32)] * g_n,
        compiler_params=_params(("parallel",)),
    )(do, *outs, *lses)


def _loss_head(name, y, target):
    n, d = y.shape
    steps = n // LN_ROWS

    def body(y_ref, t_ref, dy_ref, l_ref, acc):
        i = pl.program_id(0)

        @pl.when(i == 0)
        def _():
            acc[...] = jnp.zeros_like(acc)

        def chunk(s):
            rows = pl.ds(s, LN_CH)
            err = y_ref[rows, :] - t_ref[rows, :]
            dy_ref[rows, :] = err / d
            acc[...] += _fold8(err * err)

        _chunks(LN_ROWS, LN_CH, chunk)

        @pl.when(i == steps - 1)
        def _():
            l_ref[...] = jnp.full((8, 128), 0.5 / d, F32) * jnp.sum(acc[...])

    tile = pl.BlockSpec((LN_ROWS, d), lambda i: (i, 0))
    return pl.pallas_call(
        body, name=name, grid=(steps,), in_specs=[tile, tile],
        out_specs=[tile, pl.BlockSpec((8, 128), lambda i: (0, 0))],
        out_shape=[jax.ShapeDtypeStruct((n, d), F32), jax.ShapeDtypeStruct((8, 128), F32)],
        scratch_shapes=[pltpu.VMEM((8, d), F32)], compiler_params=_params(("arbitrary",)),
    )(y, target)


EW_TILE_BYTES = 1 << 20


def _row_tile(rows, cols):
    tr = 8
    while rows % (2 * tr) == 0 and 2 * tr * cols * 4 <= EW_TILE_BYTES:
        tr *= 2
    return tr if rows % tr == 0 else rows


def _add_halves(name, grad, recv, half, out_dtype):
    j_n, _, r, c = grad.shape
    tr = _row_tile(r, c)

    def body(half_ref, g_ref, r_ref, o_ref):
        o_ref[...] = (g_ref[...] + r_ref[...]).astype(o_ref.dtype)

    return pl.pallas_call(
        body, name=name, out_shape=jax.ShapeDtypeStruct((j_n, r, c), out_dtype),
        grid_spec=pltpu.PrefetchScalarGridSpec(
            num_scalar_prefetch=1, grid=(j_n, r // tr),
            in_specs=[pl.BlockSpec((None, None, tr, c), lambda j, i, hf: (j, hf[0], i, 0)),
                      pl.BlockSpec((None, tr, c), lambda j, i, hf: (j, i, 0))],
            out_specs=pl.BlockSpec((None, tr, c), lambda j, i, hf: (j, i, 0))),
        compiler_params=_params(("parallel", "parallel")),
    )(half, grad, recv)


def _add_chips(name, mine, recv, chip):
    j_n, r, c = mine.shape
    tr = _row_tile(r, c)

    def body(chip_ref, m_ref, r_ref, o_ref):
        total = m_ref[...].astype(F32)
        for k in range(j_n - 1):
            total = total + r_ref[k].astype(F32)
        o_ref[...] = total

    return pl.pallas_call(
        body, name=name, out_shape=jax.ShapeDtypeStruct((r, c), F32),
        grid_spec=pltpu.PrefetchScalarGridSpec(
            num_scalar_prefetch=1, grid=(r // tr,),
            in_specs=[pl.BlockSpec((None, tr, c), lambda i, ch: (ch[0], i, 0)),
                      pl.BlockSpec((j_n - 1, tr, c), lambda i, ch: (0, i, 0))],
            out_specs=pl.BlockSpec((tr, c), lambda i, ch: (i, 0))),
        compiler_params=_params(("parallel",)),
    )(chip, mine, recv)


def _adam_math(w, g, m, v):
    m = ADAM_B1 * m + (1.0 - ADAM_B1) * g
    v = ADAM_B2 * v + (1.0 - ADAM_B2) * (g * g)
    m_hat = m / (1.0 - ADAM_B1 ** ADAM_STEP)
    v_hat = v / (1.0 - ADAM_B2 ** ADAM_STEP)
    delta = -ADAM_LR * (m_hat / (jnp.sqrt(v_hat) + ADAM_EPS) + ADAM_WD * w)
    return delta, m, v


def _adam_halves(name, own, other, half, w, m, v):
    _, r, c = w.shape
    tr = _row_tile(r, c)

    def body(half_ref, own_ref, oth_ref, w_ref, m_ref, v_ref, g_out, d_out, m_out, v_out):
        g = jnp.where(pl.program_id(0) == half_ref[0], own_ref[...], oth_ref[...])
        delta, m_new, v_new = _adam_math(w_ref[...], g, m_ref[...], v_ref[...])
        g_out[...] = g
        d_out[...] = delta
        m_out[...] = m_new
        v_out[...] = v_new

    flat = pl.BlockSpec((tr, c), lambda h, i, hf: (i, 0))
    full = pl.BlockSpec((None, tr, c), lambda h, i, hf: (h, i, 0))
    out = jax.ShapeDtypeStruct(w.shape, F32)
    return pl.pallas_call(
        body, name=name, out_shape=[out] * 4,
        grid_spec=pltpu.PrefetchScalarGridSpec(num_scalar_prefetch=1, grid=(2, r // tr), in_specs=[flat, flat, full, full, full],
                                               out_specs=[full] * 4),
        compiler_params=_params(("parallel", "parallel")),
    )(half, own, other, w, m, v)


def _adam_small(name, g, w, m, v):
    def body(g_ref, w_ref, m_ref, v_ref, d_out, m_out, v_out):
        delta, m_new, v_new = _adam_math(w_ref[...], g_ref[...], m_ref[...], v_ref[...])
        d_out[...] = delta
        m_out[...] = m_new
        v_out[...] = v_new

    out = jax.ShapeDtypeStruct(w.shape, F32)
    return pl.pallas_call(body, name=name, out_shape=[out] * 3)(g, w, m, v)


def _place():
    x, y, c = lax.axis_index("x"), lax.axis_index("y"), lax.axis_index("c")
    chips = [(1 - x, y), (x, 1 - y), (1 - x, 1 - y)]
    return x, y, c, chips


ANY = pl.BlockSpec(memory_space=pl.ANY)


def _gather_shards(name, shards):
    n = len(shards)

    def body(*refs):
        ins, outs = refs[:n], refs[n:2 * n]
        send, recv, local = refs[2 * n:]
        x, y, c, chips = _place()
        me = 2 * x + y
        sibling = (x, y, 1 - c)

        def copy(t, k, src, dst, to):
            return pltpu.make_async_remote_copy(src_ref=src, dst_ref=dst, send_sem=send.at[t, k], recv_sem=recv.at[t, k],
                                                device_id=to, device_id_type=MESH)

        own = [pltpu.make_async_copy(ins[t], outs[t].at[me], local.at[t]) for t in range(n)]
        for cp in own:
            cp.start()
        sent = []
        for t in range(n):
            for k, (px, py) in enumerate(chips):
                sent.append(copy(t, k, ins[t].at[c], outs[t].at[me, c], (px, py, c)))
                sent[-1].start()
        for t in range(n):
            for k, (px, py) in enumerate(chips):
                slab = outs[t].at[2 * px + py, c]
                copy(t, k, ins[t].at[c], slab, (px, py, c)).wait_recv()
                sent.append(copy(t, 3 + k, slab, slab, sibling))
                sent[-1].start()
        for t in range(n):
            for k, (px, py) in enumerate(chips):
                slab = outs[t].at[2 * px + py, 1 - c]
                copy(t, 3 + k, slab, slab, sibling).wait_recv()
        for cp in sent:
            cp.wait_send()
        for cp in own:
            cp.wait()

    return pl.pallas_call(
        body, name=name, in_specs=[ANY] * n, out_specs=[ANY] * n,
        out_shape=[jax.ShapeDtypeStruct((N_CHIPS, *s.shape), s.dtype) for s in shards],
        scratch_shapes=[pltpu.SemaphoreType.DMA((n, 6)), pltpu.SemaphoreType.DMA((n, 6)), pltpu.SemaphoreType.DMA((n,))],
    )(*shards)


def _to_sibling(name, arrays, pick_other_half):
    n = len(arrays)

    def body(*refs):
        ins, outs = refs[:n], refs[n:2 * n]
        send, recv = refs[2 * n:]
        x, y, c, _ = _place()
        copies = []
        for t in range(n):
            src = ins[t].at[:, 1 - c] if pick_other_half else ins[t]
            copies.append(pltpu.make_async_remote_copy(src_ref=src, dst_ref=outs[t], send_sem=send.at[t], recv_sem=recv.at[t],
                                                       device_id=(x, y, 1 - c), device_id_type=MESH))
            copies[-1].start()
        for cp in copies:
            cp.wait()

    shapes = [(a.shape[0], *a.shape[2:]) if pick_other_half else a.shape for a in arrays]
    return pl.pallas_call(
        body, name=name, in_specs=[ANY] * n, out_specs=[ANY] * n,
        out_shape=[jax.ShapeDtypeStruct(s, a.dtype) for s, a in zip(shapes, arrays)],
        scratch_shapes=[pltpu.SemaphoreType.DMA((n,)), pltpu.SemaphoreType.DMA((n,))],
    )(*arrays)


def _to_owner_chips(name, arrays):
    n = len(arrays)

    def body(*refs):
        ins, outs = refs[:n], refs[n:2 * n]
        send, recv = refs[2 * n:]
        x, y, c, chips = _place()
        copies = []
        for t in range(n):
            for k, (px, py) in enumerate(chips):
                copies.append(pltpu.make_async_remote_copy(
                    src_ref=ins[t].at[2 * px + py], dst_ref=outs[t].at[k], send_sem=send.at[t, k], recv_sem=recv.at[t, k],
                    device_id=(px, py, c), device_id_type=MESH))
                copies[-1].start()
        for cp in copies:
            cp.wait()

    return pl.pallas_call(
        body, name=name, in_specs=[ANY] * n, out_specs=[ANY] * n,
        out_shape=[jax.ShapeDtypeStruct((N_CHIPS - 1, *a.shape[1:]), a.dtype) for a in arrays],
        scratch_shapes=[pltpu.SemaphoreType.DMA((n, 3)), pltpu.SemaphoreType.DMA((n, 3))],
    )(*arrays)


def _sum_all_devices(name, part):
    r, c = part.shape

    def body(p_ref, o_ref, buf, send, recv):
        x, y, cc, _ = _place()
        me = 4 * x + 2 * y + cc
        copies = []
        for mask in range(1, 8):
            fx, fy, fc = (mask >> 2) & 1, (mask >> 1) & 1, mask & 1
            to = (x ^ fx, y ^ fy, cc ^ fc)
            copies.append((mask, pltpu.make_async_remote_copy(
                src_ref=p_ref, dst_ref=buf.at[me], send_sem=send.at[mask - 1], recv_sem=recv.at[mask - 1],
                device_id=to, device_id_type=MESH)))
            copies[-1][1].start()
        buf[me] = p_ref[...]
        for mask, cp in copies:
            pltpu.make_async_remote_copy(src_ref=p_ref, dst_ref=buf.at[me ^ mask], send_sem=send.at[mask - 1],
                                         recv_sem=recv.at[mask - 1], device_id=(x, y, cc), device_id_type=MESH).wait_recv()
        for _, cp in copies:
            cp.wait_send()
        total = buf[0]
        for d in range(1, 8):
            total = total + buf[d]
        o_ref[...] = total

    vm = pl.BlockSpec(memory_space=pltpu.VMEM)
    return pl.pallas_call(
        body, name=name, in_specs=[vm], out_specs=vm, out_shape=jax.ShapeDtypeStruct((r, c), F32),
        scratch_shapes=[pltpu.VMEM((8, r, c), F32), pltpu.SemaphoreType.DMA((7,)), pltpu.SemaphoreType.DMA((7,))],
    )(part)


def kernel(x, pool_w, pool_scale, w_q, w_kv, w_o, ffn_w_gate, ffn_w_up, ffn_conv_w, ffn_conv_b, ffn_w_down, ln1_g, ln1_b, ln2_g, ln2_b, loss_target, m_pool_w, m_pool_scale, m_w_q, m_w_kv, m_w_o, m_ffn_w_gate, m_ffn_w_up, m_ffn_conv_w, m_ffn_conv_b, m_ffn_w_down, m_ln1_g, m_ln1_b, m_ln2_g, m_ln2_b, v_pool_w, v_pool_scale, v_w_q, v_w_kv, v_w_o, v_ffn_w_gate, v_ffn_w_up, v_ffn_conv_w, v_ffn_conv_b, v_ffn_w_down, v_ln1_g, v_ln1_b, v_ln2_g, v_ln2_b):
    b_n, s_n, d = x.shape
    n = b_n * s_n
    f = ffn_w_gate.shape[-1]
    qc = w_q.shape[-1]
    kvb = w_kv.shape[-1] // 2
    n_attn = w_q.shape[0]
    g_n = len(DILATIONS)
    cw = N_HEADS * HEAD_DIM
    xi, yi, ci = lax.axis_index("x"), lax.axis_index("y"), lax.axis_index("c")
    half = jnp.reshape(ci, (1,)).astype(jnp.int32)
    chip = jnp.reshape(2 * xi + yi, (1,)).astype(jnp.int32)

    sharded = {
        "pool_w": (pool_w, m_pool_w, v_pool_w, (2, 4 * 64, POOL_GROUP_DIM)),
        "pool_scale": (pool_scale, m_pool_scale, v_pool_scale, (2, 1, pool_scale.shape[-1])),
        "w_q": (w_q, m_w_q, v_w_q, (2, d, qc)),
        "w_kv": (w_kv, m_w_kv, v_w_kv, (2, d // 2, w_kv.shape[-1])),
        "w_o": (w_o, m_w_o, v_w_o, (2, w_o.shape[1], d)),
        "ffn_w_gate": (ffn_w_gate, m_ffn_w_gate, v_ffn_w_gate, (2, 2 * d, f)),
        "ffn_w_up": (ffn_w_up, m_ffn_w_up, v_ffn_w_up, (2, 2 * d, f)),
        "ffn_conv_w": (ffn_conv_w, m_ffn_conv_w, v_ffn_conv_w, (2, 6, f)),
        "ffn_w_down": (ffn_w_down, m_ffn_w_down, v_ffn_w_down, (2, 2 * f, d)),
    }
    mxu_weights = ("pool_w", "w_q", "w_kv", "w_o", "ffn_w_gate", "ffn_w_up", "ffn_w_down")
    names = list(sharded)
    gathered = _gather_shards("gather_weights", [
        sharded[k][0].reshape(sharded[k][3]).astype(MXU if k in mxu_weights else F32) for k in names])
    full = dict(zip(names, gathered))
    per_layer = lambda t, shape: [t.reshape(N_CHIPS, shape[0], *shape[1:])[:, i] for i in range(shape[0])]
    wq_l = per_layer(full["w_q"], (n_attn, d, qc))
    wkv_all = full["w_kv"].reshape(N_CHIPS, d, w_kv.shape[-1])
    wg_l = per_layer(full["ffn_w_gate"], (DEPTH, d, f))
    wu_l = per_layer(full["ffn_w_up"], (DEPTH, d, f))
    wd_l = per_layer(full["ffn_w_down"], (DEPTH, f, d))
    cw_l = per_layer(full["ffn_conv_w"], (DEPTH, 3, f))
    cb_l = [ffn_conv_b[i].reshape(N_CHIPS, 1, f) for i in range(DEPTH)]
    pw_nat = full["pool_w"].reshape(N_CHIPS, N_POOL_LAYERS, 4, 64, POOL_GROUP_DIM).transpose(1, 2, 0, 3, 4).reshape(
        N_POOL_LAYERS, 4, POOL_GROUP_DIM, POOL_GROUP_DIM)
    ps_nat = full["pool_scale"].reshape(N_CHIPS, N_POOL_LAYERS, -1).transpose(1, 0, 2).reshape(N_POOL_LAYERS, 1, d)
    wo_nat = full["w_o"].reshape(N_CHIPS, n_attn, -1, d).transpose(1, 0, 2, 3).reshape(n_attn, cw, d)
    cos_l, sin_l = _rope_tables(s_n)

    def vec(a, layer):
        return a[layer].reshape(1, d)

    h = x.reshape(n, d)
    hb = None
    saved = []
    k_str = v_str = None
    for layer in range(DEPTH):
        keep = {"h": h, "hb": hb}
        if layer < N_POOL_LAYERS:
            mix = _pool_fwd("pool_fwd", h.reshape(b_n, s_n, d), pw_nat[layer], ps_nat[layer]).reshape(n, d)
        else:
            a = layer - N_POOL_LAYERS
            q = _mm_cols("q_proj", hb, wq_l[a], (), F32)
            qr = _rope("q_rope", q.reshape(b_n, s_n, -1), 0, g_n, cos_l, sin_l, HEAD_DIM ** -0.5, MXU)
            q_str, o_str, lse_str, outs, lses = [], [], [], [], []
            for g, dil in enumerate(DILATIONS):
                if dil == 1:
                    qg, col = qr, g
                else:
                    qg, col = _to_strided(qr[:, :, g * cw:(g + 1) * cw], dil), 0
                o_g, lse_g = _attn_fwd(f"attn_fwd_d{dil}", qg, col, k_str[g], v_str[g], s_n // dil // SPAN)
                q_str.append((qg, col))
                lse_str.append(lse_g)
                outs.append(_from_strided(o_g, dil).reshape(n, cw))
                lses.append(_from_strided(lse_g, dil).reshape(n, N_HEADS))
            ob = _attn_mix("attn_mix", outs, lses)
            mix = _mm("o_proj", [(ob, (TM, cw), lambda m, _: (m, 0), wo_nat[a], (cw, d), lambda m, _: (0, 0))],
                      NN, (n // TM, 1), (n, d), F32, (TM, d), lambda m, _: (m, 0), 1)
            keep.update(q_str=q_str, lse_str=lse_str, outs=outs, lses=lses, ob=ob)
        r1, h1, h1b = _add_ln("ln_fwd", h, mix, vec(ln1_g, layer), vec(ln1_b, layer))
        gate = _mm_shard_out("gate_up_proj", h1b, wg_l[layer], (), F32)
        up = _mm_shard_out("gate_up_proj", h1b, wu_l[layer], (), F32)
        g4, u4 = gate.reshape(N_CHIPS, b_n, s_n, f), up.reshape(N_CHIPS, b_n, s_n, f)
        hmid = _glu_fwd("glu_fwd", g4, u4, cw_l[layer], cb_l[layer]).reshape(N_CHIPS, n, f)
        ffn = _mm_shard_in("down_proj", hmid, wd_l[layer], ())
        r2, h2, h2b = _add_ln("ln_fwd", h1, ffn, vec(ln2_g, layer), vec(ln2_b, layer))
        keep.update(r1=r1, h1b=h1b, g4=g4, u4=u4, hmid=hmid, r2=r2, h2b=h2b)
        saved.append(keep)
        if layer == N_POOL_LAYERS - 1:
            kv = _mm_cols("kv_proj", h2b, wkv_all, (), F32, cb=kvb).reshape(b_n, s_n, -1)
            k_nat = _rope("k_rope", kv, 0, g_n, cos_l, sin_l, 1.0, MXU)
            v_nat = kv[:, :, g_n * cw:].astype(MXU)
            k_str = [_to_strided(k_nat[:, :, g * cw:(g + 1) * cw], dil) for g, dil in enumerate(DILATIONS)]
            v_str = [_to_strided(v_nat[:, :, g * cw:(g + 1) * cw], dil) for g, dil in enumerate(DILATIONS)]
        h, hb = h2, h2b

    dy, loss_tile = _loss_head("loss_head", h, loss_target.reshape(n, d))

    d_wq = d_wo = d_wg = d_wu = d_wd = None
    d_cw = [None] * DEPTH
    d_pw, d_ps = [None] * N_POOL_LAYERS, [None] * N_POOL_LAYERS
    d_ln = {}
    dk_str = dv_str = None
    d_top, top_scale, top_rest = dy, 1.0, []
    for layer in reversed(range(DEPTH)):
        sv = saved[layer]
        dr2, dr2b, d_ln["ln2", layer] = _ln_bwd("ln_bwd", sv["r2"], vec(ln2_g, layer), d_top, top_scale, top_rest)
        dhmid = _mm_nt_shard_out("down_bwd", dr2b, wd_l[layer], ())
        d_wd = _mm_tn("down_dw", sv["hmid"], f, dr2b, d, N_CHIPS, (N_CHIPS, DEPTH, f, d), (None, None, f, d),
                      lambda j: (j, layer, 0, 0), into=d_wd, a_lead=lambda j: (j,))
        dg4, du4, d_cw[layer] = _glu_bwd("glu_bwd", dhmid.reshape(N_CHIPS, b_n, s_n, f), sv["g4"], sv["u4"],
                                         cw_l[layer], cb_l[layer])
        dg, du = dg4.reshape(N_CHIPS, n, f), du4.reshape(N_CHIPS, n, f)
        dh1 = _mm_nt_shard_in("gate_up_bwd", [(dg, wg_l[layer], ()), (du, wu_l[layer], ())])
        d_wg = _mm_tn("gate_up_dw", sv["h1b"], d, dg, f, N_CHIPS, (N_CHIPS, DEPTH, d, f), (None, None, d, f),
                      lambda j: (j, layer, 0, 0), into=d_wg, b_lead=lambda j: (j,))
        d_wu = _mm_tn("gate_up_dw", sv["h1b"], d, du, f, N_CHIPS, (N_CHIPS, DEPTH, d, f), (None, None, d, f),
                      lambda j: (j, layer, 0, 0), into=d_wu, b_lead=lambda j: (j,))
        dr1, dr1b, d_ln["ln1", layer] = _ln_bwd("ln_bwd", sv["r1"], vec(ln1_g, layer), dr2, ALPHA, [dh1])
        if layer < N_POOL_LAYERS:
            d_in, d_pw[layer], d_ps[layer] = _pool_bwd("pool_bwd", sv["h"].reshape(b_n, s_n, d),
                                                       dr1.reshape(b_n, s_n, d), pw_nat[layer], ps_nat[layer])
            d_top, top_scale, top_rest = d_in.reshape(n, d), 1.0, []
        else:
            a = layer - N_POOL_LAYERS
            do = _mm("o_bwd", [(dr1b, (TM, d), lambda m, _: (m, 0), wo_nat[a], (cw, d), lambda m, _: (0, 0))],
                     NT, (n // TM, 1), (n, cw), F32, (TM, cw), lambda m, _: (m, 0), 1)
            d_wo = _mm_tn("o_dw", sv["ob"], cw // N_CHIPS, dr1b, d, N_CHIPS, (N_CHIPS, n_attn, cw // N_CHIPS, d),
                          (None, None, cw // N_CHIPS, d), lambda j: (j, a, 0, 0), into=d_wo)
            mixed = _attn_mix_bwd("attn_mix_bwd", do, sv["outs"], sv["lses"])
            dq_nat, dk_new, dv_new = [], [], []
            for g, dil in enumerate(DILATIONS):
                do_g = _to_strided(mixed[g].reshape(b_n, s_n, cw), dil)
                dd_g = _to_strided(mixed[g_n + g].reshape(b_n, s_n, N_HEADS), dil)
                qg, col = sv["q_str"][g]
                dq_g, dk_g, dv_g = _attn_bwd(f"attn_bwd_d{dil}", qg, col, k_str[g], v_str[g], do_g, sv["lse_str"][g], dd_g,
                                             s_n // dil // SPAN, *((dk_str[g], dv_str[g]) if dk_str else ()))
                dq_nat.append(_from_strided(dq_g, dil))
                dk_new.append(dk_g)
                dv_new.append(dv_g)
            dk_str, dv_str = dk_new, dv_new
            dq = None
            for g in range(g_n):
                dq = _rope("q_rope_bwd", dq_nat[g], 0, 1, cos_l, -sin_l, HEAD_DIM ** -0.5, MXU, out_cols=g_n, out_col0=g, into=dq)
            dq = dq.reshape(n, g_n * cw)
            d_attn = _mm_nt_cols_in("q_bwd", dq, wq_l[a], (), qc)
            d_wq = _mm_tn("q_dw", sv["hb"], d, dq, qc, N_CHIPS, (N_CHIPS, n_attn, d, qc), (None, None, d, qc),
                          lambda j: (j, a, 0, 0), into=d_wq)
            d_top, top_scale, top_rest = dr1, ALPHA, [d_attn]
            if a == 0:
                dkv = None
                for g, dil in enumerate(DILATIONS):
                    dkv = _rope("k_rope_bwd", _from_strided(dk_str[g], dil), 0, 1, cos_l, -sin_l, 1.0, MXU,
                                out_cols=2 * g_n, out_col0=g, into=dkv)
                for g, dil in enumerate(DILATIONS):
                    dkv = _rope("v_cast_bwd", _from_strided(dv_str[g], dil), 0, 1, cos_l, sin_l, 1.0, MXU, rotate=False,
                                out_cols=2 * g_n, out_col0=g_n + g, into=dkv)
                dkv = dkv.reshape(n, 2 * g_n * cw)
                h_kv = saved[N_POOL_LAYERS - 1]["h2b"]
                top_rest = top_rest + [_mm_nt_cols_in("kv_bwd", dkv, wkv_all, (), kvb)]
                d_wkv = _mm_tn("kv_dw", h_kv, d, dkv, kvb, 2 * N_CHIPS, (2 * N_CHIPS, d, kvb), (None, d, kvb), lambda q: (q, 0, 0))
    grad_x = d_top.reshape(b_n, s_n, d)

    d_wkv = d_wkv.reshape(N_CHIPS, 2, d, kvb).transpose(0, 2, 1, 3)
    d_pw_all = jnp.stack(d_pw).reshape(N_POOL_LAYERS, 4, N_CHIPS, 64, POOL_GROUP_DIM).transpose(2, 0, 1, 3, 4)
    d_ps_all = jnp.stack(d_ps).reshape(N_POOL_LAYERS, N_CHIPS, -1).transpose(1, 0, 2)
    d_cw_all = jnp.stack([t[:, :3] for t in d_cw], axis=1)
    local = {
        "pool_w": d_pw_all, "pool_scale": d_ps_all, "w_q": d_wq, "w_kv": d_wkv, "w_o": d_wo,
        "ffn_w_gate": d_wg, "ffn_w_up": d_wu, "ffn_conv_w": d_cw_all, "ffn_w_down": d_wd,
    }
    grads4 = [local[k].reshape(N_CHIPS, *sharded[k][3]) for k in names]
    from_sibling = _to_sibling("grads_to_sibling", grads4, True)
    core_sums = [_add_halves(f"add_halves_{k}", g4_, r_, half, MXU if k in mxu_weights and k != "pool_w" else F32)
                 for k, g4_, r_ in zip(names, grads4, from_sibling)]
    from_chips = _to_owner_chips("grads_to_owner", core_sums)
    owned = [_add_chips(f"add_chips_{k}", s_, r_, chip) for k, s_, r_ in zip(names, core_sums, from_chips)]
    others = _to_sibling("halves_to_sibling", owned, False)

    out_grad, out_delta, out_m, out_v = {}, {}, {}, {}
    for k, own, oth in zip(names, owned, others):
        w_, m_, v_, shape = sharded[k]
        res = _adam_halves(f"adam_{k}", own, oth, half, w_.reshape(shape), m_.reshape(shape), v_.reshape(shape))
        out_grad[k], out_delta[k], out_m[k], out_v[k] = (t.reshape(w_.shape) for t in res)

    d_cb = jnp.stack([t[:, 3] for t in d_cw], axis=0).reshape(DEPTH * N_CHIPS * f // d, d)
    ln_rows = jnp.concatenate([jnp.stack([d_ln[which, layer][row] for layer in range(DEPTH)])
                               for which, row in (("ln1", 0), ("ln1", 1), ("ln2", 0), ("ln2", 1))])
    rows = jnp.concatenate([ln_rows, d_cb, jnp.broadcast_to(loss_tile[0:1, 0:1], (1, d))])
    pad = (-rows.shape[0]) % 8
    total = _sum_all_devices("sum_small", jnp.pad(rows, ((0, pad), (0, 0))))
    small = {"ln1_g": total[0:4], "ln1_b": total[4:8], "ln2_g": total[8:12], "ln2_b": total[12:16],
             "ffn_conv_b": total[16:16 + d_cb.shape[0]].reshape(ffn_conv_b.shape)}
    loss = total[16 + d_cb.shape[0], 0]
    small_in = {"ln1_g": (ln1_g, m_ln1_g, v_ln1_g), "ln1_b": (ln1_b, m_ln1_b, v_ln1_b), "ln2_g": (ln2_g, m_ln2_g, v_ln2_g),
                "ln2_b": (ln2_b, m_ln2_b, v_ln2_b), "ffn_conv_b": (ffn_conv_b, m_ffn_conv_b, v_ffn_conv_b)}
    for k, (w_, m_, v_) in small_in.items():
        out_grad[k] = small[k]
        out_delta[k], out_m[k], out_v[k] = _adam_small(f"adam_{k}", small[k], w_, m_, v_)

    order = ["pool_w", "pool_scale", "w_q", "w_kv", "w_o", "ffn_w_gate", "ffn_w_up", "ffn_conv_w", "ffn_conv_b",
             "ffn_w_down", "ln1_g", "ln1_b", "ln2_g", "ln2_b"]
    return (loss, grad_x, *[out_grad[k] for k in order], *[out_delta[k] for k in order],
            *[out_m[k] for k in order], *[out_v[k] for k in order])
```

```python
import functools
import math

import jax
import jax.numpy as jnp
from jax import lax
from jax.experimental import pallas as pl
from jax.experimental.pallas import tpu as pltpu

F32 = jnp.float32
BF16 = jnp.bfloat16
MXU = jnp.bfloat16

DEPTH = 4
N_POOL_LAYERS = 2
POOL_WINDOWS = (2, 4, 8, 16)
POOL_GROUP_DIM = 256
HEAD_DIM = 64
N_HEADS = 16
DILATIONS = (1, 4, 16)
SPAN = 128
ROPE_THETA = 10000.0
ALPHA = (2.0 * DEPTH) ** 0.25
LN_EPS = 1e-5
ADAM_LR, ADAM_B1, ADAM_B2, ADAM_EPS, ADAM_WD, ADAM_STEP = 0.001, 0.9, 0.999, 1e-08, 0.01, 10

N_CHIPS = 4
VMEM_LIMIT = 56 * 1024 * 1024
MESH = pl.DeviceIdType.MESH

NN = (((1,), (0,)), ((), ()))
NT = (((1,), (1,)), ((), ()))
TN = (((0,), (0,)), ((), ()))


def _params(sem=None):
    return pltpu.CompilerParams(dimension_semantics=sem, vmem_limit_bytes=VMEM_LIMIT)


def _chunks(n_rows, ch, fn, unroll=1):
    def step(i, carry):
        fn(pl.multiple_of(i * ch, ch))
        return carry

    lax.fori_loop(0, n_rows // ch, step, 0, unroll=unroll)


def _fold8(v):
    return jnp.sum(v.reshape(v.shape[0] // 8, 8, v.shape[1]), axis=0)


def _down(v, k):
    return pltpu.roll(v, k, 0)


def _up(v, k):
    return pltpu.roll(v, v.shape[0] - k, 0)


def _mm(name, pairs, dims, grid, out_shape, out_dtype, out_block, out_map, nk, into=None):
    n_pairs = len(pairs)
    kax = len(grid) - 1

    def body(*refs):
        o_ref = refs[2 * n_pairs + (1 if into is not None else 0)]
        part = None
        for p in range(n_pairs):
            t = lax.dot_general(refs[2 * p][...], refs[2 * p + 1][...], dims, preferred_element_type=F32)
            part = t if part is None else part + t
        if nk == 1:
            o_ref[...] = part.astype(o_ref.dtype)
        else:
            k = pl.program_id(kax)

            @pl.when(k == 0)
            def _():
                o_ref[...] = part

            @pl.when(k > 0)
            def _():
                o_ref[...] += part

    operands, in_specs = [], []
    for a, a_block, a_map, b, b_block, b_map in pairs:
        operands += [a, b]
        in_specs += [pl.BlockSpec(a_block, a_map), pl.BlockSpec(b_block, b_map)]
    aliases = {}
    if into is not None:
        operands.append(into)
        in_specs.append(pl.BlockSpec(memory_space=pl.ANY))
        aliases = {2 * n_pairs: 0}
    assert nk == 1 or out_dtype == F32
    sem = ("parallel",) * kax + ("arbitrary",)
    return pl.pallas_call(
        body, name=name, grid=grid, in_specs=in_specs, out_specs=pl.BlockSpec(out_block, out_map),
        out_shape=jax.ShapeDtypeStruct(out_shape, out_dtype), input_output_aliases=aliases,
        compiler_params=_params(sem),
    )(*operands)


TM = 2048


def _mm_cols(name, a, w, w_idx, out_dtype, cb=None):
    n, k = a.shape
    j_n, c = w.shape[0], w.shape[-1]
    cb = c if cb is None else cb
    s = c // cb
    wb = (None,) * (w.ndim - 2) + (k, cb)
    return _mm(name, [(a, (TM, k), lambda q, m, _: (m, 0), w, wb, lambda q, m, _: (q // s, *w_idx, 0, q % s))], NN,
               (j_n * s, n // TM, 1), (n, j_n * c), out_dtype, (TM, cb), lambda q, m, _: (m, q), 1)


def _mm_shard_out(name, a, w, w_idx, out_dtype):
    n, k = a.shape
    j_n, c = w.shape[0], w.shape[-1]
    wb = (None,) * (w.ndim - 2) + (k, c)
    return _mm(name, [(a, (TM, k), lambda j, m, _: (m, 0), w, wb, lambda j, m, _: (j, *w_idx, 0, 0))], NN,
               (j_n, n // TM, 1), (j_n, n, c), out_dtype, (None, TM, c), lambda j, m, _: (j, m, 0), 1)


def _mm_shard_in(name, a4, w, w_idx):
    j_n, n, c = a4.shape
    k = w.shape[-1]
    wb = (None,) * (w.ndim - 2) + (c, k)
    return _mm(name, [(a4, (None, TM, c), lambda m, j: (j, m, 0), w, wb, lambda m, j: (j, *w_idx, 0, 0))], NN,
               (n // TM, j_n), (n, k), F32, (TM, k), lambda m, j: (m, 0), j_n)


def _mm_nt_shard_out(name, a, w, w_idx):
    n, k = a.shape
    j_n, c = w.shape[0], w.shape[-2]
    wb = (None,) * (w.ndim - 2) + (c, k)
    return _mm(name, [(a, (TM, k), lambda j, m, _: (m, 0), w, wb, lambda j, m, _: (j, *w_idx, 0, 0))], NT,
               (j_n, n // TM, 1), (j_n, n, c), F32, (None, TM, c), lambda j, m, _: (j, m, 0), 1)


def _mm_nt_shard_in(name, terms):
    pairs = []
    for a4, w, w_idx in terms:
        j_n, n, c = a4.shape
        k = w.shape[-2]
        wb = (None,) * (w.ndim - 2) + (k, c)
        pairs.append((a4, (None, TM, c), lambda m, j: (j, m, 0), w, wb,
                      functools.partial(lambda m, j, w_idx: (j, *w_idx, 0, 0), w_idx=w_idx)))
    return _mm(name, pairs, NT, (n // TM, j_n), (n, k), F32, (TM, k), lambda m, j: (m, 0), j_n)


def _mm_nt_cols_in(name, a, w, w_idx, cb):
    n, ct = a.shape
    j_n, k, c = w.shape[0], w.shape[-2], w.shape[-1]
    s = c // cb
    wb = (None,) * (w.ndim - 2) + (k, cb)
    return _mm(name, [(a, (TM, cb), lambda m, q: (m, q), w, wb, lambda m, q: (q // s, *w_idx, 0, q % s))], NT,
               (n // TM, ct // cb), (n, k), F32, (TM, k), lambda m, q: (m, 0), ct // cb)


def _mm_tn(name, a, a_cols, b, b_cols, n_blocks, out_shape, out_block, out_map, into=None, a_lead=None, b_lead=None):
    n = a.shape[-2]
    a_nb = a.shape[-1] // a_cols
    b_nb = b.shape[-1] // b_cols
    if a_lead is None:
        a_block, a_map = (TM, a_cols), lambda q, t: (t, q if a_nb > 1 else 0)
    else:
        a_block, a_map = (None, TM, a_cols), lambda q, t: (*a_lead(q), t, 0)
    if b_lead is None:
        b_block, b_map = (TM, b_cols), lambda q, t: (t, q if b_nb > 1 else 0)
    else:
        b_block, b_map = (None, TM, b_cols), lambda q, t: (*b_lead(q), t, 0)
    return _mm(name, [(a, a_block, a_map, b, b_block, b_map)], TN, (n_blocks, n // TM), out_shape, F32,
               out_block, lambda q, t: out_map(q), n // TM, into=into)


LN_ROWS = 512
LN_CH = 16
LN_UNROLL = 4


def _ln_stats(r):
    mu = jnp.mean(r, -1, keepdims=True)
    xc = r - mu
    var = jnp.mean(xc * xc, -1, keepdims=True)
    return xc, lax.rsqrt(var + LN_EPS)


def _add_ln(name, a, mix, g, b):
    n, d = a.shape

    def body(a_ref, m_ref, g_ref, b_ref, r_ref, h_ref, hb_ref):
        gg, bb = g_ref[...], b_ref[...]

        def chunk(s):
            rows = pl.ds(s, LN_CH)
            r = ALPHA * a_ref[rows, :] + m_ref[rows, :]
            xc, rstd = _ln_stats(r)
            y = xc * rstd * gg + bb
            r_ref[rows, :] = r
            h_ref[rows, :] = y
            hb_ref[rows, :] = y.astype(MXU)

        _chunks(LN_ROWS, LN_CH, chunk, LN_UNROLL)

    tile = pl.BlockSpec((LN_ROWS, d), lambda i: (i, 0))
    vec = pl.BlockSpec((1, d), lambda i: (0, 0))
    return pl.pallas_call(
        body, name=name, grid=(n // LN_ROWS,), in_specs=[tile, tile, vec, vec], out_specs=[tile, tile, tile],
        out_shape=[jax.ShapeDtypeStruct((n, d), F32), jax.ShapeDtypeStruct((n, d), F32), jax.ShapeDtypeStruct((n, d), MXU)],
        compiler_params=_params(("parallel",)),
    )(a, mix, g, b)


def _ln_bwd(name, r, g, d_a, scale_a, d_rest):
    n, d = r.shape
    n_rest = len(d_rest)
    steps = n // LN_ROWS

    def body(*refs):
        r_ref, g_ref, da_ref = refs[:3]
        rest = refs[3:3 + n_rest]
        dr_ref, drb_ref, gb_ref, acc = refs[3 + n_rest:]
        i = pl.program_id(0)

        @pl.when(i == 0)
        def _():
            acc[...] = jnp.zeros_like(acc)

        gg = g_ref[...]

        def chunk(s):
            rows = pl.ds(s, LN_CH)
            xc, rstd = _ln_stats(r_ref[rows, :])
            xhat = xc * rstd
            dy = da_ref[rows, :] if scale_a == 1.0 else scale_a * da_ref[rows, :]
            for t in rest:
                dy = dy + t[rows, :]
            dyg = dy * gg
            m1 = jnp.mean(dyg, -1, keepdims=True)
            m2 = jnp.mean(dyg * xhat, -1, keepdims=True)
            dr = rstd * (dyg - m1 - xhat * m2)
            dr_ref[rows, :] = dr
            drb_ref[rows, :] = dr.astype(MXU)
            acc[0] += _fold8(dy * xhat)
            acc[1] += _fold8(dy)

        _chunks(LN_ROWS, LN_CH, chunk, LN_UNROLL)

        @pl.when(i == steps - 1)
        def _():
            gb_ref[0:1, :] = jnp.sum(acc[0], axis=0, keepdims=True)
            gb_ref[1:2, :] = jnp.sum(acc[1], axis=0, keepdims=True)

    tile = pl.BlockSpec((LN_ROWS, d), lambda i: (i, 0))
    vec = pl.BlockSpec((1, d), lambda i: (0, 0))
    return pl.pallas_call(
        body, name=name, grid=(steps,), in_specs=[tile, vec, tile] + [tile] * n_rest,
        out_specs=[tile, tile, pl.BlockSpec((2, d), lambda i: (0, 0))],
        out_shape=[jax.ShapeDtypeStruct((n, d), F32), jax.ShapeDtypeStruct((n, d), MXU), jax.ShapeDtypeStruct((2, d), F32)],
        scratch_shapes=[pltpu.VMEM((2, 8, d), F32)],
        compiler_params=_params(("arbitrary",)),
    )(r, g, d_a, *d_rest)


FFN_ROWS = 512
FFN_CH = 32
GELU_C1 = math.sqrt(2.0 / math.pi)
GELU_C2 = 0.044715


def _conv3(v, prev8, w0, w1, w2, bias):
    n = v.shape[0]
    ext = jnp.concatenate([prev8, v], axis=0)
    g1 = _down(ext, 1)[8:8 + n]
    g2 = _down(ext, 2)[8:8 + n]
    return bias + w0 * g2 + w1 * g1 + w2 * v, g1, g2


def _glu_fwd(name, g4, u4, conv_w, conv_b):
    j_n, b_n, s_n, f = g4.shape
    tiles = s_n // FFN_ROWS

    def body(g_ref, halo_ref, u_ref, w_ref, b_ref, o_ref, gs):
        s = pl.program_id(2)
        gs[0:8, :] = jnp.where(s > 0, halo_ref[...], 0.0)
        gs[8:, :] = g_ref[...]
        w0, w1, w2, bias = w_ref[0:1, :], w_ref[1:2, :], w_ref[2:3, :], b_ref[...]

        def chunk(st):
            v = gs[pl.ds(pl.multiple_of(st + 8, 8), FFN_CH), :]
            conv, _, _ = _conv3(v, gs[pl.ds(st, 8), :], w0, w1, w2, bias)
            cdf = 0.5 * (1.0 + jnp.tanh(GELU_C1 * (conv + GELU_C2 * (conv * conv * conv))))
            o_ref[pl.ds(st, FFN_CH), :] = (conv * cdf * u_ref[pl.ds(st, FFN_CH), :]).astype(o_ref.dtype)

        _chunks(FFN_ROWS, FFN_CH, chunk)

    tile = pl.BlockSpec((None, None, FFN_ROWS, f), lambda j, b, s: (j, b, s, 0))
    halo = pl.BlockSpec((None, None, 8, f), lambda j, b, s: (j, b, jnp.maximum(s * (FFN_ROWS // 8) - 1, 0), 0))
    return pl.pallas_call(
        body, name=name, grid=(j_n, b_n, tiles),
        in_specs=[tile, halo, tile,
                  pl.BlockSpec((None, 3, f), lambda j, b, s: (j, 0, 0)),
                  pl.BlockSpec((None, 1, f), lambda j, b, s: (j, 0, 0))],
        out_specs=tile, out_shape=jax.ShapeDtypeStruct(g4.shape, MXU),
        scratch_shapes=[pltpu.VMEM((8 + FFN_ROWS, f), F32)],
        compiler_params=_params(("parallel", "parallel", "parallel")),
    )(g4, g4, u4, conv_w, conv_b)


def _glu_bwd(name, dh4, g4, u4, conv_w, conv_b):
    j_n, b_n, s_n, f = g4.shape
    tiles = s_n // FFN_ROWS
    ext = FFN_CH + 8

    def body(d_ref, dnext_ref, g_ref, gprev_ref, gnext_ref, u_ref, unext_ref, w_ref, b_ref,
             dg_ref, du_ref, wb_ref, gs, us, ds, acc):
        b, s = pl.program_id(1), pl.program_id(2)
        last = s == tiles - 1

        @pl.when((b == 0) & (s == 0))
        def _():
            acc[...] = jnp.zeros_like(acc)

        gs[0:8, :] = jnp.where(s > 0, gprev_ref[...], 0.0)
        gs[8:8 + FFN_ROWS, :] = g_ref[...]
        gs[8 + FFN_ROWS:, :] = gnext_ref[...]
        us[0:FFN_ROWS, :] = u_ref[...]
        us[FFN_ROWS:, :] = unext_ref[...]
        ds[0:FFN_ROWS, :] = d_ref[...]
        ds[FFN_ROWS:, :] = jnp.where(last, 0.0, dnext_ref[...])
        w0, w1, w2, bias = w_ref[0:1, :], w_ref[1:2, :], w_ref[2:3, :], b_ref[...]

        def chunk(st):
            v = gs[pl.ds(pl.multiple_of(st + 8, 8), ext), :]
            conv, g1, g2 = _conv3(v, gs[pl.ds(st, 8), :], w0, w1, w2, bias)
            th = jnp.tanh(GELU_C1 * (conv + GELU_C2 * (conv * conv * conv)))
            cdf = 0.5 * (1.0 + th)
            dact = cdf + conv * (0.5 * GELU_C1) * (1.0 - th * th) * (1.0 + (3.0 * GELU_C2) * (conv * conv))
            de = ds[pl.ds(st, ext), :]
            dconv = de * us[pl.ds(st, ext), :] * dact
            du_ref[pl.ds(st, FFN_CH), :] = (de * (conv * cdf))[:FFN_CH].astype(du_ref.dtype)
            dg = w2 * dconv + w1 * _up(dconv, 1) + w0 * _up(dconv, 2)
            dg_ref[pl.ds(st, FFN_CH), :] = dg[:FFN_CH].astype(dg_ref.dtype)
            dc = dconv[:FFN_CH]
            acc[0] += _fold8(dc * g2[:FFN_CH])
            acc[1] += _fold8(dc * g1[:FFN_CH])
            acc[2] += _fold8(dc * v[:FFN_CH])
            acc[3] += _fold8(dc)

        _chunks(FFN_ROWS, FFN_CH, chunk)

        @pl.when((b == b_n - 1) & last)
        def _():
            for k in range(4):
                wb_ref[k:k + 1, :] = jnp.sum(acc[k], axis=0, keepdims=True)

    blocks8 = FFN_ROWS // 8
    tile = pl.BlockSpec((None, None, FFN_ROWS, f), lambda j, b, s: (j, b, s, 0))
    prev = pl.BlockSpec((None, None, 8, f), lambda j, b, s: (j, b, jnp.maximum(s * blocks8 - 1, 0), 0))
    nxt = pl.BlockSpec((None, None, 8, f), lambda j, b, s: (j, b, jnp.minimum((s + 1) * blocks8, s_n // 8 - 1), 0))
    return pl.pallas_call(
        body, name=name, grid=(j_n, b_n, tiles),
        in_specs=[tile, nxt, tile, prev, nxt, tile, nxt,
                  pl.BlockSpec((None, 3, f), lambda j, b, s: (j, 0, 0)),
                  pl.BlockSpec((None, 1, f), lambda j, b, s: (j, 0, 0))],
        out_specs=[tile, tile, pl.BlockSpec((None, 4, f), lambda j, b, s: (j, 0, 0))],
        out_shape=[jax.ShapeDtypeStruct(g4.shape, MXU), jax.ShapeDtypeStruct(g4.shape, MXU),
                   jax.ShapeDtypeStruct((j_n, 4, f), F32)],
        scratch_shapes=[pltpu.VMEM((16 + FFN_ROWS, f), F32), pltpu.VMEM((8 + FFN_ROWS, f), F32),
                        pltpu.VMEM((8 + FFN_ROWS, f), F32), pltpu.VMEM((4, 8, f), F32)],
        compiler_params=_params(("parallel", "arbitrary", "arbitrary")),
    )(dh4, dh4, g4, g4, g4, u4, u4, conv_w, conv_b)


POOL_ROWS = 512
POOL_CH = 32
POOL_HALO = 16


def _pool_windows(v, t0, gi, causal):
    shift = _down if causal else _up
    acc, k = v, 1
    while k < POOL_WINDOWS[gi]:
        acc = acc + shift(acc, k)
        k *= 2
    return acc


def _count(t0, n, w):
    t = t0 + lax.broadcasted_iota(jnp.int32, (n, 1), 0)
    return jnp.minimum(t + 1, w).astype(F32)


def _pooled_into(xs, pooled, t_tile):
    def chunk(st):
        for gi, w in enumerate(POOL_WINDOWS):
            cols = slice(gi * POOL_GROUP_DIM, (gi + 1) * POOL_GROUP_DIM)
            v = xs[pl.ds(st, POOL_CH + POOL_HALO), cols]
            sums = _pool_windows(v, None, gi, True)[POOL_HALO:]
            val = sums / _count(t_tile + st, POOL_CH, w) - v[POOL_HALO:]
            pooled[pl.ds(st, POOL_CH), cols] = val.astype(pooled.dtype)

    _chunks(POOL_ROWS, POOL_CH, chunk)


def _pool_specs(b_n, s_n, d):
    per = POOL_ROWS // POOL_HALO
    tile = pl.BlockSpec((None, POOL_ROWS, d), lambda b, s: (b, s, 0))
    prev = pl.BlockSpec((None, POOL_HALO, d), lambda b, s: (b, jnp.maximum(s * per - 1, 0), 0))
    nxt = pl.BlockSpec((None, POOL_HALO, d), lambda b, s: (b, jnp.minimum((s + 1) * per, s_n // POOL_HALO - 1), 0))
    return tile, prev, nxt


def _pool_fwd(name, h3, w, scale):
    b_n, s_n, d = h3.shape
    tile, prev, _ = _pool_specs(b_n, s_n, d)

    def body(h_ref, halo_ref, w_ref, sc_ref, o_ref, xs, pooled):
        s = pl.program_id(1)
        xs[0:POOL_HALO, :] = jnp.where(s > 0, halo_ref[...], 0.0)
        xs[POOL_HALO:, :] = h_ref[...]
        _pooled_into(xs, pooled, s * POOL_ROWS)
        for gi in range(len(POOL_WINDOWS)):
            cols = slice(gi * POOL_GROUP_DIM, (gi + 1) * POOL_GROUP_DIM)
            y = jnp.dot(pooled[:, cols], w_ref[gi], preferred_element_type=F32)
            o_ref[:, cols] = y * sc_ref[:, cols]

    return pl.pallas_call(
        body, name=name, grid=(b_n, s_n // POOL_ROWS),
        in_specs=[tile, prev, pl.BlockSpec(w.shape, lambda b, s: (0, 0, 0)), pl.BlockSpec((1, d), lambda b, s: (0, 0))],
        out_specs=tile, out_shape=jax.ShapeDtypeStruct(h3.shape, F32),
        scratch_shapes=[pltpu.VMEM((POOL_HALO + POOL_ROWS, d), F32), pltpu.VMEM((POOL_ROWS, d), MXU)],
        compiler_params=_params(("parallel", "parallel")),
    )(h3, h3, w, scale)


def _pool_bwd(name, h3, dm3, w, scale):
    b_n, s_n, d = h3.shape
    tile, prev, nxt = _pool_specs(b_n, s_n, d)
    tiles = s_n // POOL_ROWS
    ext = POOL_ROWS + POOL_HALO

    def body(h_ref, halo_ref, dm_ref, dnext_ref, w_ref, sc_ref, dh_ref, dw_ref, dsc_ref, xs, pooled, ds, dp):
        b, s = pl.program_id(0), pl.program_id(1)

        @pl.when((b == 0) & (s == 0))
        def _():
            dw_ref[...] = jnp.zeros_like(dw_ref)
            dsc_ref[...] = jnp.zeros_like(dsc_ref)

        xs[0:POOL_HALO, :] = jnp.where(s > 0, halo_ref[...], 0.0)
        xs[POOL_HALO:, :] = h_ref[...]
        ds[0:POOL_ROWS, :] = dm_ref[...]
        ds[POOL_ROWS:, :] = jnp.where(s == tiles - 1, 0.0, dnext_ref[...])
        _pooled_into(xs, pooled, s * POOL_ROWS)
        for gi in range(len(POOL_WINDOWS)):
            cols = slice(gi * POOL_GROUP_DIM, (gi + 1) * POOL_GROUP_DIM)
            dyb = (ds[:, cols] * sc_ref[:, cols]).astype(MXU)
            dp[:, cols] = lax.dot_general(dyb, w_ref[gi], NT, preferred_element_type=F32)
            pg = pooled[:, cols]
            dw_ref[gi] += lax.dot_general(pg, dyb[:POOL_ROWS], TN, preferred_element_type=F32)
            ypre = jnp.dot(pg, w_ref[gi], preferred_element_type=F32)
            dsc_ref[:, cols] += jnp.sum(ds[0:POOL_ROWS, cols] * ypre, axis=0, keepdims=True)

        def chunk(st):
            for gi, w_len in enumerate(POOL_WINDOWS):
                cols = slice(gi * POOL_GROUP_DIM, (gi + 1) * POOL_GROUP_DIM)
                v = dp[pl.ds(st, POOL_CH + POOL_HALO), cols]
                q = v / _count(s * POOL_ROWS + st, POOL_CH + POOL_HALO, w_len)
                back = _pool_windows(q, None, gi, False)[:POOL_CH] - v[:POOL_CH]
                dh_ref[pl.ds(st, POOL_CH), cols] = ALPHA * ds[pl.ds(st, POOL_CH), cols] + back

        _chunks(POOL_ROWS, POOL_CH, chunk)

    return pl.pallas_call(
        body, name=name, grid=(b_n, tiles),
        in_specs=[tile, prev, tile, nxt, pl.BlockSpec(w.shape, lambda b, s: (0, 0, 0)), pl.BlockSpec((1, d), lambda b, s: (0, 0))],
        out_specs=[tile, pl.BlockSpec(w.shape, lambda b, s: (0, 0, 0)), pl.BlockSpec((1, d), lambda b, s: (0, 0))],
        out_shape=[jax.ShapeDtypeStruct(h3.shape, F32), jax.ShapeDtypeStruct(w.shape, F32), jax.ShapeDtypeStruct((1, d), F32)],
        scratch_shapes=[pltpu.VMEM((POOL_HALO + POOL_ROWS, d), F32), pltpu.VMEM((POOL_ROWS, d), MXU),
                        pltpu.VMEM((ext, d), F32), pltpu.VMEM((ext, d), F32)],
        compiler_params=_params(("arbitrary", "arbitrary")),
    )(h3, h3, dm3, dm3, w, scale)


ROPE_ROWS = 256


def _rope_tables(s_n):
    inv_freq = ROPE_THETA ** (-jnp.arange(0, HEAD_DIM, 2, dtype=F32) / HEAD_DIM)
    ang = jnp.arange(s_n, dtype=F32)[:, None] * inv_freq[None, :]
    cos, sin = jnp.cos(ang), jnp.sin(ang)
    cos_l = jnp.tile(cos, (1, 4))
    sin_l = jnp.tile(jnp.concatenate([-sin, sin], axis=1), (1, 2))
    return cos_l, sin_l


def _rope(name, x3, col0, n_col, cos_l, sin_l, scale, out_dtype, rotate=True, out_cols=None, out_col0=0, into=None):
    b_n, s_n, _ = x3.shape
    cw = N_HEADS * HEAD_DIM
    out_cols = n_col if out_cols is None else out_cols

    def body(x_ref, c_ref, s_ref, *rest):
        o_ref = rest[-1]
        lane = lax.broadcasted_iota(jnp.int32, (ROPE_ROWS, 128), 1)
        first_half = (lane % HEAD_DIM) < (HEAD_DIM // 2)
        cos, sin = c_ref[...], s_ref[...]
        for cb in range(cw // 128):
            cols = slice(cb * 128, (cb + 1) * 128)
            y = x_ref[:, cols]
            if rotate:
                other = jnp.where(first_half, pltpu.roll(y, 128 - HEAD_DIM // 2, 1), pltpu.roll(y, HEAD_DIM // 2, 1))
                y = y * cos + other * sin
            o_ref[:, cols] = (y if scale == 1.0 else y * scale).astype(o_ref.dtype)

    tile = pl.BlockSpec((None, ROPE_ROWS, cw), lambda b, s, c: (b, s, col0 + c))
    tab = pl.BlockSpec((ROPE_ROWS, 128), lambda b, s, c: (s, 0))
    extra, extra_specs, aliases = [], [], {}
    if into is not None:
        extra, extra_specs, aliases = [into], [pl.BlockSpec(memory_space=pl.ANY)], {3: 0}
    return pl.pallas_call(
        body, name=name, grid=(b_n, s_n // ROPE_ROWS, n_col), in_specs=[tile, tab, tab] + extra_specs,
        out_specs=pl.BlockSpec((None, ROPE_ROWS, cw), lambda b, s, c: (b, s, out_col0 + c)),
        out_shape=jax.ShapeDtypeStruct((b_n, s_n, out_cols * cw), out_dtype), input_output_aliases=aliases,
        compiler_params=_params(("parallel", "parallel", "parallel")),
    )(x3, cos_l, sin_l, *extra)


def _by_residue(a, dil):
    b_n, s_n, c = a.shape
    return a.reshape(b_n, s_n // dil, dil * c)


LSE_LANES = 128


def _attn_fwd(name, q, k, v, group, dil):
    b_n, s_n, _ = k.shape
    cw = N_HEADS * HEAD_DIM
    gq, gk = q.shape[-1] // cw, k.shape[-1] // cw
    nb = s_n // dil // SPAN
    with_prev = nb > 1

    keys = 2 * SPAN if with_prev else SPAN
    pair = 2 * HEAD_DIM

    def body(*refs):
        if with_prev:
            q_ref, kc_ref, kp_ref, vc_ref, vp_ref, o_ref, lse_ref, k_all, v_all, s_buf, m_buf = refs
            k_all[0:SPAN, :] = kp_ref[...]
            k_all[SPAN:, :] = kc_ref[...]
            v_all[0:SPAN, :] = vp_ref[...]
            v_all[SPAN:, :] = vc_ref[...]
        else:
            q_ref, k_all, v_all, o_ref, lse_ref, s_buf, m_buf = refs
        n = pl.program_id(2)
        qi = lax.broadcasted_iota(jnp.int32, (SPAN, keys), 0)
        kj = lax.broadcasted_iota(jnp.int32, (SPAN, keys), 1)
        if with_prev:
            back = jnp.where(n != 0, 0, 2 * SPAN)
            mask = ((kj < SPAN) & (kj >= qi + back)) | ((kj >= SPAN) & (kj - SPAN <= qi))
        else:
            mask = kj <= qi
        lane = lax.broadcasted_iota(jnp.int32, (SPAN, pair), 1)
        low = lane < HEAD_DIM
        ones = jnp.ones((keys, pair), MXU)
        for h in range(N_HEADS):
            grp = slice((h // 2) * pair, (h // 2 + 1) * pair)
            q2 = q_ref[:, grp]
            qh = jnp.where(low if h % 2 == 0 else ~low, q2, jnp.zeros_like(q2))
            s = jnp.where(mask, lax.dot_general(qh, k_all[:, grp], NT, preferred_element_type=F32), -jnp.inf)
            s_buf[h] = s
            m_buf[h] = jnp.broadcast_to(jnp.max(s, -1, keepdims=True), (SPAN, pair))
        lse_all = jnp.zeros((SPAN, pair), F32)
        for hp in range(N_HEADS // 2):
            grp = slice(hp * pair, (hp + 1) * pair)
            v2 = v_all[:, grp]
            halves = []
            for h in (2 * hp, 2 * hp + 1):
                m = m_buf[h]
                p = jnp.exp(s_buf[h] - jnp.tile(m, (1, keys // pair))).astype(MXU)
                tot = jnp.dot(p, ones, preferred_element_type=F32)
                halves.append(jnp.dot(p, v2, preferred_element_type=F32) / tot)
                lse_all = jnp.where(lane == h, m + jnp.log(tot), lse_all)
            o_ref[:, grp] = jnp.where(low, halves[0], halves[1])
        lse_ref[...] = lse_all

    assert pair == LSE_LANES
    blk = (None, SPAN, cw)
    cur = lambda b, r, n: (b, n, r * gk + group)
    prv = lambda b, r, n: (b, jnp.maximum(n - 1, 0), r * gk + group)
    out_at = lambda b, r, n: (b, n, r)
    kv_specs = [pl.BlockSpec(blk, cur), pl.BlockSpec(blk, prv)] if with_prev else [pl.BlockSpec(blk, cur)]
    qv, kv_, vv = _by_residue(q, dil), _by_residue(k, dil), _by_residue(v, dil)
    operands = [qv, kv_, kv_, vv, vv] if with_prev else [qv, kv_, vv]
    stage = [pltpu.VMEM((keys, cw), MXU)] * 2 if with_prev else []
    o, lse = pl.pallas_call(
        body, name=name, grid=(b_n, dil, nb),
        in_specs=[pl.BlockSpec(blk, lambda b, r, n: (b, n, r * gq + group))] + kv_specs + kv_specs,
        out_specs=[pl.BlockSpec(blk, out_at), pl.BlockSpec((None, SPAN, LSE_LANES), out_at)],
        out_shape=[jax.ShapeDtypeStruct((b_n, s_n // dil, dil * cw), F32),
                   jax.ShapeDtypeStruct((b_n, s_n // dil, dil * LSE_LANES), F32)],
        scratch_shapes=stage + [pltpu.VMEM((N_HEADS, SPAN, keys), F32), pltpu.VMEM((N_HEADS, SPAN, pair), F32)],
        compiler_params=_params(("parallel", "parallel", "parallel")),
    )(*operands)
    return o.reshape(b_n, s_n, cw), lse.reshape(b_n, s_n, LSE_LANES)


def _attn_bwd(name, q, k, v, do, lse, dd, group, dil, prev=None, into=None):
    b_n, s_n, _ = k.shape
    cw = N_HEADS * HEAD_DIM
    gq, gk = q.shape[-1] // cw, k.shape[-1] // cw
    nb = s_n // dil // SPAN
    with_next = nb > 1
    accumulate = prev is not None
    rows = 2 * SPAN if with_next else SPAN

    def body(*refs):
        refs = list(refs)
        qc_ref, doc_ref, lsec_ref, ddc_ref = refs[:4]
        del refs[:4]
        if with_next:
            qn_ref, don_ref, lsen_ref, ddn_ref = refs[:4]
            del refs[:4]
        k_ref, v_ref = refs[:2]
        del refs[:2]
        if accumulate:
            dkp_ref, dvp_ref = refs[:2]
            del refs[:2]
        if into is not None:
            del refs[:3]
        dq_ref, dk_ref, dv_ref = refs[:3]
        del refs[:3]
        if with_next:
            carry, q_all, do_all, side = refs[:4]
            del refs[:4]
            q_all[0:SPAN, :] = qc_ref[...]
            q_all[SPAN:, :] = qn_ref[...]
            do_all[0:SPAN, :] = doc_ref[...]
            do_all[SPAN:, :] = don_ref[...]
            side[0, 0:SPAN, :] = lsec_ref[...]
            side[0, SPAN:, :] = lsen_ref[...]
            side[1, 0:SPAN, :] = ddc_ref[...]
            side[1, SPAN:, :] = ddn_ref[...]
            lse_at = lambda h: side[0, :, h:h + 1]
            dd_at = lambda h: side[1, :, h:h + 1]
        else:
            q_all, do_all = qc_ref, doc_ref
            lse_at = lambda h: lsec_ref[:, h:h + 1]
            dd_at = lambda h: ddc_ref[:, h:h + 1]
        p_buf, ds_buf = refs
        n = pl.program_id(2)
        qi = lax.broadcasted_iota(jnp.int32, (rows, SPAN), 0)
        kj = lax.broadcasted_iota(jnp.int32, (rows, SPAN), 1)
        if with_next:
            first = n == 0
            reach = jnp.where(n + 1 < nb, SPAN, -2 * SPAN)
            mask = ((qi < SPAN) & (kj <= qi)) | ((qi >= SPAN) & (kj >= qi - reach))
        else:
            mask = kj <= qi
        pair = 2 * HEAD_DIM
        low = lax.broadcasted_iota(jnp.int32, (rows, pair), 1) < HEAD_DIM
        low_k = lax.broadcasted_iota(jnp.int32, (SPAN, pair), 1) < HEAD_DIM

        def pick(v, h, low_mask):
            return jnp.where(low_mask if h % 2 == 0 else ~low_mask, v, jnp.zeros_like(v))

        for h in range(N_HEADS):
            grp = slice((h // 2) * pair, (h // 2 + 1) * pair)
            s = lax.dot_general(pick(q_all[:, grp], h, low), k_ref[:, grp], NT, preferred_element_type=F32)
            p = jnp.where(mask, jnp.exp(s - lse_at(h)), 0.0)
            dp = lax.dot_general(pick(do_all[:, grp], h, low), v_ref[:, grp], NT, preferred_element_type=F32)
            p_buf[h] = p.astype(MXU)
            ds_buf[h] = (p * (dp - dd_at(h))).astype(MXU)
        for hp in range(N_HEADS // 2):
            grp = slice(hp * pair, (hp + 1) * pair)
            q2, do2, k2 = q_all[:, grp], do_all[:, grp], k_ref[:, grp]
            dv = dk = dq2 = None
            for h in (2 * hp, 2 * hp + 1):
                t_dv = lax.dot_general(p_buf[h], pick(do2, h, low), TN, preferred_element_type=F32)
                t_dk = lax.dot_general(ds_buf[h], pick(q2, h, low), TN, preferred_element_type=F32)
                t_dq = jnp.dot(ds_buf[h], pick(k2, h, low_k), preferred_element_type=F32)
                dv = t_dv if dv is None else dv + t_dv
                dk = t_dk if dk is None else dk + t_dk
                dq2 = t_dq if dq2 is None else dq2 + t_dq
            if accumulate:
                dk = dk + dkp_ref[:, grp]
                dv = dv + dvp_ref[:, grp]
            dk_ref[:, grp] = dk
            dv_ref[:, grp] = dv
            if with_next:
                dq_ref[:, grp] = dq2[:SPAN] + jnp.where(first, 0.0, carry[:, grp])
                carry[:, grp] = dq2[SPAN:]
            else:
                dq_ref[:, grp] = dq2

    blk, hblk = (None, SPAN, cw), (None, SPAN, LSE_LANES)
    at_n = lambda n: n
    at_next = lambda n: jnp.minimum(n + 1, nb - 1)
    q_specs = lambda m: [pl.BlockSpec(blk, lambda b, r, n: (b, m(n), r * gq + group)),
                         pl.BlockSpec(blk, lambda b, r, n: (b, m(n), r)),
                         pl.BlockSpec(hblk, lambda b, r, n: (b, m(n), r)), pl.BlockSpec(hblk, lambda b, r, n: (b, m(n), r))]
    wide = pl.BlockSpec(blk, lambda b, r, n: (b, n, r * gk + group))
    in_specs = q_specs(at_n) + (q_specs(at_next) if with_next else []) + [wide] * (4 if accumulate else 2)
    views = [_by_residue(t, dil) for t in (q, do, lse, dd)]
    operands = views * (2 if with_next else 1) + [_by_residue(k, dil), _by_residue(v, dil)]
    operands += [_by_residue(t, dil) for t in prev] if accumulate else []
    aliases = {}
    if into is not None:
        aliases = {len(operands) + i: i for i in range(3)}
        operands += [_by_residue(t, dil) for t in into]
        in_specs += [pl.BlockSpec(memory_space=pl.ANY)] * 3
    out = jax.ShapeDtypeStruct((b_n, s_n // dil, dil * gk * cw), F32)
    res = pl.pallas_call(
        body, name=name, grid=(b_n, dil, nb), in_specs=in_specs, out_specs=[wide] * 3,
        out_shape=[out, out, out], input_output_aliases=aliases,
        scratch_shapes=([pltpu.VMEM((SPAN, cw), F32), pltpu.VMEM((rows, cw), MXU), pltpu.VMEM((rows, cw), MXU),
                         pltpu.VMEM((2, rows, LSE_LANES), F32)] if with_next else [])
        + [pltpu.VMEM((N_HEADS, rows, SPAN), MXU)] * 2,
        compiler_params=_params(("parallel", "parallel", "arbitrary")),
    )(*operands)
    return tuple(t.reshape(b_n, s_n, gk * cw) for t in res)


MIX_ROWS = 256


def _group_weights(ls):
    m = functools.reduce(jnp.maximum, ls)
    es = [jnp.exp(l - m) for l in ls]
    tot = functools.reduce(lambda a, b: a + b, es)
    return [e / tot for e in es]


def _attn_mix(name, outs, lses):
    n, cw = outs[0].shape
    g_n = len(outs)

    def body(*refs):
        o_refs, l_refs, out_ref = refs[:g_n], refs[g_n:2 * g_n], refs[2 * g_n]
        ws = _group_weights([r[...] for r in l_refs])
        for h in range(N_HEADS):
            cols = slice(h * HEAD_DIM, (h + 1) * HEAD_DIM)
            acc = None
            for g in range(g_n):
                t = ws[g][:, h:h + 1] * o_refs[g][:, cols]
                acc = t if acc is None else acc + t
            out_ref[:, cols] = acc.astype(out_ref.dtype)

    tile = pl.BlockSpec((MIX_ROWS, cw), lambda i: (i, 0))
    htile = pl.BlockSpec((MIX_ROWS, LSE_LANES), lambda i: (i, 0))
    return pl.pallas_call(
        body, name=name, grid=(n // MIX_ROWS,), in_specs=[tile] * g_n + [htile] * g_n, out_specs=tile,
        out_shape=jax.ShapeDtypeStruct((n, cw), MXU), compiler_params=_params(("parallel",)),
    )(*outs, *lses)


def _attn_mix_bwd(name, do, outs, lses):
    n, cw = do.shape
    g_n = len(outs)

    def body(*refs):
        do_ref, o_refs, l_refs = refs[0], refs[1:1 + g_n], refs[1 + g_n:1 + 2 * g_n]
        dog_refs, dd_refs = refs[1 + 2 * g_n:1 + 3 * g_n], refs[1 + 3 * g_n:]
        ws = _group_weights([r[...] for r in l_refs])
        for r in dd_refs:
            r[...] = jnp.zeros_like(r)
        for h in range(N_HEADS):
            cols = slice(h * HEAD_DIM, (h + 1) * HEAD_DIM)
            dh = do_ref[:, cols]
            o = None
            for g in range(g_n):
                t = ws[g][:, h:h + 1] * o_refs[g][:, cols]
                o = t if o is None else o + t
            dot = jnp.sum(dh * o, -1, keepdims=True)
            for g in range(g_n):
                wg = ws[g][:, h:h + 1]
                dog_refs[g][:, cols] = (wg * dh).astype(dog_refs[g].dtype)
                dd_refs[g][:, h:h + 1] = wg * dot

    tile = pl.BlockSpec((MIX_ROWS, cw), lambda i: (i, 0))
    htile = pl.BlockSpec((MIX_ROWS, LSE_LANES), lambda i: (i, 0))
    return pl.pallas_call(
        body, name=name, grid=(n // MIX_ROWS,), in_specs=[tile] * (1 + g_n) + [htile] * g_n,
        out_specs=[tile] * g_n + [htile] * g_n,
        out_shape=[jax.ShapeDtypeStruct((n, cw), MXU)] * g_n + [jax.ShapeDtypeStruct((n, LSE_LANES), F32)] * g_n,
        compiler_params=_params(("parallel",)),
    )(do, *outs, *lses)


def _loss_head(name, y, target):
    n, d = y.shape
    steps = n // LN_ROWS

    def body(y_ref, t_ref, dy_ref, l_ref, acc):
        i = pl.program_id(0)

        @pl.when(i == 0)
        def _():
            acc[...] = jnp.zeros_like(acc)

        def chunk(s):
            rows = pl.ds(s, LN_CH)
            err = y_ref[rows, :] - t_ref[rows, :]
            dy_ref[rows, :] = err / d
            acc[...] += _fold8(err * err)

        _chunks(LN_ROWS, LN_CH, chunk, LN_UNROLL)

        @pl.when(i == steps - 1)
        def _():
            l_ref[...] = jnp.full((8, 128), 0.5 / d, F32) * jnp.sum(acc[...])

    tile = pl.BlockSpec((LN_ROWS, d), lambda i: (i, 0))
    return pl.pallas_call(
        body, name=name, grid=(steps,), in_specs=[tile, tile],
        out_specs=[tile, pl.BlockSpec((8, 128), lambda i: (0, 0))],
        out_shape=[jax.ShapeDtypeStruct((n, d), F32), jax.ShapeDtypeStruct((8, 128), F32)],
        scratch_shapes=[pltpu.VMEM((8, d), F32)], compiler_params=_params(("arbitrary",)),
    )(y, target)


EW_TILE_BYTES = 1 << 20


def _row_tile(rows, cols):
    tr = 8
    while rows % (2 * tr) == 0 and 2 * tr * cols * 4 <= EW_TILE_BYTES:
        tr *= 2
    return tr if rows % tr == 0 else rows


def _add_halves(name, grad, recv, half, out_dtype):
    j_n, _, r, c = grad.shape
    tr = _row_tile(r, c)

    def body(half_ref, g_ref, r_ref, o_ref):
        o_ref[...] = (g_ref[...] + r_ref[...]).astype(o_ref.dtype)

    return pl.pallas_call(
        body, name=name, out_shape=jax.ShapeDtypeStruct((j_n, r, c), out_dtype),
        grid_spec=pltpu.PrefetchScalarGridSpec(
            num_scalar_prefetch=1, grid=(j_n, r // tr),
            in_specs=[pl.BlockSpec((None, None, tr, c), lambda j, i, hf: (j, hf[0], i, 0)),
                      pl.BlockSpec((None, tr, c), lambda j, i, hf: (j, i, 0))],
            out_specs=pl.BlockSpec((None, tr, c), lambda j, i, hf: (j, i, 0))),
        compiler_params=_params(("parallel", "parallel")),
    )(half, grad, recv)


def _add_chips(name, mine, recv, chip):
    j_n, r, c = mine.shape
    tr = _row_tile(r, c)

    def body(chip_ref, m_ref, r_ref, o_ref):
        total = m_ref[...].astype(F32)
        for k in range(j_n - 1):
            total = total + r_ref[k].astype(F32)
        o_ref[...] = total

    return pl.pallas_call(
        body, name=name, out_shape=jax.ShapeDtypeStruct((r, c), F32),
        grid_spec=pltpu.PrefetchScalarGridSpec(
            num_scalar_prefetch=1, grid=(r // tr,),
            in_specs=[pl.BlockSpec((None, tr, c), lambda i, ch: (ch[0], i, 0)),
                      pl.BlockSpec((j_n - 1, tr, c), lambda i, ch: (0, i, 0))],
            out_specs=pl.BlockSpec((tr, c), lambda i, ch: (i, 0))),
        compiler_params=_params(("parallel",)),
    )(chip, mine, recv)


def _adam_math(w, g, m, v):
    m = ADAM_B1 * m + (1.0 - ADAM_B1) * g
    v = ADAM_B2 * v + (1.0 - ADAM_B2) * (g * g)
    m_hat = m / (1.0 - ADAM_B1 ** ADAM_STEP)
    v_hat = v / (1.0 - ADAM_B2 ** ADAM_STEP)
    delta = -ADAM_LR * (m_hat / (jnp.sqrt(v_hat) + ADAM_EPS) + ADAM_WD * w)
    return delta, m, v


def _adam_halves(name, own, other, half, w, m, v):
    _, r, c = w.shape
    tr = _row_tile(r, c)

    def body(half_ref, own_ref, oth_ref, w_ref, m_ref, v_ref, g_out, d_out, m_out, v_out):
        g = jnp.where(pl.program_id(0) == half_ref[0], own_ref[...], oth_ref[...])
        delta, m_new, v_new = _adam_math(w_ref[...], g, m_ref[...], v_ref[...])
        g_out[...] = g
        d_out[...] = delta
        m_out[...] = m_new
        v_out[...] = v_new

    flat = pl.BlockSpec((tr, c), lambda h, i, hf: (i, 0))
    full = pl.BlockSpec((None, tr, c), lambda h, i, hf: (h, i, 0))
    out = jax.ShapeDtypeStruct(w.shape, F32)
    return pl.pallas_call(
        body, name=name, out_shape=[out] * 4,
        grid_spec=pltpu.PrefetchScalarGridSpec(num_scalar_prefetch=1, grid=(2, r // tr), in_specs=[flat, flat, full, full, full],
                                               out_specs=[full] * 4),
        compiler_params=_params(("parallel", "parallel")),
    )(half, own, other, w, m, v)


def _adam_small(name, g, w, m, v):
    def body(g_ref, w_ref, m_ref, v_ref, d_out, m_out, v_out):
        delta, m_new, v_new = _adam_math(w_ref[...], g_ref[...], m_ref[...], v_ref[...])
        d_out[...] = delta
        m_out[...] = m_new
        v_out[...] = v_new

    out = jax.ShapeDtypeStruct(w.shape, F32)
    return pl.pallas_call(body, name=name, out_shape=[out] * 3)(g, w, m, v)


def _place():
    x, y, c = lax.axis_index("x"), lax.axis_index("y"), lax.axis_index("c")
    chips = [(1 - x, y), (x, 1 - y), (1 - x, 1 - y)]
    return x, y, c, chips


ANY = pl.BlockSpec(memory_space=pl.ANY)


def _gather_shards(name, shards):
    n = len(shards)

    def body(*refs):
        ins, outs = refs[:n], refs[n:2 * n]
        send, recv, local = refs[2 * n:]
        x, y, c, chips = _place()
        me = 2 * x + y
        sibling = (x, y, 1 - c)

        def copy(t, k, src, dst, to):
            return pltpu.make_async_remote_copy(src_ref=src, dst_ref=dst, send_sem=send.at[t, k], recv_sem=recv.at[t, k],
                                                device_id=to, device_id_type=MESH)

        own = [pltpu.make_async_copy(ins[t], outs[t].at[me], local.at[t]) for t in range(n)]
        for cp in own:
            cp.start()
        sent = []
        for t in range(n):
            for k, (px, py) in enumerate(chips):
                sent.append(copy(t, k, ins[t].at[c], outs[t].at[me, c], (px, py, c)))
                sent[-1].start()
        for t in range(n):
            for k, (px, py) in enumerate(chips):
                slab = outs[t].at[2 * px + py, c]
                copy(t, k, ins[t].at[c], slab, (px, py, c)).wait_recv()
                sent.append(copy(t, 3 + k, slab, slab, sibling))
                sent[-1].start()
        for t in range(n):
            for k, (px, py) in enumerate(chips):
                slab = outs[t].at[2 * px + py, 1 - c]
                copy(t, 3 + k, slab, slab, sibling).wait_recv()
        for cp in sent:
            cp.wait_send()
        for cp in own:
            cp.wait()

    return pl.pallas_call(
        body, name=name, in_specs=[ANY] * n, out_specs=[ANY] * n,
        out_shape=[jax.ShapeDtypeStruct((N_CHIPS, *s.shape), s.dtype) for s in shards],
        scratch_shapes=[pltpu.SemaphoreType.DMA((n, 6)), pltpu.SemaphoreType.DMA((n, 6)), pltpu.SemaphoreType.DMA((n,))],
    )(*shards)


def _to_sibling(name, arrays, pick_other_half):
    n = len(arrays)

    def body(*refs):
        ins, outs = refs[:n], refs[n:2 * n]
        send, recv = refs[2 * n:]
        x, y, c, _ = _place()
        copies = []
        for t in range(n):
            src = ins[t].at[:, 1 - c] if pick_other_half else ins[t]
            copies.append(pltpu.make_async_remote_copy(src_ref=src, dst_ref=outs[t], send_sem=send.at[t], recv_sem=recv.at[t],
                                                       device_id=(x, y, 1 - c), device_id_type=MESH))
            copies[-1].start()
        for cp in copies:
            cp.wait()

    shapes = [(a.shape[0], *a.shape[2:]) if pick_other_half else a.shape for a in arrays]
    return pl.pallas_call(
        body, name=name, in_specs=[ANY] * n, out_specs=[ANY] * n,
        out_shape=[jax.ShapeDtypeStruct(s, a.dtype) for s, a in zip(shapes, arrays)],
        scratch_shapes=[pltpu.SemaphoreType.DMA((n,)), pltpu.SemaphoreType.DMA((n,))],
    )(*arrays)


def _to_owner_chips(name, arrays):
    n = len(arrays)

    def body(*refs):
        ins, outs = refs[:n], refs[n:2 * n]
        send, recv = refs[2 * n:]
        x, y, c, chips = _place()
        copies = []
        for t in range(n):
            for k, (px, py) in enumerate(chips):
                copies.append(pltpu.make_async_remote_copy(
                    src_ref=ins[t].at[2 * px + py], dst_ref=outs[t].at[k], send_sem=send.at[t, k], recv_sem=recv.at[t, k],
                    device_id=(px, py, c), device_id_type=MESH))
                copies[-1].start()
        for cp in copies:
            cp.wait()

    return pl.pallas_call(
        body, name=name, in_specs=[ANY] * n, out_specs=[ANY] * n,
        out_shape=[jax.ShapeDtypeStruct((N_CHIPS - 1, *a.shape[1:]), a.dtype) for a in arrays],
        scratch_shapes=[pltpu.SemaphoreType.DMA((n, 3)), pltpu.SemaphoreType.DMA((n, 3))],
    )(*arrays)


def _sum_all_devices(name, part):
    r, c = part.shape

    def body(p_ref, o_ref, buf, send, recv):
        x, y, cc, _ = _place()
        me = 4 * x + 2 * y + cc
        copies = []
        for mask in range(1, 8):
            fx, fy, fc = (mask >> 2) & 1, (mask >> 1) & 1, mask & 1
            to = (x ^ fx, y ^ fy, cc ^ fc)
            copies.append((mask, pltpu.make_async_remote_copy(
                src_ref=p_ref, dst_ref=buf.at[me], send_sem=send.at[mask - 1], recv_sem=recv.at[mask - 1],
                device_id=to, device_id_type=MESH)))
            copies[-1][1].start()
        buf[me] = p_ref[...]
        for mask, cp in copies:
            pltpu.make_async_remote_copy(src_ref=p_ref, dst_ref=buf.at[me ^ mask], send_sem=send.at[mask - 1],
                                         recv_sem=recv.at[mask - 1], device_id=(x, y, cc), device_id_type=MESH).wait_recv()
        for _, cp in copies:
            cp.wait_send()
        total = buf[0]
        for d in range(1, 8):
            total = total + buf[d]
        o_ref[...] = total

    vm = pl.BlockSpec(memory_space=pltpu.VMEM)
    return pl.pallas_call(
        body, name=name, in_specs=[vm], out_specs=vm, out_shape=jax.ShapeDtypeStruct((r, c), F32),
        scratch_shapes=[pltpu.VMEM((8, r, c), F32), pltpu.SemaphoreType.DMA((7,)), pltpu.SemaphoreType.DMA((7,))],
    )(part)


def kernel(x, pool_w, pool_scale, w_q, w_kv, w_o, ffn_w_gate, ffn_w_up, ffn_conv_w, ffn_conv_b, ffn_w_down, ln1_g, ln1_b, ln2_g, ln2_b, loss_target, m_pool_w, m_pool_scale, m_w_q, m_w_kv, m_w_o, m_ffn_w_gate, m_ffn_w_up, m_ffn_conv_w, m_ffn_conv_b, m_ffn_w_down, m_ln1_g, m_ln1_b, m_ln2_g, m_ln2_b, v_pool_w, v_pool_scale, v_w_q, v_w_kv, v_w_o, v_ffn_w_gate, v_ffn_w_up, v_ffn_conv_w, v_ffn_conv_b, v_ffn_w_down, v_ln1_g, v_ln1_b, v_ln2_g, v_ln2_b):
    b_n, s_n, d = x.shape
    n = b_n * s_n
    f = ffn_w_gate.shape[-1]
    qc = w_q.shape[-1]
    kvb = w_kv.shape[-1] // 2
    n_attn = w_q.shape[0]
    g_n = len(DILATIONS)
    cw = N_HEADS * HEAD_DIM
    xi, yi, ci = lax.axis_index("x"), lax.axis_index("y"), lax.axis_index("c")
    half = jnp.reshape(ci, (1,)).astype(jnp.int32)
    chip = jnp.reshape(2 * xi + yi, (1,)).astype(jnp.int32)

    sharded = {
        "pool_w": (pool_w, m_pool_w, v_pool_w, (2, 4 * 64, POOL_GROUP_DIM)),
        "pool_scale": (pool_scale, m_pool_scale, v_pool_scale, (2, 1, pool_scale.shape[-1])),
        "w_q": (w_q, m_w_q, v_w_q, (2, d, qc)),
        "w_kv": (w_kv, m_w_kv, v_w_kv, (2, d // 2, w_kv.shape[-1])),
        "w_o": (w_o, m_w_o, v_w_o, (2, w_o.shape[1], d)),
        "ffn_w_gate": (ffn_w_gate, m_ffn_w_gate, v_ffn_w_gate, (2, 2 * d, f)),
        "ffn_w_up": (ffn_w_up, m_ffn_w_up, v_ffn_w_up, (2, 2 * d, f)),
        "ffn_conv_w": (ffn_conv_w, m_ffn_conv_w, v_ffn_conv_w, (2, 6, f)),
        "ffn_w_down": (ffn_w_down, m_ffn_w_down, v_ffn_w_down, (2, 2 * f, d)),
    }
    mxu_weights = ("pool_w", "w_q", "w_kv", "w_o", "ffn_w_gate", "ffn_w_up", "ffn_w_down")
    names = list(sharded)
    gathered = _gather_shards("gather_weights", [
        sharded[k][0].reshape(sharded[k][3]).astype(MXU if k in mxu_weights else F32) for k in names])
    full = dict(zip(names, gathered))
    per_layer = lambda t, shape: [t.reshape(N_CHIPS, shape[0], *shape[1:])[:, i] for i in range(shape[0])]
    wq_l = per_layer(full["w_q"], (n_attn, d, qc))
    wkv_all = full["w_kv"].reshape(N_CHIPS, d, w_kv.shape[-1])
    wg_l = per_layer(full["ffn_w_gate"], (DEPTH, d, f))
    wu_l = per_layer(full["ffn_w_up"], (DEPTH, d, f))
    wd_l = per_layer(full["ffn_w_down"], (DEPTH, f, d))
    cw_l = per_layer(full["ffn_conv_w"], (DEPTH, 3, f))
    cb_l = [ffn_conv_b[i].reshape(N_CHIPS, 1, f) for i in range(DEPTH)]
    pw_nat = full["pool_w"].reshape(N_CHIPS, N_POOL_LAYERS, 4, 64, POOL_GROUP_DIM).transpose(1, 2, 0, 3, 4).reshape(
        N_POOL_LAYERS, 4, POOL_GROUP_DIM, POOL_GROUP_DIM)
    ps_nat = full["pool_scale"].reshape(N_CHIPS, N_POOL_LAYERS, -1).transpose(1, 0, 2).reshape(N_POOL_LAYERS, 1, d)
    wo_nat = full["w_o"].reshape(N_CHIPS, n_attn, -1, d).transpose(1, 0, 2, 3).reshape(n_attn, cw, d)
    cos_l, sin_l = _rope_tables(s_n)

    def vec(a, layer):
        return a[layer].reshape(1, d)

    h = x.reshape(n, d)
    hb = None
    saved = []
    k_nat = v_nat = None
    for layer in range(DEPTH):
        keep = {"h": h, "hb": hb}
        if layer < N_POOL_LAYERS:
            mix = _pool_fwd("pool_fwd", h.reshape(b_n, s_n, d), pw_nat[layer], ps_nat[layer]).reshape(n, d)
        else:
            a = layer - N_POOL_LAYERS
            q = _mm_cols("q_proj", hb, wq_l[a], (), F32)
            qr = _rope("q_rope", q.reshape(b_n, s_n, -1), 0, g_n, cos_l, sin_l, HEAD_DIM ** -0.5, MXU)
            outs, lses = [], []
            for g, dil in enumerate(DILATIONS):
                o_g, lse_g = _attn_fwd(f"attn_fwd_d{dil}", qr, k_nat, v_nat, g, dil)
                outs.append(o_g.reshape(n, cw))
                lses.append(lse_g.reshape(n, LSE_LANES))
            ob = _attn_mix("attn_mix", outs, lses)
            mix = _mm("o_proj", [(ob, (TM, cw), lambda m, _: (m, 0), wo_nat[a], (cw, d), lambda m, _: (0, 0))],
                      NN, (n // TM, 1), (n, d), F32, (TM, d), lambda m, _: (m, 0), 1)
            keep.update(qr=qr, outs=outs, lses=lses, ob=ob)
        r1, h1, h1b = _add_ln("ln_fwd", h, mix, vec(ln1_g, layer), vec(ln1_b, layer))
        gate = _mm_shard_out("gate_up_proj", h1b, wg_l[layer], (), F32)
        up = _mm_shard_out("gate_up_proj", h1b, wu_l[layer], (), F32)
        g4, u4 = gate.reshape(N_CHIPS, b_n, s_n, f), up.reshape(N_CHIPS, b_n, s_n, f)
        hmid = _glu_fwd("glu_fwd", g4, u4, cw_l[layer], cb_l[layer]).reshape(N_CHIPS, n, f)
        ffn = _mm_shard_in("down_proj", hmid, wd_l[layer], ())
        r2, h2, h2b = _add_ln("ln_fwd", h1, ffn, vec(ln2_g, layer), vec(ln2_b, layer))
        keep.update(r1=r1, h1b=h1b, g4=g4, u4=u4, hmid=hmid, r2=r2, h2b=h2b)
        saved.append(keep)
        if layer == N_POOL_LAYERS - 1:
            kv = _mm_cols("kv_proj", h2b, wkv_all, (), F32, cb=kvb).reshape(b_n, s_n, -1)
            k_nat = _rope("k_rope", kv, 0, g_n, cos_l, sin_l, 1.0, MXU)
            v_nat = _rope("v_cast", kv, g_n, g_n, cos_l, sin_l, 1.0, MXU, rotate=False)
        h, hb = h2, h2b

    dy, loss_tile = _loss_head("loss_head", h, loss_target.reshape(n, d))

    d_wq = d_wo = d_wg = d_wu = d_wd = None
    d_cw = [None] * DEPTH
    d_pw, d_ps = [None] * N_POOL_LAYERS, [None] * N_POOL_LAYERS
    d_ln = {}
    dkv_prev = None
    d_top, top_scale, top_rest = dy, 1.0, []
    for layer in reversed(range(DEPTH)):
        sv = saved[layer]
        dr2, dr2b, d_ln["ln2", layer] = _ln_bwd("ln_bwd", sv["r2"], vec(ln2_g, layer), d_top, top_scale, top_rest)
        dhmid = _mm_nt_shard_out("down_bwd", dr2b, wd_l[layer], ())
        d_wd = _mm_tn("down_dw", sv["hmid"], f, dr2b, d, N_CHIPS, (N_CHIPS, DEPTH, f, d), (None, None, f, d),
                      lambda j: (j, layer, 0, 0), into=d_wd, a_lead=lambda j: (j,))
        dg4, du4, d_cw[layer] = _glu_bwd("glu_bwd", dhmid.reshape(N_CHIPS, b_n, s_n, f), sv["g4"], sv["u4"],
                                         cw_l[layer], cb_l[layer])
        dg, du = dg4.reshape(N_CHIPS, n, f), du4.reshape(N_CHIPS, n, f)
        dh1 = _mm_nt_shard_in("gate_up_bwd", [(dg, wg_l[layer], ()), (du, wu_l[layer], ())])
        d_wg = _mm_tn("gate_up_dw", sv["h1b"], d, dg, f, N_CHIPS, (N_CHIPS, DEPTH, d, f), (None, None, d, f),
                      lambda j: (j, layer, 0, 0), into=d_wg, b_lead=lambda j: (j,))
        d_wu = _mm_tn("gate_up_dw", sv["h1b"], d, du, f, N_CHIPS, (N_CHIPS, DEPTH, d, f), (None, None, d, f),
                      lambda j: (j, layer, 0, 0), into=d_wu, b_lead=lambda j: (j,))
        dr1, dr1b, d_ln["ln1", layer] = _ln_bwd("ln_bwd", sv["r1"], vec(ln1_g, layer), dr2, ALPHA, [dh1])
        if layer < N_POOL_LAYERS:
            d_in, d_pw[layer], d_ps[layer] = _pool_bwd("pool_bwd", sv["h"].reshape(b_n, s_n, d),
                                                       dr1.reshape(b_n, s_n, d), pw_nat[layer], ps_nat[layer])
            d_top, top_scale, top_rest = d_in.reshape(n, d), 1.0, []
        else:
            a = layer - N_POOL_LAYERS
            do = _mm("o_bwd", [(dr1b, (TM, d), lambda m, _: (m, 0), wo_nat[a], (cw, d), lambda m, _: (0, 0))],
                     NT, (n // TM, 1), (n, cw), F32, (TM, cw), lambda m, _: (m, 0), 1)
            d_wo = _mm_tn("o_dw", sv["ob"], cw // N_CHIPS, dr1b, d, N_CHIPS, (N_CHIPS, n_attn, cw // N_CHIPS, d),
                          (None, None, cw // N_CHIPS, d), lambda j: (j, a, 0, 0), into=d_wo)
            mixed = _attn_mix_bwd("attn_mix_bwd", do, sv["outs"], sv["lses"])
            bufs = None
            for g, dil in enumerate(DILATIONS):
                bufs = _attn_bwd(f"attn_bwd_d{dil}", sv["qr"], k_nat, v_nat, mixed[g].reshape(b_n, s_n, cw),
                                 sv["lses"][g].reshape(b_n, s_n, LSE_LANES), mixed[g_n + g].reshape(b_n, s_n, LSE_LANES),
                                 g, dil, prev=dkv_prev, into=bufs)
            dkv_prev = bufs[1:]
            dq = _rope("q_rope_bwd", bufs[0], 0, g_n, cos_l, -sin_l, HEAD_DIM ** -0.5, MXU).reshape(n, g_n * cw)
            d_attn = _mm_nt_cols_in("q_bwd", dq, wq_l[a], (), qc)
            d_wq = _mm_tn("q_dw", sv["hb"], d, dq, qc, N_CHIPS, (N_CHIPS, n_attn, d, qc), (None, None, d, qc),
                          lambda j: (j, a, 0, 0), into=d_wq)
            d_top, top_scale, top_rest = dr1, ALPHA, [d_attn]
            if a == 0:
                dkv = _rope("k_rope_bwd", dkv_prev[0], 0, g_n, cos_l, -sin_l, 1.0, MXU, out_cols=2 * g_n)
                dkv = _rope("v_cast_bwd", dkv_prev[1], 0, g_n, cos_l, sin_l, 1.0, MXU, rotate=False,
                            out_cols=2 * g_n, out_col0=g_n, into=dkv).reshape(n, 2 * g_n * cw)
                h_kv = saved[N_POOL_LAYERS - 1]["h2b"]
                top_rest = top_rest + [_mm_nt_cols_in("kv_bwd", dkv, wkv_all, (), kvb)]
                d_wkv = _mm_tn("kv_dw", h_kv, d, dkv, kvb, 2 * N_CHIPS, (2 * N_CHIPS, d, kvb), (None, d, kvb), lambda q: (q, 0, 0))
    grad_x = d_top.reshape(b_n, s_n, d)

    d_wkv = d_wkv.reshape(N_CHIPS, 2, d, kvb).transpose(0, 2, 1, 3)
    d_pw_all = jnp.stack(d_pw).reshape(N_POOL_LAYERS, 4, N_CHIPS, 64, POOL_GROUP_DIM).transpose(2, 0, 1, 3, 4)
    d_ps_all = jnp.stack(d_ps).reshape(N_POOL_LAYERS, N_CHIPS, -1).transpose(1, 0, 2)
    d_cw_all = jnp.stack([t[:, :3] for t in d_cw], axis=1)
    local = {
        "pool_w": d_pw_all, "pool_scale": d_ps_all, "w_q": d_wq, "w_kv": d_wkv, "w_o": d_wo,
        "ffn_w_gate": d_wg, "ffn_w_up": d_wu, "ffn_conv_w": d_cw_all, "ffn_w_down": d_wd,
    }
    grads4 = [local[k].reshape(N_CHIPS, *sharded[k][3]) for k in names]
    from_sibling = _to_sibling("grads_to_sibling", grads4, True)
    core_sums = [_add_halves(f"add_halves_{k}", g4_, r_, half, MXU if k in mxu_weights and k != "pool_w" else F32)
                 for k, g4_, r_ in zip(names, grads4, from_sibling)]
    from_chips = _to_owner_chips("grads_to_owner", core_sums)
    owned = [_add_chips(f"add_chips_{k}", s_, r_, chip) for k, s_, r_ in zip(names, core_sums, from_chips)]
    others = _to_sibling("halves_to_sibling", owned, False)

    out_grad, out_delta, out_m, out_v = {}, {}, {}, {}
    for k, own, oth in zip(names, owned, others):
        w_, m_, v_, shape = sharded[k]
        res = _adam_halves(f"adam_{k}", own, oth, half, w_.reshape(shape), m_.reshape(shape), v_.reshape(shape))
        out_grad[k], out_delta[k], out_m[k], out_v[k] = (t.reshape(w_.shape) for t in res)

    d_cb = jnp.stack([t[:, 3] for t in d_cw], axis=0).reshape(DEPTH * N_CHIPS * f // d, d)
    ln_rows = jnp.concatenate([jnp.stack([d_ln[which, layer][row] for layer in range(DEPTH)])
                               for which, row in (("ln1", 0), ("ln1", 1), ("ln2", 0), ("ln2", 1))])
    rows = jnp.concatenate([ln_rows, d_cb, jnp.broadcast_to(loss_tile[0:1, 0:1], (1, d))])
    pad = (-rows.shape[0]) % 8
    total = _sum_all_devices("sum_small", jnp.pad(rows, ((0, pad), (0, 0))))
    small = {"ln1_g": total[0:4], "ln1_b": total[4:8], "ln2_g": total[8:12], "ln2_b": total[12:16],
             "ffn_conv_b": total[16:16 + d_cb.shape[0]].reshape(ffn_conv_b.shape)}
    loss = total[16 + d_cb.shape[0], 0]
    small_in = {"ln1_g": (ln1_g, m_ln1_g, v_ln1_g), "ln1_b": (ln1_b, m_ln1_b, v_ln1_b), "ln2_g": (ln2_g, m_ln2_g, v_ln2_g),
                "ln2_b": (ln2_b, m_ln2_b, v_ln2_b), "ffn_conv_b": (ffn_conv_b, m_ffn_conv_b, v_ffn_conv_b)}
    for k, (w_, m_, v_) in small_in.items():
        out_grad[k] = small[k]
        out_delta[k], out_m[k], out_v[k] = _adam_small(f"adam_{k}", small[k], w_, m_, v_)

    order = ["pool_w", "pool_scale", "w_q", "w_kv", "w_o", "ffn_w_gate", "ffn_w_up", "ffn_conv_w", "ffn_conv_b",
             "ffn_w_down", "ln1_g", "ln1_b", "ln2_g", "ln2_b"]
    return (loss, grad_x, *[out_grad[k] for k in order], *[out_delta[k] for k in order],
            *[out_m[k] for k in order], *[out_v[k] for k in order])
```

```python
import functools
import math

import jax
import jax.numpy as jnp
from jax import lax
from jax.experimental import pallas as pl
from jax.experimental.pallas import tpu as pltpu

F32 = jnp.float32
BF16 = jnp.bfloat16
MXU = jnp.bfloat16

DEPTH = 4
N_POOL_LAYERS = 2
POOL_WINDOWS = (2, 4, 8, 16)
POOL_GROUP_DIM = 256
HEAD_DIM = 64
N_HEADS = 16
DILATIONS = (1, 4, 16)
SPAN = 128
ROPE_THETA = 10000.0
ALPHA = (2.0 * DEPTH) ** 0.25
LN_EPS = 1e-5
ADAM_LR, ADAM_B1, ADAM_B2, ADAM_EPS, ADAM_WD, ADAM_STEP = 0.001, 0.9, 0.999, 1e-08, 0.01, 10

N_CHIPS = 4
VMEM_LIMIT = 56 * 1024 * 1024
MESH = pl.DeviceIdType.MESH

NN = (((1,), (0,)), ((), ()))
NT = (((1,), (1,)), ((), ()))
TN = (((0,), (0,)), ((), ()))


def _params(sem=None):
    return pltpu.CompilerParams(dimension_semantics=sem, vmem_limit_bytes=VMEM_LIMIT)


def _chunks(n_rows, ch, fn, unroll=1):
    def step(i, carry):
        fn(pl.multiple_of(i * ch, ch))
        return carry

    lax.fori_loop(0, n_rows // ch, step, 0, unroll=unroll)


def _fold8(v):
    return jnp.sum(v.reshape(v.shape[0] // 8, 8, v.shape[1]), axis=0)


def _down(v, k):
    return pltpu.roll(v, k, 0)


def _up(v, k):
    return pltpu.roll(v, v.shape[0] - k, 0)


def _mm(name, pairs, dims, grid, out_shape, out_dtype, out_block, out_map, nk, into=None):
    n_pairs = len(pairs)
    kax = len(grid) - 1

    def body(*refs):
        o_ref = refs[2 * n_pairs + (1 if into is not None else 0)]
        part = None
        for p in range(n_pairs):
            t = lax.dot_general(refs[2 * p][...], refs[2 * p + 1][...], dims, preferred_element_type=F32)
            part = t if part is None else part + t
        if nk == 1:
            o_ref[...] = part.astype(o_ref.dtype)
        else:
            k = pl.program_id(kax)

            @pl.when(k == 0)
            def _():
                o_ref[...] = part

            @pl.when(k > 0)
            def _():
                o_ref[...] += part

    operands, in_specs = [], []
    for a, a_block, a_map, b, b_block, b_map in pairs:
        operands += [a, b]
        in_specs += [pl.BlockSpec(a_block, a_map), pl.BlockSpec(b_block, b_map)]
    aliases = {}
    if into is not None:
        operands.append(into)
        in_specs.append(pl.BlockSpec(memory_space=pl.ANY))
        aliases = {2 * n_pairs: 0}
    assert nk == 1 or out_dtype == F32
    sem = ("parallel",) * kax + ("arbitrary",)
    return pl.pallas_call(
        body, name=name, grid=grid, in_specs=in_specs, out_specs=pl.BlockSpec(out_block, out_map),
        out_shape=jax.ShapeDtypeStruct(out_shape, out_dtype), input_output_aliases=aliases,
        compiler_params=_params(sem),
    )(*operands)


TM = 2048


def _mm_cols(name, a, w, w_idx, out_dtype, cb=None):
    n, k = a.shape
    j_n, c = w.shape[0], w.shape[-1]
    cb = c if cb is None else cb
    s = c // cb
    wb = (None,) * (w.ndim - 2) + (k, cb)
    return _mm(name, [(a, (TM, k), lambda q, m, _: (m, 0), w, wb, lambda q, m, _: (q // s, *w_idx, 0, q % s))], NN,
               (j_n * s, n // TM, 1), (n, j_n * c), out_dtype, (TM, cb), lambda q, m, _: (m, q), 1)


def _mm_shard_out(name, a, w, w_idx, out_dtype):
    n, k = a.shape
    j_n, c = w.shape[0], w.shape[-1]
    wb = (None,) * (w.ndim - 2) + (k, c)
    return _mm(name, [(a, (TM, k), lambda j, m, _: (m, 0), w, wb, lambda j, m, _: (j, *w_idx, 0, 0))], NN,
               (j_n, n // TM, 1), (j_n, n, c), out_dtype, (None, TM, c), lambda j, m, _: (j, m, 0), 1)


def _mm_shard_in(name, a4, w, w_idx):
    j_n, n, c = a4.shape
    k = w.shape[-1]
    wb = (None,) * (w.ndim - 2) + (c, k)
    return _mm(name, [(a4, (None, TM, c), lambda m, j: (j, m, 0), w, wb, lambda m, j: (j, *w_idx, 0, 0))], NN,
               (n // TM, j_n), (n, k), F32, (TM, k), lambda m, j: (m, 0), j_n)


def _mm_nt_shard_out(name, a, w, w_idx):
    n, k = a.shape
    j_n, c = w.shape[0], w.shape[-2]
    wb = (None,) * (w.ndim - 2) + (c, k)
    return _mm(name, [(a, (TM, k), lambda j, m, _: (m, 0), w, wb, lambda j, m, _: (j, *w_idx, 0, 0))], NT,
               (j_n, n // TM, 1), (j_n, n, c), F32, (None, TM, c), lambda j, m, _: (j, m, 0), 1)


def _mm_nt_shard_in(name, terms):
    pairs = []
    for a4, w, w_idx in terms:
        j_n, n, c = a4.shape
        k = w.shape[-2]
        wb = (None,) * (w.ndim - 2) + (k, c)
        pairs.append((a4, (None, TM, c), lambda m, j: (j, m, 0), w, wb,
                      functools.partial(lambda m, j, w_idx: (j, *w_idx, 0, 0), w_idx=w_idx)))
    return _mm(name, pairs, NT, (n // TM, j_n), (n, k), F32, (TM, k), lambda m, j: (m, 0), j_n)


def _mm_nt_cols_in(name, a, w, w_idx, cb):
    n, ct = a.shape
    j_n, k, c = w.shape[0], w.shape[-2], w.shape[-1]
    s = c // cb
    wb = (None,) * (w.ndim - 2) + (k, cb)
    return _mm(name, [(a, (TM, cb), lambda m, q: (m, q), w, wb, lambda m, q: (q // s, *w_idx, 0, q % s))], NT,
               (n // TM, ct // cb), (n, k), F32, (TM, k), lambda m, q: (m, 0), ct // cb)


def _mm_tn(name, a, a_cols, b, b_cols, n_blocks, out_shape, out_block, out_map, into=None, a_lead=None, b_lead=None):
    n = a.shape[-2]
    a_nb = a.shape[-1] // a_cols
    b_nb = b.shape[-1] // b_cols
    if a_lead is None:
        a_block, a_map = (TM, a_cols), lambda q, t: (t, q if a_nb > 1 else 0)
    else:
        a_block, a_map = (None, TM, a_cols), lambda q, t: (*a_lead(q), t, 0)
    if b_lead is None:
        b_block, b_map = (TM, b_cols), lambda q, t: (t, q if b_nb > 1 else 0)
    else:
        b_block, b_map = (None, TM, b_cols), lambda q, t: (*b_lead(q), t, 0)
    return _mm(name, [(a, a_block, a_map, b, b_block, b_map)], TN, (n_blocks, n // TM), out_shape, F32,
               out_block, lambda q, t: out_map(q), n // TM, into=into)


LN_ROWS = 512
LN_CH = 16
LN_UNROLL = 4


def _ln_stats(r):
    mu = jnp.mean(r, -1, keepdims=True)
    xc = r - mu
    var = jnp.mean(xc * xc, -1, keepdims=True)
    return xc, lax.rsqrt(var + LN_EPS)


def _add_ln(name, a, mix, g, b):
    n, d = a.shape

    def body(a_ref, m_ref, g_ref, b_ref, r_ref, h_ref, hb_ref):
        gg, bb = g_ref[...], b_ref[...]

        def chunk(s):
            rows = pl.ds(s, LN_CH)
            r = ALPHA * a_ref[rows, :] + m_ref[rows, :]
            xc, rstd = _ln_stats(r)
            y = xc * rstd * gg + bb
            r_ref[rows, :] = r
            h_ref[rows, :] = y
            hb_ref[rows, :] = y.astype(MXU)

        _chunks(LN_ROWS, LN_CH, chunk, LN_UNROLL)

    tile = pl.BlockSpec((LN_ROWS, d), lambda i: (i, 0))
    vec = pl.BlockSpec((1, d), lambda i: (0, 0))
    return pl.pallas_call(
        body, name=name, grid=(n // LN_ROWS,), in_specs=[tile, tile, vec, vec], out_specs=[tile, tile, tile],
        out_shape=[jax.ShapeDtypeStruct((n, d), F32), jax.ShapeDtypeStruct((n, d), F32), jax.ShapeDtypeStruct((n, d), MXU)],
        compiler_params=_params(("parallel",)),
    )(a, mix, g, b)


def _ln_bwd(name, r, g, d_a, scale_a, d_rest):
    n, d = r.shape
    n_rest = len(d_rest)
    steps = n // LN_ROWS

    def body(*refs):
        r_ref, g_ref, da_ref = refs[:3]
        rest = refs[3:3 + n_rest]
        dr_ref, drb_ref, gb_ref, acc = refs[3 + n_rest:]
        i = pl.program_id(0)

        @pl.when(i == 0)
        def _():
            acc[...] = jnp.zeros_like(acc)

        gg = g_ref[...]

        def chunk(s):
            rows = pl.ds(s, LN_CH)
            xc, rstd = _ln_stats(r_ref[rows, :])
            xhat = xc * rstd
            dy = da_ref[rows, :] if scale_a == 1.0 else scale_a * da_ref[rows, :]
            for t in rest:
                dy = dy + t[rows, :]
            dyg = dy * gg
            m1 = jnp.mean(dyg, -1, keepdims=True)
            m2 = jnp.mean(dyg * xhat, -1, keepdims=True)
            dr = rstd * (dyg - m1 - xhat * m2)
            dr_ref[rows, :] = dr
            drb_ref[rows, :] = dr.astype(MXU)
            acc[0] += _fold8(dy * xhat)
            acc[1] += _fold8(dy)

        _chunks(LN_ROWS, LN_CH, chunk, LN_UNROLL)

        @pl.when(i == steps - 1)
        def _():
            gb_ref[0:1, :] = jnp.sum(acc[0], axis=0, keepdims=True)
            gb_ref[1:2, :] = jnp.sum(acc[1], axis=0, keepdims=True)

    tile = pl.BlockSpec((LN_ROWS, d), lambda i: (i, 0))
    vec = pl.BlockSpec((1, d), lambda i: (0, 0))
    return pl.pallas_call(
        body, name=name, grid=(steps,), in_specs=[tile, vec, tile] + [tile] * n_rest,
        out_specs=[tile, tile, pl.BlockSpec((2, d), lambda i: (0, 0))],
        out_shape=[jax.ShapeDtypeStruct((n, d), F32), jax.ShapeDtypeStruct((n, d), MXU), jax.ShapeDtypeStruct((2, d), F32)],
        scratch_shapes=[pltpu.VMEM((2, 8, d), F32)],
        compiler_params=_params(("arbitrary",)),
    )(r, g, d_a, *d_rest)


FFN_ROWS = 512
FFN_CH = 32
GELU_C1 = math.sqrt(2.0 / math.pi)
GELU_C2 = 0.044715


def _conv3(v, prev8, w0, w1, w2, bias):
    n = v.shape[0]
    ext = jnp.concatenate([prev8, v], axis=0)
    g1 = _down(ext, 1)[8:8 + n]
    g2 = _down(ext, 2)[8:8 + n]
    return bias + w0 * g2 + w1 * g1 + w2 * v, g1, g2


def _glu_fwd(name, g4, u4, conv_w, conv_b):
    j_n, b_n, s_n, f = g4.shape
    tiles = s_n // FFN_ROWS

    def body(g_ref, halo_ref, u_ref, w_ref, b_ref, o_ref, gs):
        s = pl.program_id(2)
        gs[0:8, :] = jnp.where(s > 0, halo_ref[...], 0.0)
        gs[8:, :] = g_ref[...]
        w0, w1, w2, bias = w_ref[0:1, :], w_ref[1:2, :], w_ref[2:3, :], b_ref[...]

        def chunk(st):
            v = gs[pl.ds(pl.multiple_of(st + 8, 8), FFN_CH), :]
            conv, _, _ = _conv3(v, gs[pl.ds(st, 8), :], w0, w1, w2, bias)
            cdf = 0.5 * (1.0 + jnp.tanh(GELU_C1 * (conv + GELU_C2 * (conv * conv * conv))))
            o_ref[pl.ds(st, FFN_CH), :] = (conv * cdf * u_ref[pl.ds(st, FFN_CH), :]).astype(o_ref.dtype)

        _chunks(FFN_ROWS, FFN_CH, chunk)

    tile = pl.BlockSpec((None, None, FFN_ROWS, f), lambda j, b, s: (j, b, s, 0))
    halo = pl.BlockSpec((None, None, 8, f), lambda j, b, s: (j, b, jnp.maximum(s * (FFN_ROWS // 8) - 1, 0), 0))
    return pl.pallas_call(
        body, name=name, grid=(j_n, b_n, tiles),
        in_specs=[tile, halo, tile,
                  pl.BlockSpec((None, 3, f), lambda j, b, s: (j, 0, 0)),
                  pl.BlockSpec((None, 1, f), lambda j, b, s: (j, 0, 0))],
        out_specs=tile, out_shape=jax.ShapeDtypeStruct(g4.shape, MXU),
        scratch_shapes=[pltpu.VMEM((8 + FFN_ROWS, f), F32)],
        compiler_params=_params(("parallel", "parallel", "parallel")),
    )(g4, g4, u4, conv_w, conv_b)


def _glu_bwd(name, dh4, g4, u4, conv_w, conv_b):
    j_n, b_n, s_n, f = g4.shape
    tiles = s_n // FFN_ROWS
    ext = FFN_CH + 8

    def body(d_ref, dnext_ref, g_ref, gprev_ref, gnext_ref, u_ref, unext_ref, w_ref, b_ref,
             dg_ref, du_ref, wb_ref, gs, us, ds, acc):
        b, s = pl.program_id(1), pl.program_id(2)
        last = s == tiles - 1

        @pl.when((b == 0) & (s == 0))
        def _():
            acc[...] = jnp.zeros_like(acc)

        gs[0:8, :] = jnp.where(s > 0, gprev_ref[...], 0.0)
        gs[8:8 + FFN_ROWS, :] = g_ref[...]
        gs[8 + FFN_ROWS:, :] = gnext_ref[...]
        us[0:FFN_ROWS, :] = u_ref[...]
        us[FFN_ROWS:, :] = unext_ref[...]
        ds[0:FFN_ROWS, :] = d_ref[...]
        ds[FFN_ROWS:, :] = jnp.where(last, 0.0, dnext_ref[...])
        w0, w1, w2, bias = w_ref[0:1, :], w_ref[1:2, :], w_ref[2:3, :], b_ref[...]

        def chunk(st):
            v = gs[pl.ds(pl.multiple_of(st + 8, 8), ext), :]
            conv, g1, g2 = _conv3(v, gs[pl.ds(st, 8), :], w0, w1, w2, bias)
            th = jnp.tanh(GELU_C1 * (conv + GELU_C2 * (conv * conv * conv)))
            cdf = 0.5 * (1.0 + th)
            dact = cdf + conv * (0.5 * GELU_C1) * (1.0 - th * th) * (1.0 + (3.0 * GELU_C2) * (conv * conv))
            de = ds[pl.ds(st, ext), :]
            dconv = de * us[pl.ds(st, ext), :] * dact
            du_ref[pl.ds(st, FFN_CH), :] = (de * (conv * cdf))[:FFN_CH].astype(du_ref.dtype)
            dg = w2 * dconv + w1 * _up(dconv, 1) + w0 * _up(dconv, 2)
            dg_ref[pl.ds(st, FFN_CH), :] = dg[:FFN_CH].astype(dg_ref.dtype)
            dc = dconv[:FFN_CH]
            acc[0] += _fold8(dc * g2[:FFN_CH])
            acc[1] += _fold8(dc * g1[:FFN_CH])
            acc[2] += _fold8(dc * v[:FFN_CH])
            acc[3] += _fold8(dc)

        _chunks(FFN_ROWS, FFN_CH, chunk)

        @pl.when((b == b_n - 1) & last)
        def _():
            for k in range(4):
                wb_ref[k:k + 1, :] = jnp.sum(acc[k], axis=0, keepdims=True)

    blocks8 = FFN_ROWS // 8
    tile = pl.BlockSpec((None, None, FFN_ROWS, f), lambda j, b, s: (j, b, s, 0))
    prev = pl.BlockSpec((None, None, 8, f), lambda j, b, s: (j, b, jnp.maximum(s * blocks8 - 1, 0), 0))
    nxt = pl.BlockSpec((None, None, 8, f), lambda j, b, s: (j, b, jnp.minimum((s + 1) * blocks8, s_n // 8 - 1), 0))
    return pl.pallas_call(
        body, name=name, grid=(j_n, b_n, tiles),
        in_specs=[tile, nxt, tile, prev, nxt, tile, nxt,
                  pl.BlockSpec((None, 3, f), lambda j, b, s: (j, 0, 0)),
                  pl.BlockSpec((None, 1, f), lambda j, b, s: (j, 0, 0))],
        out_specs=[tile, tile, pl.BlockSpec((None, 4, f), lambda j, b, s: (j, 0, 0))],
        out_shape=[jax.ShapeDtypeStruct(g4.shape, MXU), jax.ShapeDtypeStruct(g4.shape, MXU),
                   jax.ShapeDtypeStruct((j_n, 4, f), F32)],
        scratch_shapes=[pltpu.VMEM((16 + FFN_ROWS, f), F32), pltpu.VMEM((8 + FFN_ROWS, f), F32),
                        pltpu.VMEM((8 + FFN_ROWS, f), F32), pltpu.VMEM((4, 8, f), F32)],
        compiler_params=_params(("parallel", "arbitrary", "arbitrary")),
    )(dh4, dh4, g4, g4, g4, u4, u4, conv_w, conv_b)


POOL_ROWS = 512
POOL_CH = 32
POOL_HALO = 16


def _pool_windows(v, t0, gi, causal):
    shift = _down if causal else _up
    acc, k = v, 1
    while k < POOL_WINDOWS[gi]:
        acc = acc + shift(acc, k)
        k *= 2
    return acc


def _count(t0, n, w):
    t = t0 + lax.broadcasted_iota(jnp.int32, (n, 1), 0)
    return jnp.minimum(t + 1, w).astype(F32)


def _pooled_into(xs, pooled, t_tile):
    def chunk(st):
        for gi, w in enumerate(POOL_WINDOWS):
            cols = slice(gi * POOL_GROUP_DIM, (gi + 1) * POOL_GROUP_DIM)
            v = xs[pl.ds(st, POOL_CH + POOL_HALO), cols]
            sums = _pool_windows(v, None, gi, True)[POOL_HALO:]
            val = sums / _count(t_tile + st, POOL_CH, w) - v[POOL_HALO:]
            pooled[pl.ds(st, POOL_CH), cols] = val.astype(pooled.dtype)

    _chunks(POOL_ROWS, POOL_CH, chunk)


def _pool_specs(b_n, s_n, d):
    per = POOL_ROWS // POOL_HALO
    tile = pl.BlockSpec((None, POOL_ROWS, d), lambda b, s: (b, s, 0))
    prev = pl.BlockSpec((None, POOL_HALO, d), lambda b, s: (b, jnp.maximum(s * per - 1, 0), 0))
    nxt = pl.BlockSpec((None, POOL_HALO, d), lambda b, s: (b, jnp.minimum((s + 1) * per, s_n // POOL_HALO - 1), 0))
    return tile, prev, nxt


def _pool_fwd(name, h3, w, scale):
    b_n, s_n, d = h3.shape
    tile, prev, _ = _pool_specs(b_n, s_n, d)

    def body(h_ref, halo_ref, w_ref, sc_ref, o_ref, xs, pooled):
        s = pl.program_id(1)
        xs[0:POOL_HALO, :] = jnp.where(s > 0, halo_ref[...], 0.0)
        xs[POOL_HALO:, :] = h_ref[...]
        _pooled_into(xs, pooled, s * POOL_ROWS)
        for gi in range(len(POOL_WINDOWS)):
            cols = slice(gi * POOL_GROUP_DIM, (gi + 1) * POOL_GROUP_DIM)
            y = jnp.dot(pooled[:, cols], w_ref[gi], preferred_element_type=F32)
            o_ref[:, cols] = y * sc_ref[:, cols]

    return pl.pallas_call(
        body, name=name, grid=(b_n, s_n // POOL_ROWS),
        in_specs=[tile, prev, pl.BlockSpec(w.shape, lambda b, s: (0, 0, 0)), pl.BlockSpec((1, d), lambda b, s: (0, 0))],
        out_specs=tile, out_shape=jax.ShapeDtypeStruct(h3.shape, F32),
        scratch_shapes=[pltpu.VMEM((POOL_HALO + POOL_ROWS, d), F32), pltpu.VMEM((POOL_ROWS, d), MXU)],
        compiler_params=_params(("parallel", "parallel")),
    )(h3, h3, w, scale)


def _pool_bwd(name, h3, dm3, w, scale):
    b_n, s_n, d = h3.shape
    tile, prev, nxt = _pool_specs(b_n, s_n, d)
    tiles = s_n // POOL_ROWS
    ext = POOL_ROWS + POOL_HALO

    def body(h_ref, halo_ref, dm_ref, dnext_ref, w_ref, sc_ref, dh_ref, dw_ref, dsc_ref, xs, pooled, ds, dp):
        b, s = pl.program_id(0), pl.program_id(1)

        @pl.when((b == 0) & (s == 0))
        def _():
            dw_ref[...] = jnp.zeros_like(dw_ref)
            dsc_ref[...] = jnp.zeros_like(dsc_ref)

        xs[0:POOL_HALO, :] = jnp.where(s > 0, halo_ref[...], 0.0)
        xs[POOL_HALO:, :] = h_ref[...]
        ds[0:POOL_ROWS, :] = dm_ref[...]
        ds[POOL_ROWS:, :] = jnp.where(s == tiles - 1, 0.0, dnext_ref[...])
        _pooled_into(xs, pooled, s * POOL_ROWS)
        for gi in range(len(POOL_WINDOWS)):
            cols = slice(gi * POOL_GROUP_DIM, (gi + 1) * POOL_GROUP_DIM)
            dyb = (ds[:, cols] * sc_ref[:, cols]).astype(MXU)
            dp[:, cols] = lax.dot_general(dyb, w_ref[gi], NT, preferred_element_type=F32)
            pg = pooled[:, cols]
            dw_ref[gi] += lax.dot_general(pg, dyb[:POOL_ROWS], TN, preferred_element_type=F32)
            ypre = jnp.dot(pg, w_ref[gi], preferred_element_type=F32)
            dsc_ref[:, cols] += jnp.sum(ds[0:POOL_ROWS, cols] * ypre, axis=0, keepdims=True)

        def chunk(st):
            for gi, w_len in enumerate(POOL_WINDOWS):
                cols = slice(gi * POOL_GROUP_DIM, (gi + 1) * POOL_GROUP_DIM)
                v = dp[pl.ds(st, POOL_CH + POOL_HALO), cols]
                q = v / _count(s * POOL_ROWS + st, POOL_CH + POOL_HALO, w_len)
                back = _pool_windows(q, None, gi, False)[:POOL_CH] - v[:POOL_CH]
                dh_ref[pl.ds(st, POOL_CH), cols] = ALPHA * ds[pl.ds(st, POOL_CH), cols] + back

        _chunks(POOL_ROWS, POOL_CH, chunk)

    return pl.pallas_call(
        body, name=name, grid=(b_n, tiles),
        in_specs=[tile, prev, tile, nxt, pl.BlockSpec(w.shape, lambda b, s: (0, 0, 0)), pl.BlockSpec((1, d), lambda b, s: (0, 0))],
        out_specs=[tile, pl.BlockSpec(w.shape, lambda b, s: (0, 0, 0)), pl.BlockSpec((1, d), lambda b, s: (0, 0))],
        out_shape=[jax.ShapeDtypeStruct(h3.shape, F32), jax.ShapeDtypeStruct(w.shape, F32), jax.ShapeDtypeStruct((1, d), F32)],
        scratch_shapes=[pltpu.VMEM((POOL_HALO + POOL_ROWS, d), F32), pltpu.VMEM((POOL_ROWS, d), MXU),
                        pltpu.VMEM((ext, d), F32), pltpu.VMEM((ext, d), F32)],
        compiler_params=_params(("arbitrary", "arbitrary")),
    )(h3, h3, dm3, dm3, w, scale)


ROPE_ROWS = 256


def _rope_tables(s_n):
    inv_freq = ROPE_THETA ** (-jnp.arange(0, HEAD_DIM, 2, dtype=F32) / HEAD_DIM)
    ang = jnp.arange(s_n, dtype=F32)[:, None] * inv_freq[None, :]
    cos, sin = jnp.cos(ang), jnp.sin(ang)
    cos_l = jnp.tile(cos, (1, 4))
    sin_l = jnp.tile(jnp.concatenate([-sin, sin], axis=1), (1, 2))
    return cos_l, sin_l


def _rope(name, x3, col0, n_col, cos_l, sin_l, scale, out_dtype, rotate=True, out_cols=None, out_col0=0, into=None):
    b_n, s_n, _ = x3.shape
    cw = N_HEADS * HEAD_DIM
    out_cols = n_col if out_cols is None else out_cols

    def body(x_ref, c_ref, s_ref, *rest):
        o_ref = rest[-1]
        lane = lax.broadcasted_iota(jnp.int32, (ROPE_ROWS, 128), 1)
        first_half = (lane % HEAD_DIM) < (HEAD_DIM // 2)
        cos, sin = c_ref[...], s_ref[...]
        for cb in range(cw // 128):
            cols = slice(cb * 128, (cb + 1) * 128)
            y = x_ref[:, cols]
            if rotate:
                other = jnp.where(first_half, pltpu.roll(y, 128 - HEAD_DIM // 2, 1), pltpu.roll(y, HEAD_DIM // 2, 1))
                y = y * cos + other * sin
            o_ref[:, cols] = (y if scale == 1.0 else y * scale).astype(o_ref.dtype)

    tile = pl.BlockSpec((None, ROPE_ROWS, cw), lambda b, s, c: (b, s, col0 + c))
    tab = pl.BlockSpec((ROPE_ROWS, 128), lambda b, s, c: (s, 0))
    extra, extra_specs, aliases = [], [], {}
    if into is not None:
        extra, extra_specs, aliases = [into], [pl.BlockSpec(memory_space=pl.ANY)], {3: 0}
    return pl.pallas_call(
        body, name=name, grid=(b_n, s_n // ROPE_ROWS, n_col), in_specs=[tile, tab, tab] + extra_specs,
        out_specs=pl.BlockSpec((None, ROPE_ROWS, cw), lambda b, s, c: (b, s, out_col0 + c)),
        out_shape=jax.ShapeDtypeStruct((b_n, s_n, out_cols * cw), out_dtype), input_output_aliases=aliases,
        compiler_params=_params(("parallel", "parallel", "parallel")),
    )(x3, cos_l, sin_l, *extra)


def _to_strided(a, d):
    if d == 1:
        return a
    b_n, s_n, c = a.shape
    return a.reshape(b_n, s_n // d, d, c).transpose(0, 2, 1, 3).reshape(b_n, s_n, c)


def _from_strided(a, d):
    if d == 1:
        return a
    b_n, s_n, c = a.shape
    return a.reshape(b_n, d, s_n // d, c).transpose(0, 2, 1, 3).reshape(b_n, s_n, c)


def _attn_fwd(name, q, q_col, k, v, blocks_per_seq):
    b_n, s_n, cw = k.shape
    nb = s_n // SPAN
    with_prev = blocks_per_seq > 1

    keys = 2 * SPAN if with_prev else SPAN
    pair = 2 * HEAD_DIM

    def body(*refs):
        if with_prev:
            q_ref, kc_ref, kp_ref, vc_ref, vp_ref, o_ref, lse_ref, k_all, v_all, s_buf, m_buf = refs
            k_all[0:SPAN, :] = kp_ref[...]
            k_all[SPAN:, :] = kc_ref[...]
            v_all[0:SPAN, :] = vp_ref[...]
            v_all[SPAN:, :] = vc_ref[...]
        else:
            q_ref, k_all, v_all, o_ref, lse_ref, s_buf, m_buf = refs
        n = pl.program_id(1)
        qi = lax.broadcasted_iota(jnp.int32, (SPAN, keys), 0)
        kj = lax.broadcasted_iota(jnp.int32, (SPAN, keys), 1)
        if with_prev:
            back = jnp.where((n % blocks_per_seq) != 0, 0, 2 * SPAN)
            mask = ((kj < SPAN) & (kj >= qi + back)) | ((kj >= SPAN) & (kj - SPAN <= qi))
        else:
            mask = kj <= qi
        lane = lax.broadcasted_iota(jnp.int32, (SPAN, pair), 1)
        low = lane < HEAD_DIM
        ones = jnp.ones((keys, pair), MXU)
        for h in range(N_HEADS):
            grp = slice((h // 2) * pair, (h // 2 + 1) * pair)
            q2 = q_ref[:, grp]
            qh = jnp.where(low if h % 2 == 0 else ~low, q2, jnp.zeros_like(q2))
            s = jnp.where(mask, lax.dot_general(qh, k_all[:, grp], NT, preferred_element_type=F32), -jnp.inf)
            s_buf[h] = s
            m_buf[h] = jnp.broadcast_to(jnp.max(s, -1, keepdims=True), (SPAN, pair))
        lse_all = jnp.zeros((SPAN, pair), F32)
        for hp in range(N_HEADS // 2):
            grp = slice(hp * pair, (hp + 1) * pair)
            v2 = v_all[:, grp]
            halves = []
            for h in (2 * hp, 2 * hp + 1):
                m = m_buf[h]
                p = jnp.exp(s_buf[h] - jnp.tile(m, (1, keys // pair))).astype(MXU)
                tot = jnp.dot(p, ones, preferred_element_type=F32)
                halves.append(jnp.dot(p, v2, preferred_element_type=F32) / tot)
                lse_all = jnp.where(lane == h, m + jnp.log(tot), lse_all)
            o_ref[:, grp] = jnp.where(low, halves[0], halves[1])
        lse_ref[...] = lse_all[:, 0:N_HEADS]

    cur = lambda b, n: (b, n, 0)
    prv = lambda b, n: (b, jnp.maximum(n - 1, 0), 0)
    blk = (None, SPAN, cw)
    kv_specs = [pl.BlockSpec(blk, cur), pl.BlockSpec(blk, prv)] if with_prev else [pl.BlockSpec(blk, cur)]
    operands = [q, k, k, v, v] if with_prev else [q, k, v]
    stage = [pltpu.VMEM((keys, cw), MXU)] * 2 if with_prev else []
    return pl.pallas_call(
        body, name=name, grid=(b_n, nb),
        in_specs=[pl.BlockSpec(blk, lambda b, n: (b, n, q_col))] + kv_specs + kv_specs,
        out_specs=[pl.BlockSpec(blk, cur), pl.BlockSpec((None, SPAN, N_HEADS), cur)],
        out_shape=[jax.ShapeDtypeStruct((b_n, s_n, cw), F32), jax.ShapeDtypeStruct((b_n, s_n, N_HEADS), F32)],
        scratch_shapes=stage + [pltpu.VMEM((N_HEADS, SPAN, keys), F32), pltpu.VMEM((N_HEADS, SPAN, pair), F32)],
        compiler_params=_params(("parallel", "parallel")),
    )(*operands)


def _attn_bwd(name, q, q_col, k, v, do, lse, dd, blocks_per_seq, dk_prev=None, dv_prev=None):
    b_n, s_n, cw = k.shape
    nb = s_n // SPAN
    with_next = blocks_per_seq > 1
    accumulate = dk_prev is not None
    rows = 2 * SPAN if with_next else SPAN

    def body(*refs):
        refs = list(refs)
        qc_ref, doc_ref, lsec_ref, ddc_ref = refs[:4]
        del refs[:4]
        if with_next:
            qn_ref, don_ref, lsen_ref, ddn_ref = refs[:4]
            del refs[:4]
        k_ref, v_ref = refs[:2]
        del refs[:2]
        if accumulate:
            dkp_ref, dvp_ref = refs[:2]
            del refs[:2]
        dq_ref, dk_ref, dv_ref = refs[:3]
        del refs[:3]
        if with_next:
            carry, q_all, do_all, side = refs[:4]
            del refs[:4]
            q_all[0:SPAN, :] = qc_ref[...]
            q_all[SPAN:, :] = qn_ref[...]
            do_all[0:SPAN, :] = doc_ref[...]
            do_all[SPAN:, :] = don_ref[...]
            side[0, 0:SPAN, :] = lsec_ref[...]
            side[0, SPAN:, :] = lsen_ref[...]
            side[1, 0:SPAN, :] = ddc_ref[...]
            side[1, SPAN:, :] = ddn_ref[...]
            lse_at = lambda h: side[0, :, h:h + 1]
            dd_at = lambda h: side[1, :, h:h + 1]
        else:
            q_all, do_all = qc_ref, doc_ref
            lse_at = lambda h: lsec_ref[:, h:h + 1]
            dd_at = lambda h: ddc_ref[:, h:h + 1]
        p_buf, ds_buf = refs
        n = pl.program_id(1)
        qi = lax.broadcasted_iota(jnp.int32, (rows, SPAN), 0)
        kj = lax.broadcasted_iota(jnp.int32, (rows, SPAN), 1)
        if with_next:
            first = (n % blocks_per_seq) == 0
            reach = jnp.where(((n + 1) % blocks_per_seq) != 0, SPAN, -2 * SPAN)
            mask = ((qi < SPAN) & (kj <= qi)) | ((qi >= SPAN) & (kj >= qi - reach))
        else:
            mask = kj <= qi
        pair = 2 * HEAD_DIM
        low = lax.broadcasted_iota(jnp.int32, (rows, pair), 1) < HEAD_DIM
        low_k = lax.broadcasted_iota(jnp.int32, (SPAN, pair), 1) < HEAD_DIM

        def pick(v, h, low_mask):
            return jnp.where(low_mask if h % 2 == 0 else ~low_mask, v, jnp.zeros_like(v))

        for h in range(N_HEADS):
            grp = slice((h // 2) * pair, (h // 2 + 1) * pair)
            s = lax.dot_general(pick(q_all[:, grp], h, low), k_ref[:, grp], NT, preferred_element_type=F32)
            p = jnp.where(mask, jnp.exp(s - lse_at(h)), 0.0)
            dp = lax.dot_general(pick(do_all[:, grp], h, low), v_ref[:, grp], NT, preferred_element_type=F32)
            p_buf[h] = p.astype(MXU)
            ds_buf[h] = (p * (dp - dd_at(h))).astype(MXU)
        for hp in range(N_HEADS // 2):
            grp = slice(hp * pair, (hp + 1) * pair)
            q2, do2, k2 = q_all[:, grp], do_all[:, grp], k_ref[:, grp]
            dv = dk = dq2 = None
            for h in (2 * hp, 2 * hp + 1):
                t_dv = lax.dot_general(p_buf[h], pick(do2, h, low), TN, preferred_element_type=F32)
                t_dk = lax.dot_general(ds_buf[h], pick(q2, h, low), TN, preferred_element_type=F32)
                t_dq = jnp.dot(ds_buf[h], pick(k2, h, low_k), preferred_element_type=F32)
                dv = t_dv if dv is None else dv + t_dv
                dk = t_dk if dk is None else dk + t_dk
                dq2 = t_dq if dq2 is None else dq2 + t_dq
            if accumulate:
                dk = dk + dkp_ref[:, grp]
                dv = dv + dvp_ref[:, grp]
            dk_ref[:, grp] = dk
            dv_ref[:, grp] = dv
            if with_next:
                dq_ref[:, grp] = dq2[:SPAN] + jnp.where(first, 0.0, carry[:, grp])
                carry[:, grp] = dq2[SPAN:]
            else:
                dq_ref[:, grp] = dq2

    cur = lambda b, n: (b, n, 0)
    nxt = lambda b, n: (b, jnp.minimum(n + 1, nb - 1), 0)
    blk, hblk = (None, SPAN, cw), (None, SPAN, N_HEADS)
    q_specs = lambda m: [pl.BlockSpec(blk, lambda b, n: (*m(b, n)[:2], q_col)), pl.BlockSpec(blk, m),
                         pl.BlockSpec(hblk, m), pl.BlockSpec(hblk, m)]
    in_specs = q_specs(cur) + (q_specs(nxt) if with_next else []) + [pl.BlockSpec(blk, cur)] * (4 if accumulate else 2)
    operands = [q, do, lse, dd] * (2 if with_next else 1) + [k, v] + ([dk_prev, dv_prev] if accumulate else [])
    out = jax.ShapeDtypeStruct((b_n, s_n, cw), F32)
    return pl.pallas_call(
        body, name=name, grid=(b_n, nb), in_specs=in_specs, out_specs=[pl.BlockSpec(blk, cur)] * 3,
        out_shape=[out, out, out],
        scratch_shapes=([pltpu.VMEM((SPAN, cw), F32), pltpu.VMEM((rows, cw), MXU), pltpu.VMEM((rows, cw), MXU),
                         pltpu.VMEM((2, rows, N_HEADS), F32)] if with_next else [])
        + [pltpu.VMEM((N_HEADS, rows, SPAN), MXU)] * 2,
        compiler_params=_params(("parallel", "arbitrary")),
    )(*operands)


MIX_ROWS = 256


def _group_weights(ls):
    m = functools.reduce(jnp.maximum, ls)
    es = [jnp.exp(l - m) for l in ls]
    tot = functools.reduce(lambda a, b: a + b, es)
    return [e / tot for e in es]


def _attn_mix(name, outs, lses):
    n, cw = outs[0].shape
    g_n = len(outs)

    def body(*refs):
        o_refs, l_refs, out_ref = refs[:g_n], refs[g_n:2 * g_n], refs[2 * g_n]
        ws = _group_weights([r[...] for r in l_refs])
        for h in range(N_HEADS):
            cols = slice(h * HEAD_DIM, (h + 1) * HEAD_DIM)
            acc = None
            for g in range(g_n):
                t = ws[g][:, h:h + 1] * o_refs[g][:, cols]
                acc = t if acc is None else acc + t
            out_ref[:, cols] = acc.astype(out_ref.dtype)

    tile = pl.BlockSpec((MIX_ROWS, cw), lambda i: (i, 0))
    htile = pl.BlockSpec((MIX_ROWS, N_HEADS), lambda i: (i, 0))
    return pl.pallas_call(
        body, name=name, grid=(n // MIX_ROWS,), in_specs=[tile] * g_n + [htile] * g_n, out_specs=tile,
        out_shape=jax.ShapeDtypeStruct((n, cw), MXU), compiler_params=_params(("parallel",)),
    )(*outs, *lses)


def _attn_mix_bwd(name, do, outs, lses):
    n, cw = do.shape
    g_n = len(outs)

    def body(*refs):
        do_ref, o_refs, l_refs = refs[0], refs[1:1 + g_n], refs[1 + g_n:1 + 2 * g_n]
        dog_refs, dd_refs = refs[1 + 2 * g_n:1 + 3 * g_n], refs[1 + 3 * g_n:]
        ws = _group_weights([r[...] for r in l_refs])
        for h in range(N_HEADS):
            cols = slice(h * HEAD_DIM, (h + 1) * HEAD_DIM)
            dh = do_ref[:, cols]
            o = None
            for g in range(g_n):
                t = ws[g][:, h:h + 1] * o_refs[g][:, cols]
                o = t if o is None else o + t
            dot = jnp.sum(dh * o, -1, keepdims=True)
            for g in range(g_n):
                wg = ws[g][:, h:h + 1]
                dog_refs[g][:, cols] = (wg * dh).astype(dog_refs[g].dtype)
                dd_refs[g][:, h:h + 1] = wg * dot

    tile = pl.BlockSpec((MIX_ROWS, cw), lambda i: (i, 0))
    htile = pl.BlockSpec((MIX_ROWS, N_HEADS), lambda i: (i, 0))
    return pl.pallas_call(
        body, name=name, grid=(n // MIX_ROWS,), in_specs=[tile] * (1 + g_n) + [htile] * g_n,
        out_specs=[tile] * g_n + [htile] * g_n,
        out_shape=[jax.ShapeDtypeStruct((n, cw), MXU)] * g_n + [jax.ShapeDtypeStruct((n, N_HEADS), F32)] * g_n,
        compiler_params=_params(("parallel",)),
    )(do, *outs, *lses)


def _loss_head(name, y, target):
    n, d = y.shape
    steps = n // LN_ROWS

    def body(y_ref, t_ref, dy_ref, l_ref, acc):
        i = pl.program_id(0)

        @pl.when(i == 0)
        def _():
            acc[...] = jnp.zeros_like(acc)

        def chunk(s):
            rows = pl.ds(s, LN_CH)
            err = y_ref[rows, :] - t_ref[rows, :]
            dy_ref[rows, :] = err / d
            acc[...] += _fold8(err * err)

        _chunks(LN_ROWS, LN_CH, chunk, LN_UNROLL)

        @pl.when(i == steps - 1)
        def _():
            l_ref[...] = jnp.full((8, 128), 0.5 / d, F32) * jnp.sum(acc[...])

    tile = pl.BlockSpec((LN_ROWS, d), lambda i: (i, 0))
    return pl.pallas_call(
        body, name=name, grid=(steps,), in_specs=[tile, tile],
        out_specs=[tile, pl.BlockSpec((8, 128), lambda i: (0, 0))],
        out_shape=[jax.ShapeDtypeStruct((n, d), F32), jax.ShapeDtypeStruct((8, 128), F32)],
        scratch_shapes=[pltpu.VMEM((8, d), F32)], compiler_params=_params(("arbitrary",)),
    )(y, target)


EW_TILE_BYTES = 1 << 20


def _row_tile(rows, cols):
    tr = 8
    while rows % (2 * tr) == 0 and 2 * tr * cols * 4 <= EW_TILE_BYTES:
        tr *= 2
    return tr if rows % tr == 0 else rows


def _add_halves(name, grad, recv, half, out_dtype):
    j_n, _, r, c = grad.shape
    tr = _row_tile(r, c)

    def body(half_ref, g_ref, r_ref, o_ref):
        o_ref[...] = (g_ref[...] + r_ref[...]).astype(o_ref.dtype)

    return pl.pallas_call(
        body, name=name, out_shape=jax.ShapeDtypeStruct((j_n, r, c), out_dtype),
        grid_spec=pltpu.PrefetchScalarGridSpec(
            num_scalar_prefetch=1, grid=(j_n, r // tr),
            in_specs=[pl.BlockSpec((None, None, tr, c), lambda j, i, hf: (j, hf[0], i, 0)),
                      pl.BlockSpec((None, tr, c), lambda j, i, hf: (j, i, 0))],
            out_specs=pl.BlockSpec((None, tr, c), lambda j, i, hf: (j, i, 0))),
        compiler_params=_params(("parallel", "parallel")),
    )(half, grad, recv)


def _add_chips(name, mine, recv, chip):
    j_n, r, c = mine.shape
    tr = _row_tile(r, c)

    def body(chip_ref, m_ref, r_ref, o_ref):
        total = m_ref[...].astype(F32)
        for k in range(j_n - 1):
            total = total + r_ref[k].astype(F32)
        o_ref[...] = total

    return pl.pallas_call(
        body, name=name, out_shape=jax.ShapeDtypeStruct((r, c), F32),
        grid_spec=pltpu.PrefetchScalarGridSpec(
            num_scalar_prefetch=1, grid=(r // tr,),
            in_specs=[pl.BlockSpec((None, tr, c), lambda i, ch: (ch[0], i, 0)),
                      pl.BlockSpec((j_n - 1, tr, c), lambda i, ch: (0, i, 0))],
            out_specs=pl.BlockSpec((tr, c), lambda i, ch: (i, 0))),
        compiler_params=_params(("parallel",)),
    )(chip, mine, recv)


def _adam_math(w, g, m, v):
    m = ADAM_B1 * m + (1.0 - ADAM_B1) * g
    v = ADAM_B2 * v + (1.0 - ADAM_B2) * (g * g)
    m_hat = m / (1.0 - ADAM_B1 ** ADAM_STEP)
    v_hat = v / (1.0 - ADAM_B2 ** ADAM_STEP)
    delta = -ADAM_LR * (m_hat / (jnp.sqrt(v_hat) + ADAM_EPS) + ADAM_WD * w)
    return delta, m, v


def _adam_halves(name, own, other, half, w, m, v):
    _, r, c = w.shape
    tr = _row_tile(r, c)

    def body(half_ref, own_ref, oth_ref, w_ref, m_ref, v_ref, g_out, d_out, m_out, v_out):
        g = jnp.where(pl.program_id(0) == half_ref[0], own_ref[...], oth_ref[...])
        delta, m_new, v_new = _adam_math(w_ref[...], g, m_ref[...], v_ref[...])
        g_out[...] = g
        d_out[...] = delta
        m_out[...] = m_new
        v_out[...] = v_new

    flat = pl.BlockSpec((tr, c), lambda h, i, hf: (i, 0))
    full = pl.BlockSpec((None, tr, c), lambda h, i, hf: (h, i, 0))
    out = jax.ShapeDtypeStruct(w.shape, F32)
    return pl.pallas_call(
        body, name=name, out_shape=[out] * 4,
        grid_spec=pltpu.PrefetchScalarGridSpec(num_scalar_prefetch=1, grid=(2, r // tr), in_specs=[flat, flat, full, full, full],
                                               out_specs=[full] * 4),
        compiler_params=_params(("parallel", "parallel")),
    )(half, own, other, w, m, v)


def _adam_small(name, g, w, m, v):
    def body(g_ref, w_ref, m_ref, v_ref, d_out, m_out, v_out):
        delta, m_new, v_new = _adam_math(w_ref[...], g_ref[...], m_ref[...], v_ref[...])
        d_out[...] = delta
        m_out[...] = m_new
        v_out[...] = v_new

    out = jax.ShapeDtypeStruct(w.shape, F32)
    return pl.pallas_call(body, name=name, out_shape=[out] * 3)(g, w, m, v)


def _place():
    x, y, c = lax.axis_index("x"), lax.axis_index("y"), lax.axis_index("c")
    chips = [(1 - x, y), (x, 1 - y), (1 - x, 1 - y)]
    return x, y, c, chips


ANY = pl.BlockSpec(memory_space=pl.ANY)


def _gather_shards(name, shards):
    n = len(shards)

    def body(*refs):
        ins, outs = refs[:n], refs[n:2 * n]
        send, recv, local = refs[2 * n:]
        x, y, c, chips = _place()
        me = 2 * x + y
        sibling = (x, y, 1 - c)

        def copy(t, k, src, dst, to):
            return pltpu.make_async_remote_copy(src_ref=src, dst_ref=dst, send_sem=send.at[t, k], recv_sem=recv.at[t, k],
                                                device_id=to, device_id_type=MESH)

        own = [pltpu.make_async_copy(ins[t], outs[t].at[me], local.at[t]) for t in range(n)]
        for cp in own:
            cp.start()
        sent = []
        for t in range(n):
            for k, (px, py) in enumerate(chips):
                sent.append(copy(t, k, ins[t].at[c], outs[t].at[me, c], (px, py, c)))
                sent[-1].start()
        for t in range(n):
            for k, (px, py) in enumerate(chips):
                slab = outs[t].at[2 * px + py, c]
                copy(t, k, ins[t].at[c], slab, (px, py, c)).wait_recv()
                sent.append(copy(t, 3 + k, slab, slab, sibling))
                sent[-1].start()
        for t in range(n):
            for k, (px, py) in enumerate(chips):
                slab = outs[t].at[2 * px + py, 1 - c]
                copy(t, 3 + k, slab, slab, sibling).wait_recv()
        for cp in sent:
            cp.wait_send()
        for cp in own:
            cp.wait()

    return pl.pallas_call(
        body, name=name, in_specs=[ANY] * n, out_specs=[ANY] * n,
        out_shape=[jax.ShapeDtypeStruct((N_CHIPS, *s.shape), s.dtype) for s in shards],
        scratch_shapes=[pltpu.SemaphoreType.DMA((n, 6)), pltpu.SemaphoreType.DMA((n, 6)), pltpu.SemaphoreType.DMA((n,))],
    )(*shards)


def _to_sibling(name, arrays, pick_other_half):
    n = len(arrays)

    def body(*refs):
        ins, outs = refs[:n], refs[n:2 * n]
        send, recv = refs[2 * n:]
        x, y, c, _ = _place()
        copies = []
        for t in range(n):
            src = ins[t].at[:, 1 - c] if pick_other_half else ins[t]
            copies.append(pltpu.make_async_remote_copy(src_ref=src, dst_ref=outs[t], send_sem=send.at[t], recv_sem=recv.at[t],
                                                       device_id=(x, y, 1 - c), device_id_type=MESH))
            copies[-1].start()
        for cp in copies:
            cp.wait()

    shapes = [(a.shape[0], *a.shape[2:]) if pick_other_half else a.shape for a in arrays]
    return pl.pallas_call(
        body, name=name, in_specs=[ANY] * n, out_specs=[ANY] * n,
        out_shape=[jax.ShapeDtypeStruct(s, a.dtype) for s, a in zip(shapes, arrays)],
        scratch_shapes=[pltpu.SemaphoreType.DMA((n,)), pltpu.SemaphoreType.DMA((n,))],
    )(*arrays)


def _to_owner_chips(name, arrays):
    n = len(arrays)

    def body(*refs):
        ins, outs = refs[:n], refs[n:2 * n]
        send, recv = refs[2 * n:]
        x, y, c, chips = _place()
        copies = []
        for t in range(n):
            for k, (px, py) in enumerate(chips):
                copies.append(pltpu.make_async_remote_copy(
                    src_ref=ins[t].at[2 * px + py], dst_ref=outs[t].at[k], send_sem=send.at[t, k], recv_sem=recv.at[t, k],
                    device_id=(px, py, c), device_id_type=MESH))
                copies[-1].start()
        for cp in copies:
            cp.wait()

    return pl.pallas_call(
        body, name=name, in_specs=[ANY] * n, out_specs=[ANY] * n,
        out_shape=[jax.ShapeDtypeStruct((N_CHIPS - 1, *a.shape[1:]), a.dtype) for a in arrays],
        scratch_shapes=[pltpu.SemaphoreType.DMA((n, 3)), pltpu.SemaphoreType.DMA((n, 3))],
    )(*arrays)


def _sum_all_devices(name, part):
    r, c = part.shape

    def body(p_ref, o_ref, buf, send, recv):
        x, y, cc, _ = _place()
        me = 4 * x + 2 * y + cc
        copies = []
        for mask in range(1, 8):
            fx, fy, fc = (mask >> 2) & 1, (mask >> 1) & 1, mask & 1
            to = (x ^ fx, y ^ fy, cc ^ fc)
            copies.append((mask, pltpu.make_async_remote_copy(
                src_ref=p_ref, dst_ref=buf.at[me], send_sem=send.at[mask - 1], recv_sem=recv.at[mask - 1],
                device_id=to, device_id_type=MESH)))
            copies[-1][1].start()
        buf[me] = p_ref[...]
        for mask, cp in copies:
            pltpu.make_async_remote_copy(src_ref=p_ref, dst_ref=buf.at[me ^ mask], send_sem=send.at[mask - 1],
                                         recv_sem=recv.at[mask - 1], device_id=(x, y, cc), device_id_type=MESH).wait_recv()
        for _, cp in copies:
            cp.wait_send()
        total = buf[0]
        for d in range(1, 8):
            total = total + buf[d]
        o_ref[...] = total

    vm = pl.BlockSpec(memory_space=pltpu.VMEM)
    return pl.pallas_call(
        body, name=name, in_specs=[vm], out_specs=vm, out_shape=jax.ShapeDtypeStruct((r, c), F32),
        scratch_shapes=[pltpu.VMEM((8, r, c), F32), pltpu.SemaphoreType.DMA((7,)), pltpu.SemaphoreType.DMA((7,))],
    )(part)


def kernel(x, pool_w, pool_scale, w_q, w_kv, w_o, ffn_w_gate, ffn_w_up, ffn_conv_w, ffn_conv_b, ffn_w_down, ln1_g, ln1_b, ln2_g, ln2_b, loss_target, m_pool_w, m_pool_scale, m_w_q, m_w_kv, m_w_o, m_ffn_w_gate, m_ffn_w_up, m_ffn_conv_w, m_ffn_conv_b, m_ffn_w_down, m_ln1_g, m_ln1_b, m_ln2_g, m_ln2_b, v_pool_w, v_pool_scale, v_w_q, v_w_kv, v_w_o, v_ffn_w_gate, v_ffn_w_up, v_ffn_conv_w, v_ffn_conv_b, v_ffn_w_down, v_ln1_g, v_ln1_b, v_ln2_g, v_ln2_b):
    b_n, s_n, d = x.shape
    n = b_n * s_n
    f = ffn_w_gate.shape[-1]
    qc = w_q.shape[-1]
    kvb = w_kv.shape[-1] // 2
    n_attn = w_q.shape[0]
    g_n = len(DILATIONS)
    cw = N_HEADS * HEAD_DIM
    xi, yi, ci = lax.axis_index("x"), lax.axis_index("y"), lax.axis_index("c")
    half = jnp.reshape(ci, (1,)).astype(jnp.int32)
    chip = jnp.reshape(2 * xi + yi, (1,)).astype(jnp.int32)

    sharded = {
        "pool_w": (pool_w, m_pool_w, v_pool_w, (2, 4 * 64, POOL_GROUP_DIM)),
        "pool_scale": (pool_scale, m_pool_scale, v_pool_scale, (2, 1, pool_scale.shape[-1])),
        "w_q": (w_q, m_w_q, v_w_q, (2, d, qc)),
        "w_kv": (w_kv, m_w_kv, v_w_kv, (2, d // 2, w_kv.shape[-1])),
        "w_o": (w_o, m_w_o, v_w_o, (2, w_o.shape[1], d)),
        "ffn_w_gate": (ffn_w_gate, m_ffn_w_gate, v_ffn_w_gate, (2, 2 * d, f)),
        "ffn_w_up": (ffn_w_up, m_ffn_w_up, v_ffn_w_up, (2, 2 * d, f)),
        "ffn_conv_w": (ffn_conv_w, m_ffn_conv_w, v_ffn_conv_w, (2, 6, f)),
        "ffn_w_down": (ffn_w_down, m_ffn_w_down, v_ffn_w_down, (2, 2 * f, d)),
    }
    mxu_weights = ("pool_w", "w_q", "w_kv", "w_o", "ffn_w_gate", "ffn_w_up", "ffn_w_down")
    names = list(sharded)
    gathered = _gather_shards("gather_weights", [
        sharded[k][0].reshape(sharded[k][3]).astype(MXU if k in mxu_weights else F32) for k in names])
    full = dict(zip(names, gathered))
    per_layer = lambda t, shape: [t.reshape(N_CHIPS, shape[0], *shape[1:])[:, i] for i in range(shape[0])]
    wq_l = per_layer(full["w_q"], (n_attn, d, qc))
    wkv_all = full["w_kv"].reshape(N_CHIPS, d, w_kv.shape[-1])
    wg_l = per_layer(full["ffn_w_gate"], (DEPTH, d, f))
    wu_l = per_layer(full["ffn_w_up"], (DEPTH, d, f))
    wd_l = per_layer(full["ffn_w_down"], (DEPTH, f, d))
    cw_l = per_layer(full["ffn_conv_w"], (DEPTH, 3, f))
    cb_l = [ffn_conv_b[i].reshape(N_CHIPS, 1, f) for i in range(DEPTH)]
    pw_nat = full["pool_w"].reshape(N_CHIPS, N_POOL_LAYERS, 4, 64, POOL_GROUP_DIM).transpose(1, 2, 0, 3, 4).reshape(
        N_POOL_LAYERS, 4, POOL_GROUP_DIM, POOL_GROUP_DIM)
    ps_nat = full["pool_scale"].reshape(N_CHIPS, N_POOL_LAYERS, -1).transpose(1, 0, 2).reshape(N_POOL_LAYERS, 1, d)
    wo_nat = full["w_o"].reshape(N_CHIPS, n_attn, -1, d).transpose(1, 0, 2, 3).reshape(n_attn, cw, d)
    cos_l, sin_l = _rope_tables(s_n)

    def vec(a, layer):
        return a[layer].reshape(1, d)

    h = x.reshape(n, d)
    hb = None
    saved = []
    k_str = v_str = None
    for layer in range(DEPTH):
        keep = {"h": h, "hb": hb}
        if layer < N_POOL_LAYERS:
            mix = _pool_fwd("pool_fwd", h.reshape(b_n, s_n, d), pw_nat[layer], ps_nat[layer]).reshape(n, d)
        else:
            a = layer - N_POOL_LAYERS
            q = _mm_cols("q_proj", hb, wq_l[a], (), F32)
            qr = _rope("q_rope", q.reshape(b_n, s_n, -1), 0, g_n, cos_l, sin_l, HEAD_DIM ** -0.5, MXU)
            q_str, lse_str, outs, lses = [], [], [], []
            for g, dil in enumerate(DILATIONS):
                if dil == 1:
                    qg, col = qr, g
                else:
                    qg, col = _to_strided(qr[:, :, g * cw:(g + 1) * cw], dil), 0
                o_g, lse_g = _attn_fwd(f"attn_fwd_d{dil}", qg, col, k_str[g], v_str[g], s_n // dil // SPAN)
                q_str.append((qg, col))
                lse_str.append(lse_g)
                outs.append(_from_strided(o_g, dil).reshape(n, cw))
                lses.append(_from_strided(lse_g, dil).reshape(n, N_HEADS))
            ob = _attn_mix("attn_mix", outs, lses)
            mix = _mm("o_proj", [(ob, (TM, cw), lambda m, _: (m, 0), wo_nat[a], (cw, d), lambda m, _: (0, 0))],
                      NN, (n // TM, 1), (n, d), F32, (TM, d), lambda m, _: (m, 0), 1)
            keep.update(q_str=q_str, lse_str=lse_str, outs=outs, lses=lses, ob=ob)
        r1, h1, h1b = _add_ln("ln_fwd", h, mix, vec(ln1_g, layer), vec(ln1_b, layer))
        gate = _mm_shard_out("gate_up_proj", h1b, wg_l[layer], (), F32)
        up = _mm_shard_out("gate_up_proj", h1b, wu_l[layer], (), F32)
        g4, u4 = gate.reshape(N_CHIPS, b_n, s_n, f), up.reshape(N_CHIPS, b_n, s_n, f)
        hmid = _glu_fwd("glu_fwd", g4, u4, cw_l[layer], cb_l[layer]).reshape(N_CHIPS, n, f)
        ffn = _mm_shard_in("down_proj", hmid, wd_l[layer], ())
        r2, h2, h2b = _add_ln("ln_fwd", h1, ffn, vec(ln2_g, layer), vec(ln2_b, layer))
        keep.update(r1=r1, h1b=h1b, g4=g4, u4=u4, hmid=hmid, r2=r2, h2b=h2b)
        saved.append(keep)
        if layer == N_POOL_LAYERS - 1:
            kv = _mm_cols("kv_proj", h2b, wkv_all, (), F32, cb=kvb).reshape(b_n, s_n, -1)
            k_nat = _rope("k_rope", kv, 0, g_n, cos_l, sin_l, 1.0, MXU)
            v_nat = _rope("v_cast", kv, g_n, g_n, cos_l, sin_l, 1.0, MXU, rotate=False)
            k_str = [_to_strided(k_nat[:, :, g * cw:(g + 1) * cw], dil) for g, dil in enumerate(DILATIONS)]
            v_str = [_to_strided(v_nat[:, :, g * cw:(g + 1) * cw], dil) for g, dil in enumerate(DILATIONS)]
        h, hb = h2, h2b

    dy, loss_tile = _loss_head("loss_head", h, loss_target.reshape(n, d))

    d_wq = d_wo = d_wg = d_wu = d_wd = None
    d_cw = [None] * DEPTH
    d_pw, d_ps = [None] * N_POOL_LAYERS, [None] * N_POOL_LAYERS
    d_ln = {}
    dk_str = dv_str = None
    d_top, top_scale, top_rest = dy, 1.0, []
    for layer in reversed(range(DEPTH)):
        sv = saved[layer]
        dr2, dr2b, d_ln["ln2", layer] = _ln_bwd("ln_bwd", sv["r2"], vec(ln2_g, layer), d_top, top_scale, top_rest)
        dhmid = _mm_nt_shard_out("down_bwd", dr2b, wd_l[layer], ())
        d_wd = _mm_tn("down_dw", sv["hmid"], f, dr2b, d, N_CHIPS, (N_CHIPS, DEPTH, f, d), (None, None, f, d),
                      lambda j: (j, layer, 0, 0), into=d_wd, a_lead=lambda j: (j,))
        dg4, du4, d_cw[layer] = _glu_bwd("glu_bwd", dhmid.reshape(N_CHIPS, b_n, s_n, f), sv["g4"], sv["u4"],
                                         cw_l[layer], cb_l[layer])
        dg, du = dg4.reshape(N_CHIPS, n, f), du4.reshape(N_CHIPS, n, f)
        dh1 = _mm_nt_shard_in("gate_up_bwd", [(dg, wg_l[layer], ()), (du, wu_l[layer], ())])
        d_wg = _mm_tn("gate_up_dw", sv["h1b"], d, dg, f, N_CHIPS, (N_CHIPS, DEPTH, d, f), (None, None, d, f),
                      lambda j: (j, layer, 0, 0), into=d_wg, b_lead=lambda j: (j,))
        d_wu = _mm_tn("gate_up_dw", sv["h1b"], d, du, f, N_CHIPS, (N_CHIPS, DEPTH, d, f), (None, None, d, f),
                      lambda j: (j, layer, 0, 0), into=d_wu, b_lead=lambda j: (j,))
        dr1, dr1b, d_ln["ln1", layer] = _ln_bwd("ln_bwd", sv["r1"], vec(ln1_g, layer), dr2, ALPHA, [dh1])
        if layer < N_POOL_LAYERS:
            d_in, d_pw[layer], d_ps[layer] = _pool_bwd("pool_bwd", sv["h"].reshape(b_n, s_n, d),
                                                       dr1.reshape(b_n, s_n, d), pw_nat[layer], ps_nat[layer])
            d_top, top_scale, top_rest = d_in.reshape(n, d), 1.0, []
        else:
            a = layer - N_POOL_LAYERS
            do = _mm("o_bwd", [(dr1b, (TM, d), lambda m, _: (m, 0), wo_nat[a], (cw, d), lambda m, _: (0, 0))],
                     NT, (n // TM, 1), (n, cw), F32, (TM, cw), lambda m, _: (m, 0), 1)
            d_wo = _mm_tn("o_dw", sv["ob"], cw // N_CHIPS, dr1b, d, N_CHIPS, (N_CHIPS, n_attn, cw // N_CHIPS, d),
                          (None, None, cw // N_CHIPS, d), lambda j: (j, a, 0, 0), into=d_wo)
            mixed = _attn_mix_bwd("attn_mix_bwd", do, sv["outs"], sv["lses"])
            dq_nat, dk_new, dv_new = [], [], []
            for g, dil in enumerate(DILATIONS):
                do_g = _to_strided(mixed[g].reshape(b_n, s_n, cw), dil)
                dd_g = _to_strided(mixed[g_n + g].reshape(b_n, s_n, N_HEADS), dil)
                qg, col = sv["q_str"][g]
                dq_g, dk_g, dv_g = _attn_bwd(f"attn_bwd_d{dil}", qg, col, k_str[g], v_str[g], do_g, sv["lse_str"][g], dd_g,
                                             s_n // dil // SPAN, *((dk_str[g], dv_str[g]) if dk_str else ()))
                dq_nat.append(_from_strided(dq_g, dil))
                dk_new.append(dk_g)
                dv_new.append(dv_g)
            dk_str, dv_str = dk_new, dv_new
            dq = None
            for g in range(g_n):
                dq = _rope("q_rope_bwd", dq_nat[g], 0, 1, cos_l, -sin_l, HEAD_DIM ** -0.5, MXU, out_cols=g_n, out_col0=g, into=dq)
            dq = dq.reshape(n, g_n * cw)
            d_attn = _mm_nt_cols_in("q_bwd", dq, wq_l[a], (), qc)
            d_wq = _mm_tn("q_dw", sv["hb"], d, dq, qc, N_CHIPS, (N_CHIPS, n_attn, d, qc), (None, None, d, qc),
                          lambda j: (j, a, 0, 0), into=d_wq)
            d_top, top_scale, top_rest = dr1, ALPHA, [d_attn]
            if a == 0:
                dkv = None
                for g, dil in enumerate(DILATIONS):
                    dkv = _rope("k_rope_bwd", _from_strided(dk_str[g], dil), 0, 1, cos_l, -sin_l, 1.0, MXU,
                                out_cols=2 * g_n, out_col0=g, into=dkv)
                for g, dil in enumerate(DILATIONS):
                    dkv = _rope("v_cast_bwd", _from_strided(dv_str[g], dil), 0, 1, cos_l, sin_l, 1.0, MXU, rotate=False,
                                out_cols=2 * g_n, out_col0=g_n + g, into=dkv)
                dkv = dkv.reshape(n, 2 * g_n * cw)
                h_kv = saved[N_POOL_LAYERS - 1]["h2b"]
                top_rest = top_rest + [_mm_nt_cols_in("kv_bwd", dkv, wkv_all, (), kvb)]
                d_wkv = _mm_tn("kv_dw", h_kv, d, dkv, kvb, 2 * N_CHIPS, (2 * N_CHIPS, d, kvb), (None, d, kvb), lambda q: (q, 0, 0))
    grad_x = d_top.reshape(b_n, s_n, d)

    d_wkv = d_wkv.reshape(N_CHIPS, 2, d, kvb).transpose(0, 2, 1, 3)
    d_pw_all = jnp.stack(d_pw).reshape(N_POOL_LAYERS, 4, N_CHIPS, 64, POOL_GROUP_DIM).transpose(2, 0, 1, 3, 4)
    d_ps_all = jnp.stack(d_ps).reshape(N_POOL_LAYERS, N_CHIPS, -1).transpose(1, 0, 2)
    d_cw_all = jnp.stack([t[:, :3] for t in d_cw], axis=1)
    local = {
        "pool_w": d_pw_all, "pool_scale": d_ps_all, "w_q": d_wq, "w_kv": d_wkv, "w_o": d_wo,
        "ffn_w_gate": d_wg, "ffn_w_up": d_wu, "ffn_conv_w": d_cw_all, "ffn_w_down": d_wd,
    }
    grads4 = [local[k].reshape(N_CHIPS, *sharded[k][3]) for k in names]
    from_sibling = _to_sibling("grads_to_sibling", grads4, True)
    core_sums = [_add_halves(f"add_halves_{k}", g4_, r_, half, MXU if k in mxu_weights and k != "pool_w" else F32)
                 for k, g4_, r_ in zip(names, grads4, from_sibling)]
    from_chips = _to_owner_chips("grads_to_owner", core_sums)
    owned = [_add_chips(f"add_chips_{k}", s_, r_, chip) for k, s_, r_ in zip(names, core_sums, from_chips)]
    others = _to_sibling("halves_to_sibling", owned, False)

    out_grad, out_delta, out_m, out_v = {}, {}, {}, {}
    for k, own, oth in zip(names, owned, others):
        w_, m_, v_, shape = sharded[k]
        res = _adam_halves(f"adam_{k}", own, oth, half, w_.reshape(shape), m_.reshape(shape), v_.reshape(shape))
        out_grad[k], out_delta[k], out_m[k], out_v[k] = (t.reshape(w_.shape) for t in res)

    d_cb = jnp.stack([t[:, 3] for t in d_cw], axis=0).reshape(DEPTH * N_CHIPS * f // d, d)
    ln_rows = jnp.concatenate([jnp.stack([d_ln[which, layer][row] for layer in range(DEPTH)])
                               for which, row in (("ln1", 0), ("ln1", 1), ("ln2", 0), ("ln2", 1))])
    rows = jnp.concatenate([ln_rows, d_cb, jnp.broadcast_to(loss_tile[0:1, 0:1], (1, d))])
    pad = (-rows.shape[0]) % 8
    total = _sum_all_devices("sum_small", jnp.pad(rows, ((0, pad), (0, 0))))
    small = {"ln1_g": total[0:4], "ln1_b": total[4:8], "ln2_g": total[8:12], "ln2_b": total[12:16],
             "ffn_conv_b": total[16:16 + d_cb.shape[0]].reshape(ffn_conv_b.shape)}
    loss = total[16 + d_cb.shape[0], 0]
    small_in = {"ln1_g": (ln1_g, m_ln1_g, v_ln1_g), "ln1_b": (ln1_b, m_ln1_b, v_ln1_b), "ln2_g": (ln2_g, m_ln2_g, v_ln2_g),
                "ln2_b": (ln2_b, m_ln2_b, v_ln2_b), "ffn_conv_b": (ffn_conv_b, m_ffn_conv_b, v_ffn_conv_b)}
    for k, (w_, m_, v_) in small_in.items():
        out_grad[k] = small[k]
        out_delta[k], out_m[k], out_v[k] = _adam_small(f"adam_{k}", small[k], w_, m_, v_)

    order = ["pool_w", "pool_scale", "w_q", "w_kv", "w_o", "ffn_w_gate", "ffn_w_up", "ffn_conv_w", "ffn_conv_b",
             "ffn_w_down", "ln1_g", "ln1_b", "ln2_g", "ln2_b"]
    return (loss, grad_x, *[out_grad[k] for k in order], *[out_delta[k] for k in order],
            *[out_m[k] for k in order], *[out_v[k] for k in order])
```

```python
import functools
import math

import jax
import jax.numpy as jnp
from jax import lax
from jax.experimental import pallas as pl
from jax.experimental.pallas import tpu as pltpu

F32 = jnp.float32
BF16 = jnp.bfloat16
MXU = jnp.bfloat16

DEPTH = 4
N_POOL_LAYERS = 2
POOL_WINDOWS = (2, 4, 8, 16)
POOL_GROUP_DIM = 256
HEAD_DIM = 64
N_HEADS = 16
DILATIONS = (1, 4, 16)
SPAN = 128
ROPE_THETA = 10000.0
ALPHA = (2.0 * DEPTH) ** 0.25
LN_EPS = 1e-5
ADAM_LR, ADAM_B1, ADAM_B2, ADAM_EPS, ADAM_WD, ADAM_STEP = 0.001, 0.9, 0.999, 1e-08, 0.01, 10

N_CHIPS = 4
VMEM_LIMIT = 56 * 1024 * 1024
MESH = pl.DeviceIdType.MESH

NN = (((1,), (0,)), ((), ()))
NT = (((1,), (1,)), ((), ()))
TN = (((0,), (0,)), ((), ()))


def _params(sem=None):
    return pltpu.CompilerParams(dimension_semantics=sem, vmem_limit_bytes=VMEM_LIMIT)


def _chunks(n_rows, ch, fn, unroll=1):
    def step(i, carry):
        fn(pl.multiple_of(i * ch, ch))
        return carry

    lax.fori_loop(0, n_rows // ch, step, 0, unroll=unroll)


def _fold8(v):
    return jnp.sum(v.reshape(v.shape[0] // 8, 8, v.shape[1]), axis=0)


def _down(v, k):
    return pltpu.roll(v, k, 0)


def _up(v, k):
    return pltpu.roll(v, v.shape[0] - k, 0)


def _mm(name, pairs, dims, grid, out_shape, out_dtype, out_block, out_map, nk, into=None, carry=()):
    n_pairs = len(pairs)
    kax = len(grid) - 1
    n_c = len(carry)
    n_in = 2 * n_pairs + (1 if into is not None else 0)

    def body(*refs):
        o_ref = refs[n_in + n_c]
        if n_c:
            job = _ShardGather(refs[n_in:n_in + n_c], refs[n_in + n_c + 1:n_in + 2 * n_c + 1], *refs[n_in + 2 * n_c + 1:])
            ids = [pl.program_id(i) for i in range(len(grid))]
            at_first = functools.reduce(lambda a, b: a & b, [i == 0 for i in ids])
            at_last = functools.reduce(lambda a, b: a & b, [i == g - 1 for i, g in zip(ids, grid)])
            pl.when(at_first)(job.begin)
        part = None
        for p in range(n_pairs):
            t = lax.dot_general(refs[2 * p][...], refs[2 * p + 1][...], dims, preferred_element_type=F32)
            part = t if part is None else part + t
        if nk == 1:
            o_ref[...] = part.astype(o_ref.dtype)
        else:
            k = pl.program_id(kax)

            @pl.when(k == 0)
            def _():
                o_ref[...] = part

            @pl.when(k > 0)
            def _():
                o_ref[...] += part

        if n_c:
            pl.when(at_last)(job.end)

    operands, in_specs = [], []
    for a, a_block, a_map, b, b_block, b_map in pairs:
        operands += [a, b]
        in_specs += [pl.BlockSpec(a_block, a_map), pl.BlockSpec(b_block, b_map)]
    aliases = {}
    if into is not None:
        operands.append(into)
        in_specs.append(pl.BlockSpec(memory_space=pl.ANY))
        aliases = {2 * n_pairs: 0}
    assert nk == 1 or out_dtype == F32
    operands += list(carry)
    in_specs += [pl.BlockSpec(memory_space=pl.ANY)] * n_c
    sem = ("arbitrary",) * len(grid) if n_c else ("parallel",) * kax + ("arbitrary",)
    res = pl.pallas_call(
        body, name=name, grid=grid, in_specs=in_specs,
        out_specs=[pl.BlockSpec(out_block, out_map)] + [pl.BlockSpec(memory_space=pl.ANY)] * n_c,
        out_shape=[jax.ShapeDtypeStruct(out_shape, out_dtype)] + _ShardGather.out_shapes(carry),
        input_output_aliases=aliases, scratch_shapes=_ShardGather.scratch(n_c) if n_c else [],
        compiler_params=_params(sem),
    )(*operands)
    return (res[0], res[1:]) if n_c else res[0]


TM = 2048


def _mm_cols(name, a, w, w_idx, out_dtype, cb=None, carry=()):
    n, k = a.shape
    j_n, c = w.shape[0], w.shape[-1]
    cb = c if cb is None else cb
    s = c // cb
    wb = (None,) * (w.ndim - 2) + (k, cb)
    return _mm(name, [(a, (TM, k), lambda q, m, _: (m, 0), w, wb, lambda q, m, _: (q // s, *w_idx, 0, q % s))], NN,
               (j_n * s, n // TM, 1), (n, j_n * c), out_dtype, (TM, cb), lambda q, m, _: (m, q), 1, carry=carry)


def _mm_shard_out(name, a, w, w_idx, out_dtype, carry=()):
    n, k = a.shape
    j_n, c = w.shape[0], w.shape[-1]
    wb = (None,) * (w.ndim - 2) + (k, c)
    return _mm(name, [(a, (TM, k), lambda j, m, _: (m, 0), w, wb, lambda j, m, _: (j, *w_idx, 0, 0))], NN,
               (j_n, n // TM, 1), (j_n, n, c), out_dtype, (None, TM, c), lambda j, m, _: (j, m, 0), 1, carry=carry)


def _mm_shard_in(name, a4, w, w_idx, carry=()):
    j_n, n, c = a4.shape
    k = w.shape[-1]
    wb = (None,) * (w.ndim - 2) + (c, k)
    return _mm(name, [(a4, (None, TM, c), lambda m, j: (j, m, 0), w, wb, lambda m, j: (j, *w_idx, 0, 0))], NN,
               (n // TM, j_n), (n, k), F32, (TM, k), lambda m, j: (m, 0), j_n, carry=carry)


def _mm_nt_shard_out(name, a, w, w_idx):
    n, k = a.shape
    j_n, c = w.shape[0], w.shape[-2]
    wb = (None,) * (w.ndim - 2) + (c, k)
    return _mm(name, [(a, (TM, k), lambda j, m, _: (m, 0), w, wb, lambda j, m, _: (j, *w_idx, 0, 0))], NT,
               (j_n, n // TM, 1), (j_n, n, c), F32, (None, TM, c), lambda j, m, _: (j, m, 0), 1)


def _mm_nt_shard_in(name, terms):
    pairs = []
    for a4, w, w_idx in terms:
        j_n, n, c = a4.shape
        k = w.shape[-2]
        wb = (None,) * (w.ndim - 2) + (k, c)
        pairs.append((a4, (None, TM, c), lambda m, j: (j, m, 0), w, wb,
                      functools.partial(lambda m, j, w_idx: (j, *w_idx, 0, 0), w_idx=w_idx)))
    return _mm(name, pairs, NT, (n // TM, j_n), (n, k), F32, (TM, k), lambda m, j: (m, 0), j_n)


def _mm_nt_cols_in(name, a, w, w_idx, cb):
    n, ct = a.shape
    j_n, k, c = w.shape[0], w.shape[-2], w.shape[-1]
    s = c // cb
    wb = (None,) * (w.ndim - 2) + (k, cb)
    return _mm(name, [(a, (TM, cb), lambda m, q: (m, q), w, wb, lambda m, q: (q // s, *w_idx, 0, q % s))], NT,
               (n // TM, ct // cb), (n, k), F32, (TM, k), lambda m, q: (m, 0), ct // cb)


def _mm_tn(name, a, a_cols, b, b_cols, n_blocks, out_shape, out_block, out_map, into=None, a_lead=None, b_lead=None):
    n = a.shape[-2]
    a_nb = a.shape[-1] // a_cols
    b_nb = b.shape[-1] // b_cols
    if a_lead is None:
        a_block, a_map = (TM, a_cols), lambda q, t: (t, q if a_nb > 1 else 0)
    else:
        a_block, a_map = (None, TM, a_cols), lambda q, t: (*a_lead(q), t, 0)
    if b_lead is None:
        b_block, b_map = (TM, b_cols), lambda q, t: (t, q if b_nb > 1 else 0)
    else:
        b_block, b_map = (None, TM, b_cols), lambda q, t: (*b_lead(q), t, 0)
    return _mm(name, [(a, a_block, a_map, b, b_block, b_map)], TN, (n_blocks, n // TM), out_shape, F32,
               out_block, lambda q, t: out_map(q), n // TM, into=into)


LN_ROWS = 512
LN_CH = 16
LN_UNROLL = 4


def _ln_stats(r):
    mu = jnp.mean(r, -1, keepdims=True)
    xc = r - mu
    var = jnp.mean(xc * xc, -1, keepdims=True)
    return xc, lax.rsqrt(var + LN_EPS)


def _add_ln(name, a, mix, g, b):
    n, d = a.shape

    def body(a_ref, m_ref, g_ref, b_ref, r_ref, h_ref, hb_ref):
        gg, bb = g_ref[...], b_ref[...]

        def chunk(s):
            rows = pl.ds(s, LN_CH)
            r = ALPHA * a_ref[rows, :] + m_ref[rows, :]
            xc, rstd = _ln_stats(r)
            y = xc * rstd * gg + bb
            r_ref[rows, :] = r
            h_ref[rows, :] = y
            hb_ref[rows, :] = y.astype(MXU)

        _chunks(LN_ROWS, LN_CH, chunk, LN_UNROLL)

    tile = pl.BlockSpec((LN_ROWS, d), lambda i: (i, 0))
    vec = pl.BlockSpec((1, d), lambda i: (0, 0))
    return pl.pallas_call(
        body, name=name, grid=(n // LN_ROWS,), in_specs=[tile, tile, vec, vec], out_specs=[tile, tile, tile],
        out_shape=[jax.ShapeDtypeStruct((n, d), F32), jax.ShapeDtypeStruct((n, d), F32), jax.ShapeDtypeStruct((n, d), MXU)],
        compiler_params=_params(("parallel",)),
    )(a, mix, g, b)


def _ln_bwd(name, r, g, d_a, scale_a, d_rest):
    n, d = r.shape
    n_rest = len(d_rest)
    steps = n // LN_ROWS

    def body(*refs):
        r_ref, g_ref, da_ref = refs[:3]
        rest = refs[3:3 + n_rest]
        dr_ref, drb_ref, gb_ref, acc = refs[3 + n_rest:]
        i = pl.program_id(0)

        @pl.when(i == 0)
        def _():
            acc[...] = jnp.zeros_like(acc)

        gg = g_ref[...]

        def chunk(s):
            rows = pl.ds(s, LN_CH)
            xc, rstd = _ln_stats(r_ref[rows, :])
            xhat = xc * rstd
            dy = da_ref[rows, :] if scale_a == 1.0 else scale_a * da_ref[rows, :]
            for t in rest:
                dy = dy + t[rows, :]
            dyg = dy * gg
            m1 = jnp.mean(dyg, -1, keepdims=True)
            m2 = jnp.mean(dyg * xhat, -1, keepdims=True)
            dr = rstd * (dyg - m1 - xhat * m2)
            dr_ref[rows, :] = dr
            drb_ref[rows, :] = dr.astype(MXU)
            acc[0] += _fold8(dy * xhat)
            acc[1] += _fold8(dy)

        _chunks(LN_ROWS, LN_CH, chunk, LN_UNROLL)

        @pl.when(i == steps - 1)
        def _():
            gb_ref[0:1, :] = jnp.sum(acc[0], axis=0, keepdims=True)
            gb_ref[1:2, :] = jnp.sum(acc[1], axis=0, keepdims=True)

    tile = pl.BlockSpec((LN_ROWS, d), lambda i: (i, 0))
    vec = pl.BlockSpec((1, d), lambda i: (0, 0))
    return pl.pallas_call(
        body, name=name, grid=(steps,), in_specs=[tile, vec, tile] + [tile] * n_rest,
        out_specs=[tile, tile, pl.BlockSpec((2, d), lambda i: (0, 0))],
        out_shape=[jax.ShapeDtypeStruct((n, d), F32), jax.ShapeDtypeStruct((n, d), MXU), jax.ShapeDtypeStruct((2, d), F32)],
        scratch_shapes=[pltpu.VMEM((2, 8, d), F32)],
        compiler_params=_params(("arbitrary",)),
    )(r, g, d_a, *d_rest)


FFN_ROWS = 512
FFN_CH = 32
GELU_C1 = math.sqrt(2.0 / math.pi)
GELU_C2 = 0.044715


def _conv3(v, prev8, w0, w1, w2, bias):
    n = v.shape[0]
    ext = jnp.concatenate([prev8, v], axis=0)
    g1 = _down(ext, 1)[8:8 + n]
    g2 = _down(ext, 2)[8:8 + n]
    return bias + w0 * g2 + w1 * g1 + w2 * v, g1, g2


def _glu_fwd(name, g4, u4, conv_w, conv_b):
    j_n, b_n, s_n, f = g4.shape
    tiles = s_n // FFN_ROWS

    def body(g_ref, halo_ref, u_ref, w_ref, b_ref, o_ref, gs):
        s = pl.program_id(2)
        gs[0:8, :] = jnp.where(s > 0, halo_ref[...], 0.0)
        gs[8:, :] = g_ref[...]
        w0, w1, w2, bias = w_ref[0:1, :], w_ref[1:2, :], w_ref[2:3, :], b_ref[...]

        def chunk(st):
            v = gs[pl.ds(pl.multiple_of(st + 8, 8), FFN_CH), :]
            conv, _, _ = _conv3(v, gs[pl.ds(st, 8), :], w0, w1, w2, bias)
            cdf = 0.5 * (1.0 + jnp.tanh(GELU_C1 * (conv + GELU_C2 * (conv * conv * conv))))
            o_ref[pl.ds(st, FFN_CH), :] = (conv * cdf * u_ref[pl.ds(st, FFN_CH), :]).astype(o_ref.dtype)

        _chunks(FFN_ROWS, FFN_CH, chunk)

    tile = pl.BlockSpec((None, None, FFN_ROWS, f), lambda j, b, s: (j, b, s, 0))
    halo = pl.BlockSpec((None, None, 8, f), lambda j, b, s: (j, b, jnp.maximum(s * (FFN_ROWS // 8) - 1, 0), 0))
    return pl.pallas_call(
        body, name=name, grid=(j_n, b_n, tiles),
        in_specs=[tile, halo, tile,
                  pl.BlockSpec((None, 3, f), lambda j, b, s: (j, 0, 0)),
                  pl.BlockSpec((None, 1, f), lambda j, b, s: (j, 0, 0))],
        out_specs=tile, out_shape=jax.ShapeDtypeStruct(g4.shape, MXU),
        scratch_shapes=[pltpu.VMEM((8 + FFN_ROWS, f), F32)],
        compiler_params=_params(("parallel", "parallel", "parallel")),
    )(g4, g4, u4, conv_w, conv_b)


def _glu_bwd(name, dh4, g4, u4, conv_w, conv_b):
    j_n, b_n, s_n, f = g4.shape
    tiles = s_n // FFN_ROWS
    ext = FFN_CH + 8

    def body(d_ref, dnext_ref, g_ref, gprev_ref, gnext_ref, u_ref, unext_ref, w_ref, b_ref,
             dg_ref, du_ref, wb_ref, gs, us, ds, acc):
        b, s = pl.program_id(1), pl.program_id(2)
        last = s == tiles - 1

        @pl.when((b == 0) & (s == 0))
        def _():
            acc[...] = jnp.zeros_like(acc)

        gs[0:8, :] = jnp.where(s > 0, gprev_ref[...], 0.0)
        gs[8:8 + FFN_ROWS, :] = g_ref[...]
        gs[8 + FFN_ROWS:, :] = gnext_ref[...]
        us[0:FFN_ROWS, :] = u_ref[...]
        us[FFN_ROWS:, :] = unext_ref[...]
        ds[0:FFN_ROWS, :] = d_ref[...]
        ds[FFN_ROWS:, :] = jnp.where(last, 0.0, dnext_ref[...])
        w0, w1, w2, bias = w_ref[0:1, :], w_ref[1:2, :], w_ref[2:3, :], b_ref[...]

        def chunk(st):
            v = gs[pl.ds(pl.multiple_of(st + 8, 8), ext), :]
            conv, g1, g2 = _conv3(v, gs[pl.ds(st, 8), :], w0, w1, w2, bias)
            th = jnp.tanh(GELU_C1 * (conv + GELU_C2 * (conv * conv * conv)))
            cdf = 0.5 * (1.0 + th)
            dact = cdf + conv * (0.5 * GELU_C1) * (1.0 - th * th) * (1.0 + (3.0 * GELU_C2) * (conv * conv))
            de = ds[pl.ds(st, ext), :]
            dconv = de * us[pl.ds(st, ext), :] * dact
            du_ref[pl.ds(st, FFN_CH), :] = (de * (conv * cdf))[:FFN_CH].astype(du_ref.dtype)
            dg = w2 * dconv + w1 * _up(dconv, 1) + w0 * _up(dconv, 2)
            dg_ref[pl.ds(st, FFN_CH), :] = dg[:FFN_CH].astype(dg_ref.dtype)
            dc = dconv[:FFN_CH]
            acc[0] += _fold8(dc * g2[:FFN_CH])
            acc[1] += _fold8(dc * g1[:FFN_CH])
            acc[2] += _fold8(dc * v[:FFN_CH])
            acc[3] += _fold8(dc)

        _chunks(FFN_ROWS, FFN_CH, chunk)

        @pl.when((b == b_n - 1) & last)
        def _():
            for k in range(4):
                wb_ref[k:k + 1, :] = jnp.sum(acc[k], axis=0, keepdims=True)

    blocks8 = FFN_ROWS // 8
    tile = pl.BlockSpec((None, None, FFN_ROWS, f), lambda j, b, s: (j, b, s, 0))
    prev = pl.BlockSpec((None, None, 8, f), lambda j, b, s: (j, b, jnp.maximum(s * blocks8 - 1, 0), 0))
    nxt = pl.BlockSpec((None, None, 8, f), lambda j, b, s: (j, b, jnp.minimum((s + 1) * blocks8, s_n // 8 - 1), 0))
    return pl.pallas_call(
        body, name=name, grid=(j_n, b_n, tiles),
        in_specs=[tile, nxt, tile, prev, nxt, tile, nxt,
                  pl.BlockSpec((None, 3, f), lambda j, b, s: (j, 0, 0)),
                  pl.BlockSpec((None, 1, f), lambda j, b, s: (j, 0, 0))],
        out_specs=[tile, tile, pl.BlockSpec((None, 4, f), lambda j, b, s: (j, 0, 0))],
        out_shape=[jax.ShapeDtypeStruct(g4.shape, MXU), jax.ShapeDtypeStruct(g4.shape, MXU),
                   jax.ShapeDtypeStruct((j_n, 4, f), F32)],
        scratch_shapes=[pltpu.VMEM((16 + FFN_ROWS, f), F32), pltpu.VMEM((8 + FFN_ROWS, f), F32),
                        pltpu.VMEM((8 + FFN_ROWS, f), F32), pltpu.VMEM((4, 8, f), F32)],
        compiler_params=_params(("parallel", "arbitrary", "arbitrary")),
    )(dh4, dh4, g4, g4, g4, u4, u4, conv_w, conv_b)


POOL_ROWS = 512
POOL_CH = 32
POOL_HALO = 16


def _pool_windows(v, t0, gi, causal):
    shift = _down if causal else _up
    acc, k = v, 1
    while k < POOL_WINDOWS[gi]:
        acc = acc + shift(acc, k)
        k *= 2
    return acc


def _count(t0, n, w):
    t = t0 + lax.broadcasted_iota(jnp.int32, (n, 1), 0)
    return jnp.minimum(t + 1, w).astype(F32)


def _pooled_into(xs, pooled, t_tile):
    def chunk(st):
        for gi, w in enumerate(POOL_WINDOWS):
            cols = slice(gi * POOL_GROUP_DIM, (gi + 1) * POOL_GROUP_DIM)
            v = xs[pl.ds(st, POOL_CH + POOL_HALO), cols]
            sums = _pool_windows(v, None, gi, True)[POOL_HALO:]
            val = sums / _count(t_tile + st, POOL_CH, w) - v[POOL_HALO:]
            pooled[pl.ds(st, POOL_CH), cols] = val.astype(pooled.dtype)

    _chunks(POOL_ROWS, POOL_CH, chunk)


def _pool_specs(b_n, s_n, d):
    per = POOL_ROWS // POOL_HALO
    tile = pl.BlockSpec((None, POOL_ROWS, d), lambda b, s: (b, s, 0))
    prev = pl.BlockSpec((None, POOL_HALO, d), lambda b, s: (b, jnp.maximum(s * per - 1, 0), 0))
    nxt = pl.BlockSpec((None, POOL_HALO, d), lambda b, s: (b, jnp.minimum((s + 1) * per, s_n // POOL_HALO - 1), 0))
    return tile, prev, nxt


def _pool_fwd(name, h3, w, scale):
    b_n, s_n, d = h3.shape
    tile, prev, _ = _pool_specs(b_n, s_n, d)

    def body(h_ref, halo_ref, w_ref, sc_ref, o_ref, xs, pooled):
        s = pl.program_id(1)
        xs[0:POOL_HALO, :] = jnp.where(s > 0, halo_ref[...], 0.0)
        xs[POOL_HALO:, :] = h_ref[...]
        _pooled_into(xs, pooled, s * POOL_ROWS)
        for gi in range(len(POOL_WINDOWS)):
            cols = slice(gi * POOL_GROUP_DIM, (gi + 1) * POOL_GROUP_DIM)
            y = jnp.dot(pooled[:, cols], w_ref[gi], preferred_element_type=F32)
            o_ref[:, cols] = y * sc_ref[:, cols]

    return pl.pallas_call(
        body, name=name, grid=(b_n, s_n // POOL_ROWS),
        in_specs=[tile, prev, pl.BlockSpec(w.shape, lambda b, s: (0, 0, 0)), pl.BlockSpec((1, d), lambda b, s: (0, 0))],
        out_specs=tile, out_shape=jax.ShapeDtypeStruct(h3.shape, F32),
        scratch_shapes=[pltpu.VMEM((POOL_HALO + POOL_ROWS, d), F32), pltpu.VMEM((POOL_ROWS, d), MXU)],
        compiler_params=_params(("parallel", "parallel")),
    )(h3, h3, w, scale)


def _pool_bwd(name, h3, dm3, w, scale):
    b_n, s_n, d = h3.shape
    tile, prev, nxt = _pool_specs(b_n, s_n, d)
    tiles = s_n // POOL_ROWS
    ext = POOL_ROWS + POOL_HALO

    def body(h_ref, halo_ref, dm_ref, dnext_ref, w_ref, sc_ref, dh_ref, dw_ref, dsc_ref, xs, pooled, ds, dp):
        b, s = pl.program_id(0), pl.program_id(1)

        @pl.when((b == 0) & (s == 0))
        def _():
            dw_ref[...] = jnp.zeros_like(dw_ref)
            dsc_ref[...] = jnp.zeros_like(dsc_ref)

        xs[0:POOL_HALO, :] = jnp.where(s > 0, halo_ref[...], 0.0)
        xs[POOL_HALO:, :] = h_ref[...]
        ds[0:POOL_ROWS, :] = dm_ref[...]
        ds[POOL_ROWS:, :] = jnp.where(s == tiles - 1, 0.0, dnext_ref[...])
        _pooled_into(xs, pooled, s * POOL_ROWS)
        for gi in range(len(POOL_WINDOWS)):
            cols = slice(gi * POOL_GROUP_DIM, (gi + 1) * POOL_GROUP_DIM)
            dyb = (ds[:, cols] * sc_ref[:, cols]).astype(MXU)
            dp[:, cols] = lax.dot_general(dyb, w_ref[gi], NT, preferred_element_type=F32)
            pg = pooled[:, cols]
            dw_ref[gi] += lax.dot_general(pg, dyb[:POOL_ROWS], TN, preferred_element_type=F32)
            ypre = jnp.dot(pg, w_ref[gi], preferred_element_type=F32)
            dsc_ref[:, cols] += jnp.sum(ds[0:POOL_ROWS, cols] * ypre, axis=0, keepdims=True)

        def chunk(st):
            for gi, w_len in enumerate(POOL_WINDOWS):
                cols = slice(gi * POOL_GROUP_DIM, (gi + 1) * POOL_GROUP_DIM)
                v = dp[pl.ds(st, POOL_CH + POOL_HALO), cols]
                q = v / _count(s * POOL_ROWS + st, POOL_CH + POOL_HALO, w_len)
                back = _pool_windows(q, None, gi, False)[:POOL_CH] - v[:POOL_CH]
                dh_ref[pl.ds(st, POOL_CH), cols] = ALPHA * ds[pl.ds(st, POOL_CH), cols] + back

        _chunks(POOL_ROWS, POOL_CH, chunk)

    return pl.pallas_call(
        body, name=name, grid=(b_n, tiles),
        in_specs=[tile, prev, tile, nxt, pl.BlockSpec(w.shape, lambda b, s: (0, 0, 0)), pl.BlockSpec((1, d), lambda b, s: (0, 0))],
        out_specs=[tile, pl.BlockSpec(w.shape, lambda b, s: (0, 0, 0)), pl.BlockSpec((1, d), lambda b, s: (0, 0))],
        out_shape=[jax.ShapeDtypeStruct(h3.shape, F32), jax.ShapeDtypeStruct(w.shape, F32), jax.ShapeDtypeStruct((1, d), F32)],
        scratch_shapes=[pltpu.VMEM((POOL_HALO + POOL_ROWS, d), F32), pltpu.VMEM((POOL_ROWS, d), MXU),
                        pltpu.VMEM((ext, d), F32), pltpu.VMEM((ext, d), F32)],
        compiler_params=_params(("arbitrary", "arbitrary")),
    )(h3, h3, dm3, dm3, w, scale)


ROPE_ROWS = 256


def _rope_tables(s_n):
    inv_freq = ROPE_THETA ** (-jnp.arange(0, HEAD_DIM, 2, dtype=F32) / HEAD_DIM)
    ang = jnp.arange(s_n, dtype=F32)[:, None] * inv_freq[None, :]
    cos, sin = jnp.cos(ang), jnp.sin(ang)
    cos_l = jnp.tile(cos, (1, 4))
    sin_l = jnp.tile(jnp.concatenate([-sin, sin], axis=1), (1, 2))
    return cos_l, sin_l


def _rope(name, x3, col0, n_col, cos_l, sin_l, scale, out_dtype, rotate=True, out_cols=None, out_col0=0, into=None):
    b_n, s_n, _ = x3.shape
    cw = N_HEADS * HEAD_DIM
    out_cols = n_col if out_cols is None else out_cols

    def body(x_ref, c_ref, s_ref, *rest):
        o_ref = rest[-1]
        lane = lax.broadcasted_iota(jnp.int32, (ROPE_ROWS, 128), 1)
        first_half = (lane % HEAD_DIM) < (HEAD_DIM // 2)
        cos, sin = c_ref[...], s_ref[...]
        for cb in range(cw // 128):
            cols = slice(cb * 128, (cb + 1) * 128)
            y = x_ref[:, cols]
            if rotate:
                other = jnp.where(first_half, pltpu.roll(y, 128 - HEAD_DIM // 2, 1), pltpu.roll(y, HEAD_DIM // 2, 1))
                y = y * cos + other * sin
            o_ref[:, cols] = (y if scale == 1.0 else y * scale).astype(o_ref.dtype)

    tile = pl.BlockSpec((None, ROPE_ROWS, cw), lambda b, s, c: (b, s, col0 + c))
    tab = pl.BlockSpec((ROPE_ROWS, 128), lambda b, s, c: (s, 0))
    extra, extra_specs, aliases = [], [], {}
    if into is not None:
        extra, extra_specs, aliases = [into], [pl.BlockSpec(memory_space=pl.ANY)], {3: 0}
    return pl.pallas_call(
        body, name=name, grid=(b_n, s_n // ROPE_ROWS, n_col), in_specs=[tile, tab, tab] + extra_specs,
        out_specs=pl.BlockSpec((None, ROPE_ROWS, cw), lambda b, s, c: (b, s, out_col0 + c)),
        out_shape=jax.ShapeDtypeStruct((b_n, s_n, out_cols * cw), out_dtype), input_output_aliases=aliases,
        compiler_params=_params(("parallel", "parallel", "parallel")),
    )(x3, cos_l, sin_l, *extra)


def _to_strided(a, d):
    if d == 1:
        return a
    b_n, s_n, c = a.shape
    return a.reshape(b_n, s_n // d, d, c).transpose(0, 2, 1, 3).reshape(b_n, s_n, c)


def _from_strided(a, d):
    if d == 1:
        return a
    b_n, s_n, c = a.shape
    return a.reshape(b_n, d, s_n // d, c).transpose(0, 2, 1, 3).reshape(b_n, s_n, c)


def _attn_fwd(name, q, q_col, k, v, blocks_per_seq):
    b_n, s_n, cw = k.shape
    nb = s_n // SPAN
    with_prev = blocks_per_seq > 1

    keys = 2 * SPAN if with_prev else SPAN
    pair = 2 * HEAD_DIM

    def body(*refs):
        if with_prev:
            q_ref, kc_ref, kp_ref, vc_ref, vp_ref, o_ref, lse_ref, k_all, v_all, s_buf, m_buf = refs
            k_all[0:SPAN, :] = kp_ref[...]
            k_all[SPAN:, :] = kc_ref[...]
            v_all[0:SPAN, :] = vp_ref[...]
            v_all[SPAN:, :] = vc_ref[...]
        else:
            q_ref, k_all, v_all, o_ref, lse_ref, s_buf, m_buf = refs
        n = pl.program_id(1)
        qi = lax.broadcasted_iota(jnp.int32, (SPAN, keys), 0)
        kj = lax.broadcasted_iota(jnp.int32, (SPAN, keys), 1)
        if with_prev:
            back = jnp.where((n % blocks_per_seq) != 0, 0, 2 * SPAN)
            mask = ((kj < SPAN) & (kj >= qi + back)) | ((kj >= SPAN) & (kj - SPAN <= qi))
        else:
            mask = kj <= qi
        lane = lax.broadcasted_iota(jnp.int32, (SPAN, pair), 1)
        low = lane < HEAD_DIM
        ones = jnp.ones((keys, pair), MXU)
        for h in range(N_HEADS):
            grp = slice((h // 2) * pair, (h // 2 + 1) * pair)
            q2 = q_ref[:, grp]
            qh = jnp.where(low if h % 2 == 0 else ~low, q2, jnp.zeros_like(q2))
            s = jnp.where(mask, lax.dot_general(qh, k_all[:, grp], NT, preferred_element_type=F32), -jnp.inf)
            s_buf[h] = s
            m_buf[h] = jnp.broadcast_to(jnp.max(s, -1, keepdims=True), (SPAN, pair))
        lse_all = jnp.zeros((SPAN, pair), F32)
        for hp in range(N_HEADS // 2):
            grp = slice(hp * pair, (hp + 1) * pair)
            v2 = v_all[:, grp]
            halves = []
            for h in (2 * hp, 2 * hp + 1):
                m = m_buf[h]
                p = jnp.exp(s_buf[h] - jnp.tile(m, (1, keys // pair))).astype(MXU)
                tot = jnp.dot(p, ones, preferred_element_type=F32)
                halves.append(jnp.dot(p, v2, preferred_element_type=F32) / tot)
                lse_all = jnp.where(lane == h, m + jnp.log(tot), lse_all)
            o_ref[:, grp] = jnp.where(low, halves[0], halves[1])
        lse_ref[...] = lse_all[:, 0:N_HEADS]

    cur = lambda b, n: (b, n, 0)
    prv = lambda b, n: (b, jnp.maximum(n - 1, 0), 0)
    blk = (None, SPAN, cw)
    kv_specs = [pl.BlockSpec(blk, cur), pl.BlockSpec(blk, prv)] if with_prev else [pl.BlockSpec(blk, cur)]
    operands = [q, k, k, v, v] if with_prev else [q, k, v]
    stage = [pltpu.VMEM((keys, cw), MXU)] * 2 if with_prev else []
    return pl.pallas_call(
        body, name=name, grid=(b_n, nb),
        in_specs=[pl.BlockSpec(blk, lambda b, n: (b, n, q_col))] + kv_specs + kv_specs,
        out_specs=[pl.BlockSpec(blk, cur), pl.BlockSpec((None, SPAN, N_HEADS), cur)],
        out_shape=[jax.ShapeDtypeStruct((b_n, s_n, cw), F32), jax.ShapeDtypeStruct((b_n, s_n, N_HEADS), F32)],
        scratch_shapes=stage + [pltpu.VMEM((N_HEADS, SPAN, keys), F32), pltpu.VMEM((N_HEADS, SPAN, pair), F32)],
        compiler_params=_params(("parallel", "parallel")),
    )(*operands)


def _attn_bwd(name, q, q_col, k, v, do, lse, dd, blocks_per_seq, dk_prev=None, dv_prev=None):
    b_n, s_n, cw = k.shape
    nb = s_n // SPAN
    with_next = blocks_per_seq > 1
    accumulate = dk_prev is not None
    rows = 2 * SPAN if with_next else SPAN

    def body(*refs):
        refs = list(refs)
        qc_ref, doc_ref, lsec_ref, ddc_ref = refs[:4]
        del refs[:4]
        if with_next:
            qn_ref, don_ref, lsen_ref, ddn_ref = refs[:4]
            del refs[:4]
        k_ref, v_ref = refs[:2]
        del refs[:2]
        if accumulate:
            dkp_ref, dvp_ref = refs[:2]
            del refs[:2]
        dq_ref, dk_ref, dv_ref = refs[:3]
        del refs[:3]
        if with_next:
            carry, q_all, do_all, side = refs[:4]
            del refs[:4]
            q_all[0:SPAN, :] = qc_ref[...]
            q_all[SPAN:, :] = qn_ref[...]
            do_all[0:SPAN, :] = doc_ref[...]
            do_all[SPAN:, :] = don_ref[...]
            side[0, 0:SPAN, :] = lsec_ref[...]
            side[0, SPAN:, :] = lsen_ref[...]
            side[1, 0:SPAN, :] = ddc_ref[...]
            side[1, SPAN:, :] = ddn_ref[...]
            lse_at = lambda h: side[0, :, h:h + 1]
            dd_at = lambda h: side[1, :, h:h + 1]
        else:
            q_all, do_all = qc_ref, doc_ref
            lse_at = lambda h: lsec_ref[:, h:h + 1]
            dd_at = lambda h: ddc_ref[:, h:h + 1]
        p_buf, ds_buf = refs
        n = pl.program_id(1)
        qi = lax.broadcasted_iota(jnp.int32, (rows, SPAN), 0)
        kj = lax.broadcasted_iota(jnp.int32, (rows, SPAN), 1)
        if with_next:
            first = (n % blocks_per_seq) == 0
            reach = jnp.where(((n + 1) % blocks_per_seq) != 0, SPAN, -2 * SPAN)
            mask = ((qi < SPAN) & (kj <= qi)) | ((qi >= SPAN) & (kj >= qi - reach))
        else:
            mask = kj <= qi
        pair = 2 * HEAD_DIM
        low = lax.broadcasted_iota(jnp.int32, (rows, pair), 1) < HEAD_DIM
        low_k = lax.broadcasted_iota(jnp.int32, (SPAN, pair), 1) < HEAD_DIM

        def pick(v, h, low_mask):
            return jnp.where(low_mask if h % 2 == 0 else ~low_mask, v, jnp.zeros_like(v))

        for h in range(N_HEADS):
            grp = slice((h // 2) * pair, (h // 2 + 1) * pair)
            s = lax.dot_general(pick(q_all[:, grp], h, low), k_ref[:, grp], NT, preferred_element_type=F32)
            p = jnp.where(mask, jnp.exp(s - lse_at(h)), 0.0)
            dp = lax.dot_general(pick(do_all[:, grp], h, low), v_ref[:, grp], NT, preferred_element_type=F32)
            p_buf[h] = p.astype(MXU)
            ds_buf[h] = (p * (dp - dd_at(h))).astype(MXU)
        for hp in range(N_HEADS // 2):
            grp = slice(hp * pair, (hp + 1) * pair)
            q2, do2, k2 = q_all[:, grp], do_all[:, grp], k_ref[:, grp]
            dv = dk = dq2 = None
            for h in (2 * hp, 2 * hp + 1):
                t_dv = lax.dot_general(p_buf[h], pick(do2, h, low), TN, preferred_element_type=F32)
                t_dk = lax.dot_general(ds_buf[h], pick(q2, h, low), TN, preferred_element_type=F32)
                t_dq = jnp.dot(ds_buf[h], pick(k2, h, low_k), preferred_element_type=F32)
                dv = t_dv if dv is None else dv + t_dv
                dk = t_dk if dk is None else dk + t_dk
                dq2 = t_dq if dq2 is None else dq2 + t_dq
            if accumulate:
                dk = dk + dkp_ref[:, grp]
                dv = dv + dvp_ref[:, grp]
            dk_ref[:, grp] = dk
            dv_ref[:, grp] = dv
            if with_next:
                dq_ref[:, grp] = dq2[:SPAN] + jnp.where(first, 0.0, carry[:, grp])
                carry[:, grp] = dq2[SPAN:]
            else:
                dq_ref[:, grp] = dq2

    cur = lambda b, n: (b, n, 0)
    nxt = lambda b, n: (b, jnp.minimum(n + 1, nb - 1), 0)
    blk, hblk = (None, SPAN, cw), (None, SPAN, N_HEADS)
    q_specs = lambda m: [pl.BlockSpec(blk, lambda b, n: (*m(b, n)[:2], q_col)), pl.BlockSpec(blk, m),
                         pl.BlockSpec(hblk, m), pl.BlockSpec(hblk, m)]
    in_specs = q_specs(cur) + (q_specs(nxt) if with_next else []) + [pl.BlockSpec(blk, cur)] * (4 if accumulate else 2)
    operands = [q, do, lse, dd] * (2 if with_next else 1) + [k, v] + ([dk_prev, dv_prev] if accumulate else [])
    out = jax.ShapeDtypeStruct((b_n, s_n, cw), F32)
    return pl.pallas_call(
        body, name=name, grid=(b_n, nb), in_specs=in_specs, out_specs=[pl.BlockSpec(blk, cur)] * 3,
        out_shape=[out, out, out],
        scratch_shapes=([pltpu.VMEM((SPAN, cw), F32), pltpu.VMEM((rows, cw), MXU), pltpu.VMEM((rows, cw), MXU),
                         pltpu.VMEM((2, rows, N_HEADS), F32)] if with_next else [])
        + [pltpu.VMEM((N_HEADS, rows, SPAN), MXU)] * 2,
        compiler_params=_params(("parallel", "arbitrary")),
    )(*operands)


MIX_ROWS = 256


def _group_weights(ls):
    m = functools.reduce(jnp.maximum, ls)
    es = [jnp.exp(l - m) for l in ls]
    tot = functools.reduce(lambda a, b: a + b, es)
    return [e / tot for e in es]


def _attn_mix(name, outs, lses):
    n, cw = outs[0].shape
    g_n = len(outs)

    def body(*refs):
        o_refs, l_refs, out_ref = refs[:g_n], refs[g_n:2 * g_n], refs[2 * g_n]
        ws = _group_weights([r[...] for r in l_refs])
        for h in range(N_HEADS):
            cols = slice(h * HEAD_DIM, (h + 1) * HEAD_DIM)
            acc = None
            for g in range(g_n):
                t = ws[g][:, h:h + 1] * o_refs[g][:, cols]
                acc = t if acc is None else acc + t
            out_ref[:, cols] = acc.astype(out_ref.dtype)

    tile = pl.BlockSpec((MIX_ROWS, cw), lambda i: (i, 0))
    htile = pl.BlockSpec((MIX_ROWS, N_HEADS), lambda i: (i, 0))
    return pl.pallas_call(
        body, name=name, grid=(n // MIX_ROWS,), in_specs=[tile] * g_n + [htile] * g_n, out_specs=tile,
        out_shape=jax.ShapeDtypeStruct((n, cw), MXU), compiler_params=_params(("parallel",)),
    )(*outs, *lses)


def _attn_mix_bwd(name, do, outs, lses):
    n, cw = do.shape
    g_n = len(outs)

    def body(*refs):
        do_ref, o_refs, l_refs = refs[0], refs[1:1 + g_n], refs[1 + g_n:1 + 2 * g_n]
        dog_refs, dd_refs = refs[1 + 2 * g_n:1 + 3 * g_n], refs[1 + 3 * g_n:]
        ws = _group_weights([r[...] for r in l_refs])
        for h in range(N_HEADS):
            cols = slice(h * HEAD_DIM, (h + 1) * HEAD_DIM)
            dh = do_ref[:, cols]
            o = None
            for g in range(g_n):
                t = ws[g][:, h:h + 1] * o_refs[g][:, cols]
                o = t if o is None else o + t
            dot = jnp.sum(dh * o, -1, keepdims=True)
            for g in range(g_n):
                wg = ws[g][:, h:h + 1]
                dog_refs[g][:, cols] = (wg * dh).astype(dog_refs[g].dtype)
                dd_refs[g][:, h:h + 1] = wg * dot

    tile = pl.BlockSpec((MIX_ROWS, cw), lambda i: (i, 0))
    htile = pl.BlockSpec((MIX_ROWS, N_HEADS), lambda i: (i, 0))
    return pl.pallas_call(
        body, name=name, grid=(n // MIX_ROWS,), in_specs=[tile] * (1 + g_n) + [htile] * g_n,
        out_specs=[tile] * g_n + [htile] * g_n,
        out_shape=[jax.ShapeDtypeStruct((n, cw), MXU)] * g_n + [jax.ShapeDtypeStruct((n, N_HEADS), F32)] * g_n,
        compiler_params=_params(("parallel",)),
    )(do, *outs, *lses)


def _loss_head(name, y, target):
    n, d = y.shape
    steps = n // LN_ROWS

    def body(y_ref, t_ref, dy_ref, l_ref, acc):
        i = pl.program_id(0)

        @pl.when(i == 0)
        def _():
            acc[...] = jnp.zeros_like(acc)

        def chunk(s):
            rows = pl.ds(s, LN_CH)
            err = y_ref[rows, :] - t_ref[rows, :]
            dy_ref[rows, :] = err / d
            acc[...] += _fold8(err * err)

        _chunks(LN_ROWS, LN_CH, chunk, LN_UNROLL)

        @pl.when(i == steps - 1)
        def _():
            l_ref[...] = jnp.full((8, 128), 0.5 / d, F32) * jnp.sum(acc[...])

    tile = pl.BlockSpec((LN_ROWS, d), lambda i: (i, 0))
    return pl.pallas_call(
        body, name=name, grid=(steps,), in_specs=[tile, tile],
        out_specs=[tile, pl.BlockSpec((8, 128), lambda i: (0, 0))],
        out_shape=[jax.ShapeDtypeStruct((n, d), F32), jax.ShapeDtypeStruct((8, 128), F32)],
        scratch_shapes=[pltpu.VMEM((8, d), F32)], compiler_params=_params(("arbitrary",)),
    )(y, target)


EW_TILE_BYTES = 1 << 20


def _row_tile(rows, cols):
    tr = 8
    while rows % (2 * tr) == 0 and 2 * tr * cols * 4 <= EW_TILE_BYTES:
        tr *= 2
    return tr if rows % tr == 0 else rows


def _add_halves(name, grad, recv, half, out_dtype):
    j_n, _, r, c = grad.shape
    tr = _row_tile(r, c)

    def body(half_ref, g_ref, r_ref, o_ref):
        o_ref[...] = (g_ref[...] + r_ref[...]).astype(o_ref.dtype)

    return pl.pallas_call(
        body, name=name, out_shape=jax.ShapeDtypeStruct((j_n, r, c), out_dtype),
        grid_spec=pltpu.PrefetchScalarGridSpec(
            num_scalar_prefetch=1, grid=(j_n, r // tr),
            in_specs=[pl.BlockSpec((None, None, tr, c), lambda j, i, hf: (j, hf[0], i, 0)),
                      pl.BlockSpec((None, tr, c), lambda j, i, hf: (j, i, 0))],
            out_specs=pl.BlockSpec((None, tr, c), lambda j, i, hf: (j, i, 0))),
        compiler_params=_params(("parallel", "parallel")),
    )(half, grad, recv)


def _add_chips(name, mine, recv, chip):
    j_n, r, c = mine.shape
    tr = _row_tile(r, c)

    def body(chip_ref, m_ref, r_ref, o_ref):
        total = m_ref[...].astype(F32)
        for k in range(j_n - 1):
            total = total + r_ref[k].astype(F32)
        o_ref[...] = total

    return pl.pallas_call(
        body, name=name, out_shape=jax.ShapeDtypeStruct((r, c), F32),
        grid_spec=pltpu.PrefetchScalarGridSpec(
            num_scalar_prefetch=1, grid=(r // tr,),
            in_specs=[pl.BlockSpec((None, tr, c), lambda i, ch: (ch[0], i, 0)),
                      pl.BlockSpec((j_n - 1, tr, c), lambda i, ch: (0, i, 0))],
            out_specs=pl.BlockSpec((tr, c), lambda i, ch: (i, 0))),
        compiler_params=_params(("parallel",)),
    )(chip, mine, recv)


def _adam_math(w, g, m, v):
    m = ADAM_B1 * m + (1.0 - ADAM_B1) * g
    v = ADAM_B2 * v + (1.0 - ADAM_B2) * (g * g)
    m_hat = m / (1.0 - ADAM_B1 ** ADAM_STEP)
    v_hat = v / (1.0 - ADAM_B2 ** ADAM_STEP)
    delta = -ADAM_LR * (m_hat / (jnp.sqrt(v_hat) + ADAM_EPS) + ADAM_WD * w)
    return delta, m, v


def _adam_halves(name, own, other, half, w, m, v):
    _, r, c = w.shape
    tr = _row_tile(r, c)

    def body(half_ref, own_ref, oth_ref, w_ref, m_ref, v_ref, g_out, d_out, m_out, v_out):
        g = jnp.where(pl.program_id(0) == half_ref[0], own_ref[...], oth_ref[...])
        delta, m_new, v_new = _adam_math(w_ref[...], g, m_ref[...], v_ref[...])
        g_out[...] = g
        d_out[...] = delta
        m_out[...] = m_new
        v_out[...] = v_new

    flat = pl.BlockSpec((tr, c), lambda h, i, hf: (i, 0))
    full = pl.BlockSpec((None, tr, c), lambda h, i, hf: (h, i, 0))
    out = jax.ShapeDtypeStruct(w.shape, F32)
    return pl.pallas_call(
        body, name=name, out_shape=[out] * 4,
        grid_spec=pltpu.PrefetchScalarGridSpec(num_scalar_prefetch=1, grid=(2, r // tr), in_specs=[flat, flat, full, full, full],
                                               out_specs=[full] * 4),
        compiler_params=_params(("parallel", "parallel")),
    )(half, own, other, w, m, v)


def _adam_small(name, g, w, m, v):
    def body(g_ref, w_ref, m_ref, v_ref, d_out, m_out, v_out):
        delta, m_new, v_new = _adam_math(w_ref[...], g_ref[...], m_ref[...], v_ref[...])
        d_out[...] = delta
        m_out[...] = m_new
        v_out[...] = v_new

    out = jax.ShapeDtypeStruct(w.shape, F32)
    return pl.pallas_call(body, name=name, out_shape=[out] * 3)(g, w, m, v)


def _place():
    x, y, c = lax.axis_index("x"), lax.axis_index("y"), lax.axis_index("c")
    chips = [(1 - x, y), (x, 1 - y), (1 - x, 1 - y)]
    return x, y, c, chips


ANY = pl.BlockSpec(memory_space=pl.ANY)


class _ShardGather:
    def __init__(self, ins, outs, send, recv, local):
        self.ins, self.outs, self.send, self.recv, self.local = ins, outs, send, recv, local
        self.n = len(ins)

    @staticmethod
    def scratch(n):
        return [pltpu.SemaphoreType.DMA((n, 6)), pltpu.SemaphoreType.DMA((n, 6)), pltpu.SemaphoreType.DMA((n,))]

    @staticmethod
    def out_shapes(shards):
        return [jax.ShapeDtypeStruct((N_CHIPS, *s.shape), s.dtype) for s in shards]

    def _copy(self, t, k, src, dst, to):
        return pltpu.make_async_remote_copy(src_ref=src, dst_ref=dst, send_sem=self.send.at[t, k], recv_sem=self.recv.at[t, k],
                                            device_id=to, device_id_type=MESH)

    def _own(self, t, me):
        return pltpu.make_async_copy(self.ins[t], self.outs[t].at[me], self.local.at[t])

    def _first(self, t, k, place):
        x, y, c, chips = place
        px, py = chips[k]
        return self._copy(t, k, self.ins[t].at[c], self.outs[t].at[2 * x + y, c], (px, py, c))

    def _passed_on(self, t, k, place, half):
        x, y, c, chips = place
        px, py = chips[k]
        slab = self.outs[t].at[2 * px + py, half]
        return self._copy(t, 3 + k, slab, slab, (x, y, 1 - c))

    def begin(self):
        place = _place()
        x, y, c, _ = place
        for t in range(self.n):
            self._own(t, 2 * x + y).start()
        for t in range(self.n):
            for k in range(N_CHIPS - 1):
                self._first(t, k, place).start()

    def end(self):
        place = _place()
        x, y, c, chips = place
        for t in range(self.n):
            for k, (px, py) in enumerate(chips):
                self._copy(t, k, self.ins[t].at[c], self.outs[t].at[2 * px + py, c], (px, py, c)).wait_recv()
                self._passed_on(t, k, place, c).start()
        for t in range(self.n):
            for k in range(N_CHIPS - 1):
                self._passed_on(t, k, place, 1 - c).wait_recv()
        for t in range(self.n):
            for k in range(N_CHIPS - 1):
                self._first(t, k, place).wait_send()
                self._passed_on(t, k, place, c).wait_send()
            self._own(t, 2 * x + y).wait()


def _gather_shards(name, shards):
    n = len(shards)

    def body(*refs):
        job = _ShardGather(refs[:n], refs[n:2 * n], *refs[2 * n:])
        job.begin()
        job.end()

    return pl.pallas_call(
        body, name=name, in_specs=[ANY] * n, out_specs=[ANY] * n, out_shape=_ShardGather.out_shapes(shards),
        scratch_shapes=_ShardGather.scratch(n),
    )(*shards)


def _to_sibling(name, arrays, pick_other_half):
    n = len(arrays)

    def body(*refs):
        ins, outs = refs[:n], refs[n:2 * n]
        send, recv = refs[2 * n:]
        x, y, c, _ = _place()
        copies = []
        for t in range(n):
            src = ins[t].at[:, 1 - c] if pick_other_half else ins[t]
            copies.append(pltpu.make_async_remote_copy(src_ref=src, dst_ref=outs[t], send_sem=send.at[t], recv_sem=recv.at[t],
                                                       device_id=(x, y, 1 - c), device_id_type=MESH))
            copies[-1].start()
        for cp in copies:
            cp.wait()

    shapes = [(a.shape[0], *a.shape[2:]) if pick_other_half else a.shape for a in arrays]
    return pl.pallas_call(
        body, name=name, in_specs=[ANY] * n, out_specs=[ANY] * n,
        out_shape=[jax.ShapeDtypeStruct(s, a.dtype) for s, a in zip(shapes, arrays)],
        scratch_shapes=[pltpu.SemaphoreType.DMA((n,)), pltpu.SemaphoreType.DMA((n,))],
    )(*arrays)


def _to_owner_chips(name, arrays):
    n = len(arrays)

    def body(*refs):
        ins, outs = refs[:n], refs[n:2 * n]
        send, recv = refs[2 * n:]
        x, y, c, chips = _place()
        copies = []
        for t in range(n):
            for k, (px, py) in enumerate(chips):
                copies.append(pltpu.make_async_remote_copy(
                    src_ref=ins[t].at[2 * px + py], dst_ref=outs[t].at[k], send_sem=send.at[t, k], recv_sem=recv.at[t, k],
                    device_id=(px, py, c), device_id_type=MESH))
                copies[-1].start()
        for cp in copies:
            cp.wait()

    return pl.pallas_call(
        body, name=name, in_specs=[ANY] * n, out_specs=[ANY] * n,
        out_shape=[jax.ShapeDtypeStruct((N_CHIPS - 1, *a.shape[1:]), a.dtype) for a in arrays],
        scratch_shapes=[pltpu.SemaphoreType.DMA((n, 3)), pltpu.SemaphoreType.DMA((n, 3))],
    )(*arrays)


def _sum_all_devices(name, part):
    r, c = part.shape

    def body(p_ref, o_ref, buf, send, recv):
        x, y, cc, _ = _place()
        me = 4 * x + 2 * y + cc
        copies = []
        for mask in range(1, 8):
            fx, fy, fc = (mask >> 2) & 1, (mask >> 1) & 1, mask & 1
            to = (x ^ fx, y ^ fy, cc ^ fc)
            copies.append((mask, pltpu.make_async_remote_copy(
                src_ref=p_ref, dst_ref=buf.at[me], send_sem=send.at[mask - 1], recv_sem=recv.at[mask - 1],
                device_id=to, device_id_type=MESH)))
            copies[-1][1].start()
        buf[me] = p_ref[...]
        for mask, cp in copies:
            pltpu.make_async_remote_copy(src_ref=p_ref, dst_ref=buf.at[me ^ mask], send_sem=send.at[mask - 1],
                                         recv_sem=recv.at[mask - 1], device_id=(x, y, cc), device_id_type=MESH).wait_recv()
        for _, cp in copies:
            cp.wait_send()
        total = buf[0]
        for d in range(1, 8):
            total = total + buf[d]
        o_ref[...] = total

    vm = pl.BlockSpec(memory_space=pltpu.VMEM)
    return pl.pallas_call(
        body, name=name, in_specs=[vm], out_specs=vm, out_shape=jax.ShapeDtypeStruct((r, c), F32),
        scratch_shapes=[pltpu.VMEM((8, r, c), F32), pltpu.SemaphoreType.DMA((7,)), pltpu.SemaphoreType.DMA((7,))],
    )(part)


def kernel(x, pool_w, pool_scale, w_q, w_kv, w_o, ffn_w_gate, ffn_w_up, ffn_conv_w, ffn_conv_b, ffn_w_down, ln1_g, ln1_b, ln2_g, ln2_b, loss_target, m_pool_w, m_pool_scale, m_w_q, m_w_kv, m_w_o, m_ffn_w_gate, m_ffn_w_up, m_ffn_conv_w, m_ffn_conv_b, m_ffn_w_down, m_ln1_g, m_ln1_b, m_ln2_g, m_ln2_b, v_pool_w, v_pool_scale, v_w_q, v_w_kv, v_w_o, v_ffn_w_gate, v_ffn_w_up, v_ffn_conv_w, v_ffn_conv_b, v_ffn_w_down, v_ln1_g, v_ln1_b, v_ln2_g, v_ln2_b):
    b_n, s_n, d = x.shape
    n = b_n * s_n
    f = ffn_w_gate.shape[-1]
    qc = w_q.shape[-1]
    kvb = w_kv.shape[-1] // 2
    n_attn = w_q.shape[0]
    g_n = len(DILATIONS)
    cw = N_HEADS * HEAD_DIM
    xi, yi, ci = lax.axis_index("x"), lax.axis_index("y"), lax.axis_index("c")
    half = jnp.reshape(ci, (1,)).astype(jnp.int32)
    chip = jnp.reshape(2 * xi + yi, (1,)).astype(jnp.int32)

    sharded = {
        "pool_w": (pool_w, m_pool_w, v_pool_w, (2, 4 * 64, POOL_GROUP_DIM)),
        "pool_scale": (pool_scale, m_pool_scale, v_pool_scale, (2, 1, pool_scale.shape[-1])),
        "w_q": (w_q, m_w_q, v_w_q, (2, d, qc)),
        "w_kv": (w_kv, m_w_kv, v_w_kv, (2, d // 2, w_kv.shape[-1])),
        "w_o": (w_o, m_w_o, v_w_o, (2, w_o.shape[1], d)),
        "ffn_w_gate": (ffn_w_gate, m_ffn_w_gate, v_ffn_w_gate, (2, 2 * d, f)),
        "ffn_w_up": (ffn_w_up, m_ffn_w_up, v_ffn_w_up, (2, 2 * d, f)),
        "ffn_conv_w": (ffn_conv_w, m_ffn_conv_w, v_ffn_conv_w, (2, 6, f)),
        "ffn_w_down": (ffn_w_down, m_ffn_w_down, v_ffn_w_down, (2, 2 * f, d)),
    }
    mxu_weights = ("pool_w", "w_q", "w_kv", "w_o", "ffn_w_gate", "ffn_w_up", "ffn_w_down")
    names = list(sharded)
    wo_rows = w_o.shape[1]
    shard_of = {("wkv", 0): w_kv.astype(MXU).reshape(2, d // 2, w_kv.shape[-1])}
    for i in range(DEPTH):
        shard_of["wg", i] = ffn_w_gate[i].astype(MXU).reshape(2, d // 2, f)
        shard_of["wu", i] = ffn_w_up[i].astype(MXU).reshape(2, d // 2, f)
        shard_of["wd", i] = ffn_w_down[i].astype(MXU).reshape(2, f // 2, d)
    for i in range(n_attn):
        shard_of["wq", i] = w_q[i].astype(MXU).reshape(2, d // 2, qc)
        shard_of["wo", i] = w_o[i].astype(MXU).reshape(2, wo_rows // 2, d)
    carried_by = {
        (0, "gate"): [("wg", 1)], (0, "up"): [("wu", 1)], (0, "down"): [("wd", 1)],
        (1, "gate"): [("wkv", 0)], (1, "up"): [("wq", 0), ("wo", 0)], (1, "down"): [("wg", 2)],
        (2, "q"): [("wu", 2)], (2, "o"): [("wd", 2)], (2, "gate"): [("wq", 1), ("wo", 1)], (2, "up"): [("wg", 3)],
        (2, "down"): [("wu", 3)], (3, "q"): [("wd", 3)],
    }
    got = {}

    def carrying(site, call, *args):
        keys = carried_by.get(site, [])
        if not keys:
            return call(*args)
        out, arrived = call(*args, carry=[shard_of[k] for k in keys])
        got.update(zip(keys, arrived))
        return out

    first_keys = [("wg", 0), ("wu", 0), ("wd", 0)]
    first = _gather_shards("gather_weights", [
        pool_w.astype(MXU).reshape(sharded["pool_w"][3]), pool_scale.reshape(sharded["pool_scale"][3]),
        ffn_conv_w.reshape(sharded["ffn_conv_w"][3])] + [shard_of[k] for k in first_keys])
    got.update(zip(first_keys, first[3:]))
    wg_at = lambda i: got["wg", i].reshape(N_CHIPS, d, f)
    wu_at = lambda i: got["wu", i].reshape(N_CHIPS, d, f)
    wd_at = lambda i: got["wd", i].reshape(N_CHIPS, f, d)
    wq_at = lambda i: got["wq", i].reshape(N_CHIPS, d, qc)
    wo_at = lambda i: got["wo", i].reshape(cw, d)
    wkv_at = lambda: got["wkv", 0].reshape(N_CHIPS, d, w_kv.shape[-1])
    cw_all = first[2].reshape(N_CHIPS, DEPTH, 3, f)
    cw_l = [cw_all[:, i] for i in range(DEPTH)]
    cb_l = [ffn_conv_b[i].reshape(N_CHIPS, 1, f) for i in range(DEPTH)]
    pw_nat = first[0].reshape(N_CHIPS, N_POOL_LAYERS, 4, 64, POOL_GROUP_DIM).transpose(1, 2, 0, 3, 4).reshape(
        N_POOL_LAYERS, 4, POOL_GROUP_DIM, POOL_GROUP_DIM)
    ps_nat = first[1].reshape(N_CHIPS, N_POOL_LAYERS, -1).transpose(1, 0, 2).reshape(N_POOL_LAYERS, 1, d)
    cos_l, sin_l = _rope_tables(s_n)

    def vec(a, layer):
        return a[layer].reshape(1, d)

    h = x.reshape(n, d)
    hb = None
    saved = []
    k_str = v_str = None
    for layer in range(DEPTH):
        keep = {"h": h, "hb": hb}
        if layer < N_POOL_LAYERS:
            mix = _pool_fwd("pool_fwd", h.reshape(b_n, s_n, d), pw_nat[layer], ps_nat[layer]).reshape(n, d)
        else:
            a = layer - N_POOL_LAYERS
            q = carrying((layer, "q"), functools.partial(_mm_cols, "q_proj", hb, wq_at(a), (), F32))
            qr = _rope("q_rope", q.reshape(b_n, s_n, -1), 0, g_n, cos_l, sin_l, HEAD_DIM ** -0.5, MXU)
            q_str, lse_str, outs, lses = [], [], [], []
            for g, dil in enumerate(DILATIONS):
                if dil == 1:
                    qg, col = qr, g
                else:
                    qg, col = _to_strided(qr[:, :, g * cw:(g + 1) * cw], dil), 0
                o_g, lse_g = _attn_fwd(f"attn_fwd_d{dil}", qg, col, k_str[g], v_str[g], s_n // dil // SPAN)
                q_str.append((qg, col))
                lse_str.append(lse_g)
                outs.append(_from_strided(o_g, dil).reshape(n, cw))
                lses.append(_from_strided(lse_g, dil).reshape(n, N_HEADS))
            ob = _attn_mix("attn_mix", outs, lses)
            mix = carrying((layer, "o"), functools.partial(
                _mm, "o_proj", [(ob, (TM, cw), lambda m, _: (m, 0), wo_at(a), (cw, d), lambda m, _: (0, 0))],
                NN, (n // TM, 1), (n, d), F32, (TM, d), lambda m, _: (m, 0), 1))
            keep.update(q_str=q_str, lse_str=lse_str, outs=outs, lses=lses, ob=ob)
        r1, h1, h1b = _add_ln("ln_fwd", h, mix, vec(ln1_g, layer), vec(ln1_b, layer))
        gate = carrying((layer, "gate"), functools.partial(_mm_shard_out, "gate_up_proj", h1b, wg_at(layer), (), F32))
        up = carrying((layer, "up"), functools.partial(_mm_shard_out, "gate_up_proj", h1b, wu_at(layer), (), F32))
        g4, u4 = gate.reshape(N_CHIPS, b_n, s_n, f), up.reshape(N_CHIPS, b_n, s_n, f)
        hmid = _glu_fwd("glu_fwd", g4, u4, cw_l[layer], cb_l[layer]).reshape(N_CHIPS, n, f)
        ffn = carrying((layer, "down"), functools.partial(_mm_shard_in, "down_proj", hmid, wd_at(layer), ()))
        r2, h2, h2b = _add_ln("ln_fwd", h1, ffn, vec(ln2_g, layer), vec(ln2_b, layer))
        keep.update(r1=r1, h1b=h1b, g4=g4, u4=u4, hmid=hmid, r2=r2, h2b=h2b)
        saved.append(keep)
        if layer == N_POOL_LAYERS - 1:
            kv = _mm_cols("kv_proj", h2b, wkv_at(), (), F32, cb=kvb).reshape(b_n, s_n, -1)
            k_nat = _rope("k_rope", kv, 0, g_n, cos_l, sin_l, 1.0, MXU)
            v_nat = _rope("v_cast", kv, g_n, g_n, cos_l, sin_l, 1.0, MXU, rotate=False)
            k_str = [_to_strided(k_nat[:, :, g * cw:(g + 1) * cw], dil) for g, dil in enumerate(DILATIONS)]
            v_str = [_to_strided(v_nat[:, :, g * cw:(g + 1) * cw], dil) for g, dil in enumerate(DILATIONS)]
        h, hb = h2, h2b

    dy, loss_tile = _loss_head("loss_head", h, loss_target.reshape(n, d))

    d_wq = d_wo = d_wg = d_wu = d_wd = None
    d_cw = [None] * DEPTH
    d_pw, d_ps = [None] * N_POOL_LAYERS, [None] * N_POOL_LAYERS
    d_ln = {}
    dk_str = dv_str = None
    d_top, top_scale, top_rest = dy, 1.0, []
    for layer in reversed(range(DEPTH)):
        sv = saved[layer]
        dr2, dr2b, d_ln["ln2", layer] = _ln_bwd("ln_bwd", sv["r2"], vec(ln2_g, layer), d_top, top_scale, top_rest)
        dhmid = _mm_nt_shard_out("down_bwd", dr2b, wd_at(layer), ())
        d_wd = _mm_tn("down_dw", sv["hmid"], f, dr2b, d, N_CHIPS, (N_CHIPS, DEPTH, f, d), (None, None, f, d),
                      lambda j: (j, layer, 0, 0), into=d_wd, a_lead=lambda j: (j,))
        dg4, du4, d_cw[layer] = _glu_bwd("glu_bwd", dhmid.reshape(N_CHIPS, b_n, s_n, f), sv["g4"], sv["u4"],
                                         cw_l[layer], cb_l[layer])
        dg, du = dg4.reshape(N_CHIPS, n, f), du4.reshape(N_CHIPS, n, f)
        dh1 = _mm_nt_shard_in("gate_up_bwd", [(dg, wg_at(layer), ()), (du, wu_at(layer), ())])
        d_wg = _mm_tn("gate_up_dw", sv["h1b"], d, dg, f, N_CHIPS, (N_CHIPS, DEPTH, d, f), (None, None, d, f),
                      lambda j: (j, layer, 0, 0), into=d_wg, b_lead=lambda j: (j,))
        d_wu = _mm_tn("gate_up_dw", sv["h1b"], d, du, f, N_CHIPS, (N_CHIPS, DEPTH, d, f), (None, None, d, f),
                      lambda j: (j, layer, 0, 0), into=d_wu, b_lead=lambda j: (j,))
        dr1, dr1b, d_ln["ln1", layer] = _ln_bwd("ln_bwd", sv["r1"], vec(ln1_g, layer), dr2, ALPHA, [dh1])
        if layer < N_POOL_LAYERS:
            d_in, d_pw[layer], d_ps[layer] = _pool_bwd("pool_bwd", sv["h"].reshape(b_n, s_n, d),
                                                       dr1.reshape(b_n, s_n, d), pw_nat[layer], ps_nat[layer])
            d_top, top_scale, top_rest = d_in.reshape(n, d), 1.0, []
        else:
            a = layer - N_POOL_LAYERS
            do = _mm("o_bwd", [(dr1b, (TM, d), lambda m, _: (m, 0), wo_at(a), (cw, d), lambda m, _: (0, 0))],
                     NT, (n // TM, 1), (n, cw), F32, (TM, cw), lambda m, _: (m, 0), 1)
            d_wo = _mm_tn("o_dw", sv["ob"], cw // N_CHIPS, dr1b, d, N_CHIPS, (N_CHIPS, n_attn, cw // N_CHIPS, d),
                          (None, None, cw // N_CHIPS, d), lambda j: (j, a, 0, 0), into=d_wo)
            mixed = _attn_mix_bwd("attn_mix_bwd", do, sv["outs"], sv["lses"])
            dq_nat, dk_new, dv_new = [], [], []
            for g, dil in enumerate(DILATIONS):
                do_g = _to_strided(mixed[g].reshape(b_n, s_n, cw), dil)
                dd_g = _to_strided(mixed[g_n + g].reshape(b_n, s_n, N_HEADS), dil)
                qg, col = sv["q_str"][g]
                dq_g, dk_g, dv_g = _attn_bwd(f"attn_bwd_d{dil}", qg, col, k_str[g], v_str[g], do_g, sv["lse_str"][g], dd_g,
                                             s_n // dil // SPAN, *((dk_str[g], dv_str[g]) if dk_str else ()))
                dq_nat.append(_from_strided(dq_g, dil))
                dk_new.append(dk_g)
                dv_new.append(dv_g)
            dk_str, dv_str = dk_new, dv_new
            dq = None
            for g in range(g_n):
                dq = _rope("q_rope_bwd", dq_nat[g], 0, 1, cos_l, -sin_l, HEAD_DIM ** -0.5, MXU, out_cols=g_n, out_col0=g, into=dq)
            dq = dq.reshape(n, g_n * cw)
            d_attn = _mm_nt_cols_in("q_bwd", dq, wq_at(a), (), qc)
            d_wq = _mm_tn("q_dw", sv["hb"], d, dq, qc, N_CHIPS, (N_CHIPS, n_attn, d, qc), (None, None, d, qc),
                          lambda j: (j, a, 0, 0), into=d_wq)
            d_top, top_scale, top_rest = dr1, ALPHA, [d_attn]
            if a == 0:
                dkv = None
                for g, dil in enumerate(DILATIONS):
                    dkv = _rope("k_rope_bwd", _from_strided(dk_str[g], dil), 0, 1, cos_l, -sin_l, 1.0, MXU,
                                out_cols=2 * g_n, out_col0=g, into=dkv)
                for g, dil in enumerate(DILATIONS):
                    dkv = _rope("v_cast_bwd", _from_strided(dv_str[g], dil), 0, 1, cos_l, sin_l, 1.0, MXU, rotate=False,
                                out_cols=2 * g_n, out_col0=g_n + g, into=dkv)
                dkv = dkv.reshape(n, 2 * g_n * cw)
                h_kv = saved[N_POOL_LAYERS - 1]["h2b"]
                top_rest = top_rest + [_mm_nt_cols_in("kv_bwd", dkv, wkv_at(), (), kvb)]
                d_wkv = _mm_tn("kv_dw", h_kv, d, dkv, kvb, 2 * N_CHIPS, (2 * N_CHIPS, d, kvb), (None, d, kvb), lambda q: (q, 0, 0))
    grad_x = d_top.reshape(b_n, s_n, d)

    d_wkv = d_wkv.reshape(N_CHIPS, 2, d, kvb).transpose(0, 2, 1, 3)
    d_pw_all = jnp.stack(d_pw).reshape(N_POOL_LAYERS, 4, N_CHIPS, 64, POOL_GROUP_DIM).transpose(2, 0, 1, 3, 4)
    d_ps_all = jnp.stack(d_ps).reshape(N_POOL_LAYERS, N_CHIPS, -1).transpose(1, 0, 2)
    d_cw_all = jnp.stack([t[:, :3] for t in d_cw], axis=1)
    local = {
        "pool_w": d_pw_all, "pool_scale": d_ps_all, "w_q": d_wq, "w_kv": d_wkv, "w_o": d_wo,
        "ffn_w_gate": d_wg, "ffn_w_up": d_wu, "ffn_conv_w": d_cw_all, "ffn_w_down": d_wd,
    }
    grads4 = [local[k].reshape(N_CHIPS, *sharded[k][3]) for k in names]
    from_sibling = _to_sibling("grads_to_sibling", grads4, True)
    core_sums = [_add_halves(f"add_halves_{k}", g4_, r_, half, MXU if k in mxu_weights and k != "pool_w" else F32)
                 for k, g4_, r_ in zip(names, grads4, from_sibling)]
    from_chips = _to_owner_chips("grads_to_owner", core_sums)
    owned = [_add_chips(f"add_chips_{k}", s_, r_, chip) for k, s_, r_ in zip(names, core_sums, from_chips)]
    others = _to_sibling("halves_to_sibling", owned, False)

    out_grad, out_delta, out_m, out_v = {}, {}, {}, {}
    for k, own, oth in zip(names, owned, others):
        w_, m_, v_, shape = sharded[k]
        res = _adam_halves(f"adam_{k}", own, oth, half, w_.reshape(shape), m_.reshape(shape), v_.reshape(shape))
        out_grad[k], out_delta[k], out_m[k], out_v[k] = (t.reshape(w_.shape) for t in res)

    d_cb = jnp.stack([t[:, 3] for t in d_cw], axis=0).reshape(DEPTH * N_CHIPS * f // d, d)
    ln_rows = jnp.concatenate([jnp.stack([d_ln[which, layer][row] for layer in range(DEPTH)])
                               for which, row in (("ln1", 0), ("ln1", 1), ("ln2", 0), ("ln2", 1))])
    rows = jnp.concatenate([ln_rows, d_cb, jnp.broadcast_to(loss_tile[0:1, 0:1], (1, d))])
    pad = (-rows.shape[0]) % 8
    total = _sum_all_devices("sum_small", jnp.pad(rows, ((0, pad), (0, 0))))
    small = {"ln1_g": total[0:4], "ln1_b": total[4:8], "ln2_g": total[8:12], "ln2_b": total[12:16],
             "ffn_conv_b": total[16:16 + d_cb.shape[0]].reshape(ffn_conv_b.shape)}
    loss = total[16 + d_cb.shape[0], 0]
    small_in = {"ln1_g": (ln1_g, m_ln1_g, v_ln1_g), "ln1_b": (ln1_b, m_ln1_b, v_ln1_b), "ln2_g": (ln2_g, m_ln2_g, v_ln2_g),
                "ln2_b": (ln2_b, m_ln2_b, v_ln2_b), "ffn_conv_b": (ffn_conv_b, m_ffn_conv_b, v_ffn_conv_b)}
    for k, (w_, m_, v_) in small_in.items():
        out_grad[k] = small[k]
        out_delta[k], out_m[k], out_v[k] = _adam_small(f"adam_{k}", small[k], w_, m_, v_)

    order = ["pool_w", "pool_scale", "w_q", "w_kv", "w_o", "ffn_w_gate", "ffn_w_up", "ffn_conv_w", "ffn_conv_b",
             "ffn_w_down", "ln1_g", "ln1_b", "ln2_g", "ln2_b"]
    return (loss, grad_x, *[out_grad[k] for k in order], *[out_delta[k] for k in order],
            *[out_m[k] for k in order], *[out_v[k] for k in order])
```

```python
import functools
import math

import jax
import jax.numpy as jnp
from jax import lax
from jax.experimental import pallas as pl
from jax.experimental.pallas import tpu as pltpu

F32 = jnp.float32
BF16 = jnp.bfloat16
MXU = jnp.bfloat16

DEPTH = 4
N_POOL_LAYERS = 2
POOL_WINDOWS = (2, 4, 8, 16)
POOL_GROUP_DIM = 256
HEAD_DIM = 64
N_HEADS = 16
DILATIONS = (1, 4, 16)
SPAN = 128
ROPE_THETA = 10000.0
ALPHA = (2.0 * DEPTH) ** 0.25
LN_EPS = 1e-5
ADAM_LR, ADAM_B1, ADAM_B2, ADAM_EPS, ADAM_WD, ADAM_STEP = 0.001, 0.9, 0.999, 1e-08, 0.01, 10

N_CHIPS = 4
VMEM_LIMIT = 56 * 1024 * 1024
MESH = pl.DeviceIdType.MESH

NN = (((1,), (0,)), ((), ()))
NT = (((1,), (1,)), ((), ()))
TN = (((0,), (0,)), ((), ()))


def _params(sem=None):
    return pltpu.CompilerParams(dimension_semantics=sem, vmem_limit_bytes=VMEM_LIMIT)


def _chunks(n_rows, ch, fn, unroll=1):
    def step(i, carry):
        fn(pl.multiple_of(i * ch, ch))
        return carry

    lax.fori_loop(0, n_rows // ch, step, 0, unroll=unroll)


def _fold8(v):
    return jnp.sum(v.reshape(v.shape[0] // 8, 8, v.shape[1]), axis=0)


def _down(v, k):
    return pltpu.roll(v, k, 0)


def _up(v, k):
    return pltpu.roll(v, v.shape[0] - k, 0)


def _own_refs(job, refs, n_in, n_out, n_scratch):
    if job is None:
        return list(refs), None
    a = n_in
    b = a + len(job["ins"])
    c = b + n_out
    e = c + len(job["outs"])
    g = e + n_scratch
    return list(refs[:a]) + list(refs[b:c]) + list(refs[e:g]), job["bind"](refs[a:b], refs[c:e], *refs[g:])


def _grid_ends(grid):
    ids = [pl.program_id(i) for i in range(len(grid))]
    both = lambda conds: functools.reduce(lambda p, q: p & q, conds)
    return both([i == 0 for i in ids]), both([i == g - 1 for i, g in zip(ids, grid)])


def _mm(name, pairs, dims, grid, out_shape, out_dtype, out_block, out_map, nk, into=None, job=None):
    n_pairs = len(pairs)
    kax = len(grid) - 1
    n_in = 2 * n_pairs + (1 if into is not None else 0)

    def body(*refs):
        refs, riding = _own_refs(job, refs, n_in, 1, 0)
        o_ref = refs[n_in]
        if riding is not None:
            at_first, at_last = _grid_ends(grid)
            pl.when(at_first)(riding.begin)
        part = None
        for p in range(n_pairs):
            t = lax.dot_general(refs[2 * p][...], refs[2 * p + 1][...], dims, preferred_element_type=F32)
            part = t if part is None else part + t
        if nk == 1:
            o_ref[...] = part.astype(o_ref.dtype)
        else:
            k = pl.program_id(kax)

            @pl.when(k == 0)
            def _():
                o_ref[...] = part

            @pl.when(k > 0)
            def _():
                o_ref[...] += part

        if riding is not None:
            pl.when(at_last)(riding.end)

    operands, in_specs = [], []
    for a, a_block, a_map, b, b_block, b_map in pairs:
        operands += [a, b]
        in_specs += [pl.BlockSpec(a_block, a_map), pl.BlockSpec(b_block, b_map)]
    aliases = {}
    if into is not None:
        operands.append(into)
        in_specs.append(pl.BlockSpec(memory_space=pl.ANY))
        aliases = {2 * n_pairs: 0}
    assert nk == 1 or out_dtype == F32
    j_ins, j_outs, j_scratch = (job["ins"], job["outs"], job["scratch"]) if job else ([], [], [])
    sem = ("arbitrary",) * len(grid) if job else ("parallel",) * kax + ("arbitrary",)
    res = pl.pallas_call(
        body, name=name, grid=grid, in_specs=in_specs + [pl.BlockSpec(memory_space=pl.ANY)] * len(j_ins),
        out_specs=[pl.BlockSpec(out_block, out_map)] + [pl.BlockSpec(memory_space=pl.ANY)] * len(j_outs),
        out_shape=[jax.ShapeDtypeStruct(out_shape, out_dtype)] + list(j_outs),
        input_output_aliases=aliases, scratch_shapes=list(j_scratch), compiler_params=_params(sem),
    )(*operands, *j_ins)
    return (res[0], res[1:]) if job else res[0]


TM = 2048


def _mm_cols(name, a, w, w_idx, out_dtype, cb=None, job=None):
    n, k = a.shape
    j_n, c = w.shape[0], w.shape[-1]
    cb = c if cb is None else cb
    s = c // cb
    wb = (None,) * (w.ndim - 2) + (k, cb)
    return _mm(name, [(a, (TM, k), lambda q, m, _: (m, 0), w, wb, lambda q, m, _: (q // s, *w_idx, 0, q % s))], NN,
               (j_n * s, n // TM, 1), (n, j_n * c), out_dtype, (TM, cb), lambda q, m, _: (m, q), 1, job=job)


def _mm_shard_out(name, a, w, w_idx, out_dtype, job=None):
    n, k = a.shape
    j_n, c = w.shape[0], w.shape[-1]
    wb = (None,) * (w.ndim - 2) + (k, c)
    return _mm(name, [(a, (TM, k), lambda j, m, _: (m, 0), w, wb, lambda j, m, _: (j, *w_idx, 0, 0))], NN,
               (j_n, n // TM, 1), (j_n, n, c), out_dtype, (None, TM, c), lambda j, m, _: (j, m, 0), 1, job=job)


def _mm_shard_in(name, a4, w, w_idx, job=None):
    j_n, n, c = a4.shape
    k = w.shape[-1]
    wb = (None,) * (w.ndim - 2) + (c, k)
    return _mm(name, [(a4, (None, TM, c), lambda m, j: (j, m, 0), w, wb, lambda m, j: (j, *w_idx, 0, 0))], NN,
               (n // TM, j_n), (n, k), F32, (TM, k), lambda m, j: (m, 0), j_n, job=job)


def _mm_nt_shard_out(name, a, w, w_idx, job=None):
    n, k = a.shape
    j_n, c = w.shape[0], w.shape[-2]
    wb = (None,) * (w.ndim - 2) + (c, k)
    return _mm(name, [(a, (TM, k), lambda j, m, _: (m, 0), w, wb, lambda j, m, _: (j, *w_idx, 0, 0))], NT,
               (j_n, n // TM, 1), (j_n, n, c), F32, (None, TM, c), lambda j, m, _: (j, m, 0), 1, job=job)


def _mm_nt_shard_in(name, terms, job=None):
    pairs = []
    for a4, w, w_idx in terms:
        j_n, n, c = a4.shape
        k = w.shape[-2]
        wb = (None,) * (w.ndim - 2) + (k, c)
        pairs.append((a4, (None, TM, c), lambda m, j: (j, m, 0), w, wb,
                      functools.partial(lambda m, j, w_idx: (j, *w_idx, 0, 0), w_idx=w_idx)))
    return _mm(name, pairs, NT, (n // TM, j_n), (n, k), F32, (TM, k), lambda m, j: (m, 0), j_n, job=job)


def _mm_nt_cols_in(name, a, w, w_idx, cb):
    n, ct = a.shape
    j_n, k, c = w.shape[0], w.shape[-2], w.shape[-1]
    s = c // cb
    wb = (None,) * (w.ndim - 2) + (k, cb)
    return _mm(name, [(a, (TM, cb), lambda m, q: (m, q), w, wb, lambda m, q: (q // s, *w_idx, 0, q % s))], NT,
               (n // TM, ct // cb), (n, k), F32, (TM, k), lambda m, q: (m, 0), ct // cb)


def _mm_tn(name, a, a_cols, b, b_cols, n_blocks, out_shape, out_block, out_map, into=None, a_lead=None, b_lead=None, job=None):
    n = a.shape[-2]
    a_nb = a.shape[-1] // a_cols
    b_nb = b.shape[-1] // b_cols
    if a_lead is None:
        a_block, a_map = (TM, a_cols), lambda q, t: (t, q if a_nb > 1 else 0)
    else:
        a_block, a_map = (None, TM, a_cols), lambda q, t: (*a_lead(q), t, 0)
    if b_lead is None:
        b_block, b_map = (TM, b_cols), lambda q, t: (t, q if b_nb > 1 else 0)
    else:
        b_block, b_map = (None, TM, b_cols), lambda q, t: (*b_lead(q), t, 0)
    return _mm(name, [(a, a_block, a_map, b, b_block, b_map)], TN, (n_blocks, n // TM), out_shape, F32,
               out_block, lambda q, t: out_map(q), n // TM, into=into, job=job)


LN_ROWS = 512
LN_CH = 16
LN_UNROLL = 4


def _ln_stats(r):
    mu = jnp.mean(r, -1, keepdims=True)
    xc = r - mu
    var = jnp.mean(xc * xc, -1, keepdims=True)
    return xc, lax.rsqrt(var + LN_EPS)


def _add_ln(name, a, mix, g, b):
    n, d = a.shape

    def body(a_ref, m_ref, g_ref, b_ref, r_ref, h_ref, hb_ref):
        gg, bb = g_ref[...], b_ref[...]

        def chunk(s):
            rows = pl.ds(s, LN_CH)
            r = ALPHA * a_ref[rows, :] + m_ref[rows, :]
            xc, rstd = _ln_stats(r)
            y = xc * rstd * gg + bb
            r_ref[rows, :] = r
            h_ref[rows, :] = y
            hb_ref[rows, :] = y.astype(MXU)

        _chunks(LN_ROWS, LN_CH, chunk, LN_UNROLL)

    tile = pl.BlockSpec((LN_ROWS, d), lambda i: (i, 0))
    vec = pl.BlockSpec((1, d), lambda i: (0, 0))
    return pl.pallas_call(
        body, name=name, grid=(n // LN_ROWS,), in_specs=[tile, tile, vec, vec], out_specs=[tile, tile, tile],
        out_shape=[jax.ShapeDtypeStruct((n, d), F32), jax.ShapeDtypeStruct((n, d), F32), jax.ShapeDtypeStruct((n, d), MXU)],
        compiler_params=_params(("parallel",)),
    )(a, mix, g, b)


def _ln_bwd(name, r, g, d_a, scale_a, d_rest):
    n, d = r.shape
    n_rest = len(d_rest)
    steps = n // LN_ROWS

    def body(*refs):
        r_ref, g_ref, da_ref = refs[:3]
        rest = refs[3:3 + n_rest]
        dr_ref, drb_ref, gb_ref, acc = refs[3 + n_rest:]
        i = pl.program_id(0)

        @pl.when(i == 0)
        def _():
            acc[...] = jnp.zeros_like(acc)

        gg = g_ref[...]

        def chunk(s):
            rows = pl.ds(s, LN_CH)
            xc, rstd = _ln_stats(r_ref[rows, :])
            xhat = xc * rstd
            dy = da_ref[rows, :] if scale_a == 1.0 else scale_a * da_ref[rows, :]
            for t in rest:
                dy = dy + t[rows, :]
            dyg = dy * gg
            m1 = jnp.mean(dyg, -1, keepdims=True)
            m2 = jnp.mean(dyg * xhat, -1, keepdims=True)
            dr = rstd * (dyg - m1 - xhat * m2)
            dr_ref[rows, :] = dr
            drb_ref[rows, :] = dr.astype(MXU)
            acc[0] += _fold8(dy * xhat)
            acc[1] += _fold8(dy)

        _chunks(LN_ROWS, LN_CH, chunk, LN_UNROLL)

        @pl.when(i == steps - 1)
        def _():
            gb_ref[0:1, :] = jnp.sum(acc[0], axis=0, keepdims=True)
            gb_ref[1:2, :] = jnp.sum(acc[1], axis=0, keepdims=True)

    tile = pl.BlockSpec((LN_ROWS, d), lambda i: (i, 0))
    vec = pl.BlockSpec((1, d), lambda i: (0, 0))
    return pl.pallas_call(
        body, name=name, grid=(steps,), in_specs=[tile, vec, tile] + [tile] * n_rest,
        out_specs=[tile, tile, pl.BlockSpec((2, d), lambda i: (0, 0))],
        out_shape=[jax.ShapeDtypeStruct((n, d), F32), jax.ShapeDtypeStruct((n, d), MXU), jax.ShapeDtypeStruct((2, d), F32)],
        scratch_shapes=[pltpu.VMEM((2, 8, d), F32)],
        compiler_params=_params(("arbitrary",)),
    )(r, g, d_a, *d_rest)


FFN_ROWS = 512
FFN_CH = 32
GELU_C1 = math.sqrt(2.0 / math.pi)
GELU_C2 = 0.044715


def _conv3(v, prev8, w0, w1, w2, bias):
    n = v.shape[0]
    ext = jnp.concatenate([prev8, v], axis=0)
    g1 = _down(ext, 1)[8:8 + n]
    g2 = _down(ext, 2)[8:8 + n]
    return bias + w0 * g2 + w1 * g1 + w2 * v, g1, g2


def _glu_fwd(name, g4, u4, conv_w, conv_b):
    j_n, b_n, s_n, f = g4.shape
    tiles = s_n // FFN_ROWS

    def body(g_ref, halo_ref, u_ref, w_ref, b_ref, o_ref, gs):
        s = pl.program_id(2)
        gs[0:8, :] = jnp.where(s > 0, halo_ref[...], 0.0)
        gs[8:, :] = g_ref[...]
        w0, w1, w2, bias = w_ref[0:1, :], w_ref[1:2, :], w_ref[2:3, :], b_ref[...]

        def chunk(st):
            v = gs[pl.ds(pl.multiple_of(st + 8, 8), FFN_CH), :]
            conv, _, _ = _conv3(v, gs[pl.ds(st, 8), :], w0, w1, w2, bias)
            cdf = 0.5 * (1.0 + jnp.tanh(GELU_C1 * (conv + GELU_C2 * (conv * conv * conv))))
            o_ref[pl.ds(st, FFN_CH), :] = (conv * cdf * u_ref[pl.ds(st, FFN_CH), :]).astype(o_ref.dtype)

        _chunks(FFN_ROWS, FFN_CH, chunk)

    tile = pl.BlockSpec((None, None, FFN_ROWS, f), lambda j, b, s: (j, b, s, 0))
    halo = pl.BlockSpec((None, None, 8, f), lambda j, b, s: (j, b, jnp.maximum(s * (FFN_ROWS // 8) - 1, 0), 0))
    return pl.pallas_call(
        body, name=name, grid=(j_n, b_n, tiles),
        in_specs=[tile, halo, tile,
                  pl.BlockSpec((None, 3, f), lambda j, b, s: (j, 0, 0)),
                  pl.BlockSpec((None, 1, f), lambda j, b, s: (j, 0, 0))],
        out_specs=tile, out_shape=jax.ShapeDtypeStruct(g4.shape, MXU),
        scratch_shapes=[pltpu.VMEM((8 + FFN_ROWS, f), F32)],
        compiler_params=_params(("parallel", "parallel", "parallel")),
    )(g4, g4, u4, conv_w, conv_b)


def _glu_bwd(name, dh4, g4, u4, conv_w, conv_b):
    j_n, b_n, s_n, f = g4.shape
    tiles = s_n // FFN_ROWS
    ext = FFN_CH + 8

    def body(d_ref, dnext_ref, g_ref, gprev_ref, gnext_ref, u_ref, unext_ref, w_ref, b_ref,
             dg_ref, du_ref, wb_ref, gs, us, ds, acc):
        b, s = pl.program_id(1), pl.program_id(2)
        last = s == tiles - 1

        @pl.when((b == 0) & (s == 0))
        def _():
            acc[...] = jnp.zeros_like(acc)

        gs[0:8, :] = jnp.where(s > 0, gprev_ref[...], 0.0)
        gs[8:8 + FFN_ROWS, :] = g_ref[...]
        gs[8 + FFN_ROWS:, :] = gnext_ref[...]
        us[0:FFN_ROWS, :] = u_ref[...]
        us[FFN_ROWS:, :] = unext_ref[...]
        ds[0:FFN_ROWS, :] = d_ref[...]
        ds[FFN_ROWS:, :] = jnp.where(last, 0.0, dnext_ref[...])
        w0, w1, w2, bias = w_ref[0:1, :], w_ref[1:2, :], w_ref[2:3, :], b_ref[...]

        def chunk(st):
            v = gs[pl.ds(pl.multiple_of(st + 8, 8), ext), :]
            conv, g1, g2 = _conv3(v, gs[pl.ds(st, 8), :], w0, w1, w2, bias)
            th = jnp.tanh(GELU_C1 * (conv + GELU_C2 * (conv * conv * conv)))
            cdf = 0.5 * (1.0 + th)
            dact = cdf + conv * (0.5 * GELU_C1) * (1.0 - th * th) * (1.0 + (3.0 * GELU_C2) * (conv * conv))
            de = ds[pl.ds(st, ext), :]
            dconv = de * us[pl.ds(st, ext), :] * dact
            du_ref[pl.ds(st, FFN_CH), :] = (de * (conv * cdf))[:FFN_CH].astype(du_ref.dtype)
            dg = w2 * dconv + w1 * _up(dconv, 1) + w0 * _up(dconv, 2)
            dg_ref[pl.ds(st, FFN_CH), :] = dg[:FFN_CH].astype(dg_ref.dtype)
            dc = dconv[:FFN_CH]
            acc[0] += _fold8(dc * g2[:FFN_CH])
            acc[1] += _fold8(dc * g1[:FFN_CH])
            acc[2] += _fold8(dc * v[:FFN_CH])
            acc[3] += _fold8(dc)

        _chunks(FFN_ROWS, FFN_CH, chunk)

        @pl.when((b == b_n - 1) & last)
        def _():
            for k in range(4):
                wb_ref[k:k + 1, :] = jnp.sum(acc[k], axis=0, keepdims=True)

    blocks8 = FFN_ROWS // 8
    tile = pl.BlockSpec((None, None, FFN_ROWS, f), lambda j, b, s: (j, b, s, 0))
    prev = pl.BlockSpec((None, None, 8, f), lambda j, b, s: (j, b, jnp.maximum(s * blocks8 - 1, 0), 0))
    nxt = pl.BlockSpec((None, None, 8, f), lambda j, b, s: (j, b, jnp.minimum((s + 1) * blocks8, s_n // 8 - 1), 0))
    return pl.pallas_call(
        body, name=name, grid=(j_n, b_n, tiles),
        in_specs=[tile, nxt, tile, prev, nxt, tile, nxt,
                  pl.BlockSpec((None, 3, f), lambda j, b, s: (j, 0, 0)),
                  pl.BlockSpec((None, 1, f), lambda j, b, s: (j, 0, 0))],
        out_specs=[tile, tile, pl.BlockSpec((None, 4, f), lambda j, b, s: (j, 0, 0))],
        out_shape=[jax.ShapeDtypeStruct(g4.shape, MXU), jax.ShapeDtypeStruct(g4.shape, MXU),
                   jax.ShapeDtypeStruct((j_n, 4, f), F32)],
        scratch_shapes=[pltpu.VMEM((16 + FFN_ROWS, f), F32), pltpu.VMEM((8 + FFN_ROWS, f), F32),
                        pltpu.VMEM((8 + FFN_ROWS, f), F32), pltpu.VMEM((4, 8, f), F32)],
        compiler_params=_params(("parallel", "arbitrary", "arbitrary")),
    )(dh4, dh4, g4, g4, g4, u4, u4, conv_w, conv_b)


POOL_ROWS = 512
POOL_CH = 32
POOL_HALO = 16


def _pool_windows(v, t0, gi, causal):
    shift = _down if causal else _up
    acc, k = v, 1
    while k < POOL_WINDOWS[gi]:
        acc = acc + shift(acc, k)
        k *= 2
    return acc


def _count(t0, n, w):
    t = t0 + lax.broadcasted_iota(jnp.int32, (n, 1), 0)
    return jnp.minimum(t + 1, w).astype(F32)


def _pooled_into(xs, pooled, t_tile):
    def chunk(st):
        for gi, w in enumerate(POOL_WINDOWS):
            cols = slice(gi * POOL_GROUP_DIM, (gi + 1) * POOL_GROUP_DIM)
            v = xs[pl.ds(st, POOL_CH + POOL_HALO), cols]
            sums = _pool_windows(v, None, gi, True)[POOL_HALO:]
            val = sums / _count(t_tile + st, POOL_CH, w) - v[POOL_HALO:]
            pooled[pl.ds(st, POOL_CH), cols] = val.astype(pooled.dtype)

    _chunks(POOL_ROWS, POOL_CH, chunk)


def _pool_specs(b_n, s_n, d):
    per = POOL_ROWS // POOL_HALO
    tile = pl.BlockSpec((None, POOL_ROWS, d), lambda b, s: (b, s, 0))
    prev = pl.BlockSpec((None, POOL_HALO, d), lambda b, s: (b, jnp.maximum(s * per - 1, 0), 0))
    nxt = pl.BlockSpec((None, POOL_HALO, d), lambda b, s: (b, jnp.minimum((s + 1) * per, s_n // POOL_HALO - 1), 0))
    return tile, prev, nxt


def _pool_fwd(name, h3, w, scale):
    b_n, s_n, d = h3.shape
    tile, prev, _ = _pool_specs(b_n, s_n, d)

    def body(h_ref, halo_ref, w_ref, sc_ref, o_ref, xs, pooled):
        s = pl.program_id(1)
        xs[0:POOL_HALO, :] = jnp.where(s > 0, halo_ref[...], 0.0)
        xs[POOL_HALO:, :] = h_ref[...]
        _pooled_into(xs, pooled, s * POOL_ROWS)
        for gi in range(len(POOL_WINDOWS)):
            cols = slice(gi * POOL_GROUP_DIM, (gi + 1) * POOL_GROUP_DIM)
            y = jnp.dot(pooled[:, cols], w_ref[gi], preferred_element_type=F32)
            o_ref[:, cols] = y * sc_ref[:, cols]

    return pl.pallas_call(
        body, name=name, grid=(b_n, s_n // POOL_ROWS),
        in_specs=[tile, prev, pl.BlockSpec(w.shape, lambda b, s: (0, 0, 0)), pl.BlockSpec((1, d), lambda b, s: (0, 0))],
        out_specs=tile, out_shape=jax.ShapeDtypeStruct(h3.shape, F32),
        scratch_shapes=[pltpu.VMEM((POOL_HALO + POOL_ROWS, d), F32), pltpu.VMEM((POOL_ROWS, d), MXU)],
        compiler_params=_params(("parallel", "parallel")),
    )(h3, h3, w, scale)


def _pool_bwd(name, h3, dm3, w, scale):
    b_n, s_n, d = h3.shape
    tile, prev, nxt = _pool_specs(b_n, s_n, d)
    tiles = s_n // POOL_ROWS
    ext = POOL_ROWS + POOL_HALO

    def body(h_ref, halo_ref, dm_ref, dnext_ref, w_ref, sc_ref, dh_ref, dw_ref, dsc_ref, xs, pooled, ds, dp):
        b, s = pl.program_id(0), pl.program_id(1)

        @pl.when((b == 0) & (s == 0))
        def _():
            dw_ref[...] = jnp.zeros_like(dw_ref)
            dsc_ref[...] = jnp.zeros_like(dsc_ref)

        xs[0:POOL_HALO, :] = jnp.where(s > 0, halo_ref[...], 0.0)
        xs[POOL_HALO:, :] = h_ref[...]
        ds[0:POOL_ROWS, :] = dm_ref[...]
        ds[POOL_ROWS:, :] = jnp.where(s == tiles - 1, 0.0, dnext_ref[...])
        _pooled_into(xs, pooled, s * POOL_ROWS)
        for gi in range(len(POOL_WINDOWS)):
            cols = slice(gi * POOL_GROUP_DIM, (gi + 1) * POOL_GROUP_DIM)
            dyb = (ds[:, cols] * sc_ref[:, cols]).astype(MXU)
            dp[:, cols] = lax.dot_general(dyb, w_ref[gi], NT, preferred_element_type=F32)
            pg = pooled[:, cols]
            dw_ref[gi] += lax.dot_general(pg, dyb[:POOL_ROWS], TN, preferred_element_type=F32)
            ypre = jnp.dot(pg, w_ref[gi], preferred_element_type=F32)
            dsc_ref[:, cols] += jnp.sum(ds[0:POOL_ROWS, cols] * ypre, axis=0, keepdims=True)

        def chunk(st):
            for gi, w_len in enumerate(POOL_WINDOWS):
                cols = slice(gi * POOL_GROUP_DIM, (gi + 1) * POOL_GROUP_DIM)
                v = dp[pl.ds(st, POOL_CH + POOL_HALO), cols]
                q = v / _count(s * POOL_ROWS + st, POOL_CH + POOL_HALO, w_len)
                back = _pool_windows(q, None, gi, False)[:POOL_CH] - v[:POOL_CH]
                dh_ref[pl.ds(st, POOL_CH), cols] = ALPHA * ds[pl.ds(st, POOL_CH), cols] + back

        _chunks(POOL_ROWS, POOL_CH, chunk)

    return pl.pallas_call(
        body, name=name, grid=(b_n, tiles),
        in_specs=[tile, prev, tile, nxt, pl.BlockSpec(w.shape, lambda b, s: (0, 0, 0)), pl.BlockSpec((1, d), lambda b, s: (0, 0))],
        out_specs=[tile, pl.BlockSpec(w.shape, lambda b, s: (0, 0, 0)), pl.BlockSpec((1, d), lambda b, s: (0, 0))],
        out_shape=[jax.ShapeDtypeStruct(h3.shape, F32), jax.ShapeDtypeStruct(w.shape, F32), jax.ShapeDtypeStruct((1, d), F32)],
        scratch_shapes=[pltpu.VMEM((POOL_HALO + POOL_ROWS, d), F32), pltpu.VMEM((POOL_ROWS, d), MXU),
                        pltpu.VMEM((ext, d), F32), pltpu.VMEM((ext, d), F32)],
        compiler_params=_params(("arbitrary", "arbitrary")),
    )(h3, h3, dm3, dm3, w, scale)


ROPE_ROWS = 256


def _rope_tables(s_n):
    inv_freq = ROPE_THETA ** (-jnp.arange(0, HEAD_DIM, 2, dtype=F32) / HEAD_DIM)
    ang = jnp.arange(s_n, dtype=F32)[:, None] * inv_freq[None, :]
    cos, sin = jnp.cos(ang), jnp.sin(ang)
    cos_l = jnp.tile(cos, (1, 4))
    sin_l = jnp.tile(jnp.concatenate([-sin, sin], axis=1), (1, 2))
    return cos_l, sin_l


def _rope(name, x3, col0, n_col, cos_l, sin_l, scale, out_dtype, rotate=True, out_cols=None, out_col0=0, into=None):
    b_n, s_n, _ = x3.shape
    cw = N_HEADS * HEAD_DIM
    out_cols = n_col if out_cols is None else out_cols

    def body(x_ref, c_ref, s_ref, *rest):
        o_ref = rest[-1]
        lane = lax.broadcasted_iota(jnp.int32, (ROPE_ROWS, 128), 1)
        first_half = (lane % HEAD_DIM) < (HEAD_DIM // 2)
        cos, sin = c_ref[...], s_ref[...]
        for cb in range(cw // 128):
            cols = slice(cb * 128, (cb + 1) * 128)
            y = x_ref[:, cols]
            if rotate:
                other = jnp.where(first_half, pltpu.roll(y, 128 - HEAD_DIM // 2, 1), pltpu.roll(y, HEAD_DIM // 2, 1))
                y = y * cos + other * sin
            o_ref[:, cols] = (y if scale == 1.0 else y * scale).astype(o_ref.dtype)

    tile = pl.BlockSpec((None, ROPE_ROWS, cw), lambda b, s, c: (b, s, col0 + c))
    tab = pl.BlockSpec((ROPE_ROWS, 128), lambda b, s, c: (s, 0))
    extra, extra_specs, aliases = [], [], {}
    if into is not None:
        extra, extra_specs, aliases = [into], [pl.BlockSpec(memory_space=pl.ANY)], {3: 0}
    return pl.pallas_call(
        body, name=name, grid=(b_n, s_n // ROPE_ROWS, n_col), in_specs=[tile, tab, tab] + extra_specs,
        out_specs=pl.BlockSpec((None, ROPE_ROWS, cw), lambda b, s, c: (b, s, out_col0 + c)),
        out_shape=jax.ShapeDtypeStruct((b_n, s_n, out_cols * cw), out_dtype), input_output_aliases=aliases,
        compiler_params=_params(("parallel", "parallel", "parallel")),
    )(x3, cos_l, sin_l, *extra)


def _to_strided(a, d):
    if d == 1:
        return a
    b_n, s_n, c = a.shape
    return a.reshape(b_n, s_n // d, d, c).transpose(0, 2, 1, 3).reshape(b_n, s_n, c)


def _from_strided(a, d):
    if d == 1:
        return a
    b_n, s_n, c = a.shape
    return a.reshape(b_n, d, s_n // d, c).transpose(0, 2, 1, 3).reshape(b_n, s_n, c)


def _attn_fwd(name, q, q_col, k, v, blocks_per_seq):
    b_n, s_n, cw = k.shape
    nb = s_n // SPAN
    with_prev = blocks_per_seq > 1

    keys = 2 * SPAN if with_prev else SPAN
    pair = 2 * HEAD_DIM

    def body(*refs):
        if with_prev:
            q_ref, kc_ref, kp_ref, vc_ref, vp_ref, o_ref, lse_ref, k_all, v_all, s_buf, m_buf = refs
            k_all[0:SPAN, :] = kp_ref[...]
            k_all[SPAN:, :] = kc_ref[...]
            v_all[0:SPAN, :] = vp_ref[...]
            v_all[SPAN:, :] = vc_ref[...]
        else:
            q_ref, k_all, v_all, o_ref, lse_ref, s_buf, m_buf = refs
        n = pl.program_id(1)
        qi = lax.broadcasted_iota(jnp.int32, (SPAN, keys), 0)
        kj = lax.broadcasted_iota(jnp.int32, (SPAN, keys), 1)
        if with_prev:
            back = jnp.where((n % blocks_per_seq) != 0, 0, 2 * SPAN)
            mask = ((kj < SPAN) & (kj >= qi + back)) | ((kj >= SPAN) & (kj - SPAN <= qi))
        else:
            mask = kj <= qi
        lane = lax.broadcasted_iota(jnp.int32, (SPAN, pair), 1)
        low = lane < HEAD_DIM
        ones = jnp.ones((keys, pair), MXU)
        for h in range(N_HEADS):
            grp = slice((h // 2) * pair, (h // 2 + 1) * pair)
            q2 = q_ref[:, grp]
            qh = jnp.where(low if h % 2 == 0 else ~low, q2, jnp.zeros_like(q2))
            s = jnp.where(mask, lax.dot_general(qh, k_all[:, grp], NT, preferred_element_type=F32), -jnp.inf)
            s_buf[h] = s
            m_buf[h] = jnp.broadcast_to(jnp.max(s, -1, keepdims=True), (SPAN, pair))
        lse_all = jnp.zeros((SPAN, pair), F32)
        for hp in range(N_HEADS // 2):
            grp = slice(hp * pair, (hp + 1) * pair)
            v2 = v_all[:, grp]
            halves = []
            for h in (2 * hp, 2 * hp + 1):
                m = m_buf[h]
                p = jnp.exp(s_buf[h] - jnp.tile(m, (1, keys // pair))).astype(MXU)
                tot = jnp.dot(p, ones, preferred_element_type=F32)
                halves.append(jnp.dot(p, v2, preferred_element_type=F32) / tot)
                lse_all = jnp.where(lane == h, m + jnp.log(tot), lse_all)
            o_ref[:, grp] = jnp.where(low, halves[0], halves[1])
        lse_ref[...] = lse_all[:, 0:N_HEADS]

    cur = lambda b, n: (b, n, 0)
    prv = lambda b, n: (b, jnp.maximum(n - 1, 0), 0)
    blk = (None, SPAN, cw)
    kv_specs = [pl.BlockSpec(blk, cur), pl.BlockSpec(blk, prv)] if with_prev else [pl.BlockSpec(blk, cur)]
    operands = [q, k, k, v, v] if with_prev else [q, k, v]
    stage = [pltpu.VMEM((keys, cw), MXU)] * 2 if with_prev else []
    return pl.pallas_call(
        body, name=name, grid=(b_n, nb),
        in_specs=[pl.BlockSpec(blk, lambda b, n: (b, n, q_col))] + kv_specs + kv_specs,
        out_specs=[pl.BlockSpec(blk, cur), pl.BlockSpec((None, SPAN, N_HEADS), cur)],
        out_shape=[jax.ShapeDtypeStruct((b_n, s_n, cw), F32), jax.ShapeDtypeStruct((b_n, s_n, N_HEADS), F32)],
        scratch_shapes=stage + [pltpu.VMEM((N_HEADS, SPAN, keys), F32), pltpu.VMEM((N_HEADS, SPAN, pair), F32)],
        compiler_params=_params(("parallel", "parallel")),
    )(*operands)


def _attn_bwd(name, q, q_col, k, v, do, lse, dd, blocks_per_seq, dk_prev=None, dv_prev=None):
    b_n, s_n, cw = k.shape
    nb = s_n // SPAN
    with_next = blocks_per_seq > 1
    accumulate = dk_prev is not None
    rows = 2 * SPAN if with_next else SPAN

    def body(*refs):
        refs = list(refs)
        qc_ref, doc_ref, lsec_ref, ddc_ref = refs[:4]
        del refs[:4]
        if with_next:
            qn_ref, don_ref, lsen_ref, ddn_ref = refs[:4]
            del refs[:4]
        k_ref, v_ref = refs[:2]
        del refs[:2]
        if accumulate:
            dkp_ref, dvp_ref = refs[:2]
            del refs[:2]
        dq_ref, dk_ref, dv_ref = refs[:3]
        del refs[:3]
        if with_next:
            carry, q_all, do_all, side = refs[:4]
            del refs[:4]
            q_all[0:SPAN, :] = qc_ref[...]
            q_all[SPAN:, :] = qn_ref[...]
            do_all[0:SPAN, :] = doc_ref[...]
            do_all[SPAN:, :] = don_ref[...]
            side[0, 0:SPAN, :] = lsec_ref[...]
            side[0, SPAN:, :] = lsen_ref[...]
            side[1, 0:SPAN, :] = ddc_ref[...]
            side[1, SPAN:, :] = ddn_ref[...]
            lse_at = lambda h: side[0, :, h:h + 1]
            dd_at = lambda h: side[1, :, h:h + 1]
        else:
            q_all, do_all = qc_ref, doc_ref
            lse_at = lambda h: lsec_ref[:, h:h + 1]
            dd_at = lambda h: ddc_ref[:, h:h + 1]
        p_buf, ds_buf = refs
        n = pl.program_id(1)
        qi = lax.broadcasted_iota(jnp.int32, (rows, SPAN), 0)
        kj = lax.broadcasted_iota(jnp.int32, (rows, SPAN), 1)
        if with_next:
            first = (n % blocks_per_seq) == 0
            reach = jnp.where(((n + 1) % blocks_per_seq) != 0, SPAN, -2 * SPAN)
            mask = ((qi < SPAN) & (kj <= qi)) | ((qi >= SPAN) & (kj >= qi - reach))
        else:
            mask = kj <= qi
        pair = 2 * HEAD_DIM
        low = lax.broadcasted_iota(jnp.int32, (rows, pair), 1) < HEAD_DIM
        low_k = lax.broadcasted_iota(jnp.int32, (SPAN, pair), 1) < HEAD_DIM

        def pick(v, h, low_mask):
            return jnp.where(low_mask if h % 2 == 0 else ~low_mask, v, jnp.zeros_like(v))

        for h in range(N_HEADS):
            grp = slice((h // 2) * pair, (h // 2 + 1) * pair)
            s = lax.dot_general(pick(q_all[:, grp], h, low), k_ref[:, grp], NT, preferred_element_type=F32)
            p = jnp.where(mask, jnp.exp(s - lse_at(h)), 0.0)
            dp = lax.dot_general(pick(do_all[:, grp], h, low), v_ref[:, grp], NT, preferred_element_type=F32)
            p_buf[h] = p.astype(MXU)
            ds_buf[h] = (p * (dp - dd_at(h))).astype(MXU)
        for hp in range(N_HEADS // 2):
            grp = slice(hp * pair, (hp + 1) * pair)
            q2, do2, k2 = q_all[:, grp], do_all[:, grp], k_ref[:, grp]
            dv = dk = dq2 = None
            for h in (2 * hp, 2 * hp + 1):
                t_dv = lax.dot_general(p_buf[h], pick(do2, h, low), TN, preferred_element_type=F32)
                t_dk = lax.dot_general(ds_buf[h], pick(q2, h, low), TN, preferred_element_type=F32)
                t_dq = jnp.dot(ds_buf[h], pick(k2, h, low_k), preferred_element_type=F32)
                dv = t_dv if dv is None else dv + t_dv
                dk = t_dk if dk is None else dk + t_dk
                dq2 = t_dq if dq2 is None else dq2 + t_dq
            if accumulate:
                dk = dk + dkp_ref[:, grp]
                dv = dv + dvp_ref[:, grp]
            dk_ref[:, grp] = dk
            dv_ref[:, grp] = dv
            if with_next:
                dq_ref[:, grp] = dq2[:SPAN] + jnp.where(first, 0.0, carry[:, grp])
                carry[:, grp] = dq2[SPAN:]
            else:
                dq_ref[:, grp] = dq2

    cur = lambda b, n: (b, n, 0)
    nxt = lambda b, n: (b, jnp.minimum(n + 1, nb - 1), 0)
    blk, hblk = (None, SPAN, cw), (None, SPAN, N_HEADS)
    q_specs = lambda m: [pl.BlockSpec(blk, lambda b, n: (*m(b, n)[:2], q_col)), pl.BlockSpec(blk, m),
                         pl.BlockSpec(hblk, m), pl.BlockSpec(hblk, m)]
    in_specs = q_specs(cur) + (q_specs(nxt) if with_next else []) + [pl.BlockSpec(blk, cur)] * (4 if accumulate else 2)
    operands = [q, do, lse, dd] * (2 if with_next else 1) + [k, v] + ([dk_prev, dv_prev] if accumulate else [])
    out = jax.ShapeDtypeStruct((b_n, s_n, cw), F32)
    return pl.pallas_call(
        body, name=name, grid=(b_n, nb), in_specs=in_specs, out_specs=[pl.BlockSpec(blk, cur)] * 3,
        out_shape=[out, out, out],
        scratch_shapes=([pltpu.VMEM((SPAN, cw), F32), pltpu.VMEM((rows, cw), MXU), pltpu.VMEM((rows, cw), MXU),
                         pltpu.VMEM((2, rows, N_HEADS), F32)] if with_next else [])
        + [pltpu.VMEM((N_HEADS, rows, SPAN), MXU)] * 2,
        compiler_params=_params(("parallel", "arbitrary")),
    )(*operands)


MIX_ROWS = 256


def _group_weights(ls):
    m = functools.reduce(jnp.maximum, ls)
    es = [jnp.exp(l - m) for l in ls]
    tot = functools.reduce(lambda a, b: a + b, es)
    return [e / tot for e in es]


def _attn_mix(name, outs, lses):
    n, cw = outs[0].shape
    g_n = len(outs)

    def body(*refs):
        o_refs, l_refs, out_ref = refs[:g_n], refs[g_n:2 * g_n], refs[2 * g_n]
        ws = _group_weights([r[...] for r in l_refs])
        for h in range(N_HEADS):
            cols = slice(h * HEAD_DIM, (h + 1) * HEAD_DIM)
            acc = None
            for g in range(g_n):
                t = ws[g][:, h:h + 1] * o_refs[g][:, cols]
                acc = t if acc is None else acc + t
            out_ref[:, cols] = acc.astype(out_ref.dtype)

    tile = pl.BlockSpec((MIX_ROWS, cw), lambda i: (i, 0))
    htile = pl.BlockSpec((MIX_ROWS, N_HEADS), lambda i: (i, 0))
    return pl.pallas_call(
        body, name=name, grid=(n // MIX_ROWS,), in_specs=[tile] * g_n + [htile] * g_n, out_specs=tile,
        out_shape=jax.ShapeDtypeStruct((n, cw), MXU), compiler_params=_params(("parallel",)),
    )(*outs, *lses)


def _attn_mix_bwd(name, do, outs, lses):
    n, cw = do.shape
    g_n = len(outs)

    def body(*refs):
        do_ref, o_refs, l_refs = refs[0], refs[1:1 + g_n], refs[1 + g_n:1 + 2 * g_n]
        dog_refs, dd_refs = refs[1 + 2 * g_n:1 + 3 * g_n], refs[1 + 3 * g_n:]
        ws = _group_weights([r[...] for r in l_refs])
        for h in range(N_HEADS):
            cols = slice(h * HEAD_DIM, (h + 1) * HEAD_DIM)
            dh = do_ref[:, cols]
            o = None
            for g in range(g_n):
                t = ws[g][:, h:h + 1] * o_refs[g][:, cols]
                o = t if o is None else o + t
            dot = jnp.sum(dh * o, -1, keepdims=True)
            for g in range(g_n):
                wg = ws[g][:, h:h + 1]
                dog_refs[g][:, cols] = (wg * dh).astype(dog_refs[g].dtype)
                dd_refs[g][:, h:h + 1] = wg * dot

    tile = pl.BlockSpec((MIX_ROWS, cw), lambda i: (i, 0))
    htile = pl.BlockSpec((MIX_ROWS, N_HEADS), lambda i: (i, 0))
    return pl.pallas_call(
        body, name=name, grid=(n // MIX_ROWS,), in_specs=[tile] * (1 + g_n) + [htile] * g_n,
        out_specs=[tile] * g_n + [htile] * g_n,
        out_shape=[jax.ShapeDtypeStruct((n, cw), MXU)] * g_n + [jax.ShapeDtypeStruct((n, N_HEADS), F32)] * g_n,
        compiler_params=_params(("parallel",)),
    )(do, *outs, *lses)


def _loss_head(name, y, target):
    n, d = y.shape
    steps = n // LN_ROWS

    def body(y_ref, t_ref, dy_ref, l_ref, acc):
        i = pl.program_id(0)

        @pl.when(i == 0)
        def _():
            acc[...] = jnp.zeros_like(acc)

        def chunk(s):
            rows = pl.ds(s, LN_CH)
            err = y_ref[rows, :] - t_ref[rows, :]
            dy_ref[rows, :] = err / d
            acc[...] += _fold8(err * err)

        _chunks(LN_ROWS, LN_CH, chunk, LN_UNROLL)

        @pl.when(i == steps - 1)
        def _():
            l_ref[...] = jnp.full((8, 128), 0.5 / d, F32) * jnp.sum(acc[...])

    tile = pl.BlockSpec((LN_ROWS, d), lambda i: (i, 0))
    return pl.pallas_call(
        body, name=name, grid=(steps,), in_specs=[tile, tile],
        out_specs=[tile, pl.BlockSpec((8, 128), lambda i: (0, 0))],
        out_shape=[jax.ShapeDtypeStruct((n, d), F32), jax.ShapeDtypeStruct((8, 128), F32)],
        scratch_shapes=[pltpu.VMEM((8, d), F32)], compiler_params=_params(("arbitrary",)),
    )(y, target)


EW_TILE_BYTES = 1 << 20


def _row_tile(rows, cols):
    tr = 8
    while rows % (2 * tr) == 0 and 2 * tr * cols * 4 <= EW_TILE_BYTES:
        tr *= 2
    return tr if rows % tr == 0 else rows


def _add_halves(name, grad, recv, half, out_dtype):
    j_n, _, r, c = grad.shape
    tr = _row_tile(r, c)

    def body(half_ref, g_ref, r_ref, o_ref):
        o_ref[...] = (g_ref[...] + r_ref[...]).astype(o_ref.dtype)

    return pl.pallas_call(
        body, name=name, out_shape=jax.ShapeDtypeStruct((j_n, r, c), out_dtype),
        grid_spec=pltpu.PrefetchScalarGridSpec(
            num_scalar_prefetch=1, grid=(j_n, r // tr),
            in_specs=[pl.BlockSpec((None, None, tr, c), lambda j, i, hf: (j, hf[0], i, 0)),
                      pl.BlockSpec((None, tr, c), lambda j, i, hf: (j, i, 0))],
            out_specs=pl.BlockSpec((None, tr, c), lambda j, i, hf: (j, i, 0))),
        compiler_params=_params(("parallel", "parallel")),
    )(half, grad, recv)


def _add_chips(name, mine, recv, chip):
    j_n, r, c = mine.shape
    tr = _row_tile(r, c)

    def body(chip_ref, m_ref, r_ref, o_ref):
        total = m_ref[...].astype(F32)
        for k in range(j_n - 1):
            total = total + r_ref[k].astype(F32)
        o_ref[...] = total

    return pl.pallas_call(
        body, name=name, out_shape=jax.ShapeDtypeStruct((r, c), F32),
        grid_spec=pltpu.PrefetchScalarGridSpec(
            num_scalar_prefetch=1, grid=(r // tr,),
            in_specs=[pl.BlockSpec((None, tr, c), lambda i, ch: (ch[0], i, 0)),
                      pl.BlockSpec((j_n - 1, tr, c), lambda i, ch: (0, i, 0))],
            out_specs=pl.BlockSpec((tr, c), lambda i, ch: (i, 0))),
        compiler_params=_params(("parallel",)),
    )(chip, mine, recv)


def _adam_math(w, g, m, v):
    m = ADAM_B1 * m + (1.0 - ADAM_B1) * g
    v = ADAM_B2 * v + (1.0 - ADAM_B2) * (g * g)
    m_hat = m / (1.0 - ADAM_B1 ** ADAM_STEP)
    v_hat = v / (1.0 - ADAM_B2 ** ADAM_STEP)
    delta = -ADAM_LR * (m_hat / (jnp.sqrt(v_hat) + ADAM_EPS) + ADAM_WD * w)
    return delta, m, v


def _adam_halves(name, own, other, half, w, m, v, part, into):
    _, _, r, c = w.shape
    tr = _row_tile(r, c)

    def body(half_ref, own_ref, oth_ref, w_ref, m_ref, v_ref, *rest):
        g_out, d_out, m_out, v_out = rest[-4:]
        g = jnp.where(pl.program_id(0) == half_ref[0], own_ref[...], oth_ref[...])
        delta, m_new, v_new = _adam_math(w_ref[...], g, m_ref[...], v_ref[...])
        g_out[...] = g
        d_out[...] = delta
        m_out[...] = m_new
        v_out[...] = v_new

    flat = pl.BlockSpec((tr, c), lambda h, i, hf: (i, 0))
    full = pl.BlockSpec((None, None, tr, c), lambda h, i, hf: (part, h, i, 0))
    out = jax.ShapeDtypeStruct(w.shape, F32)
    kept = [] if into is None else list(into)
    return pl.pallas_call(
        body, name=name, out_shape=[out] * 4,
        grid_spec=pltpu.PrefetchScalarGridSpec(
            num_scalar_prefetch=1, grid=(2, r // tr),
            in_specs=[flat, flat, full, full, full] + [pl.BlockSpec(memory_space=pl.ANY)] * len(kept), out_specs=[full] * 4),
        input_output_aliases={6 + i: i for i in range(len(kept))},
        compiler_params=_params(("parallel", "parallel")),
    )(half, own, other, w, m, v, *kept)


def _adam_small(name, g, w, m, v):
    def body(g_ref, w_ref, m_ref, v_ref, d_out, m_out, v_out):
        delta, m_new, v_new = _adam_math(w_ref[...], g_ref[...], m_ref[...], v_ref[...])
        d_out[...] = delta
        m_out[...] = m_new
        v_out[...] = v_new

    out = jax.ShapeDtypeStruct(w.shape, F32)
    return pl.pallas_call(body, name=name, out_shape=[out] * 3)(g, w, m, v)


def _place():
    x, y, c = lax.axis_index("x"), lax.axis_index("y"), lax.axis_index("c")
    chips = [(1 - x, y), (x, 1 - y), (1 - x, 1 - y)]
    return x, y, c, chips


ANY = pl.BlockSpec(memory_space=pl.ANY)


class _ShardGather:
    def __init__(self, ins, outs, send, recv, local):
        self.ins, self.outs, self.send, self.recv, self.local = ins, outs, send, recv, local
        self.n = len(ins)

    @staticmethod
    def scratch(n):
        return [pltpu.SemaphoreType.DMA((n, 6)), pltpu.SemaphoreType.DMA((n, 6)), pltpu.SemaphoreType.DMA((n,))]

    @staticmethod
    def out_shapes(shards):
        return [jax.ShapeDtypeStruct((N_CHIPS, *s.shape), s.dtype) for s in shards]

    def _copy(self, t, k, src, dst, to):
        return pltpu.make_async_remote_copy(src_ref=src, dst_ref=dst, send_sem=self.send.at[t, k], recv_sem=self.recv.at[t, k],
                                            device_id=to, device_id_type=MESH)

    def _own(self, t, me):
        return pltpu.make_async_copy(self.ins[t], self.outs[t].at[me], self.local.at[t])

    def _first(self, t, k, place):
        x, y, c, chips = place
        px, py = chips[k]
        return self._copy(t, k, self.ins[t].at[c], self.outs[t].at[2 * x + y, c], (px, py, c))

    def _passed_on(self, t, k, place, half):
        x, y, c, chips = place
        px, py = chips[k]
        slab = self.outs[t].at[2 * px + py, half]
        return self._copy(t, 3 + k, slab, slab, (x, y, 1 - c))

    def begin(self):
        place = _place()
        x, y, c, _ = place
        for t in range(self.n):
            self._own(t, 2 * x + y).start()
        for t in range(self.n):
            for k in range(N_CHIPS - 1):
                self._first(t, k, place).start()

    def end(self):
        place = _place()
        x, y, c, chips = place
        for t in range(self.n):
            for k, (px, py) in enumerate(chips):
                self._copy(t, k, self.ins[t].at[c], self.outs[t].at[2 * px + py, c], (px, py, c)).wait_recv()
                self._passed_on(t, k, place, c).start()
        for t in range(self.n):
            for k in range(N_CHIPS - 1):
                self._passed_on(t, k, place, 1 - c).wait_recv()
        for t in range(self.n):
            for k in range(N_CHIPS - 1):
                self._first(t, k, place).wait_send()
                self._passed_on(t, k, place, c).wait_send()
            self._own(t, 2 * x + y).wait()


class _Exchange:
    def __init__(self, make_copies, ins, outs, send, recv):
        self.copies = lambda: make_copies(ins, outs, send, recv)

    def begin(self):
        for cp in self.copies():
            cp.start()

    def end(self):
        for cp in self.copies():
            cp.wait()


def _gather_job(shards):
    return dict(ins=list(shards), outs=_ShardGather.out_shapes(shards), scratch=_ShardGather.scratch(len(shards)),
                bind=_ShardGather)


def _sibling_job(arrays, pick_other_half):
    n = len(arrays)

    def copies(ins, outs, send, recv):
        x, y, c, _ = _place()
        return [pltpu.make_async_remote_copy(
            src_ref=ins[t].at[:, 1 - c] if pick_other_half else ins[t], dst_ref=outs[t], send_sem=send.at[t],
            recv_sem=recv.at[t], device_id=(x, y, 1 - c), device_id_type=MESH) for t in range(n)]

    shapes = [(a.shape[0], *a.shape[2:]) if pick_other_half else a.shape for a in arrays]
    return dict(ins=list(arrays), outs=[jax.ShapeDtypeStruct(s, a.dtype) for s, a in zip(shapes, arrays)],
                scratch=[pltpu.SemaphoreType.DMA((n,)), pltpu.SemaphoreType.DMA((n,))],
                bind=functools.partial(_Exchange, copies))


def _owner_job(arrays):
    n = len(arrays)

    def copies(ins, outs, send, recv):
        x, y, c, chips = _place()
        return [pltpu.make_async_remote_copy(
            src_ref=ins[t].at[2 * px + py], dst_ref=outs[t].at[k], send_sem=send.at[t, k], recv_sem=recv.at[t, k],
            device_id=(px, py, c), device_id_type=MESH) for t in range(n) for k, (px, py) in enumerate(chips)]

    return dict(ins=list(arrays), outs=[jax.ShapeDtypeStruct((N_CHIPS - 1, *a.shape[1:]), a.dtype) for a in arrays],
                scratch=[pltpu.SemaphoreType.DMA((n, 3)), pltpu.SemaphoreType.DMA((n, 3))],
                bind=functools.partial(_Exchange, copies))


def _bound(job, refs):
    n_i, n_o = len(job["ins"]), len(job["outs"])
    return job["bind"](refs[:n_i], refs[n_i:n_i + n_o], *refs[n_i + n_o:])


def _run_job(name, job):
    def body(*refs):
        bound = _bound(job, refs)
        bound.begin()
        bound.end()

    return pl.pallas_call(
        body, name=name, in_specs=[ANY] * len(job["ins"]), out_specs=[ANY] * len(job["outs"]), out_shape=job["outs"],
        scratch_shapes=job["scratch"],
    )(*job["ins"])


def _sum_all_devices(name, part):
    r, c = part.shape

    def body(p_ref, o_ref, buf, send, recv):
        x, y, cc, _ = _place()
        me = 4 * x + 2 * y + cc
        copies = []
        for mask in range(1, 8):
            fx, fy, fc = (mask >> 2) & 1, (mask >> 1) & 1, mask & 1
            to = (x ^ fx, y ^ fy, cc ^ fc)
            copies.append((mask, pltpu.make_async_remote_copy(
                src_ref=p_ref, dst_ref=buf.at[me], send_sem=send.at[mask - 1], recv_sem=recv.at[mask - 1],
                device_id=to, device_id_type=MESH)))
            copies[-1][1].start()
        buf[me] = p_ref[...]
        for mask, cp in copies:
            pltpu.make_async_remote_copy(src_ref=p_ref, dst_ref=buf.at[me ^ mask], send_sem=send.at[mask - 1],
                                         recv_sem=recv.at[mask - 1], device_id=(x, y, cc), device_id_type=MESH).wait_recv()
        for _, cp in copies:
            cp.wait_send()
        total = buf[0]
        for d in range(1, 8):
            total = total + buf[d]
        o_ref[...] = total

    vm = pl.BlockSpec(memory_space=pltpu.VMEM)
    return pl.pallas_call(
        body, name=name, in_specs=[vm], out_specs=vm, out_shape=jax.ShapeDtypeStruct((r, c), F32),
        scratch_shapes=[pltpu.VMEM((8, r, c), F32), pltpu.SemaphoreType.DMA((7,)), pltpu.SemaphoreType.DMA((7,))],
    )(part)


def kernel(x, pool_w, pool_scale, w_q, w_kv, w_o, ffn_w_gate, ffn_w_up, ffn_conv_w, ffn_conv_b, ffn_w_down, ln1_g, ln1_b, ln2_g, ln2_b, loss_target, m_pool_w, m_pool_scale, m_w_q, m_w_kv, m_w_o, m_ffn_w_gate, m_ffn_w_up, m_ffn_conv_w, m_ffn_conv_b, m_ffn_w_down, m_ln1_g, m_ln1_b, m_ln2_g, m_ln2_b, v_pool_w, v_pool_scale, v_w_q, v_w_kv, v_w_o, v_ffn_w_gate, v_ffn_w_up, v_ffn_conv_w, v_ffn_conv_b, v_ffn_w_down, v_ln1_g, v_ln1_b, v_ln2_g, v_ln2_b):
    b_n, s_n, d = x.shape
    n = b_n * s_n
    f = ffn_w_gate.shape[-1]
    qc = w_q.shape[-1]
    kvb = w_kv.shape[-1] // 2
    n_attn = w_q.shape[0]
    g_n = len(DILATIONS)
    cw = N_HEADS * HEAD_DIM
    xi, yi, ci = lax.axis_index("x"), lax.axis_index("y"), lax.axis_index("c")
    half = jnp.reshape(ci, (1,)).astype(jnp.int32)
    chip = jnp.reshape(2 * xi + yi, (1,)).astype(jnp.int32)

    sharded = {
        "pool_w": (pool_w, m_pool_w, v_pool_w, (2, 4 * 64, POOL_GROUP_DIM)),
        "pool_scale": (pool_scale, m_pool_scale, v_pool_scale, (2, 1, pool_scale.shape[-1])),
        "w_q": (w_q, m_w_q, v_w_q, (2, d, qc)),
        "w_kv": (w_kv, m_w_kv, v_w_kv, (2, d // 2, w_kv.shape[-1])),
        "w_o": (w_o, m_w_o, v_w_o, (2, w_o.shape[1], d)),
        "ffn_w_gate": (ffn_w_gate, m_ffn_w_gate, v_ffn_w_gate, (2, 2 * d, f)),
        "ffn_w_up": (ffn_w_up, m_ffn_w_up, v_ffn_w_up, (2, 2 * d, f)),
        "ffn_conv_w": (ffn_conv_w, m_ffn_conv_w, v_ffn_conv_w, (2, 6, f)),
        "ffn_w_down": (ffn_w_down, m_ffn_w_down, v_ffn_w_down, (2, 2 * f, d)),
    }
    mxu_weights = ("pool_w", "w_q", "w_kv", "w_o", "ffn_w_gate", "ffn_w_up", "ffn_w_down")
    names = list(sharded)
    wo_rows = w_o.shape[1]
    shard_of = {("wkv", 0): w_kv.astype(MXU).reshape(2, d // 2, w_kv.shape[-1])}
    for i in range(DEPTH):
        shard_of["wg", i] = ffn_w_gate[i].astype(MXU).reshape(2, d // 2, f)
        shard_of["wu", i] = ffn_w_up[i].astype(MXU).reshape(2, d // 2, f)
        shard_of["wd", i] = ffn_w_down[i].astype(MXU).reshape(2, f // 2, d)
    for i in range(n_attn):
        shard_of["wq", i] = w_q[i].astype(MXU).reshape(2, d // 2, qc)
        shard_of["wo", i] = w_o[i].astype(MXU).reshape(2, wo_rows // 2, d)
    carried_by = {
        (0, "gate"): [("wg", 1)], (0, "up"): [("wu", 1)], (0, "down"): [("wd", 1)],
        (1, "gate"): [("wkv", 0)], (1, "up"): [("wq", 0), ("wo", 0)], (1, "down"): [("wg", 2)],
        (2, "q"): [("wu", 2)], (2, "o"): [("wd", 2)], (2, "gate"): [("wq", 1), ("wo", 1)], (2, "up"): [("wg", 3)],
        (2, "down"): [("wu", 3)], (3, "q"): [("wd", 3)],
    }
    got = {}

    def carrying(site, call, *args):
        keys = carried_by.get(site, [])
        if not keys:
            return call(*args)
        out, arrived = call(*args, job=_gather_job([shard_of[k] for k in keys]))
        got.update(zip(keys, arrived))
        return out

    first_keys = [("wg", 0), ("wu", 0), ("wd", 0)]
    first = _run_job("gather_weights", _gather_job([
        pool_w.astype(MXU).reshape(sharded["pool_w"][3]), pool_scale.reshape(sharded["pool_scale"][3]),
        ffn_conv_w.reshape(sharded["ffn_conv_w"][3])] + [shard_of[k] for k in first_keys]))
    got.update(zip(first_keys, first[3:]))
    wg_at = lambda i: got["wg", i].reshape(N_CHIPS, d, f)
    wu_at = lambda i: got["wu", i].reshape(N_CHIPS, d, f)
    wd_at = lambda i: got["wd", i].reshape(N_CHIPS, f, d)
    wq_at = lambda i: got["wq", i].reshape(N_CHIPS, d, qc)
    wo_at = lambda i: got["wo", i].reshape(cw, d)
    wkv_at = lambda: got["wkv", 0].reshape(N_CHIPS, d, w_kv.shape[-1])
    cw_all = first[2].reshape(N_CHIPS, DEPTH, 3, f)
    cw_l = [cw_all[:, i] for i in range(DEPTH)]
    cb_l = [ffn_conv_b[i].reshape(N_CHIPS, 1, f) for i in range(DEPTH)]
    pw_nat = first[0].reshape(N_CHIPS, N_POOL_LAYERS, 4, 64, POOL_GROUP_DIM).transpose(1, 2, 0, 3, 4).reshape(
        N_POOL_LAYERS, 4, POOL_GROUP_DIM, POOL_GROUP_DIM)
    ps_nat = first[1].reshape(N_CHIPS, N_POOL_LAYERS, -1).transpose(1, 0, 2).reshape(N_POOL_LAYERS, 1, d)
    cos_l, sin_l = _rope_tables(s_n)

    def vec(a, layer):
        return a[layer].reshape(1, d)

    h = x.reshape(n, d)
    hb = None
    saved = []
    k_str = v_str = None
    for layer in range(DEPTH):
        keep = {"h": h, "hb": hb}
        if layer < N_POOL_LAYERS:
            mix = _pool_fwd("pool_fwd", h.reshape(b_n, s_n, d), pw_nat[layer], ps_nat[layer]).reshape(n, d)
        else:
            a = layer - N_POOL_LAYERS
            q = carrying((layer, "q"), functools.partial(_mm_cols, "q_proj", hb, wq_at(a), (), F32))
            qr = _rope("q_rope", q.reshape(b_n, s_n, -1), 0, g_n, cos_l, sin_l, HEAD_DIM ** -0.5, MXU)
            q_str, lse_str, outs, lses = [], [], [], []
            for g, dil in enumerate(DILATIONS):
                if dil == 1:
                    qg, col = qr, g
                else:
                    qg, col = _to_strided(qr[:, :, g * cw:(g + 1) * cw], dil), 0
                o_g, lse_g = _attn_fwd(f"attn_fwd_d{dil}", qg, col, k_str[g], v_str[g], s_n // dil // SPAN)
                q_str.append((qg, col))
                lse_str.append(lse_g)
                outs.append(_from_strided(o_g, dil).reshape(n, cw))
                lses.append(_from_strided(lse_g, dil).reshape(n, N_HEADS))
            ob = _attn_mix("attn_mix", outs, lses)
            mix = carrying((layer, "o"), functools.partial(
                _mm, "o_proj", [(ob, (TM, cw), lambda m, _: (m, 0), wo_at(a), (cw, d), lambda m, _: (0, 0))],
                NN, (n // TM, 1), (n, d), F32, (TM, d), lambda m, _: (m, 0), 1))
            keep.update(q_str=q_str, lse_str=lse_str, outs=outs, lses=lses, ob=ob)
        r1, h1, h1b = _add_ln("ln_fwd", h, mix, vec(ln1_g, layer), vec(ln1_b, layer))
        gate = carrying((layer, "gate"), functools.partial(_mm_shard_out, "gate_up_proj", h1b, wg_at(layer), (), F32))
        up = carrying((layer, "up"), functools.partial(_mm_shard_out, "gate_up_proj", h1b, wu_at(layer), (), F32))
        g4, u4 = gate.reshape(N_CHIPS, b_n, s_n, f), up.reshape(N_CHIPS, b_n, s_n, f)
        hmid = _glu_fwd("glu_fwd", g4, u4, cw_l[layer], cb_l[layer]).reshape(N_CHIPS, n, f)
        ffn = carrying((layer, "down"), functools.partial(_mm_shard_in, "down_proj", hmid, wd_at(layer), ()))
        r2, h2, h2b = _add_ln("ln_fwd", h1, ffn, vec(ln2_g, layer), vec(ln2_b, layer))
        keep.update(r1=r1, h1b=h1b, g4=g4, u4=u4, hmid=hmid, r2=r2, h2b=h2b)
        saved.append(keep)
        if layer == N_POOL_LAYERS - 1:
            kv = _mm_cols("kv_proj", h2b, wkv_at(), (), F32, cb=kvb).reshape(b_n, s_n, -1)
            k_nat = _rope("k_rope", kv, 0, g_n, cos_l, sin_l, 1.0, MXU)
            v_nat = _rope("v_cast", kv, g_n, g_n, cos_l, sin_l, 1.0, MXU, rotate=False)
            k_str = [_to_strided(k_nat[:, :, g * cw:(g + 1) * cw], dil) for g, dil in enumerate(DILATIONS)]
            v_str = [_to_strided(v_nat[:, :, g * cw:(g + 1) * cw], dil) for g, dil in enumerate(DILATIONS)]
        h, hb = h2, h2b

    dy, loss_tile = _loss_head("loss_head", h, loss_target.reshape(n, d))

    pending = {}
    reduced = {}
    halves_of = lambda t: t.reshape(N_CHIPS, 2, t.shape[1] // 2, t.shape[2])
    on_ici = lambda key: F32 if key[0] in ("pool_w", "pool_scale", "ffn_conv_w") else MXU
    d_cw = [None] * DEPTH
    d_pw, d_ps = [None] * N_POOL_LAYERS, [None] * N_POOL_LAYERS
    d_ln = {}
    dk_str = dv_str = None
    d_top, top_scale, top_rest = dy, 1.0, []
    for layer in reversed(range(DEPTH)):
        sv = saved[layer]
        dr2, dr2b, d_ln["ln2", layer] = _ln_bwd("ln_bwd", sv["r2"], vec(ln2_g, layer), d_top, top_scale, top_rest)
        keys = list(pending)
        job = _sibling_job([pending[k] for k in keys], True) if keys else None
        dhmid = _mm_nt_shard_out("down_bwd", dr2b, wd_at(layer), (), job=job)
        if keys:
            dhmid, from_sibling = dhmid
            core_sums = [_add_halves("add_halves", pending[k], r_, half, on_ici(k)) for k, r_ in zip(keys, from_sibling)]
        d_wd = _mm_tn("down_dw", sv["hmid"], f, dr2b, d, N_CHIPS, (N_CHIPS, f, d), (None, f, d),
                      lambda j: (j, 0, 0), a_lead=lambda j: (j,))
        dg4, du4, d_cw[layer] = _glu_bwd("glu_bwd", dhmid.reshape(N_CHIPS, b_n, s_n, f), sv["g4"], sv["u4"],
                                         cw_l[layer], cb_l[layer])
        dg, du = dg4.reshape(N_CHIPS, n, f), du4.reshape(N_CHIPS, n, f)
        dh1 = _mm_nt_shard_in("gate_up_bwd", [(dg, wg_at(layer), ()), (du, wu_at(layer), ())],
                              job=_owner_job(core_sums) if keys else None)
        if keys:
            dh1, from_chips = dh1
            owned = [_add_chips("add_chips", s_, r_, chip) for s_, r_ in zip(core_sums, from_chips)]
        d_wg = _mm_tn("gate_up_dw", sv["h1b"], d, dg, f, N_CHIPS, (N_CHIPS, d, f), (None, d, f),
                      lambda j: (j, 0, 0), b_lead=lambda j: (j,), job=_sibling_job(owned, False) if keys else None)
        if keys:
            d_wg, others = d_wg
            reduced.update({k: pair for k, pair in zip(keys, zip(owned, others))})
        d_wu = _mm_tn("gate_up_dw", sv["h1b"], d, du, f, N_CHIPS, (N_CHIPS, d, f), (None, d, f),
                      lambda j: (j, 0, 0), b_lead=lambda j: (j,))
        pending = {("ffn_w_down", layer): halves_of(d_wd), ("ffn_w_gate", layer): halves_of(d_wg),
                   ("ffn_w_up", layer): halves_of(d_wu)}
        dr1, dr1b, d_ln["ln1", layer] = _ln_bwd("ln_bwd", sv["r1"], vec(ln1_g, layer), dr2, ALPHA, [dh1])
        if layer < N_POOL_LAYERS:
            d_in, d_pw[layer], d_ps[layer] = _pool_bwd("pool_bwd", sv["h"].reshape(b_n, s_n, d),
                                                       dr1.reshape(b_n, s_n, d), pw_nat[layer], ps_nat[layer])
            d_top, top_scale, top_rest = d_in.reshape(n, d), 1.0, []
        else:
            a = layer - N_POOL_LAYERS
            do = _mm("o_bwd", [(dr1b, (TM, d), lambda m, _: (m, 0), wo_at(a), (cw, d), lambda m, _: (0, 0))],
                     NT, (n // TM, 1), (n, cw), F32, (TM, cw), lambda m, _: (m, 0), 1)
            d_wo = _mm_tn("o_dw", sv["ob"], cw // N_CHIPS, dr1b, d, N_CHIPS, (N_CHIPS, cw // N_CHIPS, d),
                          (None, cw // N_CHIPS, d), lambda j: (j, 0, 0))
            pending["w_o", a] = halves_of(d_wo)
            mixed = _attn_mix_bwd("attn_mix_bwd", do, sv["outs"], sv["lses"])
            dq_nat, dk_new, dv_new = [], [], []
            for g, dil in enumerate(DILATIONS):
                do_g = _to_strided(mixed[g].reshape(b_n, s_n, cw), dil)
                dd_g = _to_strided(mixed[g_n + g].reshape(b_n, s_n, N_HEADS), dil)
                qg, col = sv["q_str"][g]
                dq_g, dk_g, dv_g = _attn_bwd(f"attn_bwd_d{dil}", qg, col, k_str[g], v_str[g], do_g, sv["lse_str"][g], dd_g,
                                             s_n // dil // SPAN, *((dk_str[g], dv_str[g]) if dk_str else ()))
                dq_nat.append(_from_strided(dq_g, dil))
                dk_new.append(dk_g)
                dv_new.append(dv_g)
            dk_str, dv_str = dk_new, dv_new
            dq = None
            for g in range(g_n):
                dq = _rope("q_rope_bwd", dq_nat[g], 0, 1, cos_l, -sin_l, HEAD_DIM ** -0.5, MXU, out_cols=g_n, out_col0=g, into=dq)
            dq = dq.reshape(n, g_n * cw)
            d_attn = _mm_nt_cols_in("q_bwd", dq, wq_at(a), (), qc)
            d_wq = _mm_tn("q_dw", sv["hb"], d, dq, qc, N_CHIPS, (N_CHIPS, d, qc), (None, d, qc), lambda j: (j, 0, 0))
            pending["w_q", a] = halves_of(d_wq)
            d_top, top_scale, top_rest = dr1, ALPHA, [d_attn]
            if a == 0:
                dkv = None
                for g, dil in enumerate(DILATIONS):
                    dkv = _rope("k_rope_bwd", _from_strided(dk_str[g], dil), 0, 1, cos_l, -sin_l, 1.0, MXU,
                                out_cols=2 * g_n, out_col0=g, into=dkv)
                for g, dil in enumerate(DILATIONS):
                    dkv = _rope("v_cast_bwd", _from_strided(dv_str[g], dil), 0, 1, cos_l, sin_l, 1.0, MXU, rotate=False,
                                out_cols=2 * g_n, out_col0=g_n + g, into=dkv)
                dkv = dkv.reshape(n, 2 * g_n * cw)
                h_kv = saved[N_POOL_LAYERS - 1]["h2b"]
                top_rest = top_rest + [_mm_nt_cols_in("kv_bwd", dkv, wkv_at(), (), kvb)]
                d_wkv = _mm_tn("kv_dw", h_kv, d, dkv, kvb, 2 * N_CHIPS, (2 * N_CHIPS, d, kvb), (None, d, kvb), lambda q: (q, 0, 0))
                pending["w_kv", 0] = halves_of(d_wkv.reshape(N_CHIPS, 2, d, kvb).transpose(0, 2, 1, 3).reshape(N_CHIPS, d, 2 * kvb))
    grad_x = d_top.reshape(b_n, s_n, d)

    d_pw_all = jnp.stack(d_pw).reshape(N_POOL_LAYERS, 4, N_CHIPS, 64, POOL_GROUP_DIM).transpose(2, 0, 1, 3, 4)
    d_ps_all = jnp.stack(d_ps).reshape(N_POOL_LAYERS, N_CHIPS, -1).transpose(1, 0, 2)
    d_cw_all = jnp.stack([t[:, :3] for t in d_cw], axis=1)
    for k, t in (("pool_w", d_pw_all), ("pool_scale", d_ps_all), ("ffn_conv_w", d_cw_all)):
        pending[k, 0] = t.reshape(N_CHIPS, *sharded[k][3])
    keys = list(pending)
    from_sibling = _run_job("grads_to_sibling", _sibling_job([pending[k] for k in keys], True))
    core_sums = [_add_halves("add_halves", pending[k], r_, half, on_ici(k)) for k, r_ in zip(keys, from_sibling)]
    from_chips = _run_job("grads_to_owner", _owner_job(core_sums))
    owned = [_add_chips("add_chips", s_, r_, chip) for s_, r_ in zip(core_sums, from_chips)]
    others = _run_job("halves_to_sibling", _sibling_job(owned, False))
    reduced.update({k: pair for k, pair in zip(keys, zip(owned, others))})

    out_grad, out_delta, out_m, out_v = {}, {}, {}, {}
    for k in names:
        w_, m_, v_, _ = sharded[k]
        parts = sorted(i for name_, i in reduced if name_ == k)
        shape = (len(parts), 2, *reduced[k, 0][0].shape)
        res = None
        for i in parts:
            res = _adam_halves(f"adam_{k}", *reduced[k, i], half, w_.reshape(shape), m_.reshape(shape), v_.reshape(shape), i, res)
        out_grad[k], out_delta[k], out_m[k], out_v[k] = (t.reshape(w_.shape) for t in res)

    d_cb = jnp.stack([t[:, 3] for t in d_cw], axis=0).reshape(DEPTH * N_CHIPS * f // d, d)
    ln_rows = jnp.concatenate([jnp.stack([d_ln[which, layer][row] for layer in range(DEPTH)])
                               for which, row in (("ln1", 0), ("ln1", 1), ("ln2", 0), ("ln2", 1))])
    rows = jnp.concatenate([ln_rows, d_cb, jnp.broadcast_to(loss_tile[0:1, 0:1], (1, d))])
    pad = (-rows.shape[0]) % 8
    total = _sum_all_devices("sum_small", jnp.pad(rows, ((0, pad), (0, 0))))
    small = {"ln1_g": total[0:4], "ln1_b": total[4:8], "ln2_g": total[8:12], "ln2_b": total[12:16],
             "ffn_conv_b": total[16:16 + d_cb.shape[0]].reshape(ffn_conv_b.shape)}
    loss = total[16 + d_cb.shape[0], 0]
    small_in = {"ln1_g": (ln1_g, m_ln1_g, v_ln1_g), "ln1_b": (ln1_b, m_ln1_b, v_ln1_b), "ln2_g": (ln2_g, m_ln2_g, v_ln2_g),
                "ln2_b": (ln2_b, m_ln2_b, v_ln2_b), "ffn_conv_b": (ffn_conv_b, m_ffn_conv_b, v_ffn_conv_b)}
    for k, (w_, m_, v_) in small_in.items():
        out_grad[k] = small[k]
        out_delta[k], out_m[k], out_v[k] = _adam_small(f"adam_{k}", small[k], w_, m_, v_)

    order = ["pool_w", "pool_scale", "w_q", "w_kv", "w_o", "ffn_w_gate", "ffn_w_up", "ffn_conv_w", "ffn_conv_b",
             "ffn_w_down", "ln1_g", "ln1_b", "ln2_g", "ln2_b"]
    return (loss, grad_x, *[out_grad[k] for k in order], *[out_delta[k] for k in order],
            *[out_m[k] for k in order], *[out_v[k] for k in order])
```

```python
import functools
import math

import jax
import jax.numpy as jnp
from jax import lax
from jax.experimental import pallas as pl
from jax.experimental.pallas import tpu as pltpu

F32 = jnp.float32
BF16 = jnp.bfloat16
MXU = jnp.bfloat16

DEPTH = 4
N_POOL_LAYERS = 2
POOL_WINDOWS = (2, 4, 8, 16)
POOL_GROUP_DIM = 256
HEAD_DIM = 64
N_HEADS = 16
DILATIONS = (1, 4, 16)
SPAN = 128
ROPE_THETA = 10000.0
ALPHA = (2.0 * DEPTH) ** 0.25
LN_EPS = 1e-5
ADAM_LR, ADAM_B1, ADAM_B2, ADAM_EPS, ADAM_WD, ADAM_STEP = 0.001, 0.9, 0.999, 1e-08, 0.01, 10

N_CHIPS = 4
VMEM_LIMIT = 56 * 1024 * 1024
MESH = pl.DeviceIdType.MESH

NN = (((1,), (0,)), ((), ()))
NT = (((1,), (1,)), ((), ()))
TN = (((0,), (0,)), ((), ()))


def _params(sem=None):
    return pltpu.CompilerParams(dimension_semantics=sem, vmem_limit_bytes=VMEM_LIMIT)


def _chunks(n_rows, ch, fn, unroll=1):
    def step(i, carry):
        fn(pl.multiple_of(i * ch, ch))
        return carry

    lax.fori_loop(0, n_rows // ch, step, 0, unroll=unroll)


def _fold8(v):
    return jnp.sum(v.reshape(v.shape[0] // 8, 8, v.shape[1]), axis=0)


def _down(v, k):
    return pltpu.roll(v, k, 0)


def _up(v, k):
    return pltpu.roll(v, v.shape[0] - k, 0)


def _own_refs(job, refs, n_in, n_out, n_scratch):
    if job is None:
        return list(refs), None
    a = n_in
    b = a + len(job["ins"])
    c = b + n_out
    e = c + len(job["outs"])
    g = e + n_scratch
    return list(refs[:a]) + list(refs[b:c]) + list(refs[e:g]), job["bind"](refs[a:b], refs[c:e], *refs[g:])


def _grid_ends(grid):
    ids = [pl.program_id(i) for i in range(len(grid))]
    both = lambda conds: functools.reduce(lambda p, q: p & q, conds)
    return both([i == 0 for i in ids]), both([i == g - 1 for i, g in zip(ids, grid)])


def _mm(name, pairs, dims, grid, out_shape, out_dtype, out_block, out_map, nk, into=None, job=None):
    n_pairs = len(pairs)
    kax = len(grid) - 1
    n_in = 2 * n_pairs + (1 if into is not None else 0)

    def body(*refs):
        refs, riding = _own_refs(job, refs, n_in, 1, 0)
        o_ref = refs[n_in]
        if riding is not None:
            at_first, at_last = _grid_ends(grid)
            pl.when(at_first)(riding.begin)
        part = None
        for p in range(n_pairs):
            t = lax.dot_general(refs[2 * p][...], refs[2 * p + 1][...], dims, preferred_element_type=F32)
            part = t if part is None else part + t
        if nk == 1:
            o_ref[...] = part.astype(o_ref.dtype)
        else:
            k = pl.program_id(kax)

            @pl.when(k == 0)
            def _():
                o_ref[...] = part

            @pl.when(k > 0)
            def _():
                o_ref[...] += part

        if riding is not None:
            pl.when(at_last)(riding.end)

    operands, in_specs = [], []
    for a, a_block, a_map, b, b_block, b_map in pairs:
        operands += [a, b]
        in_specs += [pl.BlockSpec(a_block, a_map), pl.BlockSpec(b_block, b_map)]
    aliases = {}
    if into is not None:
        operands.append(into)
        in_specs.append(pl.BlockSpec(memory_space=pl.ANY))
        aliases = {2 * n_pairs: 0}
    assert nk == 1 or out_dtype == F32
    j_ins, j_outs, j_scratch = (job["ins"], job["outs"], job["scratch"]) if job else ([], [], [])
    sem = ("arbitrary",) * len(grid) if job else ("parallel",) * kax + ("arbitrary",)
    res = pl.pallas_call(
        body, name=name, grid=grid, in_specs=in_specs + [pl.BlockSpec(memory_space=pl.ANY)] * len(j_ins),
        out_specs=[pl.BlockSpec(out_block, out_map)] + [pl.BlockSpec(memory_space=pl.ANY)] * len(j_outs),
        out_shape=[jax.ShapeDtypeStruct(out_shape, out_dtype)] + list(j_outs),
        input_output_aliases=aliases, scratch_shapes=list(j_scratch), compiler_params=_params(sem),
    )(*operands, *j_ins)
    return (res[0], res[1:]) if job else res[0]


TM = 2048


def _mm_cols(name, a, w, w_idx, out_dtype, cb=None, job=None):
    n, k = a.shape
    j_n, c = w.shape[0], w.shape[-1]
    cb = c if cb is None else cb
    s = c // cb
    wb = (None,) * (w.ndim - 2) + (k, cb)
    return _mm(name, [(a, (TM, k), lambda q, m, _: (m, 0), w, wb, lambda q, m, _: (q // s, *w_idx, 0, q % s))], NN,
               (j_n * s, n // TM, 1), (n, j_n * c), out_dtype, (TM, cb), lambda q, m, _: (m, q), 1, job=job)


def _mm_shard_out(name, a, w, w_idx, out_dtype, job=None):
    n, k = a.shape
    j_n, c = w.shape[0], w.shape[-1]
    wb = (None,) * (w.ndim - 2) + (k, c)
    return _mm(name, [(a, (TM, k), lambda j, m, _: (m, 0), w, wb, lambda j, m, _: (j, *w_idx, 0, 0))], NN,
               (j_n, n // TM, 1), (j_n, n, c), out_dtype, (None, TM, c), lambda j, m, _: (j, m, 0), 1, job=job)


def _mm_shard_in(name, a4, w, w_idx, job=None):
    j_n, n, c = a4.shape
    k = w.shape[-1]
    wb = (None,) * (w.ndim - 2) + (c, k)
    return _mm(name, [(a4, (None, TM, c), lambda m, j: (j, m, 0), w, wb, lambda m, j: (j, *w_idx, 0, 0))], NN,
               (n // TM, j_n), (n, k), F32, (TM, k), lambda m, j: (m, 0), j_n, job=job)


def _mm_nt_shard_out(name, a, w, w_idx, job=None):
    n, k = a.shape
    j_n, c = w.shape[0], w.shape[-2]
    wb = (None,) * (w.ndim - 2) + (c, k)
    return _mm(name, [(a, (TM, k), lambda j, m, _: (m, 0), w, wb, lambda j, m, _: (j, *w_idx, 0, 0))], NT,
               (j_n, n // TM, 1), (j_n, n, c), F32, (None, TM, c), lambda j, m, _: (j, m, 0), 1, job=job)


def _mm_nt_shard_in(name, terms, job=None):
    pairs = []
    for a4, w, w_idx in terms:
        j_n, n, c = a4.shape
        k = w.shape[-2]
        wb = (None,) * (w.ndim - 2) + (k, c)
        pairs.append((a4, (None, TM, c), lambda m, j: (j, m, 0), w, wb,
                      functools.partial(lambda m, j, w_idx: (j, *w_idx, 0, 0), w_idx=w_idx)))
    return _mm(name, pairs, NT, (n // TM, j_n), (n, k), F32, (TM, k), lambda m, j: (m, 0), j_n, job=job)


def _mm_nt_cols_in(name, a, w, w_idx, cb):
    n, ct = a.shape
    j_n, k, c = w.shape[0], w.shape[-2], w.shape[-1]
    s = c // cb
    wb = (None,) * (w.ndim - 2) + (k, cb)
    return _mm(name, [(a, (TM, cb), lambda m, q: (m, q), w, wb, lambda m, q: (q // s, *w_idx, 0, q % s))], NT,
               (n // TM, ct // cb), (n, k), F32, (TM, k), lambda m, q: (m, 0), ct // cb)


def _mm_tn(name, a, a_cols, b, b_cols, n_blocks, out_shape, out_block, out_map, into=None, a_lead=None, b_lead=None, job=None):
    n = a.shape[-2]
    a_nb = a.shape[-1] // a_cols
    b_nb = b.shape[-1] // b_cols
    if a_lead is None:
        a_block, a_map = (TM, a_cols), lambda q, t: (t, q if a_nb > 1 else 0)
    else:
        a_block, a_map = (None, TM, a_cols), lambda q, t: (*a_lead(q), t, 0)
    if b_lead is None:
        b_block, b_map = (TM, b_cols), lambda q, t: (t, q if b_nb > 1 else 0)
    else:
        b_block, b_map = (None, TM, b_cols), lambda q, t: (*b_lead(q), t, 0)
    return _mm(name, [(a, a_block, a_map, b, b_block, b_map)], TN, (n_blocks, n // TM), out_shape, F32,
               out_block, lambda q, t: out_map(q), n // TM, into=into, job=job)


LN_ROWS = 512
LN_CH = 16
LN_UNROLL = 4


def _ln_stats(r):
    mu = jnp.mean(r, -1, keepdims=True)
    xc = r - mu
    var = jnp.mean(xc * xc, -1, keepdims=True)
    return xc, lax.rsqrt(var + LN_EPS)


def _add_ln(name, a, mix, g, b):
    n, d = a.shape

    def body(a_ref, m_ref, g_ref, b_ref, r_ref, h_ref, hb_ref):
        gg, bb = g_ref[...], b_ref[...]

        def chunk(s):
            rows = pl.ds(s, LN_CH)
            r = ALPHA * a_ref[rows, :] + m_ref[rows, :]
            xc, rstd = _ln_stats(r)
            y = xc * rstd * gg + bb
            r_ref[rows, :] = r
            h_ref[rows, :] = y
            hb_ref[rows, :] = y.astype(MXU)

        _chunks(LN_ROWS, LN_CH, chunk, LN_UNROLL)

    tile = pl.BlockSpec((LN_ROWS, d), lambda i: (i, 0))
    vec = pl.BlockSpec((1, d), lambda i: (0, 0))
    return pl.pallas_call(
        body, name=name, grid=(n // LN_ROWS,), in_specs=[tile, tile, vec, vec], out_specs=[tile, tile, tile],
        out_shape=[jax.ShapeDtypeStruct((n, d), F32), jax.ShapeDtypeStruct((n, d), F32), jax.ShapeDtypeStruct((n, d), MXU)],
        compiler_params=_params(("parallel",)),
    )(a, mix, g, b)


def _ln_bwd(name, r, g, d_a, scale_a, d_rest):
    n, d = r.shape
    n_rest = len(d_rest)
    steps = n // LN_ROWS

    def body(*refs):
        r_ref, g_ref, da_ref = refs[:3]
        rest = refs[3:3 + n_rest]
        dr_ref, drb_ref, gb_ref, acc = refs[3 + n_rest:]
        i = pl.program_id(0)

        @pl.when(i == 0)
        def _():
            acc[...] = jnp.zeros_like(acc)

        gg = g_ref[...]

        def chunk(s):
            rows = pl.ds(s, LN_CH)
            xc, rstd = _ln_stats(r_ref[rows, :])
            xhat = xc * rstd
            dy = da_ref[rows, :] if scale_a == 1.0 else scale_a * da_ref[rows, :]
            for t in rest:
                dy = dy + t[rows, :]
            dyg = dy * gg
            m1 = jnp.mean(dyg, -1, keepdims=True)
            m2 = jnp.mean(dyg * xhat, -1, keepdims=True)
            dr = rstd * (dyg - m1 - xhat * m2)
            dr_ref[rows, :] = dr
            drb_ref[rows, :] = dr.astype(MXU)
            acc[0] += _fold8(dy * xhat)
            acc[1] += _fold8(dy)

        _chunks(LN_ROWS, LN_CH, chunk, LN_UNROLL)

        @pl.when(i == steps - 1)
        def _():
            gb_ref[0:1, :] = jnp.sum(acc[0], axis=0, keepdims=True)
            gb_ref[1:2, :] = jnp.sum(acc[1], axis=0, keepdims=True)

    tile = pl.BlockSpec((LN_ROWS, d), lambda i: (i, 0))
    vec = pl.BlockSpec((1, d), lambda i: (0, 0))
    return pl.pallas_call(
        body, name=name, grid=(steps,), in_specs=[tile, vec, tile] + [tile] * n_rest,
        out_specs=[tile, tile, pl.BlockSpec((2, d), lambda i: (0, 0))],
        out_shape=[jax.ShapeDtypeStruct((n, d), F32), jax.ShapeDtypeStruct((n, d), MXU), jax.ShapeDtypeStruct((2, d), F32)],
        scratch_shapes=[pltpu.VMEM((2, 8, d), F32)],
        compiler_params=_params(("arbitrary",)),
    )(r, g, d_a, *d_rest)


FFN_ROWS = 512
FFN_CH = 64
GELU_C1 = math.sqrt(2.0 / math.pi)
GELU_C2 = 0.044715


def _conv3(v, prev8, w0, w1, w2, bias):
    n = v.shape[0]
    ext = jnp.concatenate([prev8, v], axis=0)
    g1 = _down(ext, 1)[8:8 + n]
    g2 = _down(ext, 2)[8:8 + n]
    return bias + w0 * g2 + w1 * g1 + w2 * v, g1, g2


def _glu_fwd(name, g4, u4, conv_w, conv_b):
    j_n, b_n, s_n, f = g4.shape
    tiles = s_n // FFN_ROWS

    def body(g_ref, halo_ref, u_ref, w_ref, b_ref, o_ref, gs):
        s = pl.program_id(2)
        gs[0:8, :] = jnp.where(s > 0, halo_ref[...], 0.0)
        gs[8:, :] = g_ref[...]
        w0, w1, w2, bias = w_ref[0:1, :], w_ref[1:2, :], w_ref[2:3, :], b_ref[...]

        def chunk(st):
            v = gs[pl.ds(pl.multiple_of(st + 8, 8), FFN_CH), :]
            conv, _, _ = _conv3(v, gs[pl.ds(st, 8), :], w0, w1, w2, bias)
            cdf = 0.5 * jnp.tanh(conv * (GELU_C1 + (GELU_C1 * GELU_C2) * (conv * conv))) + 0.5
            o_ref[pl.ds(st, FFN_CH), :] = (conv * cdf * u_ref[pl.ds(st, FFN_CH), :]).astype(o_ref.dtype)

        _chunks(FFN_ROWS, FFN_CH, chunk)

    tile = pl.BlockSpec((None, None, FFN_ROWS, f), lambda j, b, s: (j, b, s, 0))
    halo = pl.BlockSpec((None, None, 8, f), lambda j, b, s: (j, b, jnp.maximum(s * (FFN_ROWS // 8) - 1, 0), 0))
    return pl.pallas_call(
        body, name=name, grid=(j_n, b_n, tiles),
        in_specs=[tile, halo, tile,
                  pl.BlockSpec((None, 3, f), lambda j, b, s: (j, 0, 0)),
                  pl.BlockSpec((None, 1, f), lambda j, b, s: (j, 0, 0))],
        out_specs=tile, out_shape=jax.ShapeDtypeStruct(g4.shape, MXU),
        scratch_shapes=[pltpu.VMEM((8 + FFN_ROWS, f), F32)],
        compiler_params=_params(("parallel", "parallel", "parallel")),
    )(g4, g4, u4, conv_w, conv_b)


def _glu_bwd(name, dh4, g4, u4, conv_w, conv_b, job=None):
    j_n, b_n, s_n, f = g4.shape
    tiles = s_n // FFN_ROWS
    ext = FFN_CH + 8
    grid = (j_n, b_n, tiles)

    def body(*refs):
        refs, riding = _own_refs(job, refs, 9, 3, 4)
        (d_ref, dnext_ref, g_ref, gprev_ref, gnext_ref, u_ref, unext_ref, w_ref, b_ref,
         dg_ref, du_ref, wb_ref, gs, us, ds, acc) = refs
        b, s = pl.program_id(1), pl.program_id(2)
        last = s == tiles - 1
        if riding is not None:
            at_first, at_last = _grid_ends(grid)
            pl.when(at_first)(riding.begin)

        @pl.when((b == 0) & (s == 0))
        def _():
            acc[...] = jnp.zeros_like(acc)

        gs[0:8, :] = jnp.where(s > 0, gprev_ref[...], 0.0)
        gs[8:8 + FFN_ROWS, :] = g_ref[...]
        gs[8 + FFN_ROWS:, :] = gnext_ref[...]
        us[0:FFN_ROWS, :] = u_ref[...]
        us[FFN_ROWS:, :] = unext_ref[...]
        ds[0:FFN_ROWS, :] = d_ref[...]
        ds[FFN_ROWS:, :] = jnp.where(last, 0.0, dnext_ref[...])
        w0, w1, w2, bias = w_ref[0:1, :], w_ref[1:2, :], w_ref[2:3, :], b_ref[...]

        def chunk(st):
            v = gs[pl.ds(pl.multiple_of(st + 8, 8), ext), :]
            conv, g1, g2 = _conv3(v, gs[pl.ds(st, 8), :], w0, w1, w2, bias)
            c2 = conv * conv
            th = jnp.tanh(conv * (GELU_C1 + (GELU_C1 * GELU_C2) * c2))
            cdf = 0.5 * th + 0.5
            dact = cdf + (conv * (1.0 - th * th)) * ((0.5 * GELU_C1) + (1.5 * GELU_C1 * GELU_C2) * c2)
            de = ds[pl.ds(st, ext), :]
            dconv = de * us[pl.ds(st, ext), :] * dact
            du_ref[pl.ds(st, FFN_CH), :] = (de[:FFN_CH] * (conv[:FFN_CH] * cdf[:FFN_CH])).astype(du_ref.dtype)
            dg = w2 * dconv + w1 * _up(dconv, 1) + w0 * _up(dconv, 2)
            dg_ref[pl.ds(st, FFN_CH), :] = dg[:FFN_CH].astype(dg_ref.dtype)
            dc = dconv[:FFN_CH]
            acc[0] += _fold8(dc * g2[:FFN_CH])
            acc[1] += _fold8(dc * g1[:FFN_CH])
            acc[2] += _fold8(dc * v[:FFN_CH])
            acc[3] += _fold8(dc)

        _chunks(FFN_ROWS, FFN_CH, chunk)

        @pl.when((b == b_n - 1) & last)
        def _():
            for k in range(4):
                wb_ref[k:k + 1, :] = jnp.sum(acc[k], axis=0, keepdims=True)

        if riding is not None:
            pl.when(at_last)(riding.end)

    blocks8 = FFN_ROWS // 8
    tile = pl.BlockSpec((None, None, FFN_ROWS, f), lambda j, b, s: (j, b, s, 0))
    prev = pl.BlockSpec((None, None, 8, f), lambda j, b, s: (j, b, jnp.maximum(s * blocks8 - 1, 0), 0))
    nxt = pl.BlockSpec((None, None, 8, f), lambda j, b, s: (j, b, jnp.minimum((s + 1) * blocks8, s_n // 8 - 1), 0))
    j_ins, j_outs, j_scratch = (job["ins"], job["outs"], job["scratch"]) if job else ([], [], [])
    res = pl.pallas_call(
        body, name=name, grid=grid,
        in_specs=[tile, nxt, tile, prev, nxt, tile, nxt,
                  pl.BlockSpec((None, 3, f), lambda j, b, s: (j, 0, 0)),
                  pl.BlockSpec((None, 1, f), lambda j, b, s: (j, 0, 0))] + [pl.BlockSpec(memory_space=pl.ANY)] * len(j_ins),
        out_specs=[tile, tile, pl.BlockSpec((None, 4, f), lambda j, b, s: (j, 0, 0))]
        + [pl.BlockSpec(memory_space=pl.ANY)] * len(j_outs),
        out_shape=[jax.ShapeDtypeStruct(g4.shape, MXU), jax.ShapeDtypeStruct(g4.shape, MXU),
                   jax.ShapeDtypeStruct((j_n, 4, f), F32)] + list(j_outs),
        scratch_shapes=[pltpu.VMEM((16 + FFN_ROWS, f), F32), pltpu.VMEM((8 + FFN_ROWS, f), F32),
                        pltpu.VMEM((8 + FFN_ROWS, f), F32), pltpu.VMEM((4, 8, f), F32)] + list(j_scratch),
        compiler_params=_params(("arbitrary",) * 3 if job else ("parallel", "arbitrary", "arbitrary")),
    )(dh4, dh4, g4, g4, g4, u4, u4, conv_w, conv_b, *j_ins)
    return (*res[:3], res[3:]) if job else res


POOL_ROWS = 512
POOL_CH = 32
POOL_HALO = 16


def _pool_windows(v, t0, gi, causal):
    shift = _down if causal else _up
    acc, k = v, 1
    while k < POOL_WINDOWS[gi]:
        acc = acc + shift(acc, k)
        k *= 2
    return acc


def _count(t0, n, w):
    t = t0 + lax.broadcasted_iota(jnp.int32, (n, 1), 0)
    return jnp.minimum(t + 1, w).astype(F32)


def _pooled_into(xs, pooled, t_tile):
    def chunk(st):
        for gi, w in enumerate(POOL_WINDOWS):
            cols = slice(gi * POOL_GROUP_DIM, (gi + 1) * POOL_GROUP_DIM)
            v = xs[pl.ds(st, POOL_CH + POOL_HALO), cols]
            sums = _pool_windows(v, None, gi, True)[POOL_HALO:]
            val = sums / _count(t_tile + st, POOL_CH, w) - v[POOL_HALO:]
            pooled[pl.ds(st, POOL_CH), cols] = val.astype(pooled.dtype)

    _chunks(POOL_ROWS, POOL_CH, chunk)


def _pool_specs(b_n, s_n, d):
    per = POOL_ROWS // POOL_HALO
    tile = pl.BlockSpec((None, POOL_ROWS, d), lambda b, s: (b, s, 0))
    prev = pl.BlockSpec((None, POOL_HALO, d), lambda b, s: (b, jnp.maximum(s * per - 1, 0), 0))
    nxt = pl.BlockSpec((None, POOL_HALO, d), lambda b, s: (b, jnp.minimum((s + 1) * per, s_n // POOL_HALO - 1), 0))
    return tile, prev, nxt


def _pool_fwd(name, h3, w, scale):
    b_n, s_n, d = h3.shape
    tile, prev, _ = _pool_specs(b_n, s_n, d)

    def body(h_ref, halo_ref, w_ref, sc_ref, o_ref, xs, pooled):
        s = pl.program_id(1)
        xs[0:POOL_HALO, :] = jnp.where(s > 0, halo_ref[...], 0.0)
        xs[POOL_HALO:, :] = h_ref[...]
        _pooled_into(xs, pooled, s * POOL_ROWS)
        for gi in range(len(POOL_WINDOWS)):
            cols = slice(gi * POOL_GROUP_DIM, (gi + 1) * POOL_GROUP_DIM)
            y = jnp.dot(pooled[:, cols], w_ref[gi], preferred_element_type=F32)
            o_ref[:, cols] = y * sc_ref[:, cols]

    return pl.pallas_call(
        body, name=name, grid=(b_n, s_n // POOL_ROWS),
        in_specs=[tile, prev, pl.BlockSpec(w.shape, lambda b, s: (0, 0, 0)), pl.BlockSpec((1, d), lambda b, s: (0, 0))],
        out_specs=tile, out_shape=jax.ShapeDtypeStruct(h3.shape, F32),
        scratch_shapes=[pltpu.VMEM((POOL_HALO + POOL_ROWS, d), F32), pltpu.VMEM((POOL_ROWS, d), MXU)],
        compiler_params=_params(("parallel", "parallel")),
    )(h3, h3, w, scale)


def _pool_bwd(name, h3, dm3, w, scale):
    b_n, s_n, d = h3.shape
    tile, prev, nxt = _pool_specs(b_n, s_n, d)
    tiles = s_n // POOL_ROWS
    ext = POOL_ROWS + POOL_HALO

    def body(h_ref, halo_ref, dm_ref, dnext_ref, w_ref, sc_ref, dh_ref, dw_ref, dsc_ref, xs, pooled, ds, dp):
        b, s = pl.program_id(0), pl.program_id(1)

        @pl.when((b == 0) & (s == 0))
        def _():
            dw_ref[...] = jnp.zeros_like(dw_ref)
            dsc_ref[...] = jnp.zeros_like(dsc_ref)

        xs[0:POOL_HALO, :] = jnp.where(s > 0, halo_ref[...], 0.0)
        xs[POOL_HALO:, :] = h_ref[...]
        ds[0:POOL_ROWS, :] = dm_ref[...]
        ds[POOL_ROWS:, :] = jnp.where(s == tiles - 1, 0.0, dnext_ref[...])
        _pooled_into(xs, pooled, s * POOL_ROWS)
        for gi in range(len(POOL_WINDOWS)):
            cols = slice(gi * POOL_GROUP_DIM, (gi + 1) * POOL_GROUP_DIM)
            dyb = (ds[:, cols] * sc_ref[:, cols]).astype(MXU)
            dp[:, cols] = lax.dot_general(dyb, w_ref[gi], NT, preferred_element_type=F32)
            pg = pooled[:, cols]
            dw_ref[gi] += lax.dot_general(pg, dyb[:POOL_ROWS], TN, preferred_element_type=F32)
            ypre = jnp.dot(pg, w_ref[gi], preferred_element_type=F32)
            dsc_ref[:, cols] += jnp.sum(ds[0:POOL_ROWS, cols] * ypre, axis=0, keepdims=True)

        def chunk(st):
            for gi, w_len in enumerate(POOL_WINDOWS):
                cols = slice(gi * POOL_GROUP_DIM, (gi + 1) * POOL_GROUP_DIM)
                v = dp[pl.ds(st, POOL_CH + POOL_HALO), cols]
                q = v / _count(s * POOL_ROWS + st, POOL_CH + POOL_HALO, w_len)
                back = _pool_windows(q, None, gi, False)[:POOL_CH] - v[:POOL_CH]
                dh_ref[pl.ds(st, POOL_CH), cols] = ALPHA * ds[pl.ds(st, POOL_CH), cols] + back

        _chunks(POOL_ROWS, POOL_CH, chunk)

    return pl.pallas_call(
        body, name=name, grid=(b_n, tiles),
        in_specs=[tile, prev, tile, nxt, pl.BlockSpec(w.shape, lambda b, s: (0, 0, 0)), pl.BlockSpec((1, d), lambda b, s: (0, 0))],
        out_specs=[tile, pl.BlockSpec(w.shape, lambda b, s: (0, 0, 0)), pl.BlockSpec((1, d), lambda b, s: (0, 0))],
        out_shape=[jax.ShapeDtypeStruct(h3.shape, F32), jax.ShapeDtypeStruct(w.shape, F32), jax.ShapeDtypeStruct((1, d), F32)],
        scratch_shapes=[pltpu.VMEM((POOL_HALO + POOL_ROWS, d), F32), pltpu.VMEM((POOL_ROWS, d), MXU),
                        pltpu.VMEM((ext, d), F32), pltpu.VMEM((ext, d), F32)],
        compiler_params=_params(("arbitrary", "arbitrary")),
    )(h3, h3, dm3, dm3, w, scale)


ROPE_ROWS = 256


def _rope_tables(s_n):
    inv_freq = ROPE_THETA ** (-jnp.arange(0, HEAD_DIM, 2, dtype=F32) / HEAD_DIM)
    ang = jnp.arange(s_n, dtype=F32)[:, None] * inv_freq[None, :]
    cos, sin = jnp.cos(ang), jnp.sin(ang)
    cos_l = jnp.tile(cos, (1, 4))
    sin_l = jnp.tile(jnp.concatenate([-sin, sin], axis=1), (1, 2))
    return cos_l, sin_l


def _rope(name, x3, col0, n_col, cos_l, sin_l, scale, out_dtype, rotate=True, out_cols=None, out_col0=0, into=None):
    b_n, s_n, _ = x3.shape
    cw = N_HEADS * HEAD_DIM
    out_cols = n_col if out_cols is None else out_cols

    def body(x_ref, c_ref, s_ref, *rest):
        o_ref = rest[-1]
        lane = lax.broadcasted_iota(jnp.int32, (ROPE_ROWS, 128), 1)
        first_half = (lane % HEAD_DIM) < (HEAD_DIM // 2)
        cos, sin = c_ref[...], s_ref[...]
        for cb in range(cw // 128):
            cols = slice(cb * 128, (cb + 1) * 128)
            y = x_ref[:, cols]
            if rotate:
                other = jnp.where(first_half, pltpu.roll(y, 128 - HEAD_DIM // 2, 1), pltpu.roll(y, HEAD_DIM // 2, 1))
                y = y * cos + other * sin
            o_ref[:, cols] = (y if scale == 1.0 else y * scale).astype(o_ref.dtype)

    tile = pl.BlockSpec((None, ROPE_ROWS, cw), lambda b, s, c: (b, s, col0 + c))
    tab = pl.BlockSpec((ROPE_ROWS, 128), lambda b, s, c: (s, 0))
    extra, extra_specs, aliases = [], [], {}
    if into is not None:
        extra, extra_specs, aliases = [into], [pl.BlockSpec(memory_space=pl.ANY)], {3: 0}
    return pl.pallas_call(
        body, name=name, grid=(b_n, s_n // ROPE_ROWS, n_col), in_specs=[tile, tab, tab] + extra_specs,
        out_specs=pl.BlockSpec((None, ROPE_ROWS, cw), lambda b, s, c: (b, s, out_col0 + c)),
        out_shape=jax.ShapeDtypeStruct((b_n, s_n, out_cols * cw), out_dtype), input_output_aliases=aliases,
        compiler_params=_params(("parallel", "parallel", "parallel")),
    )(x3, cos_l, sin_l, *extra)


def _to_strided(a, d):
    if d == 1:
        return a
    b_n, s_n, c = a.shape
    return a.reshape(b_n, s_n // d, d, c).transpose(0, 2, 1, 3).reshape(b_n, s_n, c)


def _from_strided(a, d):
    if d == 1:
        return a
    b_n, s_n, c = a.shape
    return a.reshape(b_n, d, s_n // d, c).transpose(0, 2, 1, 3).reshape(b_n, s_n, c)


def _attn_fwd(name, q, q_col, k, v, blocks_per_seq):
    b_n, s_n, cw = k.shape
    nb = s_n // SPAN
    with_prev = blocks_per_seq > 1

    keys = 2 * SPAN if with_prev else SPAN
    pair = 2 * HEAD_DIM

    def body(*refs):
        if with_prev:
            q_ref, kc_ref, kp_ref, vc_ref, vp_ref, o_ref, lse_ref, k_all, v_all, s_buf, m_buf = refs
            k_all[0:SPAN, :] = kp_ref[...]
            k_all[SPAN:, :] = kc_ref[...]
            v_all[0:SPAN, :] = vp_ref[...]
            v_all[SPAN:, :] = vc_ref[...]
        else:
            q_ref, k_all, v_all, o_ref, lse_ref, s_buf, m_buf = refs
        n = pl.program_id(1)
        qi = lax.broadcasted_iota(jnp.int32, (SPAN, keys), 0)
        kj = lax.broadcasted_iota(jnp.int32, (SPAN, keys), 1)
        if with_prev:
            back = jnp.where((n % blocks_per_seq) != 0, 0, 2 * SPAN)
            mask = ((kj < SPAN) & (kj >= qi + back)) | ((kj >= SPAN) & (kj - SPAN <= qi))
        else:
            mask = kj <= qi
        lane = lax.broadcasted_iota(jnp.int32, (SPAN, pair), 1)
        low = lane < HEAD_DIM
        ones = jnp.ones((keys, pair), MXU)
        for h in range(N_HEADS):
            grp = slice((h // 2) * pair, (h // 2 + 1) * pair)
            q2 = q_ref[:, grp]
            qh = jnp.where(low if h % 2 == 0 else ~low, q2, jnp.zeros_like(q2))
            s = jnp.where(mask, lax.dot_general(qh, k_all[:, grp], NT, preferred_element_type=F32), -jnp.inf)
            s_buf[h] = s
            m_buf[h] = jnp.broadcast_to(jnp.max(s, -1, keepdims=True), (SPAN, pair))
        lse_all = jnp.zeros((SPAN, pair), F32)
        for hp in range(N_HEADS // 2):
            grp = slice(hp * pair, (hp + 1) * pair)
            v2 = v_all[:, grp]
            halves = []
            for h in (2 * hp, 2 * hp + 1):
                m = m_buf[h]
                p = jnp.exp(s_buf[h] - jnp.tile(m, (1, keys // pair))).astype(MXU)
                tot = jnp.dot(p, ones, preferred_element_type=F32)
                halves.append(jnp.dot(p, v2, preferred_element_type=F32) / tot)
                lse_all = jnp.where(lane == h, m + jnp.log(tot), lse_all)
            o_ref[:, grp] = jnp.where(low, halves[0], halves[1])
        lse_ref[...] = lse_all[:, 0:N_HEADS]

    cur = lambda b, n: (b, n, 0)
    prv = lambda b, n: (b, jnp.maximum(n - 1, 0), 0)
    blk = (None, SPAN, cw)
    kv_specs = [pl.BlockSpec(blk, cur), pl.BlockSpec(blk, prv)] if with_prev else [pl.BlockSpec(blk, cur)]
    operands = [q, k, k, v, v] if with_prev else [q, k, v]
    stage = [pltpu.VMEM((keys, cw), MXU)] * 2 if with_prev else []
    return pl.pallas_call(
        body, name=name, grid=(b_n, nb),
        in_specs=[pl.BlockSpec(blk, lambda b, n: (b, n, q_col))] + kv_specs + kv_specs,
        out_specs=[pl.BlockSpec(blk, cur), pl.BlockSpec((None, SPAN, N_HEADS), cur)],
        out_shape=[jax.ShapeDtypeStruct((b_n, s_n, cw), F32), jax.ShapeDtypeStruct((b_n, s_n, N_HEADS), F32)],
        scratch_shapes=stage + [pltpu.VMEM((N_HEADS, SPAN, keys), F32), pltpu.VMEM((N_HEADS, SPAN, pair), F32)],
        compiler_params=_params(("parallel", "parallel")),
    )(*operands)


def _attn_bwd(name, q, q_col, k, v, do, lse, dd, blocks_per_seq, dk_prev=None, dv_prev=None):
    b_n, s_n, cw = k.shape
    nb = s_n // SPAN
    with_next = blocks_per_seq > 1
    accumulate = dk_prev is not None
    rows = 2 * SPAN if with_next else SPAN

    def body(*refs):
        refs = list(refs)
        qc_ref, doc_ref, lsec_ref, ddc_ref = refs[:4]
        del refs[:4]
        if with_next:
            qn_ref, don_ref, lsen_ref, ddn_ref = refs[:4]
            del refs[:4]
        k_ref, v_ref = refs[:2]
        del refs[:2]
        if accumulate:
            dkp_ref, dvp_ref = refs[:2]
            del refs[:2]
        dq_ref, dk_ref, dv_ref = refs[:3]
        del refs[:3]
        if with_next:
            carry, q_all, do_all, side = refs[:4]
            del refs[:4]
            q_all[0:SPAN, :] = qc_ref[...]
            q_all[SPAN:, :] = qn_ref[...]
            do_all[0:SPAN, :] = doc_ref[...]
            do_all[SPAN:, :] = don_ref[...]
            side[0, 0:SPAN, :] = lsec_ref[...]
            side[0, SPAN:, :] = lsen_ref[...]
            side[1, 0:SPAN, :] = ddc_ref[...]
            side[1, SPAN:, :] = ddn_ref[...]
            lse_at = lambda h: side[0, :, h:h + 1]
            dd_at = lambda h: side[1, :, h:h + 1]
        else:
            q_all, do_all = qc_ref, doc_ref
            lse_at = lambda h: lsec_ref[:, h:h + 1]
            dd_at = lambda h: ddc_ref[:, h:h + 1]
        p_buf, ds_buf = refs
        n = pl.program_id(1)
        qi = lax.broadcasted_iota(jnp.int32, (rows, SPAN), 0)
        kj = lax.broadcasted_iota(jnp.int32, (rows, SPAN), 1)
        if with_next:
            first = (n % blocks_per_seq) == 0
            reach = jnp.where(((n + 1) % blocks_per_seq) != 0, SPAN, -2 * SPAN)
            mask = ((qi < SPAN) & (kj <= qi)) | ((qi >= SPAN) & (kj >= qi - reach))
        else:
            mask = kj <= qi
        pair = 2 * HEAD_DIM
        low = lax.broadcasted_iota(jnp.int32, (rows, pair), 1) < HEAD_DIM
        low_k = lax.broadcasted_iota(jnp.int32, (SPAN, pair), 1) < HEAD_DIM

        def pick(v, h, low_mask):
            return jnp.where(low_mask if h % 2 == 0 else ~low_mask, v, jnp.zeros_like(v))

        for h in range(N_HEADS):
            grp = slice((h // 2) * pair, (h // 2 + 1) * pair)
            s = lax.dot_general(pick(q_all[:, grp], h, low), k_ref[:, grp], NT, preferred_element_type=F32)
            p = jnp.where(mask, jnp.exp(s - lse_at(h)), 0.0)
            dp = lax.dot_general(pick(do_all[:, grp], h, low), v_ref[:, grp], NT, preferred_element_type=F32)
            p_buf[h] = p.astype(MXU)
            ds_buf[h] = (p * (dp - dd_at(h))).astype(MXU)
        for hp in range(N_HEADS // 2):
            grp = slice(hp * pair, (hp + 1) * pair)
            q2, do2, k2 = q_all[:, grp], do_all[:, grp], k_ref[:, grp]
            dv = dk = dq2 = None
            for h in (2 * hp, 2 * hp + 1):
                t_dv = lax.dot_general(p_buf[h], pick(do2, h, low), TN, preferred_element_type=F32)
                t_dk = lax.dot_general(ds_buf[h], pick(q2, h, low), TN, preferred_element_type=F32)
                t_dq = jnp.dot(ds_buf[h], pick(k2, h, low_k), preferred_element_type=F32)
                dv = t_dv if dv is None else dv + t_dv
                dk = t_dk if dk is None else dk + t_dk
                dq2 = t_dq if dq2 is None else dq2 + t_dq
            if accumulate:
                dk = dk + dkp_ref[:, grp]
                dv = dv + dvp_ref[:, grp]
            dk_ref[:, grp] = dk
            dv_ref[:, grp] = dv
            if with_next:
                dq_ref[:, grp] = dq2[:SPAN] + jnp.where(first, 0.0, carry[:, grp])
                carry[:, grp] = dq2[SPAN:]
            else:
                dq_ref[:, grp] = dq2

    cur = lambda b, n: (b, n, 0)
    nxt = lambda b, n: (b, jnp.minimum(n + 1, nb - 1), 0)
    blk, hblk = (None, SPAN, cw), (None, SPAN, N_HEADS)
    q_specs = lambda m: [pl.BlockSpec(blk, lambda b, n: (*m(b, n)[:2], q_col)), pl.BlockSpec(blk, m),
                         pl.BlockSpec(hblk, m), pl.BlockSpec(hblk, m)]
    in_specs = q_specs(cur) + (q_specs(nxt) if with_next else []) + [pl.BlockSpec(blk, cur)] * (4 if accumulate else 2)
    operands = [q, do, lse, dd] * (2 if with_next else 1) + [k, v] + ([dk_prev, dv_prev] if accumulate else [])
    out = jax.ShapeDtypeStruct((b_n, s_n, cw), F32)
    return pl.pallas_call(
        body, name=name, grid=(b_n, nb), in_specs=in_specs, out_specs=[pl.BlockSpec(blk, cur)] * 3,
        out_shape=[out, out, out],
        scratch_shapes=([pltpu.VMEM((SPAN, cw), F32), pltpu.VMEM((rows, cw), MXU), pltpu.VMEM((rows, cw), MXU),
                         pltpu.VMEM((2, rows, N_HEADS), F32)] if with_next else [])
        + [pltpu.VMEM((N_HEADS, rows, SPAN), MXU)] * 2,
        compiler_params=_params(("parallel", "arbitrary")),
    )(*operands)


MIX_ROWS = 256


def _group_weights(ls):
    m = functools.reduce(jnp.maximum, ls)
    es = [jnp.exp(l - m) for l in ls]
    tot = functools.reduce(lambda a, b: a + b, es)
    return [e / tot for e in es]


def _attn_mix(name, outs, lses):
    n, cw = outs[0].shape
    g_n = len(outs)

    def body(*refs):
        o_refs, l_refs, out_ref = refs[:g_n], refs[g_n:2 * g_n], refs[2 * g_n]
        ws = _group_weights([r[...] for r in l_refs])
        for h in range(N_HEADS):
            cols = slice(h * HEAD_DIM, (h + 1) * HEAD_DIM)
            acc = None
            for g in range(g_n):
                t = ws[g][:, h:h + 1] * o_refs[g][:, cols]
                acc = t if acc is None else acc + t
            out_ref[:, cols] = acc.astype(out_ref.dtype)

    tile = pl.BlockSpec((MIX_ROWS, cw), lambda i: (i, 0))
    htile = pl.BlockSpec((MIX_ROWS, N_HEADS), lambda i: (i, 0))
    return pl.pallas_call(
        body, name=name, grid=(n // MIX_ROWS,), in_specs=[tile] * g_n + [htile] * g_n, out_specs=tile,
        out_shape=jax.ShapeDtypeStruct((n, cw), MXU), compiler_params=_params(("parallel",)),
    )(*outs, *lses)


def _attn_mix_bwd(name, do, outs, lses):
    n, cw = do.shape
    g_n = len(outs)

    def body(*refs):
        do_ref, o_refs, l_refs = refs[0], refs[1:1 + g_n], refs[1 + g_n:1 + 2 * g_n]
        dog_refs, dd_refs = refs[1 + 2 * g_n:1 + 3 * g_n], refs[1 + 3 * g_n:]
        ws = _group_weights([r[...] for r in l_refs])
        for h in range(N_HEADS):
            cols = slice(h * HEAD_DIM, (h + 1) * HEAD_DIM)
            dh = do_ref[:, cols]
            o = None
            for g in range(g_n):
                t = ws[g][:, h:h + 1] * o_refs[g][:, cols]
                o = t if o is None else o + t
            dot = jnp.sum(dh * o, -1, keepdims=True)
            for g in range(g_n):
                wg = ws[g][:, h:h + 1]
                dog_refs[g][:, cols] = (wg * dh).astype(dog_refs[g].dtype)
                dd_refs[g][:, h:h + 1] = wg * dot

    tile = pl.BlockSpec((MIX_ROWS, cw), lambda i: (i, 0))
    htile = pl.BlockSpec((MIX_ROWS, N_HEADS), lambda i: (i, 0))
    return pl.pallas_call(
        body, name=name, grid=(n // MIX_ROWS,), in_specs=[tile] * (1 + g_n) + [htile] * g_n,
        out_specs=[tile] * g_n + [htile] * g_n,
        out_shape=[jax.ShapeDtypeStruct((n, cw), MXU)] * g_n + [jax.ShapeDtypeStruct((n, N_HEADS), F32)] * g_n,
        compiler_params=_params(("parallel",)),
    )(do, *outs, *lses)


def _loss_head(name, y, target):
    n, d = y.shape
    steps = n // LN_ROWS

    def body(y_ref, t_ref, dy_ref, l_ref, acc):
        i = pl.program_id(0)

        @pl.when(i == 0)
        def _():
            acc[...] = jnp.zeros_like(acc)

        def chunk(s):
            rows = pl.ds(s, LN_CH)
            err = y_ref[rows, :] - t_ref[rows, :]
            dy_ref[rows, :] = err / d
            acc[...] += _fold8(err * err)

        _chunks(LN_ROWS, LN_CH, chunk, LN_UNROLL)

        @pl.when(i == steps - 1)
        def _():
            l_ref[...] = jnp.full((8, 128), 0.5 / d, F32) * jnp.sum(acc[...])

    tile = pl.BlockSpec((LN_ROWS, d), lambda i: (i, 0))
    return pl.pallas_call(
        body, name=name, grid=(steps,), in_specs=[tile, tile],
        out_specs=[tile, pl.BlockSpec((8, 128), lambda i: (0, 0))],
        out_shape=[jax.ShapeDtypeStruct((n, d), F32), jax.ShapeDtypeStruct((8, 128), F32)],
        scratch_shapes=[pltpu.VMEM((8, d), F32)], compiler_params=_params(("arbitrary",)),
    )(y, target)


EW_TILE_BYTES = 1 << 20


def _row_tile(rows, cols):
    tr = 8
    while rows % (2 * tr) == 0 and 2 * tr * cols * 4 <= EW_TILE_BYTES:
        tr *= 2
    return tr if rows % tr == 0 else rows


def _add_halves(name, grad, recv, half, out_dtype):
    j_n, _, r, c = grad.shape
    tr = _row_tile(r, c)

    def body(half_ref, g_ref, r_ref, o_ref):
        o_ref[...] = (g_ref[...] + r_ref[...]).astype(o_ref.dtype)

    return pl.pallas_call(
        body, name=name, out_shape=jax.ShapeDtypeStruct((j_n, r, c), out_dtype),
        grid_spec=pltpu.PrefetchScalarGridSpec(
            num_scalar_prefetch=1, grid=(j_n, r // tr),
            in_specs=[pl.BlockSpec((None, None, tr, c), lambda j, i, hf: (j, hf[0], i, 0)),
                      pl.BlockSpec((None, tr, c), lambda j, i, hf: (j, i, 0))],
            out_specs=pl.BlockSpec((None, tr, c), lambda j, i, hf: (j, i, 0))),
        compiler_params=_params(("parallel", "parallel")),
    )(half, grad, recv)


def _add_chips(name, mine, recv, chip):
    j_n, r, c = mine.shape
    tr = _row_tile(r, c)

    def body(chip_ref, m_ref, r_ref, o_ref):
        total = m_ref[...].astype(F32)
        for k in range(j_n - 1):
            total = total + r_ref[k].astype(F32)
        o_ref[...] = total

    return pl.pallas_call(
        body, name=name, out_shape=jax.ShapeDtypeStruct((r, c), F32),
        grid_spec=pltpu.PrefetchScalarGridSpec(
            num_scalar_prefetch=1, grid=(r // tr,),
            in_specs=[pl.BlockSpec((None, tr, c), lambda i, ch: (ch[0], i, 0)),
                      pl.BlockSpec((j_n - 1, tr, c), lambda i, ch: (0, i, 0))],
            out_specs=pl.BlockSpec((tr, c), lambda i, ch: (i, 0))),
        compiler_params=_params(("parallel",)),
    )(chip, mine, recv)


def _adam_math(w, g, m, v):
    m = ADAM_B1 * m + (1.0 - ADAM_B1) * g
    v = ADAM_B2 * v + (1.0 - ADAM_B2) * (g * g)
    m_hat = m / (1.0 - ADAM_B1 ** ADAM_STEP)
    v_hat = v / (1.0 - ADAM_B2 ** ADAM_STEP)
    delta = -ADAM_LR * (m_hat / (jnp.sqrt(v_hat) + ADAM_EPS) + ADAM_WD * w)
    return delta, m, v


def _adam_halves(name, own, other, half, w, m, v, part, into):
    _, _, r, c = w.shape
    tr = _row_tile(r, c)

    def body(half_ref, own_ref, oth_ref, w_ref, m_ref, v_ref, *rest):
        g_out, d_out, m_out, v_out = rest[-4:]
        g = jnp.where(pl.program_id(0) == half_ref[0], own_ref[...], oth_ref[...])
        delta, m_new, v_new = _adam_math(w_ref[...], g, m_ref[...], v_ref[...])
        g_out[...] = g
        d_out[...] = delta
        m_out[...] = m_new
        v_out[...] = v_new

    flat = pl.BlockSpec((tr, c), lambda h, i, hf: (i, 0))
    full = pl.BlockSpec((None, None, tr, c), lambda h, i, hf: (part, h, i, 0))
    out = jax.ShapeDtypeStruct(w.shape, F32)
    kept = [] if into is None else list(into)
    return pl.pallas_call(
        body, name=name, out_shape=[out] * 4,
        grid_spec=pltpu.PrefetchScalarGridSpec(
            num_scalar_prefetch=1, grid=(2, r // tr),
            in_specs=[flat, flat, full, full, full] + [pl.BlockSpec(memory_space=pl.ANY)] * len(kept), out_specs=[full] * 4),
        input_output_aliases={6 + i: i for i in range(len(kept))},
        compiler_params=_params(("parallel", "parallel")),
    )(half, own, other, w, m, v, *kept)


def _adam_small(name, g, w, m, v):
    def body(g_ref, w_ref, m_ref, v_ref, d_out, m_out, v_out):
        delta, m_new, v_new = _adam_math(w_ref[...], g_ref[...], m_ref[...], v_ref[...])
        d_out[...] = delta
        m_out[...] = m_new
        v_out[...] = v_new

    out = jax.ShapeDtypeStruct(w.shape, F32)
    return pl.pallas_call(body, name=name, out_shape=[out] * 3)(g, w, m, v)


def _place():
    x, y, c = lax.axis_index("x"), lax.axis_index("y"), lax.axis_index("c")
    chips = [(1 - x, y), (x, 1 - y), (1 - x, 1 - y)]
    return x, y, c, chips


ANY = pl.BlockSpec(memory_space=pl.ANY)


class _ShardGather:
    def __init__(self, ins, outs, send, recv, local):
        self.ins, self.outs, self.send, self.recv, self.local = ins, outs, send, recv, local
        self.n = len(ins)

    @staticmethod
    def scratch(n):
        return [pltpu.SemaphoreType.DMA((n, 6)), pltpu.SemaphoreType.DMA((n, 6)), pltpu.SemaphoreType.DMA((n,))]

    @staticmethod
    def out_shapes(shards):
        return [jax.ShapeDtypeStruct((N_CHIPS, *s.shape), s.dtype) for s in shards]

    def _copy(self, t, k, src, dst, to):
        return pltpu.make_async_remote_copy(src_ref=src, dst_ref=dst, send_sem=self.send.at[t, k], recv_sem=self.recv.at[t, k],
                                            device_id=to, device_id_type=MESH)

    def _own(self, t, me):
        return pltpu.make_async_copy(self.ins[t], self.outs[t].at[me], self.local.at[t])

    def _first(self, t, k, place):
        x, y, c, chips = place
        px, py = chips[k]
        return self._copy(t, k, self.ins[t].at[c], self.outs[t].at[2 * x + y, c], (px, py, c))

    def _passed_on(self, t, k, place, half):
        x, y, c, chips = place
        px, py = chips[k]
        slab = self.outs[t].at[2 * px + py, half]
        return self._copy(t, 3 + k, slab, slab, (x, y, 1 - c))

    def begin(self):
        place = _place()
        x, y, c, _ = place
        for t in range(self.n):
            self._own(t, 2 * x + y).start()
        for t in range(self.n):
            for k in range(N_CHIPS - 1):
                self._first(t, k, place).start()

    def end(self):
        place = _place()
        x, y, c, chips = place
        for t in range(self.n):
            for k, (px, py) in enumerate(chips):
                self._copy(t, k, self.ins[t].at[c], self.outs[t].at[2 * px + py, c], (px, py, c)).wait_recv()
                self._passed_on(t, k, place, c).start()
        for t in range(self.n):
            for k in range(N_CHIPS - 1):
                self._passed_on(t, k, place, 1 - c).wait_recv()
        for t in range(self.n):
            for k in range(N_CHIPS - 1):
                self._first(t, k, place).wait_send()
                self._passed_on(t, k, place, c).wait_send()
            self._own(t, 2 * x + y).wait()


class _Exchange:
    def __init__(self, make_copies, ins, outs, send, recv):
        self.copies = lambda: make_copies(ins, outs, send, recv)

    def begin(self):
        for cp in self.copies():
            cp.start()

    def end(self):
        for cp in self.copies():
            cp.wait()


def _gather_job(shards):
    return dict(ins=list(shards), outs=_ShardGather.out_shapes(shards), scratch=_ShardGather.scratch(len(shards)),
                bind=_ShardGather)


def _sibling_job(arrays, pick_other_half):
    n = len(arrays)

    def copies(ins, outs, send, recv):
        x, y, c, _ = _place()
        return [pltpu.make_async_remote_copy(
            src_ref=ins[t].at[:, 1 - c] if pick_other_half else ins[t], dst_ref=outs[t], send_sem=send.at[t],
            recv_sem=recv.at[t], device_id=(x, y, 1 - c), device_id_type=MESH) for t in range(n)]

    shapes = [(a.shape[0], *a.shape[2:]) if pick_other_half else a.shape for a in arrays]
    return dict(ins=list(arrays), outs=[jax.ShapeDtypeStruct(s, a.dtype) for s, a in zip(shapes, arrays)],
                scratch=[pltpu.SemaphoreType.DMA((n,)), pltpu.SemaphoreType.DMA((n,))],
                bind=functools.partial(_Exchange, copies))


def _owner_job(arrays):
    n = len(arrays)

    def copies(ins, outs, send, recv):
        x, y, c, chips = _place()
        return [pltpu.make_async_remote_copy(
            src_ref=ins[t].at[2 * px + py], dst_ref=outs[t].at[k], send_sem=send.at[t, k], recv_sem=recv.at[t, k],
            device_id=(px, py, c), device_id_type=MESH) for t in range(n) for k, (px, py) in enumerate(chips)]

    return dict(ins=list(arrays), outs=[jax.ShapeDtypeStruct((N_CHIPS - 1, *a.shape[1:]), a.dtype) for a in arrays],
                scratch=[pltpu.SemaphoreType.DMA((n, 3)), pltpu.SemaphoreType.DMA((n, 3))],
                bind=functools.partial(_Exchange, copies))


def _bound(job, refs):
    n_i, n_o = len(job["ins"]), len(job["outs"])
    return job["bind"](refs[:n_i], refs[n_i:n_i + n_o], *refs[n_i + n_o:])


def _run_job(name, job):
    def body(*refs):
        bound = _bound(job, refs)
        bound.begin()
        bound.end()

    return pl.pallas_call(
        body, name=name, in_specs=[ANY] * len(job["ins"]), out_specs=[ANY] * len(job["outs"]), out_shape=job["outs"],
        scratch_shapes=job["scratch"],
    )(*job["ins"])


def _sum_all_devices(name, part):
    r, c = part.shape

    def body(p_ref, o_ref, buf, send, recv):
        x, y, cc, _ = _place()
        me = 4 * x + 2 * y + cc
        copies = []
        for mask in range(1, 8):
            fx, fy, fc = (mask >> 2) & 1, (mask >> 1) & 1, mask & 1
            to = (x ^ fx, y ^ fy, cc ^ fc)
            copies.append((mask, pltpu.make_async_remote_copy(
                src_ref=p_ref, dst_ref=buf.at[me], send_sem=send.at[mask - 1], recv_sem=recv.at[mask - 1],
                device_id=to, device_id_type=MESH)))
            copies[-1][1].start()
        buf[me] = p_ref[...]
        for mask, cp in copies:
            pltpu.make_async_remote_copy(src_ref=p_ref, dst_ref=buf.at[me ^ mask], send_sem=send.at[mask - 1],
                                         recv_sem=recv.at[mask - 1], device_id=(x, y, cc), device_id_type=MESH).wait_recv()
        for _, cp in copies:
            cp.wait_send()
        total = buf[0]
        for d in range(1, 8):
            total = total + buf[d]
        o_ref[...] = total

    vm = pl.BlockSpec(memory_space=pltpu.VMEM)
    return pl.pallas_call(
        body, name=name, in_specs=[vm], out_specs=vm, out_shape=jax.ShapeDtypeStruct((r, c), F32),
        scratch_shapes=[pltpu.VMEM((8, r, c), F32), pltpu.SemaphoreType.DMA((7,)), pltpu.SemaphoreType.DMA((7,))],
    )(part)


def kernel(x, pool_w, pool_scale, w_q, w_kv, w_o, ffn_w_gate, ffn_w_up, ffn_conv_w, ffn_conv_b, ffn_w_down, ln1_g, ln1_b, ln2_g, ln2_b, loss_target, m_pool_w, m_pool_scale, m_w_q, m_w_kv, m_w_o, m_ffn_w_gate, m_ffn_w_up, m_ffn_conv_w, m_ffn_conv_b, m_ffn_w_down, m_ln1_g, m_ln1_b, m_ln2_g, m_ln2_b, v_pool_w, v_pool_scale, v_w_q, v_w_kv, v_w_o, v_ffn_w_gate, v_ffn_w_up, v_ffn_conv_w, v_ffn_conv_b, v_ffn_w_down, v_ln1_g, v_ln1_b, v_ln2_g, v_ln2_b):
    b_n, s_n, d = x.shape
    n = b_n * s_n
    f = ffn_w_gate.shape[-1]
    qc = w_q.shape[-1]
    kvb = w_kv.shape[-1] // 2
    n_attn = w_q.shape[0]
    g_n = len(DILATIONS)
    cw = N_HEADS * HEAD_DIM
    xi, yi, ci = lax.axis_index("x"), lax.axis_index("y"), lax.axis_index("c")
    half = jnp.reshape(ci, (1,)).astype(jnp.int32)
    chip = jnp.reshape(2 * xi + yi, (1,)).astype(jnp.int32)

    sharded = {
        "pool_w": (pool_w, m_pool_w, v_pool_w, (2, 4 * 64, POOL_GROUP_DIM)),
        "pool_scale": (pool_scale, m_pool_scale, v_pool_scale, (2, 1, pool_scale.shape[-1])),
        "w_q": (w_q, m_w_q, v_w_q, (2, d, qc)),
        "w_kv": (w_kv, m_w_kv, v_w_kv, (2, d // 2, w_kv.shape[-1])),
        "w_o": (w_o, m_w_o, v_w_o, (2, w_o.shape[1], d)),
        "ffn_w_gate": (ffn_w_gate, m_ffn_w_gate, v_ffn_w_gate, (2, 2 * d, f)),
        "ffn_w_up": (ffn_w_up, m_ffn_w_up, v_ffn_w_up, (2, 2 * d, f)),
        "ffn_conv_w": (ffn_conv_w, m_ffn_conv_w, v_ffn_conv_w, (2, 6, f)),
        "ffn_w_down": (ffn_w_down, m_ffn_w_down, v_ffn_w_down, (2, 2 * f, d)),
    }
    mxu_weights = ("pool_w", "w_q", "w_kv", "w_o", "ffn_w_gate", "ffn_w_up", "ffn_w_down")
    names = list(sharded)
    wo_rows = w_o.shape[1]
    shard_of = {("wkv", 0): w_kv.astype(MXU).reshape(2, d // 2, w_kv.shape[-1])}
    for i in range(DEPTH):
        shard_of["wg", i] = ffn_w_gate[i].astype(MXU).reshape(2, d // 2, f)
        shard_of["wu", i] = ffn_w_up[i].astype(MXU).reshape(2, d // 2, f)
        shard_of["wd", i] = ffn_w_down[i].astype(MXU).reshape(2, f // 2, d)
    for i in range(n_attn):
        shard_of["wq", i] = w_q[i].astype(MXU).reshape(2, d // 2, qc)
        shard_of["wo", i] = w_o[i].astype(MXU).reshape(2, wo_rows // 2, d)
    carried_by = {
        (0, "gate"): [("wg", 1)], (0, "up"): [("wu", 1)], (0, "down"): [("wd", 1)],
        (1, "gate"): [("wkv", 0)], (1, "up"): [("wq", 0), ("wo", 0)], (1, "down"): [("wg", 2)],
        (2, "q"): [("wu", 2)], (2, "o"): [("wd", 2)], (2, "gate"): [("wq", 1), ("wo", 1)], (2, "up"): [("wg", 3)],
        (2, "down"): [("wu", 3)], (3, "q"): [("wd", 3)],
    }
    got = {}

    def carrying(site, call, *args):
        keys = carried_by.get(site, [])
        if not keys:
            return call(*args)
        out, arrived = call(*args, job=_gather_job([shard_of[k] for k in keys]))
        got.update(zip(keys, arrived))
        return out

    first_keys = [("wg", 0), ("wu", 0), ("wd", 0)]
    first = _run_job("gather_weights", _gather_job([
        pool_w.astype(MXU).reshape(sharded["pool_w"][3]), pool_scale.reshape(sharded["pool_scale"][3]),
        ffn_conv_w.reshape(sharded["ffn_conv_w"][3])] + [shard_of[k] for k in first_keys]))
    got.update(zip(first_keys, first[3:]))
    wg_at = lambda i: got["wg", i].reshape(N_CHIPS, d, f)
    wu_at = lambda i: got["wu", i].reshape(N_CHIPS, d, f)
    wd_at = lambda i: got["wd", i].reshape(N_CHIPS, f, d)
    wq_at = lambda i: got["wq", i].reshape(N_CHIPS, d, qc)
    wo_at = lambda i: got["wo", i].reshape(cw, d)
    wkv_at = lambda: got["wkv", 0].reshape(N_CHIPS, d, w_kv.shape[-1])
    cw_all = first[2].reshape(N_CHIPS, DEPTH, 3, f)
    cw_l = [cw_all[:, i] for i in range(DEPTH)]
    cb_l = [ffn_conv_b[i].reshape(N_CHIPS, 1, f) for i in range(DEPTH)]
    pw_nat = first[0].reshape(N_CHIPS, N_POOL_LAYERS, 4, 64, POOL_GROUP_DIM).transpose(1, 2, 0, 3, 4).reshape(
        N_POOL_LAYERS, 4, POOL_GROUP_DIM, POOL_GROUP_DIM)
    ps_nat = first[1].reshape(N_CHIPS, N_POOL_LAYERS, -1).transpose(1, 0, 2).reshape(N_POOL_LAYERS, 1, d)
    cos_l, sin_l = _rope_tables(s_n)

    def vec(a, layer):
        return a[layer].reshape(1, d)

    h = x.reshape(n, d)
    hb = None
    saved = []
    k_str = v_str = None
    for layer in range(DEPTH):
        keep = {"h": h, "hb": hb}
        if layer < N_POOL_LAYERS:
            mix = _pool_fwd("pool_fwd", h.reshape(b_n, s_n, d), pw_nat[layer], ps_nat[layer]).reshape(n, d)
        else:
            a = layer - N_POOL_LAYERS
            q = carrying((layer, "q"), functools.partial(_mm_cols, "q_proj", hb, wq_at(a), (), F32))
            qr = _rope("q_rope", q.reshape(b_n, s_n, -1), 0, g_n, cos_l, sin_l, HEAD_DIM ** -0.5, MXU)
            q_str, lse_str, outs, lses = [], [], [], []
            for g, dil in enumerate(DILATIONS):
                if dil == 1:
                    qg, col = qr, g
                else:
                    qg, col = _to_strided(qr[:, :, g * cw:(g + 1) * cw], dil), 0
                o_g, lse_g = _attn_fwd(f"attn_fwd_d{dil}", qg, col, k_str[g], v_str[g], s_n // dil // SPAN)
                q_str.append((qg, col))
                lse_str.append(lse_g)
                outs.append(_from_strided(o_g, dil).reshape(n, cw))
                lses.append(_from_strided(lse_g, dil).reshape(n, N_HEADS))
            ob = _attn_mix("attn_mix", outs, lses)
            mix = carrying((layer, "o"), functools.partial(
                _mm, "o_proj", [(ob, (TM, cw), lambda m, _: (m, 0), wo_at(a), (cw, d), lambda m, _: (0, 0))],
                NN, (n // TM, 1), (n, d), F32, (TM, d), lambda m, _: (m, 0), 1))
            keep.update(q_str=q_str, lse_str=lse_str, outs=outs, lses=lses, ob=ob)
        r1, h1, h1b = _add_ln("ln_fwd", h, mix, vec(ln1_g, layer), vec(ln1_b, layer))
        gate = carrying((layer, "gate"), functools.partial(_mm_shard_out, "gate_up_proj", h1b, wg_at(layer), (), F32))
        up = carrying((layer, "up"), functools.partial(_mm_shard_out, "gate_up_proj", h1b, wu_at(layer), (), F32))
        g4, u4 = gate.reshape(N_CHIPS, b_n, s_n, f), up.reshape(N_CHIPS, b_n, s_n, f)
        hmid = _glu_fwd("glu_fwd", g4, u4, cw_l[layer], cb_l[layer]).reshape(N_CHIPS, n, f)
        ffn = carrying((layer, "down"), functools.partial(_mm_shard_in, "down_proj", hmid, wd_at(layer), ()))
        r2, h2, h2b = _add_ln("ln_fwd", h1, ffn, vec(ln2_g, layer), vec(ln2_b, layer))
        keep.update(r1=r1, h1b=h1b, g4=g4, u4=u4, hmid=hmid, r2=r2, h2b=h2b)
        saved.append(keep)
        if layer == N_POOL_LAYERS - 1:
            kv = _mm_cols("kv_proj", h2b, wkv_at(), (), F32, cb=kvb).reshape(b_n, s_n, -1)
            k_nat = _rope("k_rope", kv, 0, g_n, cos_l, sin_l, 1.0, MXU)
            v_nat = _rope("v_cast", kv, g_n, g_n, cos_l, sin_l, 1.0, MXU, rotate=False)
            k_str = [_to_strided(k_nat[:, :, g * cw:(g + 1) * cw], dil) for g, dil in enumerate(DILATIONS)]
            v_str = [_to_strided(v_nat[:, :, g * cw:(g + 1) * cw], dil) for g, dil in enumerate(DILATIONS)]
        h, hb = h2, h2b

    dy, loss_tile = _loss_head("loss_head", h, loss_target.reshape(n, d))

    pending = {}
    reduced = {}
    halves_of = lambda t: t.reshape(N_CHIPS, 2, t.shape[1] // 2, t.shape[2])
    on_ici = lambda key: F32 if key[0] in ("pool_w", "pool_scale", "ffn_conv_w") else MXU
    d_cw = [None] * DEPTH
    d_pw, d_ps = [None] * N_POOL_LAYERS, [None] * N_POOL_LAYERS
    d_ln = {}
    dk_str = dv_str = None
    d_top, top_scale, top_rest = dy, 1.0, []
    for layer in reversed(range(DEPTH)):
        sv = saved[layer]
        dr2, dr2b, d_ln["ln2", layer] = _ln_bwd("ln_bwd", sv["r2"], vec(ln2_g, layer), d_top, top_scale, top_rest)
        keys = list(pending)
        job = _sibling_job([pending[k] for k in keys], True) if keys else None
        dhmid = _mm_nt_shard_out("down_bwd", dr2b, wd_at(layer), (), job=job)
        if keys:
            dhmid, from_sibling = dhmid
            core_sums = [_add_halves("add_halves", pending[k], r_, half, on_ici(k)) for k, r_ in zip(keys, from_sibling)]
        d_wd = _mm_tn("down_dw", sv["hmid"], f, dr2b, d, N_CHIPS, (N_CHIPS, f, d), (None, f, d),
                      lambda j: (j, 0, 0), a_lead=lambda j: (j,))
        glu = _glu_bwd("glu_bwd", dhmid.reshape(N_CHIPS, b_n, s_n, f), sv["g4"], sv["u4"], cw_l[layer], cb_l[layer],
                       job=_owner_job(core_sums) if keys else None)
        dg4, du4, d_cw[layer] = glu[:3]
        if keys:
            owned = [_add_chips("add_chips", s_, r_, chip) for s_, r_ in zip(core_sums, glu[3])]
        dg, du = dg4.reshape(N_CHIPS, n, f), du4.reshape(N_CHIPS, n, f)
        dh1 = _mm_nt_shard_in("gate_up_bwd", [(dg, wg_at(layer), ()), (du, wu_at(layer), ())],
                              job=_sibling_job(owned, False) if keys else None)
        if keys:
            dh1, others = dh1
            reduced.update({k: pair for k, pair in zip(keys, zip(owned, others))})
        d_wg = _mm_tn("gate_up_dw", sv["h1b"], d, dg, f, N_CHIPS, (N_CHIPS, d, f), (None, d, f),
                      lambda j: (j, 0, 0), b_lead=lambda j: (j,))
        d_wu = _mm_tn("gate_up_dw", sv["h1b"], d, du, f, N_CHIPS, (N_CHIPS, d, f), (None, d, f),
                      lambda j: (j, 0, 0), b_lead=lambda j: (j,))
        pending = {("ffn_w_down", layer): halves_of(d_wd), ("ffn_w_gate", layer): halves_of(d_wg),
                   ("ffn_w_up", layer): halves_of(d_wu)}
        dr1, dr1b, d_ln["ln1", layer] = _ln_bwd("ln_bwd", sv["r1"], vec(ln1_g, layer), dr2, ALPHA, [dh1])
        if layer < N_POOL_LAYERS:
            d_in, d_pw[layer], d_ps[layer] = _pool_bwd("pool_bwd", sv["h"].reshape(b_n, s_n, d),
                                                       dr1.reshape(b_n, s_n, d), pw_nat[layer], ps_nat[layer])
            d_top, top_scale, top_rest = d_in.reshape(n, d), 1.0, []
        else:
            a = layer - N_POOL_LAYERS
            do = _mm("o_bwd", [(dr1b, (TM, d), lambda m, _: (m, 0), wo_at(a), (cw, d), lambda m, _: (0, 0))],
                     NT, (n // TM, 1), (n, cw), F32, (TM, cw), lambda m, _: (m, 0), 1)
            d_wo = _mm_tn("o_dw", sv["ob"], cw // N_CHIPS, dr1b, d, N_CHIPS, (N_CHIPS, cw // N_CHIPS, d),
                          (None, cw // N_CHIPS, d), lambda j: (j, 0, 0))
            pending["w_o", a] = halves_of(d_wo)
            mixed = _attn_mix_bwd("attn_mix_bwd", do, sv["outs"], sv["lses"])
            dq_nat, dk_new, dv_new = [], [], []
            for g, dil in enumerate(DILATIONS):
                do_g = _to_strided(mixed[g].reshape(b_n, s_n, cw), dil)
                dd_g = _to_strided(mixed[g_n + g].reshape(b_n, s_n, N_HEADS), dil)
                qg, col = sv["q_str"][g]
                dq_g, dk_g, dv_g = _attn_bwd(f"attn_bwd_d{dil}", qg, col, k_str[g], v_str[g], do_g, sv["lse_str"][g], dd_g,
                                             s_n // dil // SPAN, *((dk_str[g], dv_str[g]) if dk_str else ()))
                dq_nat.append(_from_strided(dq_g, dil))
                dk_new.append(dk_g)
                dv_new.append(dv_g)
            dk_str, dv_str = dk_new, dv_new
            dq = None
            for g in range(g_n):
                dq = _rope("q_rope_bwd", dq_nat[g], 0, 1, cos_l, -sin_l, HEAD_DIM ** -0.5, MXU, out_cols=g_n, out_col0=g, into=dq)
            dq = dq.reshape(n, g_n * cw)
            d_attn = _mm_nt_cols_in("q_bwd", dq, wq_at(a), (), qc)
            d_wq = _mm_tn("q_dw", sv["hb"], d, dq, qc, N_CHIPS, (N_CHIPS, d, qc), (None, d, qc), lambda j: (j, 0, 0))
            pending["w_q", a] = halves_of(d_wq)
            d_top, top_scale, top_rest = dr1, ALPHA, [d_attn]
            if a == 0:
                dkv = None
                for g, dil in enumerate(DILATIONS):
                    dkv = _rope("k_rope_bwd", _from_strided(dk_str[g], dil), 0, 1, cos_l, -sin_l, 1.0, MXU,
                                out_cols=2 * g_n, out_col0=g, into=dkv)
                for g, dil in enumerate(DILATIONS):
                    dkv = _rope("v_cast_bwd", _from_strided(dv_str[g], dil), 0, 1, cos_l, sin_l, 1.0, MXU, rotate=False,
                                out_cols=2 * g_n, out_col0=g_n + g, into=dkv)
                dkv = dkv.reshape(n, 2 * g_n * cw)
                h_kv = saved[N_POOL_LAYERS - 1]["h2b"]
                top_rest = top_rest + [_mm_nt_cols_in("kv_bwd", dkv, wkv_at(), (), kvb)]
                d_wkv = _mm_tn("kv_dw", h_kv, d, dkv, kvb, 2 * N_CHIPS, (2 * N_CHIPS, d, kvb), (None, d, kvb), lambda q: (q, 0, 0))
                pending["w_kv", 0] = halves_of(d_wkv.reshape(N_CHIPS, 2, d, kvb).transpose(0, 2, 1, 3).reshape(N_CHIPS, d, 2 * kvb))
    grad_x = d_top.reshape(b_n, s_n, d)

    d_pw_all = jnp.stack(d_pw).reshape(N_POOL_LAYERS, 4, N_CHIPS, 64, POOL_GROUP_DIM).transpose(2, 0, 1, 3, 4)
    d_ps_all = jnp.stack(d_ps).reshape(N_POOL_LAYERS, N_CHIPS, -1).transpose(1, 0, 2)
    d_cw_all = jnp.stack([t[:, :3] for t in d_cw], axis=1)
    for k, t in (("pool_w", d_pw_all), ("pool_scale", d_ps_all), ("ffn_conv_w", d_cw_all)):
        pending[k, 0] = t.reshape(N_CHIPS, *sharded[k][3])
    keys = list(pending)
    from_sibling = _run_job("grads_to_sibling", _sibling_job([pending[k] for k in keys], True))
    core_sums = [_add_halves("add_halves", pending[k], r_, half, on_ici(k)) for k, r_ in zip(keys, from_sibling)]
    from_chips = _run_job("grads_to_owner", _owner_job(core_sums))
    owned = [_add_chips("add_chips", s_, r_, chip) for s_, r_ in zip(core_sums, from_chips)]
    others = _run_job("halves_to_sibling", _sibling_job(owned, False))
    reduced.update({k: pair for k, pair in zip(keys, zip(owned, others))})

    out_grad, out_delta, out_m, out_v = {}, {}, {}, {}
    for k in names:
        w_, m_, v_, _ = sharded[k]
        parts = sorted(i for name_, i in reduced if name_ == k)
        shape = (len(parts), 2, *reduced[k, 0][0].shape)
        res = None
        for i in parts:
            res = _adam_halves(f"adam_{k}", *reduced[k, i], half, w_.reshape(shape), m_.reshape(shape), v_.reshape(shape), i, res)
        out_grad[k], out_delta[k], out_m[k], out_v[k] = (t.reshape(w_.shape) for t in res)

    d_cb = jnp.stack([t[:, 3] for t in d_cw], axis=0).reshape(DEPTH * N_CHIPS * f // d, d)
    ln_rows = jnp.concatenate([jnp.stack([d_ln[which, layer][row] for layer in range(DEPTH)])
                               for which, row in (("ln1", 0), ("ln1", 1), ("ln2", 0), ("ln2", 1))])
    rows = jnp.concatenate([ln_rows, d_cb, jnp.broadcast_to(loss_tile[0:1, 0:1], (1, d))])
    pad = (-rows.shape[0]) % 8
    total = _sum_all_devices("sum_small", jnp.pad(rows, ((0, pad), (0, 0))))
    small = {"ln1_g": total[0:4], "ln1_b": total[4:8], "ln2_g": total[8:12], "ln2_b": total[12:16],
             "ffn_conv_b": total[16:16 + d_cb.shape[0]].reshape(ffn_conv_b.shape)}
    loss = total[16 + d_cb.shape[0], 0]
    small_in = {"ln1_g": (ln1_g, m_ln1_g, v_ln1_g), "ln1_b": (ln1_b, m_ln1_b, v_ln1_b), "ln2_g": (ln2_g, m_ln2_g, v_ln2_g),
                "ln2_b": (ln2_b, m_ln2_b, v_ln2_b), "ffn_conv_b": (ffn_conv_b, m_ffn_conv_b, v_ffn_conv_b)}
    for k, (w_, m_, v_) in small_in.items():
        out_grad[k] = small[k]
        out_delta[k], out_m[k], out_v[k] = _adam_small(f"adam_{k}", small[k], w_, m_, v_)

    order = ["pool_w", "pool_scale", "w_q", "w_kv", "w_o", "ffn_w_gate", "ffn_w_up", "ffn_conv_w", "ffn_conv_b",
             "ffn_w_down", "ln1_g", "ln1_b", "ln2_g", "ln2_b"]
    return (loss, grad_x, *[out_grad[k] for k in order], *[out_delta[k] for k in order],
            *[out_m[k] for k in order], *[out_v[k] for k in order])
```

```python
import functools
import math

import jax
import jax.numpy as jnp
from jax import lax
from jax.experimental import pallas as pl
from jax.experimental.pallas import tpu as pltpu

F32 = jnp.float32
BF16 = jnp.bfloat16
MXU = jnp.bfloat16

DEPTH = 4
N_POOL_LAYERS = 2
POOL_WINDOWS = (2, 4, 8, 16)
POOL_GROUP_DIM = 256
HEAD_DIM = 64
N_HEADS = 16
DILATIONS = (1, 4, 16)
SPAN = 128
ROPE_THETA = 10000.0
ALPHA = (2.0 * DEPTH) ** 0.25
LN_EPS = 1e-5
ADAM_LR, ADAM_B1, ADAM_B2, ADAM_EPS, ADAM_WD, ADAM_STEP = 0.001, 0.9, 0.999, 1e-08, 0.01, 10

N_CHIPS = 4
VMEM_LIMIT = 56 * 1024 * 1024
MESH = pl.DeviceIdType.MESH

NN = (((1,), (0,)), ((), ()))
NT = (((1,), (1,)), ((), ()))
TN = (((0,), (0,)), ((), ()))


def _params(sem=None):
    return pltpu.CompilerParams(dimension_semantics=sem, vmem_limit_bytes=VMEM_LIMIT)


def _chunks(n_rows, ch, fn, unroll=1):
    def step(i, carry):
        fn(pl.multiple_of(i * ch, ch))
        return carry

    lax.fori_loop(0, n_rows // ch, step, 0, unroll=unroll)


def _fold8(v):
    return jnp.sum(v.reshape(v.shape[0] // 8, 8, v.shape[1]), axis=0)


def _down(v, k):
    return pltpu.roll(v, k, 0)


def _up(v, k):
    return pltpu.roll(v, v.shape[0] - k, 0)


def _own_refs(job, refs, n_in, n_out, n_scratch):
    if job is None:
        return list(refs), None
    a = n_in
    b = a + len(job["ins"])
    c = b + n_out
    e = c + len(job["outs"])
    g = e + n_scratch
    return list(refs[:a]) + list(refs[b:c]) + list(refs[e:g]), job["bind"](refs[a:b], refs[c:e], *refs[g:])


def _grid_ends(grid):
    ids = [pl.program_id(i) for i in range(len(grid))]
    both = lambda conds: functools.reduce(lambda p, q: p & q, conds)
    return both([i == 0 for i in ids]), both([i == g - 1 for i, g in zip(ids, grid)])


def _mm(name, pairs, dims, grid, out_shape, out_dtype, out_block, out_map, nk, into=None, job=None):
    n_pairs = len(pairs)
    kax = len(grid) - 1
    n_in = 2 * n_pairs + (1 if into is not None else 0)

    def body(*refs):
        refs, riding = _own_refs(job, refs, n_in, 1, 0)
        o_ref = refs[n_in]
        if riding is not None:
            at_first, at_last = _grid_ends(grid)
            pl.when(at_first)(riding.begin)
        part = None
        for p in range(n_pairs):
            t = lax.dot_general(refs[2 * p][...], refs[2 * p + 1][...], dims, preferred_element_type=F32)
            part = t if part is None else part + t
        if nk == 1:
            o_ref[...] = part.astype(o_ref.dtype)
        else:
            k = pl.program_id(kax)

            @pl.when(k == 0)
            def _():
                o_ref[...] = part

            @pl.when(k > 0)
            def _():
                o_ref[...] += part

        if riding is not None:
            pl.when(at_last)(riding.end)

    operands, in_specs = [], []
    for a, a_block, a_map, b, b_block, b_map in pairs:
        operands += [a, b]
        in_specs += [pl.BlockSpec(a_block, a_map), pl.BlockSpec(b_block, b_map)]
    aliases = {}
    if into is not None:
        operands.append(into)
        in_specs.append(pl.BlockSpec(memory_space=pl.ANY))
        aliases = {2 * n_pairs: 0}
    assert nk == 1 or out_dtype == F32
    j_ins, j_outs, j_scratch = (job["ins"], job["outs"], job["scratch"]) if job else ([], [], [])
    sem = ("arbitrary",) * len(grid) if job else ("parallel",) * kax + ("arbitrary",)
    res = pl.pallas_call(
        body, name=name, grid=grid, in_specs=in_specs + [pl.BlockSpec(memory_space=pl.ANY)] * len(j_ins),
        out_specs=[pl.BlockSpec(out_block, out_map)] + [pl.BlockSpec(memory_space=pl.ANY)] * len(j_outs),
        out_shape=[jax.ShapeDtypeStruct(out_shape, out_dtype)] + list(j_outs),
        input_output_aliases=aliases, scratch_shapes=list(j_scratch), compiler_params=_params(sem),
    )(*operands, *j_ins)
    return (res[0], res[1:]) if job else res[0]


TM = 2048


def _mm_cols(name, a, w, w_idx, out_dtype, cb=None, job=None):
    n, k = a.shape
    j_n, c = w.shape[0], w.shape[-1]
    cb = c if cb is None else cb
    s = c // cb
    wb = (None,) * (w.ndim - 2) + (k, cb)
    return _mm(name, [(a, (TM, k), lambda q, m, _: (m, 0), w, wb, lambda q, m, _: (q // s, *w_idx, 0, q % s))], NN,
               (j_n * s, n // TM, 1), (n, j_n * c), out_dtype, (TM, cb), lambda q, m, _: (m, q), 1, job=job)


def _mm_shard_out(name, a, w, w_idx, out_dtype, job=None):
    n, k = a.shape
    j_n, c = w.shape[0], w.shape[-1]
    wb = (None,) * (w.ndim - 2) + (k, c)
    return _mm(name, [(a, (TM, k), lambda j, m, _: (m, 0), w, wb, lambda j, m, _: (j, *w_idx, 0, 0))], NN,
               (j_n, n // TM, 1), (j_n, n, c), out_dtype, (None, TM, c), lambda j, m, _: (j, m, 0), 1, job=job)


def _mm_shard_in(name, a4, w, w_idx, job=None):
    j_n, n, c = a4.shape
    k = w.shape[-1]
    wb = (None,) * (w.ndim - 2) + (c, k)
    return _mm(name, [(a4, (None, TM, c), lambda m, j: (j, m, 0), w, wb, lambda m, j: (j, *w_idx, 0, 0))], NN,
               (n // TM, j_n), (n, k), F32, (TM, k), lambda m, j: (m, 0), j_n, job=job)


def _mm_nt_shard_out(name, a, w, w_idx, job=None):
    n, k = a.shape
    j_n, c = w.shape[0], w.shape[-2]
    wb = (None,) * (w.ndim - 2) + (c, k)
    return _mm(name, [(a, (TM, k), lambda j, m, _: (m, 0), w, wb, lambda j, m, _: (j, *w_idx, 0, 0))], NT,
               (j_n, n // TM, 1), (j_n, n, c), F32, (None, TM, c), lambda j, m, _: (j, m, 0), 1, job=job)


def _mm_nt_shard_in(name, terms, job=None):
    pairs = []
    for a4, w, w_idx in terms:
        j_n, n, c = a4.shape
        k = w.shape[-2]
        wb = (None,) * (w.ndim - 2) + (k, c)
        pairs.append((a4, (None, TM, c), lambda m, j: (j, m, 0), w, wb,
                      functools.partial(lambda m, j, w_idx: (j, *w_idx, 0, 0), w_idx=w_idx)))
    return _mm(name, pairs, NT, (n // TM, j_n), (n, k), F32, (TM, k), lambda m, j: (m, 0), j_n, job=job)


def _mm_nt_cols_in(name, a, w, w_idx, cb):
    n, ct = a.shape
    j_n, k, c = w.shape[0], w.shape[-2], w.shape[-1]
    s = c // cb
    wb = (None,) * (w.ndim - 2) + (k, cb)
    return _mm(name, [(a, (TM, cb), lambda m, q: (m, q), w, wb, lambda m, q: (q // s, *w_idx, 0, q % s))], NT,
               (n // TM, ct // cb), (n, k), F32, (TM, k), lambda m, q: (m, 0), ct // cb)


def _mm_tn(name, a, a_cols, b, b_cols, n_blocks, out_shape, out_block, out_map, into=None, a_lead=None, b_lead=None, job=None):
    n = a.shape[-2]
    a_nb = a.shape[-1] // a_cols
    b_nb = b.shape[-1] // b_cols
    if a_lead is None:
        a_block, a_map = (TM, a_cols), lambda q, t: (t, q if a_nb > 1 else 0)
    else:
        a_block, a_map = (None, TM, a_cols), lambda q, t: (*a_lead(q), t, 0)
    if b_lead is None:
        b_block, b_map = (TM, b_cols), lambda q, t: (t, q if b_nb > 1 else 0)
    else:
        b_block, b_map = (None, TM, b_cols), lambda q, t: (*b_lead(q), t, 0)
    return _mm(name, [(a, a_block, a_map, b, b_block, b_map)], TN, (n_blocks, n // TM), out_shape, F32,
               out_block, lambda q, t: out_map(q), n // TM, into=into, job=job)


LN_ROWS = 512
LN_CH = 16
LN_UNROLL = 4


def _ln_stats(r):
    mu = jnp.mean(r, -1, keepdims=True)
    xc = r - mu
    var = jnp.mean(xc * xc, -1, keepdims=True)
    return xc, lax.rsqrt(var + LN_EPS)


def _add_ln(name, a, mix, g, b):
    n, d = a.shape

    def body(a_ref, m_ref, g_ref, b_ref, r_ref, h_ref, hb_ref):
        gg, bb = g_ref[...], b_ref[...]

        def chunk(s):
            rows = pl.ds(s, LN_CH)
            r = ALPHA * a_ref[rows, :] + m_ref[rows, :]
            xc, rstd = _ln_stats(r)
            y = xc * rstd * gg + bb
            r_ref[rows, :] = r
            h_ref[rows, :] = y
            hb_ref[rows, :] = y.astype(MXU)

        _chunks(LN_ROWS, LN_CH, chunk, LN_UNROLL)

    tile = pl.BlockSpec((LN_ROWS, d), lambda i: (i, 0))
    vec = pl.BlockSpec((1, d), lambda i: (0, 0))
    return pl.pallas_call(
        body, name=name, grid=(n // LN_ROWS,), in_specs=[tile, tile, vec, vec], out_specs=[tile, tile, tile],
        out_shape=[jax.ShapeDtypeStruct((n, d), F32), jax.ShapeDtypeStruct((n, d), F32), jax.ShapeDtypeStruct((n, d), MXU)],
        compiler_params=_params(("parallel",)),
    )(a, mix, g, b)


def _ln_bwd(name, r, g, d_a, scale_a, d_rest):
    n, d = r.shape
    n_rest = len(d_rest)
    steps = n // LN_ROWS

    def body(*refs):
        r_ref, g_ref, da_ref = refs[:3]
        rest = refs[3:3 + n_rest]
        dr_ref, drb_ref, gb_ref, acc = refs[3 + n_rest:]
        i = pl.program_id(0)

        @pl.when(i == 0)
        def _():
            acc[...] = jnp.zeros_like(acc)

        gg = g_ref[...]

        def chunk(s):
            rows = pl.ds(s, LN_CH)
            xc, rstd = _ln_stats(r_ref[rows, :])
            xhat = xc * rstd
            dy = da_ref[rows, :] if scale_a == 1.0 else scale_a * da_ref[rows, :]
            for t in rest:
                dy = dy + t[rows, :]
            dyg = dy * gg
            m1 = jnp.mean(dyg, -1, keepdims=True)
            m2 = jnp.mean(dyg * xhat, -1, keepdims=True)
            dr = rstd * (dyg - m1 - xhat * m2)
            dr_ref[rows, :] = dr
            drb_ref[rows, :] = dr.astype(MXU)
            acc[0] += _fold8(dy * xhat)
            acc[1] += _fold8(dy)

        _chunks(LN_ROWS, LN_CH, chunk, LN_UNROLL)

        @pl.when(i == steps - 1)
        def _():
            gb_ref[0:1, :] = jnp.sum(acc[0], axis=0, keepdims=True)
            gb_ref[1:2, :] = jnp.sum(acc[1], axis=0, keepdims=True)

    tile = pl.BlockSpec((LN_ROWS, d), lambda i: (i, 0))
    vec = pl.BlockSpec((1, d), lambda i: (0, 0))
    return pl.pallas_call(
        body, name=name, grid=(steps,), in_specs=[tile, vec, tile] + [tile] * n_rest,
        out_specs=[tile, tile, pl.BlockSpec((2, d), lambda i: (0, 0))],
        out_shape=[jax.ShapeDtypeStruct((n, d), F32), jax.ShapeDtypeStruct((n, d), MXU), jax.ShapeDtypeStruct((2, d), F32)],
        scratch_shapes=[pltpu.VMEM((2, 8, d), F32)],
        compiler_params=_params(("arbitrary",)),
    )(r, g, d_a, *d_rest)


FFN_ROWS = 512
FFN_CH = 64
GELU_C1 = math.sqrt(2.0 / math.pi)
GELU_C2 = 0.044715


def _conv3(behind, n, w0, w1, w2, bias):
    v = behind[8:8 + n]
    g1 = _down(behind, 1)[8:8 + n]
    g2 = _down(behind, 2)[8:8 + n]
    return bias + w0 * g2 + w1 * g1 + w2 * v, v, g1, g2


def _glu_fwd(name, g4, u4, conv_w, conv_b):
    j_n, b_n, s_n, f = g4.shape
    tiles = s_n // FFN_ROWS

    def body(g_ref, halo_ref, u_ref, w_ref, b_ref, o_ref):
        s = pl.program_id(2)
        w0, w1, w2, bias = w_ref[0:1, :], w_ref[1:2, :], w_ref[2:3, :], b_ref[...]

        def work(st, behind):
            conv, _, _, _ = _conv3(behind, FFN_CH, w0, w1, w2, bias)
            cdf = 0.5 * jnp.tanh(conv * (GELU_C1 + (GELU_C1 * GELU_C2) * (conv * conv))) + 0.5
            o_ref[pl.ds(st, FFN_CH), :] = (conv * cdf * u_ref[pl.ds(st, FFN_CH), :]).astype(o_ref.dtype)

        work(0, jnp.concatenate([jnp.where(s > 0, halo_ref[...], 0.0), g_ref[0:FFN_CH, :]], axis=0))

        def step(i, carry):
            st = pl.multiple_of(i * FFN_CH, FFN_CH)
            work(st, g_ref[pl.ds(pl.multiple_of(st - 8, 8), FFN_CH + 8), :])
            return carry

        lax.fori_loop(1, FFN_ROWS // FFN_CH, step, 0)

    tile = pl.BlockSpec((None, None, FFN_ROWS, f), lambda j, b, s: (j, b, s, 0))
    halo = pl.BlockSpec((None, None, 8, f), lambda j, b, s: (j, b, jnp.maximum(s * (FFN_ROWS // 8) - 1, 0), 0))
    return pl.pallas_call(
        body, name=name, grid=(j_n, b_n, tiles),
        in_specs=[tile, halo, tile,
                  pl.BlockSpec((None, 3, f), lambda j, b, s: (j, 0, 0)),
                  pl.BlockSpec((None, 1, f), lambda j, b, s: (j, 0, 0))],
        out_specs=tile, out_shape=jax.ShapeDtypeStruct(g4.shape, MXU),
        compiler_params=_params(("parallel", "parallel", "parallel")),
    )(g4, g4, u4, conv_w, conv_b)


def _glu_bwd(name, dh4, g4, u4, conv_w, conv_b, job=None):
    j_n, b_n, s_n, f = g4.shape
    tiles = s_n // FFN_ROWS
    ext = FFN_CH + 8
    grid = (j_n, b_n, tiles)

    def body(*refs):
        refs, riding = _own_refs(job, refs, 9, 3, 1)
        (d_ref, dnext_ref, g_ref, gprev_ref, gnext_ref, u_ref, unext_ref, w_ref, b_ref,
         dg_ref, du_ref, wb_ref, acc) = refs
        b, s = pl.program_id(1), pl.program_id(2)
        last = s == tiles - 1
        if riding is not None:
            at_first, at_last = _grid_ends(grid)
            pl.when(at_first)(riding.begin)

        @pl.when((b == 0) & (s == 0))
        def _():
            acc[...] = jnp.zeros_like(acc)

        w0, w1, w2, bias = w_ref[0:1, :], w_ref[1:2, :], w_ref[2:3, :], b_ref[...]

        def work(st, behind, ue, de):
            conv, v, g1, g2 = _conv3(behind, ext, w0, w1, w2, bias)
            c2 = conv * conv
            th = jnp.tanh(conv * (GELU_C1 + (GELU_C1 * GELU_C2) * c2))
            cdf = 0.5 * th + 0.5
            dact = cdf + (conv * (1.0 - th * th)) * ((0.5 * GELU_C1) + (1.5 * GELU_C1 * GELU_C2) * c2)
            dconv = de * ue * dact
            du_ref[pl.ds(st, FFN_CH), :] = (de[:FFN_CH] * (conv[:FFN_CH] * cdf[:FFN_CH])).astype(du_ref.dtype)
            dg = w2 * dconv + w1 * _up(dconv, 1) + w0 * _up(dconv, 2)
            dg_ref[pl.ds(st, FFN_CH), :] = dg[:FFN_CH].astype(dg_ref.dtype)
            dc = dconv[:FFN_CH]
            acc[0] += _fold8(dc * g2[:FFN_CH])
            acc[1] += _fold8(dc * g1[:FFN_CH])
            acc[2] += _fold8(dc * v[:FFN_CH])
            acc[3] += _fold8(dc)

        rows = lambda ref, lo, hi: ref[lo:hi, :]
        top = FFN_ROWS - FFN_CH
        work(0, jnp.concatenate([jnp.where(s > 0, gprev_ref[...], 0.0), rows(g_ref, 0, ext)], axis=0),
             rows(u_ref, 0, ext), rows(d_ref, 0, ext))

        def step(i, carry):
            st = pl.multiple_of(i * FFN_CH, FFN_CH)
            work(st, g_ref[pl.ds(pl.multiple_of(st - 8, 8), ext + 8), :], u_ref[pl.ds(st, ext), :], d_ref[pl.ds(st, ext), :])
            return carry

        lax.fori_loop(1, FFN_ROWS // FFN_CH - 1, step, 0)
        work(top, jnp.concatenate([rows(g_ref, top - 8, FFN_ROWS), gnext_ref[...]], axis=0),
             jnp.concatenate([rows(u_ref, top, FFN_ROWS), unext_ref[...]], axis=0),
             jnp.concatenate([rows(d_ref, top, FFN_ROWS), jnp.where(last, 0.0, dnext_ref[...])], axis=0))

        @pl.when((b == b_n - 1) & last)
        def _():
            for k in range(4):
                wb_ref[k:k + 1, :] = jnp.sum(acc[k], axis=0, keepdims=True)

        if riding is not None:
            pl.when(at_last)(riding.end)

    blocks8 = FFN_ROWS // 8
    tile = pl.BlockSpec((None, None, FFN_ROWS, f), lambda j, b, s: (j, b, s, 0))
    prev = pl.BlockSpec((None, None, 8, f), lambda j, b, s: (j, b, jnp.maximum(s * blocks8 - 1, 0), 0))
    nxt = pl.BlockSpec((None, None, 8, f), lambda j, b, s: (j, b, jnp.minimum((s + 1) * blocks8, s_n // 8 - 1), 0))
    j_ins, j_outs, j_scratch = (job["ins"], job["outs"], job["scratch"]) if job else ([], [], [])
    res = pl.pallas_call(
        body, name=name, grid=grid,
        in_specs=[tile, nxt, tile, prev, nxt, tile, nxt,
                  pl.BlockSpec((None, 3, f), lambda j, b, s: (j, 0, 0)),
                  pl.BlockSpec((None, 1, f), lambda j, b, s: (j, 0, 0))] + [pl.BlockSpec(memory_space=pl.ANY)] * len(j_ins),
        out_specs=[tile, tile, pl.BlockSpec((None, 4, f), lambda j, b, s: (j, 0, 0))]
        + [pl.BlockSpec(memory_space=pl.ANY)] * len(j_outs),
        out_shape=[jax.ShapeDtypeStruct(g4.shape, MXU), jax.ShapeDtypeStruct(g4.shape, MXU),
                   jax.ShapeDtypeStruct((j_n, 4, f), F32)] + list(j_outs),
        scratch_shapes=[pltpu.VMEM((4, 8, f), F32)] + list(j_scratch),
        compiler_params=_params(("arbitrary",) * 3 if job else ("parallel", "arbitrary", "arbitrary")),
    )(dh4, dh4, g4, g4, g4, u4, u4, conv_w, conv_b, *j_ins)
    return (*res[:3], res[3:]) if job else res


POOL_ROWS = 512
POOL_CH = 32
POOL_HALO = 16


def _pool_windows(v, t0, gi, causal):
    shift = _down if causal else _up
    acc, k = v, 1
    while k < POOL_WINDOWS[gi]:
        acc = acc + shift(acc, k)
        k *= 2
    return acc


def _count(t0, n, w):
    t = t0 + lax.broadcasted_iota(jnp.int32, (n, 1), 0)
    return jnp.minimum(t + 1, w).astype(F32)


def _pooled_into(xs, pooled, t_tile):
    def chunk(st):
        for gi, w in enumerate(POOL_WINDOWS):
            cols = slice(gi * POOL_GROUP_DIM, (gi + 1) * POOL_GROUP_DIM)
            v = xs[pl.ds(st, POOL_CH + POOL_HALO), cols]
            sums = _pool_windows(v, None, gi, True)[POOL_HALO:]
            val = sums / _count(t_tile + st, POOL_CH, w) - v[POOL_HALO:]
            pooled[pl.ds(st, POOL_CH), cols] = val.astype(pooled.dtype)

    _chunks(POOL_ROWS, POOL_CH, chunk)


def _pool_specs(b_n, s_n, d):
    per = POOL_ROWS // POOL_HALO
    tile = pl.BlockSpec((None, POOL_ROWS, d), lambda b, s: (b, s, 0))
    prev = pl.BlockSpec((None, POOL_HALO, d), lambda b, s: (b, jnp.maximum(s * per - 1, 0), 0))
    nxt = pl.BlockSpec((None, POOL_HALO, d), lambda b, s: (b, jnp.minimum((s + 1) * per, s_n // POOL_HALO - 1), 0))
    return tile, prev, nxt


def _pool_fwd(name, h3, w, scale):
    b_n, s_n, d = h3.shape
    tile, prev, _ = _pool_specs(b_n, s_n, d)

    def body(h_ref, halo_ref, w_ref, sc_ref, o_ref, xs, pooled):
        s = pl.program_id(1)
        xs[0:POOL_HALO, :] = jnp.where(s > 0, halo_ref[...], 0.0)
        xs[POOL_HALO:, :] = h_ref[...]
        _pooled_into(xs, pooled, s * POOL_ROWS)
        for gi in range(len(POOL_WINDOWS)):
            cols = slice(gi * POOL_GROUP_DIM, (gi + 1) * POOL_GROUP_DIM)
            y = jnp.dot(pooled[:, cols], w_ref[gi], preferred_element_type=F32)
            o_ref[:, cols] = y * sc_ref[:, cols]

    return pl.pallas_call(
        body, name=name, grid=(b_n, s_n // POOL_ROWS),
        in_specs=[tile, prev, pl.BlockSpec(w.shape, lambda b, s: (0, 0, 0)), pl.BlockSpec((1, d), lambda b, s: (0, 0))],
        out_specs=tile, out_shape=jax.ShapeDtypeStruct(h3.shape, F32),
        scratch_shapes=[pltpu.VMEM((POOL_HALO + POOL_ROWS, d), F32), pltpu.VMEM((POOL_ROWS, d), MXU)],
        compiler_params=_params(("parallel", "parallel")),
    )(h3, h3, w, scale)


def _pool_bwd(name, h3, dm3, w, scale):
    b_n, s_n, d = h3.shape
    tile, prev, nxt = _pool_specs(b_n, s_n, d)
    tiles = s_n // POOL_ROWS
    ext = POOL_ROWS + POOL_HALO

    def body(h_ref, halo_ref, dm_ref, dnext_ref, w_ref, sc_ref, dh_ref, dw_ref, dsc_ref, xs, pooled, ds, dp):
        b, s = pl.program_id(0), pl.program_id(1)

        @pl.when((b == 0) & (s == 0))
        def _():
            dw_ref[...] = jnp.zeros_like(dw_ref)
            dsc_ref[...] = jnp.zeros_like(dsc_ref)

        xs[0:POOL_HALO, :] = jnp.where(s > 0, halo_ref[...], 0.0)
        xs[POOL_HALO:, :] = h_ref[...]
        ds[0:POOL_ROWS, :] = dm_ref[...]
        ds[POOL_ROWS:, :] = jnp.where(s == tiles - 1, 0.0, dnext_ref[...])
        _pooled_into(xs, pooled, s * POOL_ROWS)
        for gi in range(len(POOL_WINDOWS)):
            cols = slice(gi * POOL_GROUP_DIM, (gi + 1) * POOL_GROUP_DIM)
            dyb = (ds[:, cols] * sc_ref[:, cols]).astype(MXU)
            dp[:, cols] = lax.dot_general(dyb, w_ref[gi], NT, preferred_element_type=F32)
            pg = pooled[:, cols]
            dw_ref[gi] += lax.dot_general(pg, dyb[:POOL_ROWS], TN, preferred_element_type=F32)
            ypre = jnp.dot(pg, w_ref[gi], preferred_element_type=F32)
            dsc_ref[:, cols] += jnp.sum(ds[0:POOL_ROWS, cols] * ypre, axis=0, keepdims=True)

        def chunk(st):
            for gi, w_len in enumerate(POOL_WINDOWS):
                cols = slice(gi * POOL_GROUP_DIM, (gi + 1) * POOL_GROUP_DIM)
                v = dp[pl.ds(st, POOL_CH + POOL_HALO), cols]
                q = v / _count(s * POOL_ROWS + st, POOL_CH + POOL_HALO, w_len)
                back = _pool_windows(q, None, gi, False)[:POOL_CH] - v[:POOL_CH]
                dh_ref[pl.ds(st, POOL_CH), cols] = ALPHA * ds[pl.ds(st, POOL_CH), cols] + back

        _chunks(POOL_ROWS, POOL_CH, chunk)

    return pl.pallas_call(
        body, name=name, grid=(b_n, tiles),
        in_specs=[tile, prev, tile, nxt, pl.BlockSpec(w.shape, lambda b, s: (0, 0, 0)), pl.BlockSpec((1, d), lambda b, s: (0, 0))],
        out_specs=[tile, pl.BlockSpec(w.shape, lambda b, s: (0, 0, 0)), pl.BlockSpec((1, d), lambda b, s: (0, 0))],
        out_shape=[jax.ShapeDtypeStruct(h3.shape, F32), jax.ShapeDtypeStruct(w.shape, F32), jax.ShapeDtypeStruct((1, d), F32)],
        scratch_shapes=[pltpu.VMEM((POOL_HALO + POOL_ROWS, d), F32), pltpu.VMEM((POOL_ROWS, d), MXU),
                        pltpu.VMEM((ext, d), F32), pltpu.VMEM((ext, d), F32)],
        compiler_params=_params(("arbitrary", "arbitrary")),
    )(h3, h3, dm3, dm3, w, scale)


ROPE_ROWS = 256


def _rope_tables(s_n):
    inv_freq = ROPE_THETA ** (-jnp.arange(0, HEAD_DIM, 2, dtype=F32) / HEAD_DIM)
    ang = jnp.arange(s_n, dtype=F32)[:, None] * inv_freq[None, :]
    cos, sin = jnp.cos(ang), jnp.sin(ang)
    cos_l = jnp.tile(cos, (1, 4))
    sin_l = jnp.tile(jnp.concatenate([-sin, sin], axis=1), (1, 2))
    return cos_l, sin_l


def _rope(name, x3, col0, n_col, cos_l, sin_l, scale, out_dtype, rotate=True, out_cols=None, out_col0=0, into=None):
    b_n, s_n, _ = x3.shape
    cw = N_HEADS * HEAD_DIM
    out_cols = n_col if out_cols is None else out_cols

    def body(x_ref, c_ref, s_ref, *rest):
        o_ref = rest[-1]
        lane = lax.broadcasted_iota(jnp.int32, (ROPE_ROWS, 128), 1)
        first_half = (lane % HEAD_DIM) < (HEAD_DIM // 2)
        cos, sin = c_ref[...], s_ref[...]
        for cb in range(cw // 128):
            cols = slice(cb * 128, (cb + 1) * 128)
            y = x_ref[:, cols]
            if rotate:
                other = jnp.where(first_half, pltpu.roll(y, 128 - HEAD_DIM // 2, 1), pltpu.roll(y, HEAD_DIM // 2, 1))
                y = y * cos + other * sin
            o_ref[:, cols] = (y if scale == 1.0 else y * scale).astype(o_ref.dtype)

    tile = pl.BlockSpec((None, ROPE_ROWS, cw), lambda b, s, c: (b, s, col0 + c))
    tab = pl.BlockSpec((ROPE_ROWS, 128), lambda b, s, c: (s, 0))
    extra, extra_specs, aliases = [], [], {}
    if into is not None:
        extra, extra_specs, aliases = [into], [pl.BlockSpec(memory_space=pl.ANY)], {3: 0}
    return pl.pallas_call(
        body, name=name, grid=(b_n, s_n // ROPE_ROWS, n_col), in_specs=[tile, tab, tab] + extra_specs,
        out_specs=pl.BlockSpec((None, ROPE_ROWS, cw), lambda b, s, c: (b, s, out_col0 + c)),
        out_shape=jax.ShapeDtypeStruct((b_n, s_n, out_cols * cw), out_dtype), input_output_aliases=aliases,
        compiler_params=_params(("parallel", "parallel", "parallel")),
    )(x3, cos_l, sin_l, *extra)


def _to_strided(a, d):
    if d == 1:
        return a
    b_n, s_n, c = a.shape
    return a.reshape(b_n, s_n // d, d, c).transpose(0, 2, 1, 3).reshape(b_n, s_n, c)


def _from_strided(a, d):
    if d == 1:
        return a
    b_n, s_n, c = a.shape
    return a.reshape(b_n, d, s_n // d, c).transpose(0, 2, 1, 3).reshape(b_n, s_n, c)


def _attn_fwd(name, q, q_col, k, v, blocks_per_seq):
    b_n, s_n, cw = k.shape
    nb = s_n // SPAN
    with_prev = blocks_per_seq > 1

    keys = 2 * SPAN if with_prev else SPAN
    pair = 2 * HEAD_DIM

    def body(*refs):
        if with_prev:
            q_ref, kc_ref, kp_ref, vc_ref, vp_ref, o_ref, lse_ref, k_all, v_all, s_buf, m_buf = refs
            k_all[0:SPAN, :] = kp_ref[...]
            k_all[SPAN:, :] = kc_ref[...]
            v_all[0:SPAN, :] = vp_ref[...]
            v_all[SPAN:, :] = vc_ref[...]
        else:
            q_ref, k_all, v_all, o_ref, lse_ref, s_buf, m_buf = refs
        n = pl.program_id(1)
        qi = lax.broadcasted_iota(jnp.int32, (SPAN, keys), 0)
        kj = lax.broadcasted_iota(jnp.int32, (SPAN, keys), 1)
        if with_prev:
            back = jnp.where((n % blocks_per_seq) != 0, 0, 2 * SPAN)
            mask = ((kj < SPAN) & (kj >= qi + back)) | ((kj >= SPAN) & (kj - SPAN <= qi))
        else:
            mask = kj <= qi
        lane = lax.broadcasted_iota(jnp.int32, (SPAN, pair), 1)
        low = lane < HEAD_DIM
        ones = jnp.ones((keys, pair), MXU)
        for h in range(N_HEADS):
            grp = slice((h // 2) * pair, (h // 2 + 1) * pair)
            q2 = q_ref[:, grp]
            qh = jnp.where(low if h % 2 == 0 else ~low, q2, jnp.zeros_like(q2))
            s = jnp.where(mask, lax.dot_general(qh, k_all[:, grp], NT, preferred_element_type=F32), -jnp.inf)
            s_buf[h] = s
            m_buf[h] = jnp.broadcast_to(jnp.max(s, -1, keepdims=True), (SPAN, pair))
        lse_all = jnp.zeros((SPAN, pair), F32)
        for hp in range(N_HEADS // 2):
            grp = slice(hp * pair, (hp + 1) * pair)
            v2 = v_all[:, grp]
            halves = []
            for h in (2 * hp, 2 * hp + 1):
                m = m_buf[h]
                p = jnp.exp(s_buf[h] - jnp.tile(m, (1, keys // pair))).astype(MXU)
                tot = jnp.dot(p, ones, preferred_element_type=F32)
                halves.append(jnp.dot(p, v2, preferred_element_type=F32) / tot)
                lse_all = jnp.where(lane == h, m + jnp.log(tot), lse_all)
            o_ref[:, grp] = jnp.where(low, halves[0], halves[1])
        lse_ref[...] = lse_all[:, 0:N_HEADS]

    cur = lambda b, n: (b, n, 0)
    prv = lambda b, n: (b, jnp.maximum(n - 1, 0), 0)
    blk = (None, SPAN, cw)
    kv_specs = [pl.BlockSpec(blk, cur), pl.BlockSpec(blk, prv)] if with_prev else [pl.BlockSpec(blk, cur)]
    operands = [q, k, k, v, v] if with_prev else [q, k, v]
    stage = [pltpu.VMEM((keys, cw), MXU)] * 2 if with_prev else []
    return pl.pallas_call(
        body, name=name, grid=(b_n, nb),
        in_specs=[pl.BlockSpec(blk, lambda b, n: (b, n, q_col))] + kv_specs + kv_specs,
        out_specs=[pl.BlockSpec(blk, cur), pl.BlockSpec((None, SPAN, N_HEADS), cur)],
        out_shape=[jax.ShapeDtypeStruct((b_n, s_n, cw), F32), jax.ShapeDtypeStruct((b_n, s_n, N_HEADS), F32)],
        scratch_shapes=stage + [pltpu.VMEM((N_HEADS, SPAN, keys), F32), pltpu.VMEM((N_HEADS, SPAN, pair), F32)],
        compiler_params=_params(("parallel", "parallel")),
    )(*operands)


def _attn_bwd(name, q, q_col, k, v, do, lse, dd, blocks_per_seq, dk_prev=None, dv_prev=None):
    b_n, s_n, cw = k.shape
    nb = s_n // SPAN
    with_next = blocks_per_seq > 1
    accumulate = dk_prev is not None
    rows = 2 * SPAN if with_next else SPAN

    def body(*refs):
        refs = list(refs)
        qc_ref, doc_ref, lsec_ref, ddc_ref = refs[:4]
        del refs[:4]
        if with_next:
            qn_ref, don_ref, lsen_ref, ddn_ref = refs[:4]
            del refs[:4]
        k_ref, v_ref = refs[:2]
        del refs[:2]
        if accumulate:
            dkp_ref, dvp_ref = refs[:2]
            del refs[:2]
        dq_ref, dk_ref, dv_ref = refs[:3]
        del refs[:3]
        if with_next:
            carry, q_all, do_all, side = refs[:4]
            del refs[:4]
            q_all[0:SPAN, :] = qc_ref[...]
            q_all[SPAN:, :] = qn_ref[...]
            do_all[0:SPAN, :] = doc_ref[...]
            do_all[SPAN:, :] = don_ref[...]
            side[0, 0:SPAN, :] = lsec_ref[...]
            side[0, SPAN:, :] = lsen_ref[...]
            side[1, 0:SPAN, :] = ddc_ref[...]
            side[1, SPAN:, :] = ddn_ref[...]
            lse_at = lambda h: side[0, :, h:h + 1]
            dd_at = lambda h: side[1, :, h:h + 1]
        else:
            q_all, do_all = qc_ref, doc_ref
            lse_at = lambda h: lsec_ref[:, h:h + 1]
            dd_at = lambda h: ddc_ref[:, h:h + 1]
        p_buf, ds_buf = refs
        n = pl.program_id(1)
        qi = lax.broadcasted_iota(jnp.int32, (rows, SPAN), 0)
        kj = lax.broadcasted_iota(jnp.int32, (rows, SPAN), 1)
        if with_next:
            first = (n % blocks_per_seq) == 0
            reach = jnp.where(((n + 1) % blocks_per_seq) != 0, SPAN, -2 * SPAN)
            mask = ((qi < SPAN) & (kj <= qi)) | ((qi >= SPAN) & (kj >= qi - reach))
        else:
            mask = kj <= qi
        pair = 2 * HEAD_DIM
        low = lax.broadcasted_iota(jnp.int32, (rows, pair), 1) < HEAD_DIM
        low_k = lax.broadcasted_iota(jnp.int32, (SPAN, pair), 1) < HEAD_DIM

        def pick(v, h, low_mask):
            return jnp.where(low_mask if h % 2 == 0 else ~low_mask, v, jnp.zeros_like(v))

        for h in range(N_HEADS):
            grp = slice((h // 2) * pair, (h // 2 + 1) * pair)
            s = lax.dot_general(pick(q_all[:, grp], h, low), k_ref[:, grp], NT, preferred_element_type=F32)
            p = jnp.where(mask, jnp.exp(s - lse_at(h)), 0.0)
            dp = lax.dot_general(pick(do_all[:, grp], h, low), v_ref[:, grp], NT, preferred_element_type=F32)
            p_buf[h] = p.astype(MXU)
            ds_buf[h] = (p * (dp - dd_at(h))).astype(MXU)
        for hp in range(N_HEADS // 2):
            grp = slice(hp * pair, (hp + 1) * pair)
            q2, do2, k2 = q_all[:, grp], do_all[:, grp], k_ref[:, grp]
            dv = dk = dq2 = None
            for h in (2 * hp, 2 * hp + 1):
                t_dv = lax.dot_general(p_buf[h], pick(do2, h, low), TN, preferred_element_type=F32)
                t_dk = lax.dot_general(ds_buf[h], pick(q2, h, low), TN, preferred_element_type=F32)
                t_dq = jnp.dot(ds_buf[h], pick(k2, h, low_k), preferred_element_type=F32)
                dv = t_dv if dv is None else dv + t_dv
                dk = t_dk if dk is None else dk + t_dk
                dq2 = t_dq if dq2 is None else dq2 + t_dq
            if accumulate:
                dk = dk + dkp_ref[:, grp]
                dv = dv + dvp_ref[:, grp]
            dk_ref[:, grp] = dk
            dv_ref[:, grp] = dv
            if with_next:
                dq_ref[:, grp] = dq2[:SPAN] + jnp.where(first, 0.0, carry[:, grp])
                carry[:, grp] = dq2[SPAN:]
            else:
                dq_ref[:, grp] = dq2

    cur = lambda b, n: (b, n, 0)
    nxt = lambda b, n: (b, jnp.minimum(n + 1, nb - 1), 0)
    blk, hblk = (None, SPAN, cw), (None, SPAN, N_HEADS)
    q_specs = lambda m: [pl.BlockSpec(blk, lambda b, n: (*m(b, n)[:2], q_col)), pl.BlockSpec(blk, m),
                         pl.BlockSpec(hblk, m), pl.BlockSpec(hblk, m)]
    in_specs = q_specs(cur) + (q_specs(nxt) if with_next else []) + [pl.BlockSpec(blk, cur)] * (4 if accumulate else 2)
    operands = [q, do, lse, dd] * (2 if with_next else 1) + [k, v] + ([dk_prev, dv_prev] if accumulate else [])
    out = jax.ShapeDtypeStruct((b_n, s_n, cw), F32)
    return pl.pallas_call(
        body, name=name, grid=(b_n, nb), in_specs=in_specs, out_specs=[pl.BlockSpec(blk, cur)] * 3,
        out_shape=[out, out, out],
        scratch_shapes=([pltpu.VMEM((SPAN, cw), F32), pltpu.VMEM((rows, cw), MXU), pltpu.VMEM((rows, cw), MXU),
                         pltpu.VMEM((2, rows, N_HEADS), F32)] if with_next else [])
        + [pltpu.VMEM((N_HEADS, rows, SPAN), MXU)] * 2,
        compiler_params=_params(("parallel", "arbitrary")),
    )(*operands)


MIX_ROWS = 256


def _group_weights(ls):
    m = functools.reduce(jnp.maximum, ls)
    es = [jnp.exp(l - m) for l in ls]
    tot = functools.reduce(lambda a, b: a + b, es)
    return [e / tot for e in es]


def _attn_mix(name, outs, lses):
    n, cw = outs[0].shape
    g_n = len(outs)

    def body(*refs):
        o_refs, l_refs, out_ref = refs[:g_n], refs[g_n:2 * g_n], refs[2 * g_n]
        ws = _group_weights([r[...] for r in l_refs])
        for h in range(N_HEADS):
            cols = slice(h * HEAD_DIM, (h + 1) * HEAD_DIM)
            acc = None
            for g in range(g_n):
                t = ws[g][:, h:h + 1] * o_refs[g][:, cols]
                acc = t if acc is None else acc + t
            out_ref[:, cols] = acc.astype(out_ref.dtype)

    tile = pl.BlockSpec((MIX_ROWS, cw), lambda i: (i, 0))
    htile = pl.BlockSpec((MIX_ROWS, N_HEADS), lambda i: (i, 0))
    return pl.pallas_call(
        body, name=name, grid=(n // MIX_ROWS,), in_specs=[tile] * g_n + [htile] * g_n, out_specs=tile,
        out_shape=jax.ShapeDtypeStruct((n, cw), MXU), compiler_params=_params(("parallel",)),
    )(*outs, *lses)


def _attn_mix_bwd(name, do, outs, lses):
    n, cw = do.shape
    g_n = len(outs)

    def body(*refs):
        do_ref, o_refs, l_refs = refs[0], refs[1:1 + g_n], refs[1 + g_n:1 + 2 * g_n]
        dog_refs, dd_refs = refs[1 + 2 * g_n:1 + 3 * g_n], refs[1 + 3 * g_n:]
        ws = _group_weights([r[...] for r in l_refs])
        for h in range(N_HEADS):
            cols = slice(h * HEAD_DIM, (h + 1) * HEAD_DIM)
            dh = do_ref[:, cols]
            o = None
            for g in range(g_n):
                t = ws[g][:, h:h + 1] * o_refs[g][:, cols]
                o = t if o is None else o + t
            dot = jnp.sum(dh * o, -1, keepdims=True)
            for g in range(g_n):
                wg = ws[g][:, h:h + 1]
                dog_refs[g][:, cols] = (wg * dh).astype(dog_refs[g].dtype)
                dd_refs[g][:, h:h + 1] = wg * dot

    tile = pl.BlockSpec((MIX_ROWS, cw), lambda i: (i, 0))
    htile = pl.BlockSpec((MIX_ROWS, N_HEADS), lambda i: (i, 0))
    return pl.pallas_call(
        body, name=name, grid=(n // MIX_ROWS,), in_specs=[tile] * (1 + g_n) + [htile] * g_n,
        out_specs=[tile] * g_n + [htile] * g_n,
        out_shape=[jax.ShapeDtypeStruct((n, cw), MXU)] * g_n + [jax.ShapeDtypeStruct((n, N_HEADS), F32)] * g_n,
        compiler_params=_params(("parallel",)),
    )(do, *outs, *lses)


def _loss_head(name, y, target):
    n, d = y.shape
    steps = n // LN_ROWS

    def body(y_ref, t_ref, dy_ref, l_ref, acc):
        i = pl.program_id(0)

        @pl.when(i == 0)
        def _():
            acc[...] = jnp.zeros_like(acc)

        def chunk(s):
            rows = pl.ds(s, LN_CH)
            err = y_ref[rows, :] - t_ref[rows, :]
            dy_ref[rows, :] = err / d
            acc[...] += _fold8(err * err)

        _chunks(LN_ROWS, LN_CH, chunk, LN_UNROLL)

        @pl.when(i == steps - 1)
        def _():
            l_ref[...] = jnp.full((8, 128), 0.5 / d, F32) * jnp.sum(acc[...])

    tile = pl.BlockSpec((LN_ROWS, d), lambda i: (i, 0))
    return pl.pallas_call(
        body, name=name, grid=(steps,), in_specs=[tile, tile],
        out_specs=[tile, pl.BlockSpec((8, 128), lambda i: (0, 0))],
        out_shape=[jax.ShapeDtypeStruct((n, d), F32), jax.ShapeDtypeStruct((8, 128), F32)],
        scratch_shapes=[pltpu.VMEM((8, d), F32)], compiler_params=_params(("arbitrary",)),
    )(y, target)


EW_TILE_BYTES = 1 << 20


def _row_tile(rows, cols):
    tr = 8
    while rows % (2 * tr) == 0 and 2 * tr * cols * 4 <= EW_TILE_BYTES:
        tr *= 2
    return tr if rows % tr == 0 else rows


def _add_halves(name, grad, recv, half, out_dtype):
    j_n, _, r, c = grad.shape
    tr = _row_tile(r, c)

    def body(half_ref, g_ref, r_ref, o_ref):
        o_ref[...] = (g_ref[...] + r_ref[...]).astype(o_ref.dtype)

    return pl.pallas_call(
        body, name=name, out_shape=jax.ShapeDtypeStruct((j_n, r, c), out_dtype),
        grid_spec=pltpu.PrefetchScalarGridSpec(
            num_scalar_prefetch=1, grid=(j_n, r // tr),
            in_specs=[pl.BlockSpec((None, None, tr, c), lambda j, i, hf: (j, hf[0], i, 0)),
                      pl.BlockSpec((None, tr, c), lambda j, i, hf: (j, i, 0))],
            out_specs=pl.BlockSpec((None, tr, c), lambda j, i, hf: (j, i, 0))),
        compiler_params=_params(("parallel", "parallel")),
    )(half, grad, recv)


def _add_chips(name, mine, recv, chip):
    j_n, r, c = mine.shape
    tr = _row_tile(r, c)

    def body(chip_ref, m_ref, r_ref, o_ref):
        total = m_ref[...].astype(F32)
        for k in range(j_n - 1):
            total = total + r_ref[k].astype(F32)
        o_ref[...] = total

    return pl.pallas_call(
        body, name=name, out_shape=jax.ShapeDtypeStruct((r, c), F32),
        grid_spec=pltpu.PrefetchScalarGridSpec(
            num_scalar_prefetch=1, grid=(r // tr,),
            in_specs=[pl.BlockSpec((None, tr, c), lambda i, ch: (ch[0], i, 0)),
                      pl.BlockSpec((j_n - 1, tr, c), lambda i, ch: (0, i, 0))],
            out_specs=pl.BlockSpec((tr, c), lambda i, ch: (i, 0))),
        compiler_params=_params(("parallel",)),
    )(chip, mine, recv)


def _adam_math(w, g, m, v):
    m = ADAM_B1 * m + (1.0 - ADAM_B1) * g
    v = ADAM_B2 * v + (1.0 - ADAM_B2) * (g * g)
    m_hat = m / (1.0 - ADAM_B1 ** ADAM_STEP)
    v_hat = v / (1.0 - ADAM_B2 ** ADAM_STEP)
    delta = -ADAM_LR * (m_hat / (jnp.sqrt(v_hat) + ADAM_EPS) + ADAM_WD * w)
    return delta, m, v


def _adam_halves(name, own, other, half, w, m, v, part, into):
    _, _, r, c = w.shape
    tr = _row_tile(r, c)

    def body(half_ref, own_ref, oth_ref, w_ref, m_ref, v_ref, *rest):
        g_out, d_out, m_out, v_out = rest[-4:]
        g = jnp.where(pl.program_id(0) == half_ref[0], own_ref[...], oth_ref[...])
        delta, m_new, v_new = _adam_math(w_ref[...], g, m_ref[...], v_ref[...])
        g_out[...] = g
        d_out[...] = delta
        m_out[...] = m_new
        v_out[...] = v_new

    flat = pl.BlockSpec((tr, c), lambda h, i, hf: (i, 0))
    full = pl.BlockSpec((None, None, tr, c), lambda h, i, hf: (part, h, i, 0))
    out = jax.ShapeDtypeStruct(w.shape, F32)
    kept = [] if into is None else list(into)
    return pl.pallas_call(
        body, name=name, out_shape=[out] * 4,
        grid_spec=pltpu.PrefetchScalarGridSpec(
            num_scalar_prefetch=1, grid=(2, r // tr),
            in_specs=[flat, flat, full, full, full] + [pl.BlockSpec(memory_space=pl.ANY)] * len(kept), out_specs=[full] * 4),
        input_output_aliases={6 + i: i for i in range(len(kept))},
        compiler_params=_params(("parallel", "parallel")),
    )(half, own, other, w, m, v, *kept)


def _adam_small(name, g, w, m, v):
    def body(g_ref, w_ref, m_ref, v_ref, d_out, m_out, v_out):
        delta, m_new, v_new = _adam_math(w_ref[...], g_ref[...], m_ref[...], v_ref[...])
        d_out[...] = delta
        m_out[...] = m_new
        v_out[...] = v_new

    out = jax.ShapeDtypeStruct(w.shape, F32)
    return pl.pallas_call(body, name=name, out_shape=[out] * 3)(g, w, m, v)


def _place():
    x, y, c = lax.axis_index("x"), lax.axis_index("y"), lax.axis_index("c")
    chips = [(1 - x, y), (x, 1 - y), (1 - x, 1 - y)]
    return x, y, c, chips


ANY = pl.BlockSpec(memory_space=pl.ANY)


class _ShardGather:
    def __init__(self, ins, outs, send, recv, local):
        self.ins, self.outs, self.send, self.recv, self.local = ins, outs, send, recv, local
        self.n = len(ins)

    @staticmethod
    def scratch(n):
        return [pltpu.SemaphoreType.DMA((n, 6)), pltpu.SemaphoreType.DMA((n, 6)), pltpu.SemaphoreType.DMA((n,))]

    @staticmethod
    def out_shapes(shards):
        return [jax.ShapeDtypeStruct((N_CHIPS, *s.shape), s.dtype) for s in shards]

    def _copy(self, t, k, src, dst, to):
        return pltpu.make_async_remote_copy(src_ref=src, dst_ref=dst, send_sem=self.send.at[t, k], recv_sem=self.recv.at[t, k],
                                            device_id=to, device_id_type=MESH)

    def _own(self, t, me):
        return pltpu.make_async_copy(self.ins[t], self.outs[t].at[me], self.local.at[t])

    def _first(self, t, k, place):
        x, y, c, chips = place
        px, py = chips[k]
        return self._copy(t, k, self.ins[t].at[c], self.outs[t].at[2 * x + y, c], (px, py, c))

    def _passed_on(self, t, k, place, half):
        x, y, c, chips = place
        px, py = chips[k]
        slab = self.outs[t].at[2 * px + py, half]
        return self._copy(t, 3 + k, slab, slab, (x, y, 1 - c))

    def begin(self):
        place = _place()
        x, y, c, _ = place
        for t in range(self.n):
            self._own(t, 2 * x + y).start()
        for t in range(self.n):
            for k in range(N_CHIPS - 1):
                self._first(t, k, place).start()

    def end(self):
        place = _place()
        x, y, c, chips = place
        for t in range(self.n):
            for k, (px, py) in enumerate(chips):
                self._copy(t, k, self.ins[t].at[c], self.outs[t].at[2 * px + py, c], (px, py, c)).wait_recv()
                self._passed_on(t, k, place, c).start()
        for t in range(self.n):
            for k in range(N_CHIPS - 1):
                self._passed_on(t, k, place, 1 - c).wait_recv()
        for t in range(self.n):
            for k in range(N_CHIPS - 1):
                self._first(t, k, place).wait_send()
                self._passed_on(t, k, place, c).wait_send()
            self._own(t, 2 * x + y).wait()


class _Exchange:
    def __init__(self, make_copies, ins, outs, send, recv):
        self.copies = lambda: make_copies(ins, outs, send, recv)

    def begin(self):
        for cp in self.copies():
            cp.start()

    def end(self):
        for cp in self.copies():
            cp.wait()


def _gather_job(shards):
    return dict(ins=list(shards), outs=_ShardGather.out_shapes(shards), scratch=_ShardGather.scratch(len(shards)),
                bind=_ShardGather)


def _sibling_job(arrays, pick_other_half):
    n = len(arrays)

    def copies(ins, outs, send, recv):
        x, y, c, _ = _place()
        return [pltpu.make_async_remote_copy(
            src_ref=ins[t].at[:, 1 - c] if pick_other_half else ins[t], dst_ref=outs[t], send_sem=send.at[t],
            recv_sem=recv.at[t], device_id=(x, y, 1 - c), device_id_type=MESH) for t in range(n)]

    shapes = [(a.shape[0], *a.shape[2:]) if pick_other_half else a.shape for a in arrays]
    return dict(ins=list(arrays), outs=[jax.ShapeDtypeStruct(s, a.dtype) for s, a in zip(shapes, arrays)],
                scratch=[pltpu.SemaphoreType.DMA((n,)), pltpu.SemaphoreType.DMA((n,))],
                bind=functools.partial(_Exchange, copies))


def _owner_job(arrays):
    n = len(arrays)

    def copies(ins, outs, send, recv):
        x, y, c, chips = _place()
        return [pltpu.make_async_remote_copy(
            src_ref=ins[t].at[2 * px + py], dst_ref=outs[t].at[k], send_sem=send.at[t, k], recv_sem=recv.at[t, k],
            device_id=(px, py, c), device_id_type=MESH) for t in range(n) for k, (px, py) in enumerate(chips)]

    return dict(ins=list(arrays), outs=[jax.ShapeDtypeStruct((N_CHIPS - 1, *a.shape[1:]), a.dtype) for a in arrays],
                scratch=[pltpu.SemaphoreType.DMA((n, 3)), pltpu.SemaphoreType.DMA((n, 3))],
                bind=functools.partial(_Exchange, copies))


def _bound(job, refs):
    n_i, n_o = len(job["ins"]), len(job["outs"])
    return job["bind"](refs[:n_i], refs[n_i:n_i + n_o], *refs[n_i + n_o:])


def _run_job(name, job):
    def body(*refs):
        bound = _bound(job, refs)
        bound.begin()
        bound.end()

    return pl.pallas_call(
        body, name=name, in_specs=[ANY] * len(job["ins"]), out_specs=[ANY] * len(job["outs"]), out_shape=job["outs"],
        scratch_shapes=job["scratch"],
    )(*job["ins"])


def _sum_all_devices(name, part):
    r, c = part.shape

    def body(p_ref, o_ref, buf, send, recv):
        x, y, cc, _ = _place()
        me = 4 * x + 2 * y + cc
        copies = []
        for mask in range(1, 8):
            fx, fy, fc = (mask >> 2) & 1, (mask >> 1) & 1, mask & 1
            to = (x ^ fx, y ^ fy, cc ^ fc)
            copies.append((mask, pltpu.make_async_remote_copy(
                src_ref=p_ref, dst_ref=buf.at[me], send_sem=send.at[mask - 1], recv_sem=recv.at[mask - 1],
                device_id=to, device_id_type=MESH)))
            copies[-1][1].start()
        buf[me] = p_ref[...]
        for mask, cp in copies:
            pltpu.make_async_remote_copy(src_ref=p_ref, dst_ref=buf.at[me ^ mask], send_sem=send.at[mask - 1],
                                         recv_sem=recv.at[mask - 1], device_id=(x, y, cc), device_id_type=MESH).wait_recv()
        for _, cp in copies:
            cp.wait_send()
        total = buf[0]
        for d in range(1, 8):
            total = total + buf[d]
        o_ref[...] = total

    vm = pl.BlockSpec(memory_space=pltpu.VMEM)
    return pl.pallas_call(
        body, name=name, in_specs=[vm], out_specs=vm, out_shape=jax.ShapeDtypeStruct((r, c), F32),
        scratch_shapes=[pltpu.VMEM((8, r, c), F32), pltpu.SemaphoreType.DMA((7,)), pltpu.SemaphoreType.DMA((7,))],
    )(part)


def kernel(x, pool_w, pool_scale, w_q, w_kv, w_o, ffn_w_gate, ffn_w_up, ffn_conv_w, ffn_conv_b, ffn_w_down, ln1_g, ln1_b, ln2_g, ln2_b, loss_target, m_pool_w, m_pool_scale, m_w_q, m_w_kv, m_w_o, m_ffn_w_gate, m_ffn_w_up, m_ffn_conv_w, m_ffn_conv_b, m_ffn_w_down, m_ln1_g, m_ln1_b, m_ln2_g, m_ln2_b, v_pool_w, v_pool_scale, v_w_q, v_w_kv, v_w_o, v_ffn_w_gate, v_ffn_w_up, v_ffn_conv_w, v_ffn_conv_b, v_ffn_w_down, v_ln1_g, v_ln1_b, v_ln2_g, v_ln2_b):
    b_n, s_n, d = x.shape
    n = b_n * s_n
    f = ffn_w_gate.shape[-1]
    qc = w_q.shape[-1]
    kvb = w_kv.shape[-1] // 2
    n_attn = w_q.shape[0]
    g_n = len(DILATIONS)
    cw = N_HEADS * HEAD_DIM
    xi, yi, ci = lax.axis_index("x"), lax.axis_index("y"), lax.axis_index("c")
    half = jnp.reshape(ci, (1,)).astype(jnp.int32)
    chip = jnp.reshape(2 * xi + yi, (1,)).astype(jnp.int32)

    sharded = {
        "pool_w": (pool_w, m_pool_w, v_pool_w, (2, 4 * 64, POOL_GROUP_DIM)),
        "pool_scale": (pool_scale, m_pool_scale, v_pool_scale, (2, 1, pool_scale.shape[-1])),
        "w_q": (w_q, m_w_q, v_w_q, (2, d, qc)),
        "w_kv": (w_kv, m_w_kv, v_w_kv, (2, d // 2, w_kv.shape[-1])),
        "w_o": (w_o, m_w_o, v_w_o, (2, w_o.shape[1], d)),
        "ffn_w_gate": (ffn_w_gate, m_ffn_w_gate, v_ffn_w_gate, (2, 2 * d, f)),
        "ffn_w_up": (ffn_w_up, m_ffn_w_up, v_ffn_w_up, (2, 2 * d, f)),
        "ffn_conv_w": (ffn_conv_w, m_ffn_conv_w, v_ffn_conv_w, (2, 6, f)),
        "ffn_w_down": (ffn_w_down, m_ffn_w_down, v_ffn_w_down, (2, 2 * f, d)),
    }
    mxu_weights = ("pool_w", "w_q", "w_kv", "w_o", "ffn_w_gate", "ffn_w_up", "ffn_w_down")
    names = list(sharded)
    wo_rows = w_o.shape[1]
    shard_of = {("wkv", 0): w_kv.astype(MXU).reshape(2, d // 2, w_kv.shape[-1])}
    for i in range(DEPTH):
        shard_of["wg", i] = ffn_w_gate[i].astype(MXU).reshape(2, d // 2, f)
        shard_of["wu", i] = ffn_w_up[i].astype(MXU).reshape(2, d // 2, f)
        shard_of["wd", i] = ffn_w_down[i].astype(MXU).reshape(2, f // 2, d)
    for i in range(n_attn):
        shard_of["wq", i] = w_q[i].astype(MXU).reshape(2, d // 2, qc)
        shard_of["wo", i] = w_o[i].astype(MXU).reshape(2, wo_rows // 2, d)
    carried_by = {
        (0, "gate"): [("wg", 1)], (0, "up"): [("wu", 1)], (0, "down"): [("wd", 1)],
        (1, "gate"): [("wkv", 0)], (1, "up"): [("wq", 0), ("wo", 0)], (1, "down"): [("wg", 2)],
        (2, "q"): [("wu", 2)], (2, "o"): [("wd", 2)], (2, "gate"): [("wq", 1), ("wo", 1)], (2, "up"): [("wg", 3)],
        (2, "down"): [("wu", 3)], (3, "q"): [("wd", 3)],
    }
    got = {}

    def carrying(site, call, *args):
        keys = carried_by.get(site, [])
        if not keys:
            return call(*args)
        out, arrived = call(*args, job=_gather_job([shard_of[k] for k in keys]))
        got.update(zip(keys, arrived))
        return out

    first_keys = [("wg", 0), ("wu", 0), ("wd", 0)]
    first = _run_job("gather_weights", _gather_job([
        pool_w.astype(MXU).reshape(sharded["pool_w"][3]), pool_scale.reshape(sharded["pool_scale"][3]),
        ffn_conv_w.reshape(sharded["ffn_conv_w"][3])] + [shard_of[k] for k in first_keys]))
    got.update(zip(first_keys, first[3:]))
    wg_at = lambda i: got["wg", i].reshape(N_CHIPS, d, f)
    wu_at = lambda i: got["wu", i].reshape(N_CHIPS, d, f)
    wd_at = lambda i: got["wd", i].reshape(N_CHIPS, f, d)
    wq_at = lambda i: got["wq", i].reshape(N_CHIPS, d, qc)
    wo_at = lambda i: got["wo", i].reshape(cw, d)
    wkv_at = lambda: got["wkv", 0].reshape(N_CHIPS, d, w_kv.shape[-1])
    cw_all = first[2].reshape(N_CHIPS, DEPTH, 3, f)
    cw_l = [cw_all[:, i] for i in range(DEPTH)]
    cb_l = [ffn_conv_b[i].reshape(N_CHIPS, 1, f) for i in range(DEPTH)]
    pw_nat = first[0].reshape(N_CHIPS, N_POOL_LAYERS, 4, 64, POOL_GROUP_DIM).transpose(1, 2, 0, 3, 4).reshape(
        N_POOL_LAYERS, 4, POOL_GROUP_DIM, POOL_GROUP_DIM)
    ps_nat = first[1].reshape(N_CHIPS, N_POOL_LAYERS, -1).transpose(1, 0, 2).reshape(N_POOL_LAYERS, 1, d)
    cos_l, sin_l = _rope_tables(s_n)

    def vec(a, layer):
        return a[layer].reshape(1, d)

    h = x.reshape(n, d)
    hb = None
    saved = []
    k_str = v_str = None
    for layer in range(DEPTH):
        keep = {"h": h, "hb": hb}
        if layer < N_POOL_LAYERS:
            mix = _pool_fwd("pool_fwd", h.reshape(b_n, s_n, d), pw_nat[layer], ps_nat[layer]).reshape(n, d)
        else:
            a = layer - N_POOL_LAYERS
            q = carrying((layer, "q"), functools.partial(_mm_cols, "q_proj", hb, wq_at(a), (), F32))
            qr = _rope("q_rope", q.reshape(b_n, s_n, -1), 0, g_n, cos_l, sin_l, HEAD_DIM ** -0.5, MXU)
            q_str, lse_str, outs, lses = [], [], [], []
            for g, dil in enumerate(DILATIONS):
                if dil == 1:
                    qg, col = qr, g
                else:
                    qg, col = _to_strided(qr[:, :, g * cw:(g + 1) * cw], dil), 0
                o_g, lse_g = _attn_fwd(f"attn_fwd_d{dil}", qg, col, k_str[g], v_str[g], s_n // dil // SPAN)
                q_str.append((qg, col))
                lse_str.append(lse_g)
                outs.append(_from_strided(o_g, dil).reshape(n, cw))
                lses.append(_from_strided(lse_g, dil).reshape(n, N_HEADS))
            ob = _attn_mix("attn_mix", outs, lses)
            mix = carrying((layer, "o"), functools.partial(
                _mm, "o_proj", [(ob, (TM, cw), lambda m, _: (m, 0), wo_at(a), (cw, d), lambda m, _: (0, 0))],
                NN, (n // TM, 1), (n, d), F32, (TM, d), lambda m, _: (m, 0), 1))
            keep.update(q_str=q_str, lse_str=lse_str, outs=outs, lses=lses, ob=ob)
        r1, h1, h1b = _add_ln("ln_fwd", h, mix, vec(ln1_g, layer), vec(ln1_b, layer))
        gate = carrying((layer, "gate"), functools.partial(_mm_shard_out, "gate_up_proj", h1b, wg_at(layer), (), F32))
        up = carrying((layer, "up"), functools.partial(_mm_shard_out, "gate_up_proj", h1b, wu_at(layer), (), F32))
        g4, u4 = gate.reshape(N_CHIPS, b_n, s_n, f), up.reshape(N_CHIPS, b_n, s_n, f)
        hmid = _glu_fwd("glu_fwd", g4, u4, cw_l[layer], cb_l[layer]).reshape(N_CHIPS, n, f)
        ffn = carrying((layer, "down"), functools.partial(_mm_shard_in, "down_proj", hmid, wd_at(layer), ()))
        r2, h2, h2b = _add_ln("ln_fwd", h1, ffn, vec(ln2_g, layer), vec(ln2_b, layer))
        keep.update(r1=r1, h1b=h1b, g4=g4, u4=u4, hmid=hmid, r2=r2, h2b=h2b)
        saved.append(keep)
        if layer == N_POOL_LAYERS - 1:
            kv = _mm_cols("kv_proj", h2b, wkv_at(), (), F32, cb=kvb).reshape(b_n, s_n, -1)
            k_nat = _rope("k_rope", kv, 0, g_n, cos_l, sin_l, 1.0, MXU)
            v_nat = _rope("v_cast", kv, g_n, g_n, cos_l, sin_l, 1.0, MXU, rotate=False)
            k_str = [_to_strided(k_nat[:, :, g * cw:(g + 1) * cw], dil) for g, dil in enumerate(DILATIONS)]
            v_str = [_to_strided(v_nat[:, :, g * cw:(g + 1) * cw], dil) for g, dil in enumerate(DILATIONS)]
        h, hb = h2, h2b

    dy, loss_tile = _loss_head("loss_head", h, loss_target.reshape(n, d))

    pending = {}
    reduced = {}
    halves_of = lambda t: t.reshape(N_CHIPS, 2, t.shape[1] // 2, t.shape[2])
    on_ici = lambda key: F32 if key[0] in ("pool_w", "pool_scale", "ffn_conv_w") else MXU
    d_cw = [None] * DEPTH
    d_pw, d_ps = [None] * N_POOL_LAYERS, [None] * N_POOL_LAYERS
    d_ln = {}
    dk_str = dv_str = None
    d_top, top_scale, top_rest = dy, 1.0, []
    for layer in reversed(range(DEPTH)):
        sv = saved[layer]
        dr2, dr2b, d_ln["ln2", layer] = _ln_bwd("ln_bwd", sv["r2"], vec(ln2_g, layer), d_top, top_scale, top_rest)
        keys = list(pending)
        job = _sibling_job([pending[k] for k in keys], True) if keys else None
        dhmid = _mm_nt_shard_out("down_bwd", dr2b, wd_at(layer), (), job=job)
        if keys:
            dhmid, from_sibling = dhmid
            core_sums = [_add_halves("add_halves", pending[k], r_, half, on_ici(k)) for k, r_ in zip(keys, from_sibling)]
        d_wd = _mm_tn("down_dw", sv["hmid"], f, dr2b, d, N_CHIPS, (N_CHIPS, f, d), (None, f, d),
                      lambda j: (j, 0, 0), a_lead=lambda j: (j,))
        glu = _glu_bwd("glu_bwd", dhmid.reshape(N_CHIPS, b_n, s_n, f), sv["g4"], sv["u4"], cw_l[layer], cb_l[layer],
                       job=_owner_job(core_sums) if keys else None)
        dg4, du4, d_cw[layer] = glu[:3]
        if keys:
            owned = [_add_chips("add_chips", s_, r_, chip) for s_, r_ in zip(core_sums, glu[3])]
        dg, du = dg4.reshape(N_CHIPS, n, f), du4.reshape(N_CHIPS, n, f)
        dh1 = _mm_nt_shard_in("gate_up_bwd", [(dg, wg_at(layer), ()), (du, wu_at(layer), ())],
                              job=_sibling_job(owned, False) if keys else None)
        if keys:
            dh1, others = dh1
            reduced.update({k: pair for k, pair in zip(keys, zip(owned, others))})
        d_wg = _mm_tn("gate_up_dw", sv["h1b"], d, dg, f, N_CHIPS, (N_CHIPS, d, f), (None, d, f),
                      lambda j: (j, 0, 0), b_lead=lambda j: (j,))
        d_wu = _mm_tn("gate_up_dw", sv["h1b"], d, du, f, N_CHIPS, (N_CHIPS, d, f), (None, d, f),
                      lambda j: (j, 0, 0), b_lead=lambda j: (j,))
        pending = {("ffn_w_down", layer): halves_of(d_wd), ("ffn_w_gate", layer): halves_of(d_wg),
                   ("ffn_w_up", layer): halves_of(d_wu)}
        dr1, dr1b, d_ln["ln1", layer] = _ln_bwd("ln_bwd", sv["r1"], vec(ln1_g, layer), dr2, ALPHA, [dh1])
        if layer < N_POOL_LAYERS:
            d_in, d_pw[layer], d_ps[layer] = _pool_bwd("pool_bwd", sv["h"].reshape(b_n, s_n, d),
                                                       dr1.reshape(b_n, s_n, d), pw_nat[layer], ps_nat[layer])
            d_top, top_scale, top_rest = d_in.reshape(n, d), 1.0, []
        else:
            a = layer - N_POOL_LAYERS
            do = _mm("o_bwd", [(dr1b, (TM, d), lambda m, _: (m, 0), wo_at(a), (cw, d), lambda m, _: (0, 0))],
                     NT, (n // TM, 1), (n, cw), F32, (TM, cw), lambda m, _: (m, 0), 1)
            d_wo = _mm_tn("o_dw", sv["ob"], cw // N_CHIPS, dr1b, d, N_CHIPS, (N_CHIPS, cw // N_CHIPS, d),
                          (None, cw // N_CHIPS, d), lambda j: (j, 0, 0))
            pending["w_o", a] = halves_of(d_wo)
            mixed = _attn_mix_bwd("attn_mix_bwd", do, sv["outs"], sv["lses"])
            dq_nat, dk_new, dv_new = [], [], []
            for g, dil in enumerate(DILATIONS):
                do_g = _to_strided(mixed[g].reshape(b_n, s_n, cw), dil)
                dd_g = _to_strided(mixed[g_n + g].reshape(b_n, s_n, N_HEADS), dil)
                qg, col = sv["q_str"][g]
                dq_g, dk_g, dv_g = _attn_bwd(f"attn_bwd_d{dil}", qg, col, k_str[g], v_str[g], do_g, sv["lse_str"][g], dd_g,
                                             s_n // dil // SPAN, *((dk_str[g], dv_str[g]) if dk_str else ()))
                dq_nat.append(_from_strided(dq_g, dil))
                dk_new.append(dk_g)
                dv_new.append(dv_g)
            dk_str, dv_str = dk_new, dv_new
            dq = None
            for g in range(g_n):
                dq = _rope("q_rope_bwd", dq_nat[g], 0, 1, cos_l, -sin_l, HEAD_DIM ** -0.5, MXU, out_cols=g_n, out_col0=g, into=dq)
            dq = dq.reshape(n, g_n * cw)
            d_attn = _mm_nt_cols_in("q_bwd", dq, wq_at(a), (), qc)
            d_wq = _mm_tn("q_dw", sv["hb"], d, dq, qc, N_CHIPS, (N_CHIPS, d, qc), (None, d, qc), lambda j: (j, 0, 0))
            pending["w_q", a] = halves_of(d_wq)
            d_top, top_scale, top_rest = dr1, ALPHA, [d_attn]
            if a == 0:
                dkv = None
                for g, dil in enumerate(DILATIONS):
                    dkv = _rope("k_rope_bwd", _from_strided(dk_str[g], dil), 0, 1, cos_l, -sin_l, 1.0, MXU,
                                out_cols=2 * g_n, out_col0=g, into=dkv)
                for g, dil in enumerate(DILATIONS):
                    dkv = _rope("v_cast_bwd", _from_strided(dv_str[g], dil), 0, 1, cos_l, sin_l, 1.0, MXU, rotate=False,
                                out_cols=2 * g_n, out_col0=g_n + g, into=dkv)
                dkv = dkv.reshape(n, 2 * g_n * cw)
                h_kv = saved[N_POOL_LAYERS - 1]["h2b"]
                top_rest = top_rest + [_mm_nt_cols_in("kv_bwd", dkv, wkv_at(), (), kvb)]
                d_wkv = _mm_tn("kv_dw", h_kv, d, dkv, kvb, 2 * N_CHIPS, (N_CHIPS, d, 2 * kvb), (None, d, kvb),
                               lambda q: (q // 2, 0, q % 2))
                pending["w_kv", 0] = halves_of(d_wkv)
    grad_x = d_top.reshape(b_n, s_n, d)

    d_pw_all = jnp.stack(d_pw).reshape(N_POOL_LAYERS, 4, N_CHIPS, 64, POOL_GROUP_DIM).transpose(2, 0, 1, 3, 4)
    d_ps_all = jnp.stack(d_ps).reshape(N_POOL_LAYERS, N_CHIPS, -1).transpose(1, 0, 2)
    d_cw_all = jnp.stack([t[:, :3] for t in d_cw], axis=1)
    for k, t in (("pool_w", d_pw_all), ("pool_scale", d_ps_all), ("ffn_conv_w", d_cw_all)):
        pending[k, 0] = t.reshape(N_CHIPS, *sharded[k][3])
    keys = list(pending)
    from_sibling = _run_job("grads_to_sibling", _sibling_job([pending[k] for k in keys], True))
    core_sums = [_add_halves("add_halves", pending[k], r_, half, on_ici(k)) for k, r_ in zip(keys, from_sibling)]
    from_chips = _run_job("grads_to_owner", _owner_job(core_sums))
    owned = [_add_chips("add_chips", s_, r_, chip) for s_, r_ in zip(core_sums, from_chips)]
    others = _run_job("halves_to_sibling", _sibling_job(owned, False))
    reduced.update({k: pair for k, pair in zip(keys, zip(owned, others))})

    out_grad, out_delta, out_m, out_v = {}, {}, {}, {}
    for k in names:
        w_, m_, v_, _ = sharded[k]
        parts = sorted(i for name_, i in reduced if name_ == k)
        shape = (len(parts), 2, *reduced[k, 0][0].shape)
        res = None
        for i in parts:
            res = _adam_halves(f"adam_{k}", *reduced[k, i], half, w_.reshape(shape), m_.reshape(shape), v_.reshape(shape), i, res)
        out_grad[k], out_delta[k], out_m[k], out_v[k] = (t.reshape(w_.shape) for t in res)

    d_cb = jnp.stack([t[:, 3] for t in d_cw], axis=0).reshape(DEPTH * N_CHIPS * f // d, d)
    ln_rows = jnp.concatenate([jnp.stack([d_ln[which, layer][row] for layer in range(DEPTH)])
                               for which, row in (("ln1", 0), ("ln1", 1), ("ln2", 0), ("ln2", 1))])
    rows = jnp.concatenate([ln_rows, d_cb, jnp.broadcast_to(loss_tile[0:1, 0:1], (1, d))])
    pad = (-rows.shape[0]) % 8
    total = _sum_all_devices("sum_small", jnp.pad(rows, ((0, pad), (0, 0))))
    small = {"ln1_g": total[0:4], "ln1_b": total[4:8], "ln2_g": total[8:12], "ln2_b": total[12:16],
             "ffn_conv_b": total[16:16 + d_cb.shape[0]].reshape(ffn_conv_b.shape)}
    loss = total[16 + d_cb.shape[0], 0]
    small_in = {"ln1_g": (ln1_g, m_ln1_g, v_ln1_g), "ln1_b": (ln1_b, m_ln1_b, v_ln1_b), "ln2_g": (ln2_g, m_ln2_g, v_ln2_g),
                "ln2_b": (ln2_b, m_ln2_b, v_ln2_b), "ffn_conv_b": (ffn_conv_b, m_ffn_conv_b, v_ffn_conv_b)}
    for k, (w_, m_, v_) in small_in.items():
        out_grad[k] = small[k]
        out_delta[k], out_m[k], out_v[k] = _adam_small(f"adam_{k}", small[k], w_, m_, v_)

    order = ["pool_w", "pool_scale", "w_q", "w_kv", "w_o", "ffn_w_gate", "ffn_w_up", "ffn_conv_w", "ffn_conv_b",
             "ffn_w_down", "ln1_g", "ln1_b", "ln2_g", "ln2_b"]
    return (loss, grad_x, *[out_grad[k] for k in order], *[out_delta[k] for k in order],
            *[out_m[k] for k in order], *[out_v[k] for k in order])
```

```python
import functools
import math

import jax
import jax.numpy as jnp
from jax import lax
from jax.experimental import pallas as pl
from jax.experimental.pallas import tpu as pltpu

F32 = jnp.float32
BF16 = jnp.bfloat16
MXU = jnp.bfloat16

DEPTH = 4
N_POOL_LAYERS = 2
POOL_WINDOWS = (2, 4, 8, 16)
POOL_GROUP_DIM = 256
HEAD_DIM = 64
N_HEADS = 16
DILATIONS = (1, 4, 16)
SPAN = 128
ROPE_THETA = 10000.0
ALPHA = (2.0 * DEPTH) ** 0.25
LN_EPS = 1e-5
ADAM_LR, ADAM_B1, ADAM_B2, ADAM_EPS, ADAM_WD, ADAM_STEP = 0.001, 0.9, 0.999, 1e-08, 0.01, 10

N_CHIPS = 4
VMEM_LIMIT = 56 * 1024 * 1024
MESH = pl.DeviceIdType.MESH

NN = (((1,), (0,)), ((), ()))
NT = (((1,), (1,)), ((), ()))
TN = (((0,), (0,)), ((), ()))


def _params(sem=None):
    return pltpu.CompilerParams(dimension_semantics=sem, vmem_limit_bytes=VMEM_LIMIT)


def _chunks(n_rows, ch, fn, unroll=1):
    def step(i, carry):
        fn(pl.multiple_of(i * ch, ch))
        return carry

    lax.fori_loop(0, n_rows // ch, step, 0, unroll=unroll)


def _fold8(v):
    return jnp.sum(v.reshape(v.shape[0] // 8, 8, v.shape[1]), axis=0)


def _down(v, k):
    return pltpu.roll(v, k, 0)


def _up(v, k):
    return pltpu.roll(v, v.shape[0] - k, 0)


def _own_refs(job, refs, n_in, n_out, n_scratch):
    if job is None:
        return list(refs), None
    a = n_in
    b = a + len(job["ins"])
    c = b + n_out
    e = c + len(job["outs"])
    g = e + n_scratch
    return list(refs[:a]) + list(refs[b:c]) + list(refs[e:g]), job["bind"](refs[a:b], refs[c:e], *refs[g:])


def _grid_ends(grid):
    ids = [pl.program_id(i) for i in range(len(grid))]
    both = lambda conds: functools.reduce(lambda p, q: p & q, conds)
    return both([i == 0 for i in ids]), both([i == g - 1 for i, g in zip(ids, grid)])


def _mm(name, pairs, dims, grid, out_shape, out_dtype, out_block, out_map, nk, into=None, job=None):
    n_pairs = len(pairs)
    kax = len(grid) - 1
    n_in = 2 * n_pairs + (1 if into is not None else 0)

    def body(*refs):
        refs, riding = _own_refs(job, refs, n_in, 1, 0)
        o_ref = refs[n_in]
        if riding is not None:
            at_first, at_last = _grid_ends(grid)
            pl.when(at_first)(riding.begin)
        part = None
        for p in range(n_pairs):
            t = lax.dot_general(refs[2 * p][...], refs[2 * p + 1][...], dims, preferred_element_type=F32)
            part = t if part is None else part + t
        if nk == 1:
            o_ref[...] = part.astype(o_ref.dtype)
        else:
            k = pl.program_id(kax)

            @pl.when(k == 0)
            def _():
                o_ref[...] = part

            @pl.when(k > 0)
            def _():
                o_ref[...] += part

        if riding is not None:
            pl.when(at_last)(riding.end)

    operands, in_specs = [], []
    for a, a_block, a_map, b, b_block, b_map in pairs:
        operands += [a, b]
        in_specs += [pl.BlockSpec(a_block, a_map), pl.BlockSpec(b_block, b_map)]
    aliases = {}
    if into is not None:
        operands.append(into)
        in_specs.append(pl.BlockSpec(memory_space=pl.ANY))
        aliases = {2 * n_pairs: 0}
    assert nk == 1 or out_dtype == F32
    j_ins, j_outs, j_scratch = (job["ins"], job["outs"], job["scratch"]) if job else ([], [], [])
    sem = ("arbitrary",) * len(grid) if job else ("parallel",) * kax + ("arbitrary",)
    res = pl.pallas_call(
        body, name=name, grid=grid, in_specs=in_specs + [pl.BlockSpec(memory_space=pl.ANY)] * len(j_ins),
        out_specs=[pl.BlockSpec(out_block, out_map)] + [pl.BlockSpec(memory_space=pl.ANY)] * len(j_outs),
        out_shape=[jax.ShapeDtypeStruct(out_shape, out_dtype)] + list(j_outs),
        input_output_aliases=aliases, scratch_shapes=list(j_scratch), compiler_params=_params(sem),
    )(*operands, *j_ins)
    return (res[0], res[1:]) if job else res[0]


TM = 2048


def _mm_cols(name, a, w, w_idx, out_dtype, cb=None, job=None):
    n, k = a.shape
    j_n, c = w.shape[0], w.shape[-1]
    cb = c if cb is None else cb
    s = c // cb
    wb = (None,) * (w.ndim - 2) + (k, cb)
    return _mm(name, [(a, (TM, k), lambda q, m, _: (m, 0), w, wb, lambda q, m, _: (q // s, *w_idx, 0, q % s))], NN,
               (j_n * s, n // TM, 1), (n, j_n * c), out_dtype, (TM, cb), lambda q, m, _: (m, q), 1, job=job)


def _mm_shard_out(name, a, w, w_idx, out_dtype, job=None):
    n, k = a.shape
    j_n, c = w.shape[0], w.shape[-1]
    wb = (None,) * (w.ndim - 2) + (k, c)
    return _mm(name, [(a, (TM, k), lambda j, m, _: (m, 0), w, wb, lambda j, m, _: (j, *w_idx, 0, 0))], NN,
               (j_n, n // TM, 1), (j_n, n, c), out_dtype, (None, TM, c), lambda j, m, _: (j, m, 0), 1, job=job)


def _mm_shard_in(name, a4, w, w_idx, job=None):
    j_n, n, c = a4.shape
    k = w.shape[-1]
    wb = (None,) * (w.ndim - 2) + (c, k)
    return _mm(name, [(a4, (None, TM, c), lambda m, j: (j, m, 0), w, wb, lambda m, j: (j, *w_idx, 0, 0))], NN,
               (n // TM, j_n), (n, k), F32, (TM, k), lambda m, j: (m, 0), j_n, job=job)


def _mm_nt_shard_out(name, a, w, w_idx, job=None):
    n, k = a.shape
    j_n, c = w.shape[0], w.shape[-2]
    wb = (None,) * (w.ndim - 2) + (c, k)
    return _mm(name, [(a, (TM, k), lambda j, m, _: (m, 0), w, wb, lambda j, m, _: (j, *w_idx, 0, 0))], NT,
               (j_n, n // TM, 1), (j_n, n, c), F32, (None, TM, c), lambda j, m, _: (j, m, 0), 1, job=job)


def _mm_nt_shard_in(name, terms, job=None):
    pairs = []
    for a4, w, w_idx in terms:
        j_n, n, c = a4.shape
        k = w.shape[-2]
        wb = (None,) * (w.ndim - 2) + (k, c)
        pairs.append((a4, (None, TM, c), lambda m, j: (j, m, 0), w, wb,
                      functools.partial(lambda m, j, w_idx: (j, *w_idx, 0, 0), w_idx=w_idx)))
    return _mm(name, pairs, NT, (n // TM, j_n), (n, k), F32, (TM, k), lambda m, j: (m, 0), j_n, job=job)


def _mm_nt_cols_in(name, a, w, w_idx, cb):
    n, ct = a.shape
    j_n, k, c = w.shape[0], w.shape[-2], w.shape[-1]
    s = c // cb
    wb = (None,) * (w.ndim - 2) + (k, cb)
    return _mm(name, [(a, (TM, cb), lambda m, q: (m, q), w, wb, lambda m, q: (q // s, *w_idx, 0, q % s))], NT,
               (n // TM, ct // cb), (n, k), F32, (TM, k), lambda m, q: (m, 0), ct // cb)


def _mm_tn(name, a, a_cols, b, b_cols, n_blocks, out_shape, out_block, out_map, into=None, a_lead=None, b_lead=None, job=None):
    n = a.shape[-2]
    a_nb = a.shape[-1] // a_cols
    b_nb = b.shape[-1] // b_cols
    if a_lead is None:
        a_block, a_map = (TM, a_cols), lambda q, t: (t, q if a_nb > 1 else 0)
    else:
        a_block, a_map = (None, TM, a_cols), lambda q, t: (*a_lead(q), t, 0)
    if b_lead is None:
        b_block, b_map = (TM, b_cols), lambda q, t: (t, q if b_nb > 1 else 0)
    else:
        b_block, b_map = (None, TM, b_cols), lambda q, t: (*b_lead(q), t, 0)
    return _mm(name, [(a, a_block, a_map, b, b_block, b_map)], TN, (n_blocks, n // TM), out_shape, F32,
               out_block, lambda q, t: out_map(q), n // TM, into=into, job=job)


LN_ROWS = 512
LN_CH = 16
LN_UNROLL = 4


def _ln_stats(r):
    mu = jnp.mean(r, -1, keepdims=True)
    xc = r - mu
    var = jnp.mean(xc * xc, -1, keepdims=True)
    return xc, lax.rsqrt(var + LN_EPS)


def _add_ln(name, a, mix, g, b):
    n, d = a.shape

    def body(a_ref, m_ref, g_ref, b_ref, r_ref, h_ref, hb_ref):
        gg, bb = g_ref[...], b_ref[...]

        def chunk(s):
            rows = pl.ds(s, LN_CH)
            r = ALPHA * a_ref[rows, :] + m_ref[rows, :]
            xc, rstd = _ln_stats(r)
            y = xc * rstd * gg + bb
            r_ref[rows, :] = r
            h_ref[rows, :] = y
            hb_ref[rows, :] = y.astype(MXU)

        _chunks(LN_ROWS, LN_CH, chunk, LN_UNROLL)

    tile = pl.BlockSpec((LN_ROWS, d), lambda i: (i, 0))
    vec = pl.BlockSpec((1, d), lambda i: (0, 0))
    return pl.pallas_call(
        body, name=name, grid=(n // LN_ROWS,), in_specs=[tile, tile, vec, vec], out_specs=[tile, tile, tile],
        out_shape=[jax.ShapeDtypeStruct((n, d), F32), jax.ShapeDtypeStruct((n, d), F32), jax.ShapeDtypeStruct((n, d), MXU)],
        compiler_params=_params(("parallel",)),
    )(a, mix, g, b)


def _ln_bwd(name, r, g, d_a, scale_a, d_rest):
    n, d = r.shape
    n_rest = len(d_rest)
    steps = n // LN_ROWS

    def body(*refs):
        r_ref, g_ref, da_ref = refs[:3]
        rest = refs[3:3 + n_rest]
        dr_ref, drb_ref, gb_ref, acc = refs[3 + n_rest:]
        i = pl.program_id(0)

        @pl.when(i == 0)
        def _():
            acc[...] = jnp.zeros_like(acc)

        gg = g_ref[...]

        def chunk(s):
            rows = pl.ds(s, LN_CH)
            xc, rstd = _ln_stats(r_ref[rows, :])
            xhat = xc * rstd
            dy = da_ref[rows, :] if scale_a == 1.0 else scale_a * da_ref[rows, :]
            for t in rest:
                dy = dy + t[rows, :]
            dyg = dy * gg
            m1 = jnp.mean(dyg, -1, keepdims=True)
            m2 = jnp.mean(dyg * xhat, -1, keepdims=True)
            dr = rstd * (dyg - m1 - xhat * m2)
            dr_ref[rows, :] = dr
            drb_ref[rows, :] = dr.astype(MXU)
            acc[0] += _fold8(dy * xhat)
            acc[1] += _fold8(dy)

        _chunks(LN_ROWS, LN_CH, chunk, LN_UNROLL)

        @pl.when(i == steps - 1)
        def _():
            gb_ref[0:1, :] = jnp.sum(acc[0], axis=0, keepdims=True)
            gb_ref[1:2, :] = jnp.sum(acc[1], axis=0, keepdims=True)

    tile = pl.BlockSpec((LN_ROWS, d), lambda i: (i, 0))
    vec = pl.BlockSpec((1, d), lambda i: (0, 0))
    return pl.pallas_call(
        body, name=name, grid=(steps,), in_specs=[tile, vec, tile] + [tile] * n_rest,
        out_specs=[tile, tile, pl.BlockSpec((2, d), lambda i: (0, 0))],
        out_shape=[jax.ShapeDtypeStruct((n, d), F32), jax.ShapeDtypeStruct((n, d), MXU), jax.ShapeDtypeStruct((2, d), F32)],
        scratch_shapes=[pltpu.VMEM((2, 8, d), F32)],
        compiler_params=_params(("arbitrary",)),
    )(r, g, d_a, *d_rest)


FFN_ROWS = 512
FFN_CH = 64
GELU_C1 = math.sqrt(2.0 / math.pi)
GELU_C2 = 0.044715


def _conv3(behind, n, w0, w1, w2, bias):
    v = behind[8:8 + n]
    g1 = _down(behind, 1)[8:8 + n]
    g2 = _down(behind, 2)[8:8 + n]
    return bias + w0 * g2 + w1 * g1 + w2 * v, v, g1, g2


def _glu_fwd(name, g4, u4, conv_w, conv_b):
    j_n, b_n, s_n, f = g4.shape
    tiles = s_n // FFN_ROWS

    def body(g_ref, halo_ref, u_ref, w_ref, b_ref, o_ref):
        s = pl.program_id(2)
        w0, w1, w2, bias = w_ref[0:1, :], w_ref[1:2, :], w_ref[2:3, :], b_ref[...]

        def work(st, behind):
            conv, _, _, _ = _conv3(behind, FFN_CH, w0, w1, w2, bias)
            cdf = 0.5 * jnp.tanh(conv * (GELU_C1 + (GELU_C1 * GELU_C2) * (conv * conv))) + 0.5
            o_ref[pl.ds(st, FFN_CH), :] = (conv * cdf * u_ref[pl.ds(st, FFN_CH), :]).astype(o_ref.dtype)

        work(0, jnp.concatenate([jnp.where(s > 0, halo_ref[...], 0.0), g_ref[0:FFN_CH, :]], axis=0))

        def step(i, carry):
            st = pl.multiple_of(i * FFN_CH, FFN_CH)
            work(st, g_ref[pl.ds(pl.multiple_of(st - 8, 8), FFN_CH + 8), :])
            return carry

        lax.fori_loop(1, FFN_ROWS // FFN_CH, step, 0)

    tile = pl.BlockSpec((None, None, FFN_ROWS, f), lambda j, b, s: (j, b, s, 0))
    halo = pl.BlockSpec((None, None, 8, f), lambda j, b, s: (j, b, jnp.maximum(s * (FFN_ROWS // 8) - 1, 0), 0))
    return pl.pallas_call(
        body, name=name, grid=(j_n, b_n, tiles),
        in_specs=[tile, halo, tile,
                  pl.BlockSpec((None, 3, f), lambda j, b, s: (j, 0, 0)),
                  pl.BlockSpec((None, 1, f), lambda j, b, s: (j, 0, 0))],
        out_specs=tile, out_shape=jax.ShapeDtypeStruct(g4.shape, MXU),
        compiler_params=_params(("parallel", "parallel", "parallel")),
    )(g4, g4, u4, conv_w, conv_b)


def _glu_bwd(name, dh4, g4, u4, conv_w, conv_b, job=None):
    j_n, b_n, s_n, f = g4.shape
    tiles = s_n // FFN_ROWS
    ext = FFN_CH + 8
    grid = (j_n, b_n, tiles)

    def body(*refs):
        refs, riding = _own_refs(job, refs, 9, 3, 1)
        (d_ref, dnext_ref, g_ref, gprev_ref, gnext_ref, u_ref, unext_ref, w_ref, b_ref,
         dg_ref, du_ref, wb_ref, acc) = refs
        b, s = pl.program_id(1), pl.program_id(2)
        last = s == tiles - 1
        if riding is not None:
            at_first, at_last = _grid_ends(grid)
            pl.when(at_first)(riding.begin)

        @pl.when((b == 0) & (s == 0))
        def _():
            acc[...] = jnp.zeros_like(acc)

        w0, w1, w2, bias = w_ref[0:1, :], w_ref[1:2, :], w_ref[2:3, :], b_ref[...]

        def work(st, behind, ue, de):
            conv, v, g1, g2 = _conv3(behind, ext, w0, w1, w2, bias)
            c2 = conv * conv
            th = jnp.tanh(conv * (GELU_C1 + (GELU_C1 * GELU_C2) * c2))
            cdf = 0.5 * th + 0.5
            dact = cdf + (conv * (1.0 - th * th)) * ((0.5 * GELU_C1) + (1.5 * GELU_C1 * GELU_C2) * c2)
            dconv = de * ue * dact
            du_ref[pl.ds(st, FFN_CH), :] = (de[:FFN_CH] * (conv[:FFN_CH] * cdf[:FFN_CH])).astype(du_ref.dtype)
            dg = w2 * dconv + w1 * _up(dconv, 1) + w0 * _up(dconv, 2)
            dg_ref[pl.ds(st, FFN_CH), :] = dg[:FFN_CH].astype(dg_ref.dtype)
            dc = dconv[:FFN_CH]
            acc[0] += _fold8(dc * g2[:FFN_CH])
            acc[1] += _fold8(dc * g1[:FFN_CH])
            acc[2] += _fold8(dc * v[:FFN_CH])
            acc[3] += _fold8(dc)

        rows = lambda ref, lo, hi: ref[lo:hi, :]
        top = FFN_ROWS - FFN_CH
        work(0, jnp.concatenate([jnp.where(s > 0, gprev_ref[...], 0.0), rows(g_ref, 0, ext)], axis=0),
             rows(u_ref, 0, ext), rows(d_ref, 0, ext))

        def step(i, carry):
            st = pl.multiple_of(i * FFN_CH, FFN_CH)
            work(st, g_ref[pl.ds(pl.multiple_of(st - 8, 8), ext + 8), :], u_ref[pl.ds(st, ext), :], d_ref[pl.ds(st, ext), :])
            return carry

        lax.fori_loop(1, FFN_ROWS // FFN_CH - 1, step, 0)
        work(top, jnp.concatenate([rows(g_ref, top - 8, FFN_ROWS), gnext_ref[...]], axis=0),
             jnp.concatenate([rows(u_ref, top, FFN_ROWS), unext_ref[...]], axis=0),
             jnp.concatenate([rows(d_ref, top, FFN_ROWS), jnp.where(last, 0.0, dnext_ref[...])], axis=0))

        @pl.when((b == b_n - 1) & last)
        def _():
            for k in range(4):
                wb_ref[k:k + 1, :] = jnp.sum(acc[k], axis=0, keepdims=True)

        if riding is not None:
            pl.when(at_last)(riding.end)

    blocks8 = FFN_ROWS // 8
    tile = pl.BlockSpec((None, None, FFN_ROWS, f), lambda j, b, s: (j, b, s, 0))
    prev = pl.BlockSpec((None, None, 8, f), lambda j, b, s: (j, b, jnp.maximum(s * blocks8 - 1, 0), 0))
    nxt = pl.BlockSpec((None, None, 8, f), lambda j, b, s: (j, b, jnp.minimum((s + 1) * blocks8, s_n // 8 - 1), 0))
    j_ins, j_outs, j_scratch = (job["ins"], job["outs"], job["scratch"]) if job else ([], [], [])
    res = pl.pallas_call(
        body, name=name, grid=grid,
        in_specs=[tile, nxt, tile, prev, nxt, tile, nxt,
                  pl.BlockSpec((None, 3, f), lambda j, b, s: (j, 0, 0)),
                  pl.BlockSpec((None, 1, f), lambda j, b, s: (j, 0, 0))] + [pl.BlockSpec(memory_space=pl.ANY)] * len(j_ins),
        out_specs=[tile, tile, pl.BlockSpec((None, 4, f), lambda j, b, s: (j, 0, 0))]
        + [pl.BlockSpec(memory_space=pl.ANY)] * len(j_outs),
        out_shape=[jax.ShapeDtypeStruct(g4.shape, MXU), jax.ShapeDtypeStruct(g4.shape, MXU),
                   jax.ShapeDtypeStruct((j_n, 4, f), F32)] + list(j_outs),
        scratch_shapes=[pltpu.VMEM((4, 8, f), F32)] + list(j_scratch),
        compiler_params=_params(("arbitrary",) * 3 if job else ("parallel", "arbitrary", "arbitrary")),
    )(dh4, dh4, g4, g4, g4, u4, u4, conv_w, conv_b, *j_ins)
    return (*res[:3], res[3:]) if job else res


POOL_ROWS = 512
POOL_CH = 32
POOL_HALO = 16


def _pool_windows(v, t0, gi, causal):
    shift = _down if causal else _up
    acc, k = v, 1
    while k < POOL_WINDOWS[gi]:
        acc = acc + shift(acc, k)
        k *= 2
    return acc


def _count(t0, n, w):
    t = t0 + lax.broadcasted_iota(jnp.int32, (n, 1), 0)
    return jnp.minimum(t + 1, w).astype(F32)


def _pooled_into(xs, pooled, t_tile):
    def chunk(st):
        for gi, w in enumerate(POOL_WINDOWS):
            cols = slice(gi * POOL_GROUP_DIM, (gi + 1) * POOL_GROUP_DIM)
            v = xs[pl.ds(st, POOL_CH + POOL_HALO), cols]
            sums = _pool_windows(v, None, gi, True)[POOL_HALO:]
            val = sums / _count(t_tile + st, POOL_CH, w) - v[POOL_HALO:]
            pooled[pl.ds(st, POOL_CH), cols] = val.astype(pooled.dtype)

    _chunks(POOL_ROWS, POOL_CH, chunk)


def _pool_specs(b_n, s_n, d):
    per = POOL_ROWS // POOL_HALO
    tile = pl.BlockSpec((None, POOL_ROWS, d), lambda b, s: (b, s, 0))
    prev = pl.BlockSpec((None, POOL_HALO, d), lambda b, s: (b, jnp.maximum(s * per - 1, 0), 0))
    nxt = pl.BlockSpec((None, POOL_HALO, d), lambda b, s: (b, jnp.minimum((s + 1) * per, s_n // POOL_HALO - 1), 0))
    return tile, prev, nxt


def _pool_fwd(name, h3, w, scale):
    b_n, s_n, d = h3.shape
    tile, prev, _ = _pool_specs(b_n, s_n, d)

    def body(h_ref, halo_ref, w_ref, sc_ref, o_ref, xs, pooled):
        s = pl.program_id(1)
        xs[0:POOL_HALO, :] = jnp.where(s > 0, halo_ref[...], 0.0)
        xs[POOL_HALO:, :] = h_ref[...]
        _pooled_into(xs, pooled, s * POOL_ROWS)
        for gi in range(len(POOL_WINDOWS)):
            cols = slice(gi * POOL_GROUP_DIM, (gi + 1) * POOL_GROUP_DIM)
            y = jnp.dot(pooled[:, cols], w_ref[gi], preferred_element_type=F32)
            o_ref[:, cols] = y * sc_ref[:, cols]

    return pl.pallas_call(
        body, name=name, grid=(b_n, s_n // POOL_ROWS),
        in_specs=[tile, prev, pl.BlockSpec(w.shape, lambda b, s: (0, 0, 0)), pl.BlockSpec((1, d), lambda b, s: (0, 0))],
        out_specs=tile, out_shape=jax.ShapeDtypeStruct(h3.shape, F32),
        scratch_shapes=[pltpu.VMEM((POOL_HALO + POOL_ROWS, d), F32), pltpu.VMEM((POOL_ROWS, d), MXU)],
        compiler_params=_params(("parallel", "parallel")),
    )(h3, h3, w, scale)


def _pool_bwd(name, h3, dm3, w, scale):
    b_n, s_n, d = h3.shape
    tile, prev, nxt = _pool_specs(b_n, s_n, d)
    tiles = s_n // POOL_ROWS
    ext = POOL_ROWS + POOL_HALO

    def body(h_ref, halo_ref, dm_ref, dnext_ref, w_ref, sc_ref, dh_ref, dw_ref, dsc_ref, xs, pooled, ds, dp):
        b, s = pl.program_id(0), pl.program_id(1)

        @pl.when((b == 0) & (s == 0))
        def _():
            dw_ref[...] = jnp.zeros_like(dw_ref)
            dsc_ref[...] = jnp.zeros_like(dsc_ref)

        xs[0:POOL_HALO, :] = jnp.where(s > 0, halo_ref[...], 0.0)
        xs[POOL_HALO:, :] = h_ref[...]
        ds[0:POOL_ROWS, :] = dm_ref[...]
        ds[POOL_ROWS:, :] = jnp.where(s == tiles - 1, 0.0, dnext_ref[...])
        _pooled_into(xs, pooled, s * POOL_ROWS)
        for gi in range(len(POOL_WINDOWS)):
            cols = slice(gi * POOL_GROUP_DIM, (gi + 1) * POOL_GROUP_DIM)
            dyb = (ds[:, cols] * sc_ref[:, cols]).astype(MXU)
            dp[:, cols] = lax.dot_general(dyb, w_ref[gi], NT, preferred_element_type=F32)
            pg = pooled[:, cols]
            dw_ref[gi] += lax.dot_general(pg, dyb[:POOL_ROWS], TN, preferred_element_type=F32)
            ypre = jnp.dot(pg, w_ref[gi], preferred_element_type=F32)
            dsc_ref[:, cols] += jnp.sum(ds[0:POOL_ROWS, cols] * ypre, axis=0, keepdims=True)

        def chunk(st):
            for gi, w_len in enumerate(POOL_WINDOWS):
                cols = slice(gi * POOL_GROUP_DIM, (gi + 1) * POOL_GROUP_DIM)
                v = dp[pl.ds(st, POOL_CH + POOL_HALO), cols]
                q = v / _count(s * POOL_ROWS + st, POOL_CH + POOL_HALO, w_len)
                back = _pool_windows(q, None, gi, False)[:POOL_CH] - v[:POOL_CH]
                dh_ref[pl.ds(st, POOL_CH), cols] = ALPHA * ds[pl.ds(st, POOL_CH), cols] + back

        _chunks(POOL_ROWS, POOL_CH, chunk)

    return pl.pallas_call(
        body, name=name, grid=(b_n, tiles),
        in_specs=[tile, prev, tile, nxt, pl.BlockSpec(w.shape, lambda b, s: (0, 0, 0)), pl.BlockSpec((1, d), lambda b, s: (0, 0))],
        out_specs=[tile, pl.BlockSpec(w.shape, lambda b, s: (0, 0, 0)), pl.BlockSpec((1, d), lambda b, s: (0, 0))],
        out_shape=[jax.ShapeDtypeStruct(h3.shape, F32), jax.ShapeDtypeStruct(w.shape, F32), jax.ShapeDtypeStruct((1, d), F32)],
        scratch_shapes=[pltpu.VMEM((POOL_HALO + POOL_ROWS, d), F32), pltpu.VMEM((POOL_ROWS, d), MXU),
                        pltpu.VMEM((ext, d), F32), pltpu.VMEM((ext, d), F32)],
        compiler_params=_params(("arbitrary", "arbitrary")),
    )(h3, h3, dm3, dm3, w, scale)


ROPE_ROWS = 512


def _rope_tables(s_n):
    inv_freq = ROPE_THETA ** (-jnp.arange(0, HEAD_DIM, 2, dtype=F32) / HEAD_DIM)
    ang = jnp.arange(s_n, dtype=F32)[:, None] * inv_freq[None, :]
    cos, sin = jnp.cos(ang), jnp.sin(ang)
    cos_l = jnp.tile(cos, (1, 4))
    sin_l = jnp.tile(jnp.concatenate([-sin, sin], axis=1), (1, 2))
    return cos_l, sin_l


def _rope(name, x3, col0, n_col, cos_l, sin_l, scale, out_dtype, rotate=True, out_cols=None, out_col0=0, into=None):
    b_n, s_n, _ = x3.shape
    cw = N_HEADS * HEAD_DIM
    out_cols = n_col if out_cols is None else out_cols

    def body(x_ref, c_ref, s_ref, *rest):
        o_ref = rest[-1]
        lane = lax.broadcasted_iota(jnp.int32, (ROPE_ROWS, 128), 1)
        first_half = (lane % HEAD_DIM) < (HEAD_DIM // 2)
        cos, sin = c_ref[...], s_ref[...]
        for cb in range(cw // 128):
            cols = slice(cb * 128, (cb + 1) * 128)
            y = x_ref[:, cols]
            if rotate:
                other = jnp.where(first_half, pltpu.roll(y, 128 - HEAD_DIM // 2, 1), pltpu.roll(y, HEAD_DIM // 2, 1))
                y = y * cos + other * sin
            o_ref[:, cols] = (y if scale == 1.0 else y * scale).astype(o_ref.dtype)

    tile = pl.BlockSpec((None, ROPE_ROWS, cw), lambda b, s, c: (b, s, col0 + c))
    tab = pl.BlockSpec((ROPE_ROWS, 128), lambda b, s, c: (s, 0))
    extra, extra_specs, aliases = [], [], {}
    if into is not None:
        extra, extra_specs, aliases = [into], [pl.BlockSpec(memory_space=pl.ANY)], {3: 0}
    return pl.pallas_call(
        body, name=name, grid=(b_n, s_n // ROPE_ROWS, n_col), in_specs=[tile, tab, tab] + extra_specs,
        out_specs=pl.BlockSpec((None, ROPE_ROWS, cw), lambda b, s, c: (b, s, out_col0 + c)),
        out_shape=jax.ShapeDtypeStruct((b_n, s_n, out_cols * cw), out_dtype), input_output_aliases=aliases,
        compiler_params=_params(("parallel", "parallel", "parallel")),
    )(x3, cos_l, sin_l, *extra)


def _to_strided(a, d):
    if d == 1:
        return a
    b_n, s_n, c = a.shape
    return a.reshape(b_n, s_n // d, d, c).transpose(0, 2, 1, 3).reshape(b_n, s_n, c)


def _from_strided(a, d):
    if d == 1:
        return a
    b_n, s_n, c = a.shape
    return a.reshape(b_n, d, s_n // d, c).transpose(0, 2, 1, 3).reshape(b_n, s_n, c)


def _attn_fwd(name, q, q_col, k, v, blocks_per_seq):
    b_n, s_n, cw = k.shape
    nb = s_n // SPAN
    with_prev = blocks_per_seq > 1

    keys = 2 * SPAN if with_prev else SPAN
    pair = 2 * HEAD_DIM

    def body(*refs):
        if with_prev:
            q_ref, kc_ref, kp_ref, vc_ref, vp_ref, o_ref, lse_ref, k_all, v_all, s_buf, m_buf = refs
            k_all[0:SPAN, :] = kp_ref[...]
            k_all[SPAN:, :] = kc_ref[...]
            v_all[0:SPAN, :] = vp_ref[...]
            v_all[SPAN:, :] = vc_ref[...]
        else:
            q_ref, k_all, v_all, o_ref, lse_ref, s_buf, m_buf = refs
        n = pl.program_id(1)
        qi = lax.broadcasted_iota(jnp.int32, (SPAN, keys), 0)
        kj = lax.broadcasted_iota(jnp.int32, (SPAN, keys), 1)
        if with_prev:
            back = jnp.where((n % blocks_per_seq) != 0, 0, 2 * SPAN)
            mask = ((kj < SPAN) & (kj >= qi + back)) | ((kj >= SPAN) & (kj - SPAN <= qi))
        else:
            mask = kj <= qi
        lane = lax.broadcasted_iota(jnp.int32, (SPAN, pair), 1)
        low = lane < HEAD_DIM
        ones = jnp.ones((keys, pair), MXU)
        for h in range(N_HEADS):
            grp = slice((h // 2) * pair, (h // 2 + 1) * pair)
            q2 = q_ref[:, grp]
            qh = jnp.where(low if h % 2 == 0 else ~low, q2, jnp.zeros_like(q2))
            s = jnp.where(mask, lax.dot_general(qh, k_all[:, grp], NT, preferred_element_type=F32), -jnp.inf)
            s_buf[h] = s
            m_buf[h] = jnp.broadcast_to(jnp.max(s, -1, keepdims=True), (SPAN, pair))
        lse_all = jnp.zeros((SPAN, pair), F32)
        for hp in range(N_HEADS // 2):
            grp = slice(hp * pair, (hp + 1) * pair)
            v2 = v_all[:, grp]
            halves = []
            for h in (2 * hp, 2 * hp + 1):
                m = m_buf[h]
                p = jnp.exp(s_buf[h] - jnp.tile(m, (1, keys // pair))).astype(MXU)
                tot = jnp.dot(p, ones, preferred_element_type=F32)
                halves.append(jnp.dot(p, v2, preferred_element_type=F32) / tot)
                lse_all = jnp.where(lane == h, m + jnp.log(tot), lse_all)
            o_ref[:, grp] = jnp.where(low, halves[0], halves[1])
        lse_ref[...] = lse_all[:, 0:N_HEADS]

    cur = lambda b, n: (b, n, 0)
    prv = lambda b, n: (b, jnp.maximum(n - 1, 0), 0)
    blk = (None, SPAN, cw)
    kv_specs = [pl.BlockSpec(blk, cur), pl.BlockSpec(blk, prv)] if with_prev else [pl.BlockSpec(blk, cur)]
    operands = [q, k, k, v, v] if with_prev else [q, k, v]
    stage = [pltpu.VMEM((keys, cw), MXU)] * 2 if with_prev else []
    return pl.pallas_call(
        body, name=name, grid=(b_n, nb),
        in_specs=[pl.BlockSpec(blk, lambda b, n: (b, n, q_col))] + kv_specs + kv_specs,
        out_specs=[pl.BlockSpec(blk, cur), pl.BlockSpec((None, SPAN, N_HEADS), cur)],
        out_shape=[jax.ShapeDtypeStruct((b_n, s_n, cw), F32), jax.ShapeDtypeStruct((b_n, s_n, N_HEADS), F32)],
        scratch_shapes=stage + [pltpu.VMEM((N_HEADS, SPAN, keys), F32), pltpu.VMEM((N_HEADS, SPAN, pair), F32)],
        compiler_params=_params(("parallel", "parallel")),
    )(*operands)


def _attn_bwd(name, q, q_col, k, v, do, lse, dd, blocks_per_seq, dk_prev=None, dv_prev=None):
    b_n, s_n, cw = k.shape
    nb = s_n // SPAN
    with_next = blocks_per_seq > 1
    accumulate = dk_prev is not None
    rows = 2 * SPAN if with_next else SPAN

    def body(*refs):
        refs = list(refs)
        qc_ref, doc_ref, lsec_ref, ddc_ref = refs[:4]
        del refs[:4]
        if with_next:
            qn_ref, don_ref, lsen_ref, ddn_ref = refs[:4]
            del refs[:4]
        k_ref, v_ref = refs[:2]
        del refs[:2]
        if accumulate:
            dkp_ref, dvp_ref = refs[:2]
            del refs[:2]
        dq_ref, dk_ref, dv_ref = refs[:3]
        del refs[:3]
        if with_next:
            carry, q_all, do_all, side = refs[:4]
            del refs[:4]
            q_all[0:SPAN, :] = qc_ref[...]
            q_all[SPAN:, :] = qn_ref[...]
            do_all[0:SPAN, :] = doc_ref[...]
            do_all[SPAN:, :] = don_ref[...]
            side[0, 0:SPAN, :] = lsec_ref[...]
            side[0, SPAN:, :] = lsen_ref[...]
            side[1, 0:SPAN, :] = ddc_ref[...]
            side[1, SPAN:, :] = ddn_ref[...]
            lse_at = lambda h: side[0, :, h:h + 1]
            dd_at = lambda h: side[1, :, h:h + 1]
        else:
            q_all, do_all = qc_ref, doc_ref
            lse_at = lambda h: lsec_ref[:, h:h + 1]
            dd_at = lambda h: ddc_ref[:, h:h + 1]
        p_buf, ds_buf = refs
        n = pl.program_id(1)
        qi = lax.broadcasted_iota(jnp.int32, (rows, SPAN), 0)
        kj = lax.broadcasted_iota(jnp.int32, (rows, SPAN), 1)
        if with_next:
            first = (n % blocks_per_seq) == 0
            reach = jnp.where(((n + 1) % blocks_per_seq) != 0, SPAN, -2 * SPAN)
            mask = ((qi < SPAN) & (kj <= qi)) | ((qi >= SPAN) & (kj >= qi - reach))
        else:
            mask = kj <= qi
        pair = 2 * HEAD_DIM
        low = lax.broadcasted_iota(jnp.int32, (rows, pair), 1) < HEAD_DIM
        low_k = lax.broadcasted_iota(jnp.int32, (SPAN, pair), 1) < HEAD_DIM

        def pick(v, h, low_mask):
            return jnp.where(low_mask if h % 2 == 0 else ~low_mask, v, jnp.zeros_like(v))

        for h in range(N_HEADS):
            grp = slice((h // 2) * pair, (h // 2 + 1) * pair)
            s = lax.dot_general(pick(q_all[:, grp], h, low), k_ref[:, grp], NT, preferred_element_type=F32)
            p = jnp.where(mask, jnp.exp(s - lse_at(h)), 0.0)
            dp = lax.dot_general(pick(do_all[:, grp], h, low), v_ref[:, grp], NT, preferred_element_type=F32)
            p_buf[h] = p.astype(MXU)
            ds_buf[h] = (p * (dp - dd_at(h))).astype(MXU)
        for hp in range(N_HEADS // 2):
            grp = slice(hp * pair, (hp + 1) * pair)
            q2, do2, k2 = q_all[:, grp], do_all[:, grp], k_ref[:, grp]
            dv = dk = dq2 = None
            for h in (2 * hp, 2 * hp + 1):
                t_dv = lax.dot_general(p_buf[h], pick(do2, h, low), TN, preferred_element_type=F32)
                t_dk = lax.dot_general(ds_buf[h], pick(q2, h, low), TN, preferred_element_type=F32)
                t_dq = jnp.dot(ds_buf[h], pick(k2, h, low_k), preferred_element_type=F32)
                dv = t_dv if dv is None else dv + t_dv
                dk = t_dk if dk is None else dk + t_dk
                dq2 = t_dq if dq2 is None else dq2 + t_dq
            if accumulate:
                dk = dk + dkp_ref[:, grp]
                dv = dv + dvp_ref[:, grp]
            dk_ref[:, grp] = dk
            dv_ref[:, grp] = dv
            if with_next:
                dq_ref[:, grp] = dq2[:SPAN] + jnp.where(first, 0.0, carry[:, grp])
                carry[:, grp] = dq2[SPAN:]
            else:
                dq_ref[:, grp] = dq2

    cur = lambda b, n: (b, n, 0)
    nxt = lambda b, n: (b, jnp.minimum(n + 1, nb - 1), 0)
    blk, hblk = (None, SPAN, cw), (None, SPAN, N_HEADS)
    q_specs = lambda m: [pl.BlockSpec(blk, lambda b, n: (*m(b, n)[:2], q_col)), pl.BlockSpec(blk, m),
                         pl.BlockSpec(hblk, m), pl.BlockSpec(hblk, m)]
    in_specs = q_specs(cur) + (q_specs(nxt) if with_next else []) + [pl.BlockSpec(blk, cur)] * (4 if accumulate else 2)
    operands = [q, do, lse, dd] * (2 if with_next else 1) + [k, v] + ([dk_prev, dv_prev] if accumulate else [])
    out = jax.ShapeDtypeStruct((b_n, s_n, cw), F32)
    return pl.pallas_call(
        body, name=name, grid=(b_n, nb), in_specs=in_specs, out_specs=[pl.BlockSpec(blk, cur)] * 3,
        out_shape=[out, out, out],
        scratch_shapes=([pltpu.VMEM((SPAN, cw), F32), pltpu.VMEM((rows, cw), MXU), pltpu.VMEM((rows, cw), MXU),
                         pltpu.VMEM((2, rows, N_HEADS), F32)] if with_next else [])
        + [pltpu.VMEM((N_HEADS, rows, SPAN), MXU)] * 2,
        compiler_params=_params(("parallel", "arbitrary")),
    )(*operands)


MIX_ROWS = 256


def _group_weights(ls):
    m = functools.reduce(jnp.maximum, ls)
    es = [jnp.exp(l - m) for l in ls]
    tot = functools.reduce(lambda a, b: a + b, es)
    return [e / tot for e in es]


def _attn_mix(name, outs, lses):
    n, cw = outs[0].shape
    g_n = len(outs)

    def body(*refs):
        o_refs, l_refs, out_ref = refs[:g_n], refs[g_n:2 * g_n], refs[2 * g_n]
        ws = _group_weights([r[...] for r in l_refs])
        for h in range(N_HEADS):
            cols = slice(h * HEAD_DIM, (h + 1) * HEAD_DIM)
            acc = None
            for g in range(g_n):
                t = ws[g][:, h:h + 1] * o_refs[g][:, cols]
                acc = t if acc is None else acc + t
            out_ref[:, cols] = acc.astype(out_ref.dtype)

    tile = pl.BlockSpec((MIX_ROWS, cw), lambda i: (i, 0))
    htile = pl.BlockSpec((MIX_ROWS, N_HEADS), lambda i: (i, 0))
    return pl.pallas_call(
        body, name=name, grid=(n // MIX_ROWS,), in_specs=[tile] * g_n + [htile] * g_n, out_specs=tile,
        out_shape=jax.ShapeDtypeStruct((n, cw), MXU), compiler_params=_params(("parallel",)),
    )(*outs, *lses)


def _attn_mix_bwd(name, do, outs, lses):
    n, cw = do.shape
    g_n = len(outs)

    def body(*refs):
        do_ref, o_refs, l_refs = refs[0], refs[1:1 + g_n], refs[1 + g_n:1 + 2 * g_n]
        dog_refs, dd_refs = refs[1 + 2 * g_n:1 + 3 * g_n], refs[1 + 3 * g_n:]
        ws = _group_weights([r[...] for r in l_refs])
        for h in range(N_HEADS):
            cols = slice(h * HEAD_DIM, (h + 1) * HEAD_DIM)
            dh = do_ref[:, cols]
            o = None
            for g in range(g_n):
                t = ws[g][:, h:h + 1] * o_refs[g][:, cols]
                o = t if o is None else o + t
            dot = jnp.sum(dh * o, -1, keepdims=True)
            for g in range(g_n):
                wg = ws[g][:, h:h + 1]
                dog_refs[g][:, cols] = (wg * dh).astype(dog_refs[g].dtype)
                dd_refs[g][:, h:h + 1] = wg * dot

    tile = pl.BlockSpec((MIX_ROWS, cw), lambda i: (i, 0))
    htile = pl.BlockSpec((MIX_ROWS, N_HEADS), lambda i: (i, 0))
    return pl.pallas_call(
        body, name=name, grid=(n // MIX_ROWS,), in_specs=[tile] * (1 + g_n) + [htile] * g_n,
        out_specs=[tile] * g_n + [htile] * g_n,
        out_shape=[jax.ShapeDtypeStruct((n, cw), MXU)] * g_n + [jax.ShapeDtypeStruct((n, N_HEADS), F32)] * g_n,
        compiler_params=_params(("parallel",)),
    )(do, *outs, *lses)


def _loss_head(name, y, target):
    n, d = y.shape
    steps = n // LN_ROWS

    def body(y_ref, t_ref, dy_ref, l_ref, acc):
        i = pl.program_id(0)

        @pl.when(i == 0)
        def _():
            acc[...] = jnp.zeros_like(acc)

        def chunk(s):
            rows = pl.ds(s, LN_CH)
            err = y_ref[rows, :] - t_ref[rows, :]
            dy_ref[rows, :] = err / d
            acc[...] += _fold8(err * err)

        _chunks(LN_ROWS, LN_CH, chunk, LN_UNROLL)

        @pl.when(i == steps - 1)
        def _():
            l_ref[...] = jnp.full((8, 128), 0.5 / d, F32) * jnp.sum(acc[...])

    tile = pl.BlockSpec((LN_ROWS, d), lambda i: (i, 0))
    return pl.pallas_call(
        body, name=name, grid=(steps,), in_specs=[tile, tile],
        out_specs=[tile, pl.BlockSpec((8, 128), lambda i: (0, 0))],
        out_shape=[jax.ShapeDtypeStruct((n, d), F32), jax.ShapeDtypeStruct((8, 128), F32)],
        scratch_shapes=[pltpu.VMEM((8, d), F32)], compiler_params=_params(("arbitrary",)),
    )(y, target)


EW_TILE_BYTES = 1 << 20


def _row_tile(rows, cols):
    tr = 8
    while rows % (2 * tr) == 0 and 2 * tr * cols * 4 <= EW_TILE_BYTES:
        tr *= 2
    return tr if rows % tr == 0 else rows


def _add_halves(name, grad, recv, half, out_dtype):
    j_n, _, r, c = grad.shape
    tr = _row_tile(r, c)

    def body(half_ref, g_ref, r_ref, o_ref):
        o_ref[...] = (g_ref[...] + r_ref[...]).astype(o_ref.dtype)

    return pl.pallas_call(
        body, name=name, out_shape=jax.ShapeDtypeStruct((j_n, r, c), out_dtype),
        grid_spec=pltpu.PrefetchScalarGridSpec(
            num_scalar_prefetch=1, grid=(j_n, r // tr),
            in_specs=[pl.BlockSpec((None, None, tr, c), lambda j, i, hf: (j, hf[0], i, 0)),
                      pl.BlockSpec((None, tr, c), lambda j, i, hf: (j, i, 0))],
            out_specs=pl.BlockSpec((None, tr, c), lambda j, i, hf: (j, i, 0))),
        compiler_params=_params(("parallel", "parallel")),
    )(half, grad, recv)


def _add_chips(name, mine, recv, chip):
    j_n, r, c = mine.shape
    tr = _row_tile(r, c)

    def body(chip_ref, m_ref, r_ref, o_ref):
        total = m_ref[...].astype(F32)
        for k in range(j_n - 1):
            total = total + r_ref[k].astype(F32)
        o_ref[...] = total

    return pl.pallas_call(
        body, name=name, out_shape=jax.ShapeDtypeStruct((r, c), F32),
        grid_spec=pltpu.PrefetchScalarGridSpec(
            num_scalar_prefetch=1, grid=(r // tr,),
            in_specs=[pl.BlockSpec((None, tr, c), lambda i, ch: (ch[0], i, 0)),
                      pl.BlockSpec((j_n - 1, tr, c), lambda i, ch: (0, i, 0))],
            out_specs=pl.BlockSpec((tr, c), lambda i, ch: (i, 0))),
        compiler_params=_params(("parallel",)),
    )(chip, mine, recv)


def _adam_math(w, g, m, v):
    m = ADAM_B1 * m + (1.0 - ADAM_B1) * g
    v = ADAM_B2 * v + (1.0 - ADAM_B2) * (g * g)
    m_hat = m / (1.0 - ADAM_B1 ** ADAM_STEP)
    v_hat = v / (1.0 - ADAM_B2 ** ADAM_STEP)
    delta = -ADAM_LR * (m_hat / (jnp.sqrt(v_hat) + ADAM_EPS) + ADAM_WD * w)
    return delta, m, v


def _adam_halves(name, own, other, half, w, m, v, part, into):
    _, _, r, c = w.shape
    tr = _row_tile(r, c)

    def body(half_ref, own_ref, oth_ref, w_ref, m_ref, v_ref, *rest):
        g_out, d_out, m_out, v_out = rest[-4:]
        g = jnp.where(pl.program_id(0) == half_ref[0], own_ref[...], oth_ref[...])
        delta, m_new, v_new = _adam_math(w_ref[...], g, m_ref[...], v_ref[...])
        g_out[...] = g
        d_out[...] = delta
        m_out[...] = m_new
        v_out[...] = v_new

    flat = pl.BlockSpec((tr, c), lambda h, i, hf: (i, 0))
    full = pl.BlockSpec((None, None, tr, c), lambda h, i, hf: (part, h, i, 0))
    out = jax.ShapeDtypeStruct(w.shape, F32)
    kept = [] if into is None else list(into)
    return pl.pallas_call(
        body, name=name, out_shape=[out] * 4,
        grid_spec=pltpu.PrefetchScalarGridSpec(
            num_scalar_prefetch=1, grid=(2, r // tr),
            in_specs=[flat, flat, full, full, full] + [pl.BlockSpec(memory_space=pl.ANY)] * len(kept), out_specs=[full] * 4),
        input_output_aliases={6 + i: i for i in range(len(kept))},
        compiler_params=_params(("parallel", "parallel")),
    )(half, own, other, w, m, v, *kept)


def _adam_small(name, g, w, m, v):
    def body(g_ref, w_ref, m_ref, v_ref, d_out, m_out, v_out):
        delta, m_new, v_new = _adam_math(w_ref[...], g_ref[...], m_ref[...], v_ref[...])
        d_out[...] = delta
        m_out[...] = m_new
        v_out[...] = v_new

    out = jax.ShapeDtypeStruct(w.shape, F32)
    return pl.pallas_call(body, name=name, out_shape=[out] * 3)(g, w, m, v)


def _place():
    x, y, c = lax.axis_index("x"), lax.axis_index("y"), lax.axis_index("c")
    chips = [(1 - x, y), (x, 1 - y), (1 - x, 1 - y)]
    return x, y, c, chips


ANY = pl.BlockSpec(memory_space=pl.ANY)


class _ShardGather:
    def __init__(self, ins, outs, send, recv, local):
        self.ins, self.outs, self.send, self.recv, self.local = ins, outs, send, recv, local
        self.n = len(ins)

    @staticmethod
    def scratch(n):
        return [pltpu.SemaphoreType.DMA((n, 6)), pltpu.SemaphoreType.DMA((n, 6)), pltpu.SemaphoreType.DMA((n,))]

    @staticmethod
    def out_shapes(shards):
        return [jax.ShapeDtypeStruct((N_CHIPS, *s.shape), s.dtype) for s in shards]

    def _copy(self, t, k, src, dst, to):
        return pltpu.make_async_remote_copy(src_ref=src, dst_ref=dst, send_sem=self.send.at[t, k], recv_sem=self.recv.at[t, k],
                                            device_id=to, device_id_type=MESH)

    def _own(self, t, me):
        return pltpu.make_async_copy(self.ins[t], self.outs[t].at[me], self.local.at[t])

    def _first(self, t, k, place):
        x, y, c, chips = place
        px, py = chips[k]
        return self._copy(t, k, self.ins[t].at[c], self.outs[t].at[2 * x + y, c], (px, py, c))

    def _passed_on(self, t, k, place, half):
        x, y, c, chips = place
        px, py = chips[k]
        slab = self.outs[t].at[2 * px + py, half]
        return self._copy(t, 3 + k, slab, slab, (x, y, 1 - c))

    def begin(self):
        place = _place()
        x, y, c, _ = place
        for t in range(self.n):
            self._own(t, 2 * x + y).start()
        for t in range(self.n):
            for k in range(N_CHIPS - 1):
                self._first(t, k, place).start()

    def end(self):
        place = _place()
        x, y, c, chips = place
        for t in range(self.n):
            for k, (px, py) in enumerate(chips):
                self._copy(t, k, self.ins[t].at[c], self.outs[t].at[2 * px + py, c], (px, py, c)).wait_recv()
                self._passed_on(t, k, place, c).start()
        for t in range(self.n):
            for k in range(N_CHIPS - 1):
                self._passed_on(t, k, place, 1 - c).wait_recv()
        for t in range(self.n):
            for k in range(N_CHIPS - 1):
                self._first(t, k, place).wait_send()
                self._passed_on(t, k, place, c).wait_send()
            self._own(t, 2 * x + y).wait()


class _Exchange:
    def __init__(self, make_copies, ins, outs, send, recv):
        self.copies = lambda: make_copies(ins, outs, send, recv)

    def begin(self):
        for cp in self.copies():
            cp.start()

    def end(self):
        for cp in self.copies():
            cp.wait()


def _gather_job(shards):
    return dict(ins=list(shards), outs=_ShardGather.out_shapes(shards), scratch=_ShardGather.scratch(len(shards)),
                bind=_ShardGather)


def _sibling_job(arrays, pick_other_half):
    n = len(arrays)

    def copies(ins, outs, send, recv):
        x, y, c, _ = _place()
        return [pltpu.make_async_remote_copy(
            src_ref=ins[t].at[:, 1 - c] if pick_other_half else ins[t], dst_ref=outs[t], send_sem=send.at[t],
            recv_sem=recv.at[t], device_id=(x, y, 1 - c), device_id_type=MESH) for t in range(n)]

    shapes = [(a.shape[0], *a.shape[2:]) if pick_other_half else a.shape for a in arrays]
    return dict(ins=list(arrays), outs=[jax.ShapeDtypeStruct(s, a.dtype) for s, a in zip(shapes, arrays)],
                scratch=[pltpu.SemaphoreType.DMA((n,)), pltpu.SemaphoreType.DMA((n,))],
                bind=functools.partial(_Exchange, copies))


def _owner_job(arrays):
    n = len(arrays)

    def copies(ins, outs, send, recv):
        x, y, c, chips = _place()
        return [pltpu.make_async_remote_copy(
            src_ref=ins[t].at[2 * px + py], dst_ref=outs[t].at[k], send_sem=send.at[t, k], recv_sem=recv.at[t, k],
            device_id=(px, py, c), device_id_type=MESH) for t in range(n) for k, (px, py) in enumerate(chips)]

    return dict(ins=list(arrays), outs=[jax.ShapeDtypeStruct((N_CHIPS - 1, *a.shape[1:]), a.dtype) for a in arrays],
                scratch=[pltpu.SemaphoreType.DMA((n, 3)), pltpu.SemaphoreType.DMA((n, 3))],
                bind=functools.partial(_Exchange, copies))


def _bound(job, refs):
    n_i, n_o = len(job["ins"]), len(job["outs"])
    return job["bind"](refs[:n_i], refs[n_i:n_i + n_o], *refs[n_i + n_o:])


def _run_job(name, job):
    def body(*refs):
        bound = _bound(job, refs)
        bound.begin()
        bound.end()

    return pl.pallas_call(
        body, name=name, in_specs=[ANY] * len(job["ins"]), out_specs=[ANY] * len(job["outs"]), out_shape=job["outs"],
        scratch_shapes=job["scratch"],
    )(*job["ins"])


def _sum_all_devices(name, part):
    r, c = part.shape

    def body(p_ref, o_ref, buf, send, recv):
        x, y, cc, _ = _place()
        me = 4 * x + 2 * y + cc
        copies = []
        for mask in range(1, 8):
            fx, fy, fc = (mask >> 2) & 1, (mask >> 1) & 1, mask & 1
            to = (x ^ fx, y ^ fy, cc ^ fc)
            copies.append((mask, pltpu.make_async_remote_copy(
                src_ref=p_ref, dst_ref=buf.at[me], send_sem=send.at[mask - 1], recv_sem=recv.at[mask - 1],
                device_id=to, device_id_type=MESH)))
            copies[-1][1].start()
        buf[me] = p_ref[...]
        for mask, cp in copies:
            pltpu.make_async_remote_copy(src_ref=p_ref, dst_ref=buf.at[me ^ mask], send_sem=send.at[mask - 1],
                                         recv_sem=recv.at[mask - 1], device_id=(x, y, cc), device_id_type=MESH).wait_recv()
        for _, cp in copies:
            cp.wait_send()
        total = buf[0]
        for d in range(1, 8):
            total = total + buf[d]
        o_ref[...] = total

    vm = pl.BlockSpec(memory_space=pltpu.VMEM)
    return pl.pallas_call(
        body, name=name, in_specs=[vm], out_specs=vm, out_shape=jax.ShapeDtypeStruct((r, c), F32),
        scratch_shapes=[pltpu.VMEM((8, r, c), F32), pltpu.SemaphoreType.DMA((7,)), pltpu.SemaphoreType.DMA((7,))],
    )(part)


def kernel(x, pool_w, pool_scale, w_q, w_kv, w_o, ffn_w_gate, ffn_w_up, ffn_conv_w, ffn_conv_b, ffn_w_down, ln1_g, ln1_b, ln2_g, ln2_b, loss_target, m_pool_w, m_pool_scale, m_w_q, m_w_kv, m_w_o, m_ffn_w_gate, m_ffn_w_up, m_ffn_conv_w, m_ffn_conv_b, m_ffn_w_down, m_ln1_g, m_ln1_b, m_ln2_g, m_ln2_b, v_pool_w, v_pool_scale, v_w_q, v_w_kv, v_w_o, v_ffn_w_gate, v_ffn_w_up, v_ffn_conv_w, v_ffn_conv_b, v_ffn_w_down, v_ln1_g, v_ln1_b, v_ln2_g, v_ln2_b):
    b_n, s_n, d = x.shape
    n = b_n * s_n
    f = ffn_w_gate.shape[-1]
    qc = w_q.shape[-1]
    kvb = w_kv.shape[-1] // 2
    n_attn = w_q.shape[0]
    g_n = len(DILATIONS)
    cw = N_HEADS * HEAD_DIM
    xi, yi, ci = lax.axis_index("x"), lax.axis_index("y"), lax.axis_index("c")
    half = jnp.reshape(ci, (1,)).astype(jnp.int32)
    chip = jnp.reshape(2 * xi + yi, (1,)).astype(jnp.int32)

    sharded = {
        "pool_w": (pool_w, m_pool_w, v_pool_w, (2, 4 * 64, POOL_GROUP_DIM)),
        "pool_scale": (pool_scale, m_pool_scale, v_pool_scale, (2, 1, pool_scale.shape[-1])),
        "w_q": (w_q, m_w_q, v_w_q, (2, d, qc)),
        "w_kv": (w_kv, m_w_kv, v_w_kv, (2, d // 2, w_kv.shape[-1])),
        "w_o": (w_o, m_w_o, v_w_o, (2, w_o.shape[1], d)),
        "ffn_w_gate": (ffn_w_gate, m_ffn_w_gate, v_ffn_w_gate, (2, 2 * d, f)),
        "ffn_w_up": (ffn_w_up, m_ffn_w_up, v_ffn_w_up, (2, 2 * d, f)),
        "ffn_conv_w": (ffn_conv_w, m_ffn_conv_w, v_ffn_conv_w, (2, 6, f)),
        "ffn_w_down": (ffn_w_down, m_ffn_w_down, v_ffn_w_down, (2, 2 * f, d)),
    }
    mxu_weights = ("pool_w", "w_q", "w_kv", "w_o", "ffn_w_gate", "ffn_w_up", "ffn_w_down")
    names = list(sharded)
    wo_rows = w_o.shape[1]
    shard_of = {("wkv", 0): w_kv.astype(MXU).reshape(2, d // 2, w_kv.shape[-1])}
    for i in range(DEPTH):
        shard_of["wg", i] = ffn_w_gate[i].astype(MXU).reshape(2, d // 2, f)
        shard_of["wu", i] = ffn_w_up[i].astype(MXU).reshape(2, d // 2, f)
        shard_of["wd", i] = ffn_w_down[i].astype(MXU).reshape(2, f // 2, d)
    for i in range(n_attn):
        shard_of["wq", i] = w_q[i].astype(MXU).reshape(2, d // 2, qc)
        shard_of["wo", i] = w_o[i].astype(MXU).reshape(2, wo_rows // 2, d)
    carried_by = {
        (0, "gate"): [("wg", 1)], (0, "up"): [("wu", 1)], (0, "down"): [("wd", 1)],
        (1, "gate"): [("wkv", 0)], (1, "up"): [("wq", 0), ("wo", 0)], (1, "down"): [("wg", 2)],
        (2, "q"): [("wu", 2)], (2, "o"): [("wd", 2)], (2, "gate"): [("wq", 1), ("wo", 1)], (2, "up"): [("wg", 3)],
        (2, "down"): [("wu", 3)], (3, "q"): [("wd", 3)],
    }
    got = {}

    def carrying(site, call, *args):
        keys = carried_by.get(site, [])
        if not keys:
            return call(*args)
        out, arrived = call(*args, job=_gather_job([shard_of[k] for k in keys]))
        got.update(zip(keys, arrived))
        return out

    first_keys = [("wg", 0), ("wu", 0), ("wd", 0)]
    first = _run_job("gather_weights", _gather_job([
        pool_w.astype(MXU).reshape(sharded["pool_w"][3]), pool_scale.reshape(sharded["pool_scale"][3]),
        ffn_conv_w.reshape(sharded["ffn_conv_w"][3])] + [shard_of[k] for k in first_keys]))
    got.update(zip(first_keys, first[3:]))
    wg_at = lambda i: got["wg", i].reshape(N_CHIPS, d, f)
    wu_at = lambda i: got["wu", i].reshape(N_CHIPS, d, f)
    wd_at = lambda i: got["wd", i].reshape(N_CHIPS, f, d)
    wq_at = lambda i: got["wq", i].reshape(N_CHIPS, d, qc)
    wo_at = lambda i: got["wo", i].reshape(cw, d)
    wkv_at = lambda: got["wkv", 0].reshape(N_CHIPS, d, w_kv.shape[-1])
    cw_all = first[2].reshape(N_CHIPS, DEPTH, 3, f)
    cw_l = [cw_all[:, i] for i in range(DEPTH)]
    cb_l = [ffn_conv_b[i].reshape(N_CHIPS, 1, f) for i in range(DEPTH)]
    pw_nat = first[0].reshape(N_CHIPS, N_POOL_LAYERS, 4, 64, POOL_GROUP_DIM).transpose(1, 2, 0, 3, 4).reshape(
        N_POOL_LAYERS, 4, POOL_GROUP_DIM, POOL_GROUP_DIM)
    ps_nat = first[1].reshape(N_CHIPS, N_POOL_LAYERS, -1).transpose(1, 0, 2).reshape(N_POOL_LAYERS, 1, d)
    cos_l, sin_l = _rope_tables(s_n)

    def vec(a, layer):
        return a[layer].reshape(1, d)

    h = x.reshape(n, d)
    hb = None
    saved = []
    k_str = v_str = None
    for layer in range(DEPTH):
        keep = {"h": h, "hb": hb}
        if layer < N_POOL_LAYERS:
            mix = _pool_fwd("pool_fwd", h.reshape(b_n, s_n, d), pw_nat[layer], ps_nat[layer]).reshape(n, d)
        else:
            a = layer - N_POOL_LAYERS
            q = carrying((layer, "q"), functools.partial(_mm_cols, "q_proj", hb, wq_at(a), (), F32))
            qr = _rope("q_rope", q.reshape(b_n, s_n, -1), 0, g_n, cos_l, sin_l, HEAD_DIM ** -0.5, MXU)
            q_str, lse_str, outs, lses = [], [], [], []
            for g, dil in enumerate(DILATIONS):
                if dil == 1:
                    qg, col = qr, g
                else:
                    qg, col = _to_strided(qr[:, :, g * cw:(g + 1) * cw], dil), 0
                o_g, lse_g = _attn_fwd(f"attn_fwd_d{dil}", qg, col, k_str[g], v_str[g], s_n // dil // SPAN)
                q_str.append((qg, col))
                lse_str.append(lse_g)
                outs.append(_from_strided(o_g, dil).reshape(n, cw))
                lses.append(_from_strided(lse_g, dil).reshape(n, N_HEADS))
            ob = _attn_mix("attn_mix", outs, lses)
            mix = carrying((layer, "o"), functools.partial(
                _mm, "o_proj", [(ob, (TM, cw), lambda m, _: (m, 0), wo_at(a), (cw, d), lambda m, _: (0, 0))],
                NN, (n // TM, 1), (n, d), F32, (TM, d), lambda m, _: (m, 0), 1))
            keep.update(q_str=q_str, lse_str=lse_str, outs=outs, lses=lses, ob=ob)
        r1, h1, h1b = _add_ln("ln_fwd", h, mix, vec(ln1_g, layer), vec(ln1_b, layer))
        gate = carrying((layer, "gate"), functools.partial(_mm_shard_out, "gate_up_proj", h1b, wg_at(layer), (), F32))
        up = carrying((layer, "up"), functools.partial(_mm_shard_out, "gate_up_proj", h1b, wu_at(layer), (), F32))
        g4, u4 = gate.reshape(N_CHIPS, b_n, s_n, f), up.reshape(N_CHIPS, b_n, s_n, f)
        hmid = _glu_fwd("glu_fwd", g4, u4, cw_l[layer], cb_l[layer]).reshape(N_CHIPS, n, f)
        ffn = carrying((layer, "down"), functools.partial(_mm_shard_in, "down_proj", hmid, wd_at(layer), ()))
        r2, h2, h2b = _add_ln("ln_fwd", h1, ffn, vec(ln2_g, layer), vec(ln2_b, layer))
        keep.update(r1=r1, h1b=h1b, g4=g4, u4=u4, hmid=hmid, r2=r2, h2b=h2b)
        saved.append(keep)
        if layer == N_POOL_LAYERS - 1:
            kv = _mm_cols("kv_proj", h2b, wkv_at(), (), F32, cb=kvb).reshape(b_n, s_n, -1)
            k_nat = _rope("k_rope", kv, 0, g_n, cos_l, sin_l, 1.0, MXU)
            v_nat = _rope("v_cast", kv, g_n, g_n, cos_l, sin_l, 1.0, MXU, rotate=False)
            k_str = [_to_strided(k_nat[:, :, g * cw:(g + 1) * cw], dil) for g, dil in enumerate(DILATIONS)]
            v_str = [_to_strided(v_nat[:, :, g * cw:(g + 1) * cw], dil) for g, dil in enumerate(DILATIONS)]
        h, hb = h2, h2b

    dy, loss_tile = _loss_head("loss_head", h, loss_target.reshape(n, d))

    pending = {}
    reduced = {}
    halves_of = lambda t: t.reshape(N_CHIPS, 2, t.shape[1] // 2, t.shape[2])
    on_ici = lambda key: F32 if key[0] in ("pool_w", "pool_scale", "ffn_conv_w") else MXU
    d_cw = [None] * DEPTH
    d_pw, d_ps = [None] * N_POOL_LAYERS, [None] * N_POOL_LAYERS
    d_ln = {}
    dk_str = dv_str = None
    d_top, top_scale, top_rest = dy, 1.0, []
    for layer in reversed(range(DEPTH)):
        sv = saved[layer]
        dr2, dr2b, d_ln["ln2", layer] = _ln_bwd("ln_bwd", sv["r2"], vec(ln2_g, layer), d_top, top_scale, top_rest)
        keys = list(pending)
        job = _sibling_job([pending[k] for k in keys], True) if keys else None
        dhmid = _mm_nt_shard_out("down_bwd", dr2b, wd_at(layer), (), job=job)
        if keys:
            dhmid, from_sibling = dhmid
            core_sums = [_add_halves("add_halves", pending[k], r_, half, on_ici(k)) for k, r_ in zip(keys, from_sibling)]
        d_wd = _mm_tn("down_dw", sv["hmid"], f, dr2b, d, N_CHIPS, (N_CHIPS, f, d), (None, f, d),
                      lambda j: (j, 0, 0), a_lead=lambda j: (j,))
        glu = _glu_bwd("glu_bwd", dhmid.reshape(N_CHIPS, b_n, s_n, f), sv["g4"], sv["u4"], cw_l[layer], cb_l[layer],
                       job=_owner_job(core_sums) if keys else None)
        dg4, du4, d_cw[layer] = glu[:3]
        if keys:
            owned = [_add_chips("add_chips", s_, r_, chip) for s_, r_ in zip(core_sums, glu[3])]
        dg, du = dg4.reshape(N_CHIPS, n, f), du4.reshape(N_CHIPS, n, f)
        dh1 = _mm_nt_shard_in("gate_up_bwd", [(dg, wg_at(layer), ()), (du, wu_at(layer), ())],
                              job=_sibling_job(owned, False) if keys else None)
        if keys:
            dh1, others = dh1
            reduced.update({k: pair for k, pair in zip(keys, zip(owned, others))})
        d_wg = _mm_tn("gate_up_dw", sv["h1b"], d, dg, f, N_CHIPS, (N_CHIPS, d, f), (None, d, f),
                      lambda j: (j, 0, 0), b_lead=lambda j: (j,))
        d_wu = _mm_tn("gate_up_dw", sv["h1b"], d, du, f, N_CHIPS, (N_CHIPS, d, f), (None, d, f),
                      lambda j: (j, 0, 0), b_lead=lambda j: (j,))
        pending = {("ffn_w_down", layer): halves_of(d_wd), ("ffn_w_gate", layer): halves_of(d_wg),
                   ("ffn_w_up", layer): halves_of(d_wu)}
        dr1, dr1b, d_ln["ln1", layer] = _ln_bwd("ln_bwd", sv["r1"], vec(ln1_g, layer), dr2, ALPHA, [dh1])
        if layer < N_POOL_LAYERS:
            d_in, d_pw[layer], d_ps[layer] = _pool_bwd("pool_bwd", sv["h"].reshape(b_n, s_n, d),
                                                       dr1.reshape(b_n, s_n, d), pw_nat[layer], ps_nat[layer])
            d_top, top_scale, top_rest = d_in.reshape(n, d), 1.0, []
        else:
            a = layer - N_POOL_LAYERS
            do = _mm("o_bwd", [(dr1b, (TM, d), lambda m, _: (m, 0), wo_at(a), (cw, d), lambda m, _: (0, 0))],
                     NT, (n // TM, 1), (n, cw), F32, (TM, cw), lambda m, _: (m, 0), 1)
            d_wo = _mm_tn("o_dw", sv["ob"], cw // N_CHIPS, dr1b, d, N_CHIPS, (N_CHIPS, cw // N_CHIPS, d),
                          (None, cw // N_CHIPS, d), lambda j: (j, 0, 0))
            pending["w_o", a] = halves_of(d_wo)
            mixed = _attn_mix_bwd("attn_mix_bwd", do, sv["outs"], sv["lses"])
            dq_nat, dk_new, dv_new = [], [], []
            for g, dil in enumerate(DILATIONS):
                do_g = _to_strided(mixed[g].reshape(b_n, s_n, cw), dil)
                dd_g = _to_strided(mixed[g_n + g].reshape(b_n, s_n, N_HEADS), dil)
                qg, col = sv["q_str"][g]
                dq_g, dk_g, dv_g = _attn_bwd(f"attn_bwd_d{dil}", qg, col, k_str[g], v_str[g], do_g, sv["lse_str"][g], dd_g,
                                             s_n // dil // SPAN, *((dk_str[g], dv_str[g]) if dk_str else ()))
                dq_nat.append(_from_strided(dq_g, dil))
                dk_new.append(dk_g)
                dv_new.append(dv_g)
            dk_str, dv_str = dk_new, dv_new
            dq = None
            for g in range(g_n):
                dq = _rope("q_rope_bwd", dq_nat[g], 0, 1, cos_l, -sin_l, HEAD_DIM ** -0.5, MXU, out_cols=g_n, out_col0=g, into=dq)
            dq = dq.reshape(n, g_n * cw)
            d_attn = _mm_nt_cols_in("q_bwd", dq, wq_at(a), (), qc)
            d_wq = _mm_tn("q_dw", sv["hb"], d, dq, qc, N_CHIPS, (N_CHIPS, d, qc), (None, d, qc), lambda j: (j, 0, 0))
            pending["w_q", a] = halves_of(d_wq)
            d_top, top_scale, top_rest = dr1, ALPHA, [d_attn]
            if a == 0:
                dkv = None
                for g, dil in enumerate(DILATIONS):
                    dkv = _rope("k_rope_bwd", _from_strided(dk_str[g], dil), 0, 1, cos_l, -sin_l, 1.0, MXU,
                                out_cols=2 * g_n, out_col0=g, into=dkv)
                for g, dil in enumerate(DILATIONS):
                    dkv = _rope("v_cast_bwd", _from_strided(dv_str[g], dil), 0, 1, cos_l, sin_l, 1.0, MXU, rotate=False,
                                out_cols=2 * g_n, out_col0=g_n + g, into=dkv)
                dkv = dkv.reshape(n, 2 * g_n * cw)
                h_kv = saved[N_POOL_LAYERS - 1]["h2b"]
                top_rest = top_rest + [_mm_nt_cols_in("kv_bwd", dkv, wkv_at(), (), kvb)]
                d_wkv = _mm_tn("kv_dw", h_kv, d, dkv, kvb, 2 * N_CHIPS, (N_CHIPS, d, 2 * kvb), (None, d, kvb),
                               lambda q: (q // 2, 0, q % 2))
                pending["w_kv", 0] = halves_of(d_wkv)
    grad_x = d_top.reshape(b_n, s_n, d)

    d_pw_all = jnp.stack(d_pw).reshape(N_POOL_LAYERS, 4, N_CHIPS, 64, POOL_GROUP_DIM).transpose(2, 0, 1, 3, 4)
    d_ps_all = jnp.stack(d_ps).reshape(N_POOL_LAYERS, N_CHIPS, -1).transpose(1, 0, 2)
    d_cw_all = jnp.stack([t[:, :3] for t in d_cw], axis=1)
    for k, t in (("pool_w", d_pw_all), ("pool_scale", d_ps_all), ("ffn_conv_w", d_cw_all)):
        pending[k, 0] = t.reshape(N_CHIPS, *sharded[k][3])
    keys = list(pending)
    from_sibling = _run_job("grads_to_sibling", _sibling_job([pending[k] for k in keys], True))
    core_sums = [_add_halves("add_halves", pending[k], r_, half, on_ici(k)) for k, r_ in zip(keys, from_sibling)]
    from_chips = _run_job("grads_to_owner", _owner_job(core_sums))
    owned = [_add_chips("add_chips", s_, r_, chip) for s_, r_ in zip(core_sums, from_chips)]
    others = _run_job("halves_to_sibling", _sibling_job(owned, False))
    reduced.update({k: pair for k, pair in zip(keys, zip(owned, others))})

    out_grad, out_delta, out_m, out_v = {}, {}, {}, {}
    for k in names:
        w_, m_, v_, _ = sharded[k]
        parts = sorted(i for name_, i in reduced if name_ == k)
        shape = (len(parts), 2, *reduced[k, 0][0].shape)
        res = None
        for i in parts:
            res = _adam_halves(f"adam_{k}", *reduced[k, i], half, w_.reshape(shape), m_.reshape(shape), v_.reshape(shape), i, res)
        out_grad[k], out_delta[k], out_m[k], out_v[k] = (t.reshape(w_.shape) for t in res)

    d_cb = jnp.stack([t[:, 3] for t in d_cw], axis=0).reshape(DEPTH * N_CHIPS * f // d, d)
    ln_rows = jnp.concatenate([jnp.stack([d_ln[which, layer][row] for layer in range(DEPTH)])
                               for which, row in (("ln1", 0), ("ln1", 1), ("ln2", 0), ("ln2", 1))])
    rows = jnp.concatenate([ln_rows, d_cb, jnp.broadcast_to(loss_tile[0:1, 0:1], (1, d))])
    pad = (-rows.shape[0]) % 8
    total = _sum_all_devices("sum_small", jnp.pad(rows, ((0, pad), (0, 0))))
    small = {"ln1_g": total[0:4], "ln1_b": total[4:8], "ln2_g": total[8:12], "ln2_b": total[12:16],
             "ffn_conv_b": total[16:16 + d_cb.shape[0]].reshape(ffn_conv_b.shape)}
    loss = total[16 + d_cb.shape[0], 0]
    small_in = {"ln1_g": (ln1_g, m_ln1_g, v_ln1_g), "ln1_b": (ln1_b, m_ln1_b, v_ln1_b), "ln2_g": (ln2_g, m_ln2_g, v_ln2_g),
                "ln2_b": (ln2_b, m_ln2_b, v_ln2_b), "ffn_conv_b": (ffn_conv_b, m_ffn_conv_b, v_ffn_conv_b)}
    for k, (w_, m_, v_) in small_in.items():
        out_grad[k] = small[k]
        out_delta[k], out_m[k], out_v[k] = _adam_small(f"adam_{k}", small[k], w_, m_, v_)

    order = ["pool_w", "pool_scale", "w_q", "w_kv", "w_o", "ffn_w_gate", "ffn_w_up", "ffn_conv_w", "ffn_conv_b",
             "ffn_w_down", "ln1_g", "ln1_b", "ln2_g", "ln2_b"]
    return (loss, grad_x, *[out_grad[k] for k in order], *[out_delta[k] for k in order],
            *[out_m[k] for k in order], *[out_v[k] for k in order])
```

```python
import functools
import math

import jax
import jax.numpy as jnp
from jax import lax
from jax.experimental import pallas as pl
from jax.experimental.pallas import tpu as pltpu

F32 = jnp.float32
BF16 = jnp.bfloat16
MXU = jnp.bfloat16

DEPTH = 4
N_POOL_LAYERS = 2
POOL_WINDOWS = (2, 4, 8, 16)
POOL_GROUP_DIM = 256
HEAD_DIM = 64
N_HEADS = 16
DILATIONS = (1, 4, 16)
SPAN = 128
ROPE_THETA = 10000.0
ALPHA = (2.0 * DEPTH) ** 0.25
LN_EPS = 1e-5
ADAM_LR, ADAM_B1, ADAM_B2, ADAM_EPS, ADAM_WD, ADAM_STEP = 0.001, 0.9, 0.999, 1e-08, 0.01, 10

N_CHIPS = 4
VMEM_LIMIT = 56 * 1024 * 1024
MESH = pl.DeviceIdType.MESH

NN = (((1,), (0,)), ((), ()))
NT = (((1,), (1,)), ((), ()))
TN = (((0,), (0,)), ((), ()))


def _params(sem=None):
    return pltpu.CompilerParams(dimension_semantics=sem, vmem_limit_bytes=VMEM_LIMIT)


def _chunks(n_rows, ch, fn, unroll=1):
    def step(i, carry):
        fn(pl.multiple_of(i * ch, ch))
        return carry

    lax.fori_loop(0, n_rows // ch, step, 0, unroll=unroll)


def _fold8(v):
    return jnp.sum(v.reshape(v.shape[0] // 8, 8, v.shape[1]), axis=0)


def _down(v, k):
    return pltpu.roll(v, k, 0)


def _up(v, k):
    return pltpu.roll(v, v.shape[0] - k, 0)


def _own_refs(job, refs, n_in, n_out, n_scratch):
    if job is None:
        return list(refs), None
    a = n_in
    b = a + len(job["ins"])
    c = b + n_out
    e = c + len(job["outs"])
    g = e + n_scratch
    return list(refs[:a]) + list(refs[b:c]) + list(refs[e:g]), job["bind"](refs[a:b], refs[c:e], *refs[g:])


def _grid_ends(grid):
    ids = [pl.program_id(i) for i in range(len(grid))]
    both = lambda conds: functools.reduce(lambda p, q: p & q, conds)
    return both([i == 0 for i in ids]), both([i == g - 1 for i, g in zip(ids, grid)])


def _mm(name, pairs, dims, grid, out_shape, out_dtype, out_block, out_map, nk, into=None, job=None):
    n_pairs = len(pairs)
    kax = len(grid) - 1
    n_in = 2 * n_pairs + (1 if into is not None else 0)

    def body(*refs):
        refs, riding = _own_refs(job, refs, n_in, 1, 0)
        o_ref = refs[n_in]
        if riding is not None:
            at_first, at_last = _grid_ends(grid)
            pl.when(at_first)(riding.begin)
        part = None
        for p in range(n_pairs):
            t = lax.dot_general(refs[2 * p][...], refs[2 * p + 1][...], dims, preferred_element_type=F32)
            part = t if part is None else part + t
        if nk == 1:
            o_ref[...] = part.astype(o_ref.dtype)
        else:
            k = pl.program_id(kax)

            @pl.when(k == 0)
            def _():
                o_ref[...] = part

            @pl.when(k > 0)
            def _():
                o_ref[...] += part

        if riding is not None:
            pl.when(at_last)(riding.end)

    operands, in_specs = [], []
    for a, a_block, a_map, b, b_block, b_map in pairs:
        operands += [a, b]
        in_specs += [pl.BlockSpec(a_block, a_map), pl.BlockSpec(b_block, b_map)]
    aliases = {}
    if into is not None:
        operands.append(into)
        in_specs.append(pl.BlockSpec(memory_space=pl.ANY))
        aliases = {2 * n_pairs: 0}
    assert nk == 1 or out_dtype == F32
    j_ins, j_outs, j_scratch = (job["ins"], job["outs"], job["scratch"]) if job else ([], [], [])
    sem = ("arbitrary",) * len(grid) if job else ("parallel",) * kax + ("arbitrary",)
    res = pl.pallas_call(
        body, name=name, grid=grid, in_specs=in_specs + [pl.BlockSpec(memory_space=pl.ANY)] * len(j_ins),
        out_specs=[pl.BlockSpec(out_block, out_map)] + [pl.BlockSpec(memory_space=pl.ANY)] * len(j_outs),
        out_shape=[jax.ShapeDtypeStruct(out_shape, out_dtype)] + list(j_outs),
        input_output_aliases=aliases, scratch_shapes=list(j_scratch), compiler_params=_params(sem),
    )(*operands, *j_ins)
    return (res[0], res[1:]) if job else res[0]


TM = 2048


def _mm_cols(name, a, w, w_idx, out_dtype, cb=None, job=None):
    n, k = a.shape
    j_n, c = w.shape[0], w.shape[-1]
    cb = c if cb is None else cb
    s = c // cb
    wb = (None,) * (w.ndim - 2) + (k, cb)
    return _mm(name, [(a, (TM, k), lambda q, m, _: (m, 0), w, wb, lambda q, m, _: (q // s, *w_idx, 0, q % s))], NN,
               (j_n * s, n // TM, 1), (n, j_n * c), out_dtype, (TM, cb), lambda q, m, _: (m, q), 1, job=job)


def _mm_shard_out(name, a, w, w_idx, out_dtype, job=None):
    n, k = a.shape
    j_n, c = w.shape[0], w.shape[-1]
    wb = (None,) * (w.ndim - 2) + (k, c)
    return _mm(name, [(a, (TM, k), lambda j, m, _: (m, 0), w, wb, lambda j, m, _: (j, *w_idx, 0, 0))], NN,
               (j_n, n // TM, 1), (j_n, n, c), out_dtype, (None, TM, c), lambda j, m, _: (j, m, 0), 1, job=job)


def _mm_shard_in(name, a4, w, w_idx, job=None):
    j_n, n, c = a4.shape
    k = w.shape[-1]
    wb = (None,) * (w.ndim - 2) + (c, k)
    return _mm(name, [(a4, (None, TM, c), lambda m, j: (j, m, 0), w, wb, lambda m, j: (j, *w_idx, 0, 0))], NN,
               (n // TM, j_n), (n, k), F32, (TM, k), lambda m, j: (m, 0), j_n, job=job)


def _mm_nt_shard_out(name, a, w, w_idx, job=None):
    n, k = a.shape
    j_n, c = w.shape[0], w.shape[-2]
    wb = (None,) * (w.ndim - 2) + (c, k)
    return _mm(name, [(a, (TM, k), lambda j, m, _: (m, 0), w, wb, lambda j, m, _: (j, *w_idx, 0, 0))], NT,
               (j_n, n // TM, 1), (j_n, n, c), F32, (None, TM, c), lambda j, m, _: (j, m, 0), 1, job=job)


def _mm_nt_shard_in(name, terms, job=None):
    pairs = []
    for a4, w, w_idx in terms:
        j_n, n, c = a4.shape
        k = w.shape[-2]
        wb = (None,) * (w.ndim - 2) + (k, c)
        pairs.append((a4, (None, TM, c), lambda m, j: (j, m, 0), w, wb,
                      functools.partial(lambda m, j, w_idx: (j, *w_idx, 0, 0), w_idx=w_idx)))
    return _mm(name, pairs, NT, (n // TM, j_n), (n, k), F32, (TM, k), lambda m, j: (m, 0), j_n, job=job)


def _mm_nt_cols_in(name, a, w, w_idx, cb):
    n, ct = a.shape
    j_n, k, c = w.shape[0], w.shape[-2], w.shape[-1]
    s = c // cb
    wb = (None,) * (w.ndim - 2) + (k, cb)
    return _mm(name, [(a, (TM, cb), lambda m, q: (m, q), w, wb, lambda m, q: (q // s, *w_idx, 0, q % s))], NT,
               (n // TM, ct // cb), (n, k), F32, (TM, k), lambda m, q: (m, 0), ct // cb)


def _mm_tn(name, a, a_cols, b, b_cols, n_blocks, out_shape, out_block, out_map, into=None, a_lead=None, b_lead=None, job=None):
    n = a.shape[-2]
    a_nb = a.shape[-1] // a_cols
    b_nb = b.shape[-1] // b_cols
    if a_lead is None:
        a_block, a_map = (TM, a_cols), lambda q, t: (t, q if a_nb > 1 else 0)
    else:
        a_block, a_map = (None, TM, a_cols), lambda q, t: (*a_lead(q), t, 0)
    if b_lead is None:
        b_block, b_map = (TM, b_cols), lambda q, t: (t, q if b_nb > 1 else 0)
    else:
        b_block, b_map = (None, TM, b_cols), lambda q, t: (*b_lead(q), t, 0)
    return _mm(name, [(a, a_block, a_map, b, b_block, b_map)], TN, (n_blocks, n // TM), out_shape, F32,
               out_block, lambda q, t: out_map(q), n // TM, into=into, job=job)


LN_ROWS = 1024
LN_CH = 16
LN_UNROLL = 4


def _ln_stats(r):
    mu = jnp.mean(r, -1, keepdims=True)
    xc = r - mu
    var = jnp.mean(xc * xc, -1, keepdims=True)
    return xc, lax.rsqrt(var + LN_EPS)


def _add_ln(name, a, mix, g, b):
    n, d = a.shape

    def body(a_ref, m_ref, g_ref, b_ref, r_ref, h_ref, hb_ref):
        gg, bb = g_ref[...], b_ref[...]

        def chunk(s):
            rows = pl.ds(s, LN_CH)
            r = ALPHA * a_ref[rows, :] + m_ref[rows, :]
            xc, rstd = _ln_stats(r)
            y = xc * rstd * gg + bb
            r_ref[rows, :] = r
            h_ref[rows, :] = y
            hb_ref[rows, :] = y.astype(MXU)

        _chunks(LN_ROWS, LN_CH, chunk, LN_UNROLL)

    tile = pl.BlockSpec((LN_ROWS, d), lambda i: (i, 0))
    vec = pl.BlockSpec((1, d), lambda i: (0, 0))
    return pl.pallas_call(
        body, name=name, grid=(n // LN_ROWS,), in_specs=[tile, tile, vec, vec], out_specs=[tile, tile, tile],
        out_shape=[jax.ShapeDtypeStruct((n, d), F32), jax.ShapeDtypeStruct((n, d), F32), jax.ShapeDtypeStruct((n, d), MXU)],
        compiler_params=_params(("parallel",)),
    )(a, mix, g, b)


def _ln_bwd(name, r, g, d_a, scale_a, d_rest):
    n, d = r.shape
    n_rest = len(d_rest)
    steps = n // LN_ROWS

    def body(*refs):
        r_ref, g_ref, da_ref = refs[:3]
        rest = refs[3:3 + n_rest]
        dr_ref, drb_ref, gb_ref, acc = refs[3 + n_rest:]
        i = pl.program_id(0)

        @pl.when(i == 0)
        def _():
            acc[...] = jnp.zeros_like(acc)

        gg = g_ref[...]

        def chunk(s):
            rows = pl.ds(s, LN_CH)
            xc, rstd = _ln_stats(r_ref[rows, :])
            xhat = xc * rstd
            dy = da_ref[rows, :] if scale_a == 1.0 else scale_a * da_ref[rows, :]
            for t in rest:
                dy = dy + t[rows, :]
            dyg = dy * gg
            m1 = jnp.mean(dyg, -1, keepdims=True)
            m2 = jnp.mean(dyg * xhat, -1, keepdims=True)
            dr = rstd * (dyg - m1 - xhat * m2)
            dr_ref[rows, :] = dr
            drb_ref[rows, :] = dr.astype(MXU)
            acc[0] += _fold8(dy * xhat)
            acc[1] += _fold8(dy)

        _chunks(LN_ROWS, LN_CH, chunk, LN_UNROLL)

        @pl.when(i == steps - 1)
        def _():
            gb_ref[0:1, :] = jnp.sum(acc[0], axis=0, keepdims=True)
            gb_ref[1:2, :] = jnp.sum(acc[1], axis=0, keepdims=True)

    tile = pl.BlockSpec((LN_ROWS, d), lambda i: (i, 0))
    vec = pl.BlockSpec((1, d), lambda i: (0, 0))
    return pl.pallas_call(
        body, name=name, grid=(steps,), in_specs=[tile, vec, tile] + [tile] * n_rest,
        out_specs=[tile, tile, pl.BlockSpec((2, d), lambda i: (0, 0))],
        out_shape=[jax.ShapeDtypeStruct((n, d), F32), jax.ShapeDtypeStruct((n, d), MXU), jax.ShapeDtypeStruct((2, d), F32)],
        scratch_shapes=[pltpu.VMEM((2, 8, d), F32)],
        compiler_params=_params(("arbitrary",)),
    )(r, g, d_a, *d_rest)


FFN_ROWS = 512
FFN_CH = 64
GELU_C1 = math.sqrt(2.0 / math.pi)
GELU_C2 = 0.044715


def _conv3(behind, n, w0, w1, w2, bias):
    v = behind[8:8 + n]
    g1 = _down(behind, 1)[8:8 + n]
    g2 = _down(behind, 2)[8:8 + n]
    return bias + w0 * g2 + w1 * g1 + w2 * v, v, g1, g2


def _glu_fwd(name, g4, u4, conv_w, conv_b):
    j_n, b_n, s_n, f = g4.shape
    tiles = s_n // FFN_ROWS

    def body(g_ref, halo_ref, u_ref, w_ref, b_ref, o_ref):
        s = pl.program_id(2)
        w0, w1, w2, bias = w_ref[0:1, :], w_ref[1:2, :], w_ref[2:3, :], b_ref[...]

        def work(st, behind):
            conv, _, _, _ = _conv3(behind, FFN_CH, w0, w1, w2, bias)
            cdf = 0.5 * jnp.tanh(conv * (GELU_C1 + (GELU_C1 * GELU_C2) * (conv * conv))) + 0.5
            o_ref[pl.ds(st, FFN_CH), :] = (conv * cdf * u_ref[pl.ds(st, FFN_CH), :]).astype(o_ref.dtype)

        work(0, jnp.concatenate([jnp.where(s > 0, halo_ref[...], 0.0), g_ref[0:FFN_CH, :]], axis=0))

        def step(i, carry):
            st = pl.multiple_of(i * FFN_CH, FFN_CH)
            work(st, g_ref[pl.ds(pl.multiple_of(st - 8, 8), FFN_CH + 8), :])
            return carry

        lax.fori_loop(1, FFN_ROWS // FFN_CH, step, 0)

    tile = pl.BlockSpec((None, None, FFN_ROWS, f), lambda j, b, s: (j, b, s, 0))
    halo = pl.BlockSpec((None, None, 8, f), lambda j, b, s: (j, b, jnp.maximum(s * (FFN_ROWS // 8) - 1, 0), 0))
    return pl.pallas_call(
        body, name=name, grid=(j_n, b_n, tiles),
        in_specs=[tile, halo, tile,
                  pl.BlockSpec((None, 3, f), lambda j, b, s: (j, 0, 0)),
                  pl.BlockSpec((None, 1, f), lambda j, b, s: (j, 0, 0))],
        out_specs=tile, out_shape=jax.ShapeDtypeStruct(g4.shape, MXU),
        compiler_params=_params(("parallel", "parallel", "parallel")),
    )(g4, g4, u4, conv_w, conv_b)


def _glu_bwd(name, dh4, g4, u4, conv_w, conv_b, job=None):
    j_n, b_n, s_n, f = g4.shape
    tiles = s_n // FFN_ROWS
    ext = FFN_CH + 8
    grid = (j_n, b_n, tiles)

    def body(*refs):
        refs, riding = _own_refs(job, refs, 9, 3, 1)
        (d_ref, dnext_ref, g_ref, gprev_ref, gnext_ref, u_ref, unext_ref, w_ref, b_ref,
         dg_ref, du_ref, wb_ref, acc) = refs
        b, s = pl.program_id(1), pl.program_id(2)
        last = s == tiles - 1
        if riding is not None:
            at_first, at_last = _grid_ends(grid)
            pl.when(at_first)(riding.begin)

        @pl.when((b == 0) & (s == 0))
        def _():
            acc[...] = jnp.zeros_like(acc)

        w0, w1, w2, bias = w_ref[0:1, :], w_ref[1:2, :], w_ref[2:3, :], b_ref[...]

        def work(st, behind, ue, de):
            conv, v, g1, g2 = _conv3(behind, ext, w0, w1, w2, bias)
            c2 = conv * conv
            th = jnp.tanh(conv * (GELU_C1 + (GELU_C1 * GELU_C2) * c2))
            cdf = 0.5 * th + 0.5
            dact = cdf + (conv * (1.0 - th * th)) * ((0.5 * GELU_C1) + (1.5 * GELU_C1 * GELU_C2) * c2)
            dconv = de * ue * dact
            du_ref[pl.ds(st, FFN_CH), :] = (de[:FFN_CH] * (conv[:FFN_CH] * cdf[:FFN_CH])).astype(du_ref.dtype)
            dg = w2 * dconv + w1 * _up(dconv, 1) + w0 * _up(dconv, 2)
            dg_ref[pl.ds(st, FFN_CH), :] = dg[:FFN_CH].astype(dg_ref.dtype)
            dc = dconv[:FFN_CH]
            acc[0] += _fold8(dc * g2[:FFN_CH])
            acc[1] += _fold8(dc * g1[:FFN_CH])
            acc[2] += _fold8(dc * v[:FFN_CH])
            acc[3] += _fold8(dc)

        rows = lambda ref, lo, hi: ref[lo:hi, :]
        top = FFN_ROWS - FFN_CH
        work(0, jnp.concatenate([jnp.where(s > 0, gprev_ref[...], 0.0), rows(g_ref, 0, ext)], axis=0),
             rows(u_ref, 0, ext), rows(d_ref, 0, ext))

        def step(i, carry):
            st = pl.multiple_of(i * FFN_CH, FFN_CH)
            work(st, g_ref[pl.ds(pl.multiple_of(st - 8, 8), ext + 8), :], u_ref[pl.ds(st, ext), :], d_ref[pl.ds(st, ext), :])
            return carry

        lax.fori_loop(1, FFN_ROWS // FFN_CH - 1, step, 0)
        work(top, jnp.concatenate([rows(g_ref, top - 8, FFN_ROWS), gnext_ref[...]], axis=0),
             jnp.concatenate([rows(u_ref, top, FFN_ROWS), unext_ref[...]], axis=0),
             jnp.concatenate([rows(d_ref, top, FFN_ROWS), jnp.where(last, 0.0, dnext_ref[...])], axis=0))

        @pl.when((b == b_n - 1) & last)
        def _():
            for k in range(4):
                wb_ref[k:k + 1, :] = jnp.sum(acc[k], axis=0, keepdims=True)

        if riding is not None:
            pl.when(at_last)(riding.end)

    blocks8 = FFN_ROWS // 8
    tile = pl.BlockSpec((None, None, FFN_ROWS, f), lambda j, b, s: (j, b, s, 0))
    prev = pl.BlockSpec((None, None, 8, f), lambda j, b, s: (j, b, jnp.maximum(s * blocks8 - 1, 0), 0))
    nxt = pl.BlockSpec((None, None, 8, f), lambda j, b, s: (j, b, jnp.minimum((s + 1) * blocks8, s_n // 8 - 1), 0))
    j_ins, j_outs, j_scratch = (job["ins"], job["outs"], job["scratch"]) if job else ([], [], [])
    res = pl.pallas_call(
        body, name=name, grid=grid,
        in_specs=[tile, nxt, tile, prev, nxt, tile, nxt,
                  pl.BlockSpec((None, 3, f), lambda j, b, s: (j, 0, 0)),
                  pl.BlockSpec((None, 1, f), lambda j, b, s: (j, 0, 0))] + [pl.BlockSpec(memory_space=pl.ANY)] * len(j_ins),
        out_specs=[tile, tile, pl.BlockSpec((None, 4, f), lambda j, b, s: (j, 0, 0))]
        + [pl.BlockSpec(memory_space=pl.ANY)] * len(j_outs),
        out_shape=[jax.ShapeDtypeStruct(g4.shape, MXU), jax.ShapeDtypeStruct(g4.shape, MXU),
                   jax.ShapeDtypeStruct((j_n, 4, f), F32)] + list(j_outs),
        scratch_shapes=[pltpu.VMEM((4, 8, f), F32)] + list(j_scratch),
        compiler_params=_params(("arbitrary",) * 3 if job else ("parallel", "arbitrary", "arbitrary")),
    )(dh4, dh4, g4, g4, g4, u4, u4, conv_w, conv_b, *j_ins)
    return (*res[:3], res[3:]) if job else res


POOL_ROWS = 512
POOL_CH = 32
POOL_HALO = 16


def _pool_windows(v, t0, gi, causal):
    shift = _down if causal else _up
    acc, k = v, 1
    while k < POOL_WINDOWS[gi]:
        acc = acc + shift(acc, k)
        k *= 2
    return acc


def _count(t0, n, w):
    t = t0 + lax.broadcasted_iota(jnp.int32, (n, 1), 0)
    return jnp.minimum(t + 1, w).astype(F32)


def _pooled_into(xs, pooled, t_tile):
    def chunk(st):
        for gi, w in enumerate(POOL_WINDOWS):
            cols = slice(gi * POOL_GROUP_DIM, (gi + 1) * POOL_GROUP_DIM)
            v = xs[pl.ds(st, POOL_CH + POOL_HALO), cols]
            sums = _pool_windows(v, None, gi, True)[POOL_HALO:]
            val = sums / _count(t_tile + st, POOL_CH, w) - v[POOL_HALO:]
            pooled[pl.ds(st, POOL_CH), cols] = val.astype(pooled.dtype)

    _chunks(POOL_ROWS, POOL_CH, chunk)


def _pool_specs(b_n, s_n, d):
    per = POOL_ROWS // POOL_HALO
    tile = pl.BlockSpec((None, POOL_ROWS, d), lambda b, s: (b, s, 0))
    prev = pl.BlockSpec((None, POOL_HALO, d), lambda b, s: (b, jnp.maximum(s * per - 1, 0), 0))
    nxt = pl.BlockSpec((None, POOL_HALO, d), lambda b, s: (b, jnp.minimum((s + 1) * per, s_n // POOL_HALO - 1), 0))
    return tile, prev, nxt


def _pool_fwd(name, h3, w, scale):
    b_n, s_n, d = h3.shape
    tile, prev, _ = _pool_specs(b_n, s_n, d)

    def body(h_ref, halo_ref, w_ref, sc_ref, o_ref, xs, pooled):
        s = pl.program_id(1)
        xs[0:POOL_HALO, :] = jnp.where(s > 0, halo_ref[...], 0.0)
        xs[POOL_HALO:, :] = h_ref[...]
        _pooled_into(xs, pooled, s * POOL_ROWS)
        for gi in range(len(POOL_WINDOWS)):
            cols = slice(gi * POOL_GROUP_DIM, (gi + 1) * POOL_GROUP_DIM)
            y = jnp.dot(pooled[:, cols], w_ref[gi], preferred_element_type=F32)
            o_ref[:, cols] = y * sc_ref[:, cols]

    return pl.pallas_call(
        body, name=name, grid=(b_n, s_n // POOL_ROWS),
        in_specs=[tile, prev, pl.BlockSpec(w.shape, lambda b, s: (0, 0, 0)), pl.BlockSpec((1, d), lambda b, s: (0, 0))],
        out_specs=tile, out_shape=jax.ShapeDtypeStruct(h3.shape, F32),
        scratch_shapes=[pltpu.VMEM((POOL_HALO + POOL_ROWS, d), F32), pltpu.VMEM((POOL_ROWS, d), MXU)],
        compiler_params=_params(("parallel", "parallel")),
    )(h3, h3, w, scale)


def _pool_bwd(name, h3, dm3, w, scale):
    b_n, s_n, d = h3.shape
    tile, prev, nxt = _pool_specs(b_n, s_n, d)
    tiles = s_n // POOL_ROWS
    ext = POOL_ROWS + POOL_HALO

    def body(h_ref, halo_ref, dm_ref, dnext_ref, w_ref, sc_ref, dh_ref, dw_ref, dsc_ref, xs, pooled, ds, dp):
        b, s = pl.program_id(0), pl.program_id(1)

        @pl.when((b == 0) & (s == 0))
        def _():
            dw_ref[...] = jnp.zeros_like(dw_ref)
            dsc_ref[...] = jnp.zeros_like(dsc_ref)

        xs[0:POOL_HALO, :] = jnp.where(s > 0, halo_ref[...], 0.0)
        xs[POOL_HALO:, :] = h_ref[...]
        ds[0:POOL_ROWS, :] = dm_ref[...]
        ds[POOL_ROWS:, :] = jnp.where(s == tiles - 1, 0.0, dnext_ref[...])
        _pooled_into(xs, pooled, s * POOL_ROWS)
        for gi in range(len(POOL_WINDOWS)):
            cols = slice(gi * POOL_GROUP_DIM, (gi + 1) * POOL_GROUP_DIM)
            dyb = (ds[:, cols] * sc_ref[:, cols]).astype(MXU)
            dp[:, cols] = lax.dot_general(dyb, w_ref[gi], NT, preferred_element_type=F32)
            pg = pooled[:, cols]
            dw_ref[gi] += lax.dot_general(pg, dyb[:POOL_ROWS], TN, preferred_element_type=F32)
            ypre = jnp.dot(pg, w_ref[gi], preferred_element_type=F32)
            dsc_ref[:, cols] += jnp.sum(ds[0:POOL_ROWS, cols] * ypre, axis=0, keepdims=True)

        def chunk(st):
            for gi, w_len in enumerate(POOL_WINDOWS):
                cols = slice(gi * POOL_GROUP_DIM, (gi + 1) * POOL_GROUP_DIM)
                v = dp[pl.ds(st, POOL_CH + POOL_HALO), cols]
                q = v / _count(s * POOL_ROWS + st, POOL_CH + POOL_HALO, w_len)
                back = _pool_windows(q, None, gi, False)[:POOL_CH] - v[:POOL_CH]
                dh_ref[pl.ds(st, POOL_CH), cols] = ALPHA * ds[pl.ds(st, POOL_CH), cols] + back

        _chunks(POOL_ROWS, POOL_CH, chunk)

    return pl.pallas_call(
        body, name=name, grid=(b_n, tiles),
        in_specs=[tile, prev, tile, nxt, pl.BlockSpec(w.shape, lambda b, s: (0, 0, 0)), pl.BlockSpec((1, d), lambda b, s: (0, 0))],
        out_specs=[tile, pl.BlockSpec(w.shape, lambda b, s: (0, 0, 0)), pl.BlockSpec((1, d), lambda b, s: (0, 0))],
        out_shape=[jax.ShapeDtypeStruct(h3.shape, F32), jax.ShapeDtypeStruct(w.shape, F32), jax.ShapeDtypeStruct((1, d), F32)],
        scratch_shapes=[pltpu.VMEM((POOL_HALO + POOL_ROWS, d), F32), pltpu.VMEM((POOL_ROWS, d), MXU),
                        pltpu.VMEM((ext, d), F32), pltpu.VMEM((ext, d), F32)],
        compiler_params=_params(("arbitrary", "arbitrary")),
    )(h3, h3, dm3, dm3, w, scale)


ROPE_ROWS = 512


def _rope_tables(s_n):
    inv_freq = ROPE_THETA ** (-jnp.arange(0, HEAD_DIM, 2, dtype=F32) / HEAD_DIM)
    ang = jnp.arange(s_n, dtype=F32)[:, None] * inv_freq[None, :]
    cos, sin = jnp.cos(ang), jnp.sin(ang)
    cos_l = jnp.tile(cos, (1, 4))
    sin_l = jnp.tile(jnp.concatenate([-sin, sin], axis=1), (1, 2))
    return cos_l, sin_l


def _rope(name, x3, col0, n_col, cos_l, sin_l, scale, out_dtype, rotate=True, out_cols=None, out_col0=0, into=None):
    b_n, s_n, _ = x3.shape
    cw = N_HEADS * HEAD_DIM
    out_cols = n_col if out_cols is None else out_cols

    def body(x_ref, c_ref, s_ref, *rest):
        o_ref = rest[-1]
        lane = lax.broadcasted_iota(jnp.int32, (ROPE_ROWS, 128), 1)
        first_half = (lane % HEAD_DIM) < (HEAD_DIM // 2)
        cos, sin = c_ref[...], s_ref[...]
        for cb in range(cw // 128):
            cols = slice(cb * 128, (cb + 1) * 128)
            y = x_ref[:, cols]
            if rotate:
                other = jnp.where(first_half, pltpu.roll(y, 128 - HEAD_DIM // 2, 1), pltpu.roll(y, HEAD_DIM // 2, 1))
                y = y * cos + other * sin
            o_ref[:, cols] = (y if scale == 1.0 else y * scale).astype(o_ref.dtype)

    tile = pl.BlockSpec((None, ROPE_ROWS, cw), lambda b, s, c: (b, s, col0 + c))
    tab = pl.BlockSpec((ROPE_ROWS, 128), lambda b, s, c: (s, 0))
    extra, extra_specs, aliases = [], [], {}
    if into is not None:
        extra, extra_specs, aliases = [into], [pl.BlockSpec(memory_space=pl.ANY)], {3: 0}
    return pl.pallas_call(
        body, name=name, grid=(b_n, s_n // ROPE_ROWS, n_col), in_specs=[tile, tab, tab] + extra_specs,
        out_specs=pl.BlockSpec((None, ROPE_ROWS, cw), lambda b, s, c: (b, s, out_col0 + c)),
        out_shape=jax.ShapeDtypeStruct((b_n, s_n, out_cols * cw), out_dtype), input_output_aliases=aliases,
        compiler_params=_params(("parallel", "parallel", "parallel")),
    )(x3, cos_l, sin_l, *extra)


def _to_strided(a, d):
    if d == 1:
        return a
    b_n, s_n, c = a.shape
    return a.reshape(b_n, s_n // d, d, c).transpose(0, 2, 1, 3).reshape(b_n, s_n, c)


def _from_strided(a, d):
    if d == 1:
        return a
    b_n, s_n, c = a.shape
    return a.reshape(b_n, d, s_n // d, c).transpose(0, 2, 1, 3).reshape(b_n, s_n, c)


def _attn_fwd(name, q, q_col, k, v, blocks_per_seq):
    b_n, s_n, cw = k.shape
    nb = s_n // SPAN
    with_prev = blocks_per_seq > 1

    keys = 2 * SPAN if with_prev else SPAN
    pair = 2 * HEAD_DIM

    def body(*refs):
        if with_prev:
            q_ref, kc_ref, kp_ref, vc_ref, vp_ref, o_ref, lse_ref, k_all, v_all, s_buf, m_buf = refs
            k_all[0:SPAN, :] = kp_ref[...]
            k_all[SPAN:, :] = kc_ref[...]
            v_all[0:SPAN, :] = vp_ref[...]
            v_all[SPAN:, :] = vc_ref[...]
        else:
            q_ref, k_all, v_all, o_ref, lse_ref, s_buf, m_buf = refs
        n = pl.program_id(1)
        qi = lax.broadcasted_iota(jnp.int32, (SPAN, keys), 0)
        kj = lax.broadcasted_iota(jnp.int32, (SPAN, keys), 1)
        if with_prev:
            back = jnp.where((n % blocks_per_seq) != 0, 0, 2 * SPAN)
            mask = ((kj < SPAN) & (kj >= qi + back)) | ((kj >= SPAN) & (kj - SPAN <= qi))
        else:
            mask = kj <= qi
        lane = lax.broadcasted_iota(jnp.int32, (SPAN, pair), 1)
        low = lane < HEAD_DIM
        ones = jnp.ones((keys, pair), MXU)
        for h in range(N_HEADS):
            grp = slice((h // 2) * pair, (h // 2 + 1) * pair)
            q2 = q_ref[:, grp]
            qh = jnp.where(low if h % 2 == 0 else ~low, q2, jnp.zeros_like(q2))
            s = jnp.where(mask, lax.dot_general(qh, k_all[:, grp], NT, preferred_element_type=F32), -jnp.inf)
            s_buf[h] = s
            m_buf[h] = jnp.broadcast_to(jnp.max(s, -1, keepdims=True), (SPAN, pair))
        lse_all = jnp.zeros((SPAN, pair), F32)
        for hp in range(N_HEADS // 2):
            grp = slice(hp * pair, (hp + 1) * pair)
            v2 = v_all[:, grp]
            halves = []
            for h in (2 * hp, 2 * hp + 1):
                m = m_buf[h]
                p = jnp.exp(s_buf[h] - jnp.tile(m, (1, keys // pair))).astype(MXU)
                tot = jnp.dot(p, ones, preferred_element_type=F32)
                halves.append(jnp.dot(p, v2, preferred_element_type=F32) / tot)
                lse_all = jnp.where(lane == h, m + jnp.log(tot), lse_all)
            o_ref[:, grp] = jnp.where(low, halves[0], halves[1])
        lse_ref[...] = lse_all[:, 0:N_HEADS]

    cur = lambda b, n: (b, n, 0)
    prv = lambda b, n: (b, jnp.maximum(n - 1, 0), 0)
    blk = (None, SPAN, cw)
    kv_specs = [pl.BlockSpec(blk, cur), pl.BlockSpec(blk, prv)] if with_prev else [pl.BlockSpec(blk, cur)]
    operands = [q, k, k, v, v] if with_prev else [q, k, v]
    stage = [pltpu.VMEM((keys, cw), MXU)] * 2 if with_prev else []
    return pl.pallas_call(
        body, name=name, grid=(b_n, nb),
        in_specs=[pl.BlockSpec(blk, lambda b, n: (b, n, q_col))] + kv_specs + kv_specs,
        out_specs=[pl.BlockSpec(blk, cur), pl.BlockSpec((None, SPAN, N_HEADS), cur)],
        out_shape=[jax.ShapeDtypeStruct((b_n, s_n, cw), F32), jax.ShapeDtypeStruct((b_n, s_n, N_HEADS), F32)],
        scratch_shapes=stage + [pltpu.VMEM((N_HEADS, SPAN, keys), F32), pltpu.VMEM((N_HEADS, SPAN, pair), F32)],
        compiler_params=_params(("parallel", "parallel")),
    )(*operands)


def _attn_bwd(name, q, q_col, k, v, do, lse, dd, blocks_per_seq, dk_prev=None, dv_prev=None):
    b_n, s_n, cw = k.shape
    nb = s_n // SPAN
    with_next = blocks_per_seq > 1
    accumulate = dk_prev is not None
    rows = 2 * SPAN if with_next else SPAN

    def body(*refs):
        refs = list(refs)
        qc_ref, doc_ref, lsec_ref, ddc_ref = refs[:4]
        del refs[:4]
        if with_next:
            qn_ref, don_ref, lsen_ref, ddn_ref = refs[:4]
            del refs[:4]
        k_ref, v_ref = refs[:2]
        del refs[:2]
        if accumulate:
            dkp_ref, dvp_ref = refs[:2]
            del refs[:2]
        dq_ref, dk_ref, dv_ref = refs[:3]
        del refs[:3]
        if with_next:
            carry, q_all, do_all, side = refs[:4]
            del refs[:4]
            q_all[0:SPAN, :] = qc_ref[...]
            q_all[SPAN:, :] = qn_ref[...]
            do_all[0:SPAN, :] = doc_ref[...]
            do_all[SPAN:, :] = don_ref[...]
            side[0, 0:SPAN, :] = lsec_ref[...]
            side[0, SPAN:, :] = lsen_ref[...]
            side[1, 0:SPAN, :] = ddc_ref[...]
            side[1, SPAN:, :] = ddn_ref[...]
            lse_at = lambda h: side[0, :, h:h + 1]
            dd_at = lambda h: side[1, :, h:h + 1]
        else:
            q_all, do_all = qc_ref, doc_ref
            lse_at = lambda h: lsec_ref[:, h:h + 1]
            dd_at = lambda h: ddc_ref[:, h:h + 1]
        p_buf, ds_buf = refs
        n = pl.program_id(1)
        qi = lax.broadcasted_iota(jnp.int32, (rows, SPAN), 0)
        kj = lax.broadcasted_iota(jnp.int32, (rows, SPAN), 1)
        if with_next:
            first = (n % blocks_per_seq) == 0
            reach = jnp.where(((n + 1) % blocks_per_seq) != 0, SPAN, -2 * SPAN)
            mask = ((qi < SPAN) & (kj <= qi)) | ((qi >= SPAN) & (kj >= qi - reach))
        else:
            mask = kj <= qi
        pair = 2 * HEAD_DIM
        low = lax.broadcasted_iota(jnp.int32, (rows, pair), 1) < HEAD_DIM
        low_k = lax.broadcasted_iota(jnp.int32, (SPAN, pair), 1) < HEAD_DIM

        def pick(v, h, low_mask):
            return jnp.where(low_mask if h % 2 == 0 else ~low_mask, v, jnp.zeros_like(v))

        for h in range(N_HEADS):
            grp = slice((h // 2) * pair, (h // 2 + 1) * pair)
            s = lax.dot_general(pick(q_all[:, grp], h, low), k_ref[:, grp], NT, preferred_element_type=F32)
            p = jnp.where(mask, jnp.exp(s - lse_at(h)), 0.0)
            dp = lax.dot_general(pick(do_all[:, grp], h, low), v_ref[:, grp], NT, preferred_element_type=F32)
            p_buf[h] = p.astype(MXU)
            ds_buf[h] = (p * (dp - dd_at(h))).astype(MXU)
        for hp in range(N_HEADS // 2):
            grp = slice(hp * pair, (hp + 1) * pair)
            q2, do2, k2 = q_all[:, grp], do_all[:, grp], k_ref[:, grp]
            dv = dk = dq2 = None
            for h in (2 * hp, 2 * hp + 1):
                t_dv = lax.dot_general(p_buf[h], pick(do2, h, low), TN, preferred_element_type=F32)
                t_dk = lax.dot_general(ds_buf[h], pick(q2, h, low), TN, preferred_element_type=F32)
                t_dq = jnp.dot(ds_buf[h], pick(k2, h, low_k), preferred_element_type=F32)
                dv = t_dv if dv is None else dv + t_dv
                dk = t_dk if dk is None else dk + t_dk
                dq2 = t_dq if dq2 is None else dq2 + t_dq
            if accumulate:
                dk = dk + dkp_ref[:, grp]
                dv = dv + dvp_ref[:, grp]
            dk_ref[:, grp] = dk
            dv_ref[:, grp] = dv
            if with_next:
                dq_ref[:, grp] = dq2[:SPAN] + jnp.where(first, 0.0, carry[:, grp])
                carry[:, grp] = dq2[SPAN:]
            else:
                dq_ref[:, grp] = dq2

    cur = lambda b, n: (b, n, 0)
    nxt = lambda b, n: (b, jnp.minimum(n + 1, nb - 1), 0)
    blk, hblk = (None, SPAN, cw), (None, SPAN, N_HEADS)
    q_specs = lambda m: [pl.BlockSpec(blk, lambda b, n: (*m(b, n)[:2], q_col)), pl.BlockSpec(blk, m),
                         pl.BlockSpec(hblk, m), pl.BlockSpec(hblk, m)]
    in_specs = q_specs(cur) + (q_specs(nxt) if with_next else []) + [pl.BlockSpec(blk, cur)] * (4 if accumulate else 2)
    operands = [q, do, lse, dd] * (2 if with_next else 1) + [k, v] + ([dk_prev, dv_prev] if accumulate else [])
    out = jax.ShapeDtypeStruct((b_n, s_n, cw), F32)
    return pl.pallas_call(
        body, name=name, grid=(b_n, nb), in_specs=in_specs, out_specs=[pl.BlockSpec(blk, cur)] * 3,
        out_shape=[out, out, out],
        scratch_shapes=([pltpu.VMEM((SPAN, cw), F32), pltpu.VMEM((rows, cw), MXU), pltpu.VMEM((rows, cw), MXU),
                         pltpu.VMEM((2, rows, N_HEADS), F32)] if with_next else [])
        + [pltpu.VMEM((N_HEADS, rows, SPAN), MXU)] * 2,
        compiler_params=_params(("parallel", "arbitrary")),
    )(*operands)


MIX_ROWS = 256


def _group_weights(ls):
    m = functools.reduce(jnp.maximum, ls)
    es = [jnp.exp(l - m) for l in ls]
    tot = functools.reduce(lambda a, b: a + b, es)
    return [e / tot for e in es]


def _attn_mix(name, outs, lses):
    n, cw = outs[0].shape
    g_n = len(outs)

    def body(*refs):
        o_refs, l_refs, out_ref = refs[:g_n], refs[g_n:2 * g_n], refs[2 * g_n]
        ws = _group_weights([r[...] for r in l_refs])
        for h in range(N_HEADS):
            cols = slice(h * HEAD_DIM, (h + 1) * HEAD_DIM)
            acc = None
            for g in range(g_n):
                t = ws[g][:, h:h + 1] * o_refs[g][:, cols]
                acc = t if acc is None else acc + t
            out_ref[:, cols] = acc.astype(out_ref.dtype)

    tile = pl.BlockSpec((MIX_ROWS, cw), lambda i: (i, 0))
    htile = pl.BlockSpec((MIX_ROWS, N_HEADS), lambda i: (i, 0))
    return pl.pallas_call(
        body, name=name, grid=(n // MIX_ROWS,), in_specs=[tile] * g_n + [htile] * g_n, out_specs=tile,
        out_shape=jax.ShapeDtypeStruct((n, cw), MXU), compiler_params=_params(("parallel",)),
    )(*outs, *lses)


def _attn_mix_bwd(name, do, outs, lses):
    n, cw = do.shape
    g_n = len(outs)

    def body(*refs):
        do_ref, o_refs, l_refs = refs[0], refs[1:1 + g_n], refs[1 + g_n:1 + 2 * g_n]
        dog_refs, dd_refs = refs[1 + 2 * g_n:1 + 3 * g_n], refs[1 + 3 * g_n:]
        ws = _group_weights([r[...] for r in l_refs])
        for h in range(N_HEADS):
            cols = slice(h * HEAD_DIM, (h + 1) * HEAD_DIM)
            dh = do_ref[:, cols]
            o = None
            for g in range(g_n):
                t = ws[g][:, h:h + 1] * o_refs[g][:, cols]
                o = t if o is None else o + t
            dot = jnp.sum(dh * o, -1, keepdims=True)
            for g in range(g_n):
                wg = ws[g][:, h:h + 1]
                dog_refs[g][:, cols] = (wg * dh).astype(dog_refs[g].dtype)
                dd_refs[g][:, h:h + 1] = wg * dot

    tile = pl.BlockSpec((MIX_ROWS, cw), lambda i: (i, 0))
    htile = pl.BlockSpec((MIX_ROWS, N_HEADS), lambda i: (i, 0))
    return pl.pallas_call(
        body, name=name, grid=(n // MIX_ROWS,), in_specs=[tile] * (1 + g_n) + [htile] * g_n,
        out_specs=[tile] * g_n + [htile] * g_n,
        out_shape=[jax.ShapeDtypeStruct((n, cw), MXU)] * g_n + [jax.ShapeDtypeStruct((n, N_HEADS), F32)] * g_n,
        compiler_params=_params(("parallel",)),
    )(do, *outs, *lses)


def _loss_head(name, y, target):
    n, d = y.shape
    steps = n // LN_ROWS

    def body(y_ref, t_ref, dy_ref, l_ref, acc):
        i = pl.program_id(0)

        @pl.when(i == 0)
        def _():
            acc[...] = jnp.zeros_like(acc)

        def chunk(s):
            rows = pl.ds(s, LN_CH)
            err = y_ref[rows, :] - t_ref[rows, :]
            dy_ref[rows, :] = err / d
            acc[...] += _fold8(err * err)

        _chunks(LN_ROWS, LN_CH, chunk, LN_UNROLL)

        @pl.when(i == steps - 1)
        def _():
            l_ref[...] = jnp.full((8, 128), 0.5 / d, F32) * jnp.sum(acc[...])

    tile = pl.BlockSpec((LN_ROWS, d), lambda i: (i, 0))
    return pl.pallas_call(
        body, name=name, grid=(steps,), in_specs=[tile, tile],
        out_specs=[tile, pl.BlockSpec((8, 128), lambda i: (0, 0))],
        out_shape=[jax.ShapeDtypeStruct((n, d), F32), jax.ShapeDtypeStruct((8, 128), F32)],
        scratch_shapes=[pltpu.VMEM((8, d), F32)], compiler_params=_params(("arbitrary",)),
    )(y, target)


EW_TILE_BYTES = 1 << 20


def _row_tile(rows, cols):
    tr = 8
    while rows % (2 * tr) == 0 and 2 * tr * cols * 4 <= EW_TILE_BYTES:
        tr *= 2
    return tr if rows % tr == 0 else rows


def _add_halves(name, grad, recv, half, out_dtype):
    j_n, _, r, c = grad.shape
    tr = _row_tile(r, c)

    def body(half_ref, g_ref, r_ref, o_ref):
        o_ref[...] = (g_ref[...] + r_ref[...]).astype(o_ref.dtype)

    return pl.pallas_call(
        body, name=name, out_shape=jax.ShapeDtypeStruct((j_n, r, c), out_dtype),
        grid_spec=pltpu.PrefetchScalarGridSpec(
            num_scalar_prefetch=1, grid=(j_n, r // tr),
            in_specs=[pl.BlockSpec((None, None, tr, c), lambda j, i, hf: (j, hf[0], i, 0)),
                      pl.BlockSpec((None, tr, c), lambda j, i, hf: (j, i, 0))],
            out_specs=pl.BlockSpec((None, tr, c), lambda j, i, hf: (j, i, 0))),
        compiler_params=_params(("parallel", "parallel")),
    )(half, grad, recv)


def _add_chips(name, mine, recv, chip):
    j_n, r, c = mine.shape
    tr = _row_tile(r, c)

    def body(chip_ref, m_ref, r_ref, o_ref):
        total = m_ref[...].astype(F32)
        for k in range(j_n - 1):
            total = total + r_ref[k].astype(F32)
        o_ref[...] = total

    return pl.pallas_call(
        body, name=name, out_shape=jax.ShapeDtypeStruct((r, c), F32),
        grid_spec=pltpu.PrefetchScalarGridSpec(
            num_scalar_prefetch=1, grid=(r // tr,),
            in_specs=[pl.BlockSpec((None, tr, c), lambda i, ch: (ch[0], i, 0)),
                      pl.BlockSpec((j_n - 1, tr, c), lambda i, ch: (0, i, 0))],
            out_specs=pl.BlockSpec((tr, c), lambda i, ch: (i, 0))),
        compiler_params=_params(("parallel",)),
    )(chip, mine, recv)


def _adam_math(w, g, m, v):
    m = ADAM_B1 * m + (1.0 - ADAM_B1) * g
    v = ADAM_B2 * v + (1.0 - ADAM_B2) * (g * g)
    m_hat = m / (1.0 - ADAM_B1 ** ADAM_STEP)
    v_hat = v / (1.0 - ADAM_B2 ** ADAM_STEP)
    delta = -ADAM_LR * (m_hat / (jnp.sqrt(v_hat) + ADAM_EPS) + ADAM_WD * w)
    return delta, m, v


def _adam_halves(name, own, other, half, w, m, v, part, into):
    _, _, r, c = w.shape
    tr = _row_tile(r, c)

    def body(half_ref, own_ref, oth_ref, w_ref, m_ref, v_ref, *rest):
        g_out, d_out, m_out, v_out = rest[-4:]
        g = jnp.where(pl.program_id(0) == half_ref[0], own_ref[...], oth_ref[...])
        delta, m_new, v_new = _adam_math(w_ref[...], g, m_ref[...], v_ref[...])
        g_out[...] = g
        d_out[...] = delta
        m_out[...] = m_new
        v_out[...] = v_new

    flat = pl.BlockSpec((tr, c), lambda h, i, hf: (i, 0))
    full = pl.BlockSpec((None, None, tr, c), lambda h, i, hf: (part, h, i, 0))
    out = jax.ShapeDtypeStruct(w.shape, F32)
    kept = [] if into is None else list(into)
    return pl.pallas_call(
        body, name=name, out_shape=[out] * 4,
        grid_spec=pltpu.PrefetchScalarGridSpec(
            num_scalar_prefetch=1, grid=(2, r // tr),
            in_specs=[flat, flat, full, full, full] + [pl.BlockSpec(memory_space=pl.ANY)] * len(kept), out_specs=[full] * 4),
        input_output_aliases={6 + i: i for i in range(len(kept))},
        compiler_params=_params(("parallel", "parallel")),
    )(half, own, other, w, m, v, *kept)


def _adam_small(name, g, w, m, v):
    def body(g_ref, w_ref, m_ref, v_ref, d_out, m_out, v_out):
        delta, m_new, v_new = _adam_math(w_ref[...], g_ref[...], m_ref[...], v_ref[...])
        d_out[...] = delta
        m_out[...] = m_new
        v_out[...] = v_new

    out = jax.ShapeDtypeStruct(w.shape, F32)
    return pl.pallas_call(body, name=name, out_shape=[out] * 3)(g, w, m, v)


def _place():
    x, y, c = lax.axis_index("x"), lax.axis_index("y"), lax.axis_index("c")
    chips = [(1 - x, y), (x, 1 - y), (1 - x, 1 - y)]
    return x, y, c, chips


ANY = pl.BlockSpec(memory_space=pl.ANY)


class _ShardGather:
    def __init__(self, ins, outs, send, recv, local):
        self.ins, self.outs, self.send, self.recv, self.local = ins, outs, send, recv, local
        self.n = len(ins)

    @staticmethod
    def scratch(n):
        return [pltpu.SemaphoreType.DMA((n, 6)), pltpu.SemaphoreType.DMA((n, 6)), pltpu.SemaphoreType.DMA((n,))]

    @staticmethod
    def out_shapes(shards):
        return [jax.ShapeDtypeStruct((N_CHIPS, *s.shape), s.dtype) for s in shards]

    def _copy(self, t, k, src, dst, to):
        return pltpu.make_async_remote_copy(src_ref=src, dst_ref=dst, send_sem=self.send.at[t, k], recv_sem=self.recv.at[t, k],
                                            device_id=to, device_id_type=MESH)

    def _own(self, t, me):
        return pltpu.make_async_copy(self.ins[t], self.outs[t].at[me], self.local.at[t])

    def _first(self, t, k, place):
        x, y, c, chips = place
        px, py = chips[k]
        return self._copy(t, k, self.ins[t].at[c], self.outs[t].at[2 * x + y, c], (px, py, c))

    def _passed_on(self, t, k, place, half):
        x, y, c, chips = place
        px, py = chips[k]
        slab = self.outs[t].at[2 * px + py, half]
        return self._copy(t, 3 + k, slab, slab, (x, y, 1 - c))

    def begin(self):
        place = _place()
        x, y, c, _ = place
        for t in range(self.n):
            self._own(t, 2 * x + y).start()
        for t in range(self.n):
            for k in range(N_CHIPS - 1):
                self._first(t, k, place).start()

    def end(self):
        place = _place()
        x, y, c, chips = place
        for t in range(self.n):
            for k, (px, py) in enumerate(chips):
                self._copy(t, k, self.ins[t].at[c], self.outs[t].at[2 * px + py, c], (px, py, c)).wait_recv()
                self._passed_on(t, k, place, c).start()
        for t in range(self.n):
            for k in range(N_CHIPS - 1):
                self._passed_on(t, k, place, 1 - c).wait_recv()
        for t in range(self.n):
            for k in range(N_CHIPS - 1):
                self._first(t, k, place).wait_send()
                self._passed_on(t, k, place, c).wait_send()
            self._own(t, 2 * x + y).wait()


class _Exchange:
    def __init__(self, make_copies, ins, outs, send, recv):
        self.copies = lambda: make_copies(ins, outs, send, recv)

    def begin(self):
        for cp in self.copies():
            cp.start()

    def end(self):
        for cp in self.copies():
            cp.wait()


def _gather_job(shards):
    return dict(ins=list(shards), outs=_ShardGather.out_shapes(shards), scratch=_ShardGather.scratch(len(shards)),
                bind=_ShardGather)


def _sibling_job(arrays, pick_other_half):
    n = len(arrays)

    def copies(ins, outs, send, recv):
        x, y, c, _ = _place()
        return [pltpu.make_async_remote_copy(
            src_ref=ins[t].at[:, 1 - c] if pick_other_half else ins[t], dst_ref=outs[t], send_sem=send.at[t],
            recv_sem=recv.at[t], device_id=(x, y, 1 - c), device_id_type=MESH) for t in range(n)]

    shapes = [(a.shape[0], *a.shape[2:]) if pick_other_half else a.shape for a in arrays]
    return dict(ins=list(arrays), outs=[jax.ShapeDtypeStruct(s, a.dtype) for s, a in zip(shapes, arrays)],
                scratch=[pltpu.SemaphoreType.DMA((n,)), pltpu.SemaphoreType.DMA((n,))],
                bind=functools.partial(_Exchange, copies))


def _owner_job(arrays):
    n = len(arrays)

    def copies(ins, outs, send, recv):
        x, y, c, chips = _place()
        return [pltpu.make_async_remote_copy(
            src_ref=ins[t].at[2 * px + py], dst_ref=outs[t].at[k], send_sem=send.at[t, k], recv_sem=recv.at[t, k],
            device_id=(px, py, c), device_id_type=MESH) for t in range(n) for k, (px, py) in enumerate(chips)]

    return dict(ins=list(arrays), outs=[jax.ShapeDtypeStruct((N_CHIPS - 1, *a.shape[1:]), a.dtype) for a in arrays],
                scratch=[pltpu.SemaphoreType.DMA((n, 3)), pltpu.SemaphoreType.DMA((n, 3))],
                bind=functools.partial(_Exchange, copies))


def _bound(job, refs):
    n_i, n_o = len(job["ins"]), len(job["outs"])
    return job["bind"](refs[:n_i], refs[n_i:n_i + n_o], *refs[n_i + n_o:])


def _run_job(name, job):
    def body(*refs):
        bound = _bound(job, refs)
        bound.begin()
        bound.end()

    return pl.pallas_call(
        body, name=name, in_specs=[ANY] * len(job["ins"]), out_specs=[ANY] * len(job["outs"]), out_shape=job["outs"],
        scratch_shapes=job["scratch"],
    )(*job["ins"])


def _sum_all_devices(name, part):
    r, c = part.shape

    def body(p_ref, o_ref, buf, send, recv):
        x, y, cc, _ = _place()
        me = 4 * x + 2 * y + cc
        copies = []
        for mask in range(1, 8):
            fx, fy, fc = (mask >> 2) & 1, (mask >> 1) & 1, mask & 1
            to = (x ^ fx, y ^ fy, cc ^ fc)
            copies.append((mask, pltpu.make_async_remote_copy(
                src_ref=p_ref, dst_ref=buf.at[me], send_sem=send.at[mask - 1], recv_sem=recv.at[mask - 1],
                device_id=to, device_id_type=MESH)))
            copies[-1][1].start()
        buf[me] = p_ref[...]
        for mask, cp in copies:
            pltpu.make_async_remote_copy(src_ref=p_ref, dst_ref=buf.at[me ^ mask], send_sem=send.at[mask - 1],
                                         recv_sem=recv.at[mask - 1], device_id=(x, y, cc), device_id_type=MESH).wait_recv()
        for _, cp in copies:
            cp.wait_send()
        total = buf[0]
        for d in range(1, 8):
            total = total + buf[d]
        o_ref[...] = total

    vm = pl.BlockSpec(memory_space=pltpu.VMEM)
    return pl.pallas_call(
        body, name=name, in_specs=[vm], out_specs=vm, out_shape=jax.ShapeDtypeStruct((r, c), F32),
        scratch_shapes=[pltpu.VMEM((8, r, c), F32), pltpu.SemaphoreType.DMA((7,)), pltpu.SemaphoreType.DMA((7,))],
    )(part)


def kernel(x, pool_w, pool_scale, w_q, w_kv, w_o, ffn_w_gate, ffn_w_up, ffn_conv_w, ffn_conv_b, ffn_w_down, ln1_g, ln1_b, ln2_g, ln2_b, loss_target, m_pool_w, m_pool_scale, m_w_q, m_w_kv, m_w_o, m_ffn_w_gate, m_ffn_w_up, m_ffn_conv_w, m_ffn_conv_b, m_ffn_w_down, m_ln1_g, m_ln1_b, m_ln2_g, m_ln2_b, v_pool_w, v_pool_scale, v_w_q, v_w_kv, v_w_o, v_ffn_w_gate, v_ffn_w_up, v_ffn_conv_w, v_ffn_conv_b, v_ffn_w_down, v_ln1_g, v_ln1_b, v_ln2_g, v_ln2_b):
    b_n, s_n, d = x.shape
    n = b_n * s_n
    f = ffn_w_gate.shape[-1]
    qc = w_q.shape[-1]
    kvb = w_kv.shape[-1] // 2
    n_attn = w_q.shape[0]
    g_n = len(DILATIONS)
    cw = N_HEADS * HEAD_DIM
    xi, yi, ci = lax.axis_index("x"), lax.axis_index("y"), lax.axis_index("c")
    half = jnp.reshape(ci, (1,)).astype(jnp.int32)
    chip = jnp.reshape(2 * xi + yi, (1,)).astype(jnp.int32)

    sharded = {
        "pool_w": (pool_w, m_pool_w, v_pool_w, (2, 4 * 64, POOL_GROUP_DIM)),
        "pool_scale": (pool_scale, m_pool_scale, v_pool_scale, (2, 1, pool_scale.shape[-1])),
        "w_q": (w_q, m_w_q, v_w_q, (2, d, qc)),
        "w_kv": (w_kv, m_w_kv, v_w_kv, (2, d // 2, w_kv.shape[-1])),
        "w_o": (w_o, m_w_o, v_w_o, (2, w_o.shape[1], d)),
        "ffn_w_gate": (ffn_w_gate, m_ffn_w_gate, v_ffn_w_gate, (2, 2 * d, f)),
        "ffn_w_up": (ffn_w_up, m_ffn_w_up, v_ffn_w_up, (2, 2 * d, f)),
        "ffn_conv_w": (ffn_conv_w, m_ffn_conv_w, v_ffn_conv_w, (2, 6, f)),
        "ffn_w_down": (ffn_w_down, m_ffn_w_down, v_ffn_w_down, (2, 2 * f, d)),
    }
    mxu_weights = ("pool_w", "w_q", "w_kv", "w_o", "ffn_w_gate", "ffn_w_up", "ffn_w_down")
    names = list(sharded)
    wo_rows = w_o.shape[1]
    shard_of = {("wkv", 0): w_kv.astype(MXU).reshape(2, d // 2, w_kv.shape[-1])}
    for i in range(DEPTH):
        shard_of["wg", i] = ffn_w_gate[i].astype(MXU).reshape(2, d // 2, f)
        shard_of["wu", i] = ffn_w_up[i].astype(MXU).reshape(2, d // 2, f)
        shard_of["wd", i] = ffn_w_down[i].astype(MXU).reshape(2, f // 2, d)
    for i in range(n_attn):
        shard_of["wq", i] = w_q[i].astype(MXU).reshape(2, d // 2, qc)
        shard_of["wo", i] = w_o[i].astype(MXU).reshape(2, wo_rows // 2, d)
    carried_by = {
        (0, "gate"): [("wg", 1)], (0, "up"): [("wu", 1)], (0, "down"): [("wd", 1)],
        (1, "gate"): [("wkv", 0)], (1, "up"): [("wq", 0), ("wo", 0)], (1, "down"): [("wg", 2)],
        (2, "q"): [("wu", 2)], (2, "o"): [("wd", 2)], (2, "gate"): [("wq", 1), ("wo", 1)], (2, "up"): [("wg", 3)],
        (2, "down"): [("wu", 3)], (3, "q"): [("wd", 3)],
    }
    got = {}

    def carrying(site, call, *args):
        keys = carried_by.get(site, [])
        if not keys:
            return call(*args)
        out, arrived = call(*args, job=_gather_job([shard_of[k] for k in keys]))
        got.update(zip(keys, arrived))
        return out

    first_keys = [("wg", 0), ("wu", 0), ("wd", 0)]
    first = _run_job("gather_weights", _gather_job([
        pool_w.astype(MXU).reshape(sharded["pool_w"][3]), pool_scale.reshape(sharded["pool_scale"][3]),
        ffn_conv_w.reshape(sharded["ffn_conv_w"][3])] + [shard_of[k] for k in first_keys]))
    got.update(zip(first_keys, first[3:]))
    wg_at = lambda i: got["wg", i].reshape(N_CHIPS, d, f)
    wu_at = lambda i: got["wu", i].reshape(N_CHIPS, d, f)
    wd_at = lambda i: got["wd", i].reshape(N_CHIPS, f, d)
    wq_at = lambda i: got["wq", i].reshape(N_CHIPS, d, qc)
    wo_at = lambda i: got["wo", i].reshape(cw, d)
    wkv_at = lambda: got["wkv", 0].reshape(N_CHIPS, d, w_kv.shape[-1])
    cw_all = first[2].reshape(N_CHIPS, DEPTH, 3, f)
    cw_l = [cw_all[:, i] for i in range(DEPTH)]
    cb_l = [ffn_conv_b[i].reshape(N_CHIPS, 1, f) for i in range(DEPTH)]
    pw_nat = first[0].reshape(N_CHIPS, N_POOL_LAYERS, 4, 64, POOL_GROUP_DIM).transpose(1, 2, 0, 3, 4).reshape(
        N_POOL_LAYERS, 4, POOL_GROUP_DIM, POOL_GROUP_DIM)
    ps_nat = first[1].reshape(N_CHIPS, N_POOL_LAYERS, -1).transpose(1, 0, 2).reshape(N_POOL_LAYERS, 1, d)
    cos_l, sin_l = _rope_tables(s_n)

    def vec(a, layer):
        return a[layer].reshape(1, d)

    h = x.reshape(n, d)
    hb = None
    saved = []
    k_str = v_str = None
    for layer in range(DEPTH):
        keep = {"h": h, "hb": hb}
        if layer < N_POOL_LAYERS:
            mix = _pool_fwd("pool_fwd", h.reshape(b_n, s_n, d), pw_nat[layer], ps_nat[layer]).reshape(n, d)
        else:
            a = layer - N_POOL_LAYERS
            q = carrying((layer, "q"), functools.partial(_mm_cols, "q_proj", hb, wq_at(a), (), F32))
            qr = _rope("q_rope", q.reshape(b_n, s_n, -1), 0, g_n, cos_l, sin_l, HEAD_DIM ** -0.5, MXU)
            q_str, lse_str, outs, lses = [], [], [], []
            for g, dil in enumerate(DILATIONS):
                if dil == 1:
                    qg, col = qr, g
                else:
                    qg, col = _to_strided(qr[:, :, g * cw:(g + 1) * cw], dil), 0
                o_g, lse_g = _attn_fwd(f"attn_fwd_d{dil}", qg, col, k_str[g], v_str[g], s_n // dil // SPAN)
                q_str.append((qg, col))
                lse_str.append(lse_g)
                outs.append(_from_strided(o_g, dil).reshape(n, cw))
                lses.append(_from_strided(lse_g, dil).reshape(n, N_HEADS))
            ob = _attn_mix("attn_mix", outs, lses)
            mix = carrying((layer, "o"), functools.partial(
                _mm, "o_proj", [(ob, (TM, cw), lambda m, _: (m, 0), wo_at(a), (cw, d), lambda m, _: (0, 0))],
                NN, (n // TM, 1), (n, d), F32, (TM, d), lambda m, _: (m, 0), 1))
            keep.update(q_str=q_str, lse_str=lse_str, outs=outs, lses=lses, ob=ob)
        r1, h1, h1b = _add_ln("ln_fwd", h, mix, vec(ln1_g, layer), vec(ln1_b, layer))
        gate = carrying((layer, "gate"), functools.partial(_mm_shard_out, "gate_up_proj", h1b, wg_at(layer), (), F32))
        up = carrying((layer, "up"), functools.partial(_mm_shard_out, "gate_up_proj", h1b, wu_at(layer), (), F32))
        g4, u4 = gate.reshape(N_CHIPS, b_n, s_n, f), up.reshape(N_CHIPS, b_n, s_n, f)
        hmid = _glu_fwd("glu_fwd", g4, u4, cw_l[layer], cb_l[layer]).reshape(N_CHIPS, n, f)
        ffn = carrying((layer, "down"), functools.partial(_mm_shard_in, "down_proj", hmid, wd_at(layer), ()))
        r2, h2, h2b = _add_ln("ln_fwd", h1, ffn, vec(ln2_g, layer), vec(ln2_b, layer))
        keep.update(r1=r1, h1b=h1b, g4=g4, u4=u4, hmid=hmid, r2=r2, h2b=h2b)
        saved.append(keep)
        if layer == N_POOL_LAYERS - 1:
            kv = _mm_cols("kv_proj", h2b, wkv_at(), (), F32, cb=kvb).reshape(b_n, s_n, -1)
            k_nat = _rope("k_rope", kv, 0, g_n, cos_l, sin_l, 1.0, MXU)
            v_nat = _rope("v_cast", kv, g_n, g_n, cos_l, sin_l, 1.0, MXU, rotate=False)
            k_str = [_to_strided(k_nat[:, :, g * cw:(g + 1) * cw], dil) for g, dil in enumerate(DILATIONS)]
            v_str = [_to_strided(v_nat[:, :, g * cw:(g + 1) * cw], dil) for g, dil in enumerate(DILATIONS)]
        h, hb = h2, h2b

    dy, loss_tile = _loss_head("loss_head", h, loss_target.reshape(n, d))

    pending = {}
    reduced = {}
    halves_of = lambda t: t.reshape(N_CHIPS, 2, t.shape[1] // 2, t.shape[2])
    on_ici = lambda key: F32 if key[0] in ("pool_w", "pool_scale", "ffn_conv_w") else MXU
    d_cw = [None] * DEPTH
    d_pw, d_ps = [None] * N_POOL_LAYERS, [None] * N_POOL_LAYERS
    d_ln = {}
    dk_str = dv_str = None
    d_top, top_scale, top_rest = dy, 1.0, []
    for layer in reversed(range(DEPTH)):
        sv = saved[layer]
        dr2, dr2b, d_ln["ln2", layer] = _ln_bwd("ln_bwd", sv["r2"], vec(ln2_g, layer), d_top, top_scale, top_rest)
        keys = list(pending)
        job = _sibling_job([pending[k] for k in keys], True) if keys else None
        dhmid = _mm_nt_shard_out("down_bwd", dr2b, wd_at(layer), (), job=job)
        if keys:
            dhmid, from_sibling = dhmid
            core_sums = [_add_halves("add_halves", pending[k], r_, half, on_ici(k)) for k, r_ in zip(keys, from_sibling)]
        d_wd = _mm_tn("down_dw", sv["hmid"], f, dr2b, d, N_CHIPS, (N_CHIPS, f, d), (None, f, d),
                      lambda j: (j, 0, 0), a_lead=lambda j: (j,))
        glu = _glu_bwd("glu_bwd", dhmid.reshape(N_CHIPS, b_n, s_n, f), sv["g4"], sv["u4"], cw_l[layer], cb_l[layer],
                       job=_owner_job(core_sums) if keys else None)
        dg4, du4, d_cw[layer] = glu[:3]
        if keys:
            owned = [_add_chips("add_chips", s_, r_, chip) for s_, r_ in zip(core_sums, glu[3])]
        dg, du = dg4.reshape(N_CHIPS, n, f), du4.reshape(N_CHIPS, n, f)
        dh1 = _mm_nt_shard_in("gate_up_bwd", [(dg, wg_at(layer), ()), (du, wu_at(layer), ())],
                              job=_sibling_job(owned, False) if keys else None)
        if keys:
            dh1, others = dh1
            reduced.update({k: pair for k, pair in zip(keys, zip(owned, others))})
        d_wg = _mm_tn("gate_up_dw", sv["h1b"], d, dg, f, N_CHIPS, (N_CHIPS, d, f), (None, d, f),
                      lambda j: (j, 0, 0), b_lead=lambda j: (j,))
        d_wu = _mm_tn("gate_up_dw", sv["h1b"], d, du, f, N_CHIPS, (N_CHIPS, d, f), (None, d, f),
                      lambda j: (j, 0, 0), b_lead=lambda j: (j,))
        pending = {("ffn_w_down", layer): halves_of(d_wd), ("ffn_w_gate", layer): halves_of(d_wg),
                   ("ffn_w_up", layer): halves_of(d_wu)}
        dr1, dr1b, d_ln["ln1", layer] = _ln_bwd("ln_bwd", sv["r1"], vec(ln1_g, layer), dr2, ALPHA, [dh1])
        if layer < N_POOL_LAYERS:
            d_in, d_pw[layer], d_ps[layer] = _pool_bwd("pool_bwd", sv["h"].reshape(b_n, s_n, d),
                                                       dr1.reshape(b_n, s_n, d), pw_nat[layer], ps_nat[layer])
            d_top, top_scale, top_rest = d_in.reshape(n, d), 1.0, []
        else:
            a = layer - N_POOL_LAYERS
            do = _mm("o_bwd", [(dr1b, (TM, d), lambda m, _: (m, 0), wo_at(a), (cw, d), lambda m, _: (0, 0))],
                     NT, (n // TM, 1), (n, cw), F32, (TM, cw), lambda m, _: (m, 0), 1)
            d_wo = _mm_tn("o_dw", sv["ob"], cw // N_CHIPS, dr1b, d, N_CHIPS, (N_CHIPS, cw // N_CHIPS, d),
                          (None, cw // N_CHIPS, d), lambda j: (j, 0, 0))
            pending["w_o", a] = halves_of(d_wo)
            mixed = _attn_mix_bwd("attn_mix_bwd", do, sv["outs"], sv["lses"])
            dq_nat, dk_new, dv_new = [], [], []
            for g, dil in enumerate(DILATIONS):
                do_g = _to_strided(mixed[g].reshape(b_n, s_n, cw), dil)
                dd_g = _to_strided(mixed[g_n + g].reshape(b_n, s_n, N_HEADS), dil)
                qg, col = sv["q_str"][g]
                dq_g, dk_g, dv_g = _attn_bwd(f"attn_bwd_d{dil}", qg, col, k_str[g], v_str[g], do_g, sv["lse_str"][g], dd_g,
                                             s_n // dil // SPAN, *((dk_str[g], dv_str[g]) if dk_str else ()))
                dq_nat.append(_from_strided(dq_g, dil))
                dk_new.append(dk_g)
                dv_new.append(dv_g)
            dk_str, dv_str = dk_new, dv_new
            dq = None
            for g in range(g_n):
                dq = _rope("q_rope_bwd", dq_nat[g], 0, 1, cos_l, -sin_l, HEAD_DIM ** -0.5, MXU, out_cols=g_n, out_col0=g, into=dq)
            dq = dq.reshape(n, g_n * cw)
            d_attn = _mm_nt_cols_in("q_bwd", dq, wq_at(a), (), qc)
            d_wq = _mm_tn("q_dw", sv["hb"], d, dq, qc, N_CHIPS, (N_CHIPS, d, qc), (None, d, qc), lambda j: (j, 0, 0))
            pending["w_q", a] = halves_of(d_wq)
            d_top, top_scale, top_rest = dr1, ALPHA, [d_attn]
            if a == 0:
                dkv = None
                for g, dil in enumerate(DILATIONS):
                    dkv = _rope("k_rope_bwd", _from_strided(dk_str[g], dil), 0, 1, cos_l, -sin_l, 1.0, MXU,
                                out_cols=2 * g_n, out_col0=g, into=dkv)
                for g, dil in enumerate(DILATIONS):
                    dkv = _rope("v_cast_bwd", _from_strided(dv_str[g], dil), 0, 1, cos_l, sin_l, 1.0, MXU, rotate=False,
                                out_cols=2 * g_n, out_col0=g_n + g, into=dkv)
                dkv = dkv.reshape(n, 2 * g_n * cw)
                h_kv = saved[N_POOL_LAYERS - 1]["h2b"]
                top_rest = top_rest + [_mm_nt_cols_in("kv_bwd", dkv, wkv_at(), (), kvb)]
                d_wkv = _mm_tn("kv_dw", h_kv, d, dkv, kvb, 2 * N_CHIPS, (N_CHIPS, d, 2 * kvb), (None, d, kvb),
                               lambda q: (q // 2, 0, q % 2))
                pending["w_kv", 0] = halves_of(d_wkv)
    grad_x = d_top.reshape(b_n, s_n, d)

    d_pw_all = jnp.stack(d_pw).reshape(N_POOL_LAYERS, 4, N_CHIPS, 64, POOL_GROUP_DIM).transpose(2, 0, 1, 3, 4)
    d_ps_all = jnp.stack(d_ps).reshape(N_POOL_LAYERS, N_CHIPS, -1).transpose(1, 0, 2)
    d_cw_all = jnp.stack([t[:, :3] for t in d_cw], axis=1)
    for k, t in (("pool_w", d_pw_all), ("pool_scale", d_ps_all), ("ffn_conv_w", d_cw_all)):
        pending[k, 0] = t.reshape(N_CHIPS, *sharded[k][3])
    keys = list(pending)
    from_sibling = _run_job("grads_to_sibling", _sibling_job([pending[k] for k in keys], True))
    core_sums = [_add_halves("add_halves", pending[k], r_, half, on_ici(k)) for k, r_ in zip(keys, from_sibling)]
    from_chips = _run_job("grads_to_owner", _owner_job(core_sums))
    owned = [_add_chips("add_chips", s_, r_, chip) for s_, r_ in zip(core_sums, from_chips)]
    others = _run_job("halves_to_sibling", _sibling_job(owned, False))
    reduced.update({k: pair for k, pair in zip(keys, zip(owned, others))})

    out_grad, out_delta, out_m, out_v = {}, {}, {}, {}
    for k in names:
        w_, m_, v_, _ = sharded[k]
        parts = sorted(i for name_, i in reduced if name_ == k)
        shape = (len(parts), 2, *reduced[k, 0][0].shape)
        res = None
        for i in parts:
            res = _adam_halves(f"adam_{k}", *reduced[k, i], half, w_.reshape(shape), m_.reshape(shape), v_.reshape(shape), i, res)
        out_grad[k], out_delta[k], out_m[k], out_v[k] = (t.reshape(w_.shape) for t in res)

    d_cb = jnp.stack([t[:, 3] for t in d_cw], axis=0).reshape(DEPTH * N_CHIPS * f // d, d)
    ln_rows = jnp.concatenate([jnp.stack([d_ln[which, layer][row] for layer in range(DEPTH)])
                               for which, row in (("ln1", 0), ("ln1", 1), ("ln2", 0), ("ln2", 1))])
    rows = jnp.concatenate([ln_rows, d_cb, jnp.broadcast_to(loss_tile[0:1, 0:1], (1, d))])
    pad = (-rows.shape[0]) % 8
    total = _sum_all_devices("sum_small", jnp.pad(rows, ((0, pad), (0, 0))))
    small = {"ln1_g": total[0:4], "ln1_b": total[4:8], "ln2_g": total[8:12], "ln2_b": total[12:16],
             "ffn_conv_b": total[16:16 + d_cb.shape[0]].reshape(ffn_conv_b.shape)}
    loss = total[16 + d_cb.shape[0], 0]
    small_in = {"ln1_g": (ln1_g, m_ln1_g, v_ln1_g), "ln1_b": (ln1_b, m_ln1_b, v_ln1_b), "ln2_g": (ln2_g, m_ln2_g, v_ln2_g),
                "ln2_b": (ln2_b, m_ln2_b, v_ln2_b), "ffn_conv_b": (ffn_conv_b, m_ffn_conv_b, v_ffn_conv_b)}
    for k, (w_, m_, v_) in small_in.items():
        out_grad[k] = small[k]
        out_delta[k], out_m[k], out_v[k] = _adam_small(f"adam_{k}", small[k], w_, m_, v_)

    order = ["pool_w", "pool_scale", "w_q", "w_kv", "w_o", "ffn_w_gate", "ffn_w_up", "ffn_conv_w", "ffn_conv_b",
             "ffn_w_down", "ln1_g", "ln1_b", "ln2_g", "ln2_b"]
    return (loss, grad_x, *[out_grad[k] for k in order], *[out_delta[k] for k in order],
            *[out_m[k] for k in order], *[out_v[k] for k in order])
```
